```python
import math
import jax, jax.numpy as jnp
from jax import lax
import numpy as np

D_MODEL = 1024
BATCH = 8
SEQ = 2048
DEPTH = 4

N_MIXERS = 2
N_A_LAYERS = (DEPTH + N_MIXERS - 1) // N_MIXERS
N_B_LAYERS = DEPTH // N_MIXERS

SSM_EXPAND = 2
D_INNER = SSM_EXPAND * D_MODEL
HEAD_DIM = 64
N_HEADS = D_INNER // HEAD_DIM
N_GROUPS = 4
HEADS_PER_GROUP = N_HEADS // N_GROUPS
D_STATE = 128
SSM_CONV = 4
CONV_DIM = D_INNER + 2 * N_GROUPS * D_STATE
D_IN_PROJ = 2 * D_INNER + 2 * N_GROUPS * D_STATE + N_HEADS
CHUNK = 128
DT_MIN = 0.001
DT_MAX = 0.1

CONV_KERNEL = 31

D_FF = -(-8 * D_MODEL // (3 * 256)) * 256

EPS = 1e-5

kernel_name = "hybrid_ssd_conformer_trunk"


def rmsnorm(x, g):
    xf = x.astype(jnp.float32)
    y = xf * lax.rsqrt(jnp.mean(xf * xf, axis=-1, keepdims=True) + EPS)
    return (y * g).astype(x.dtype)


def layernorm(x, g, b):
    xf = x.astype(jnp.float32)
    mu = jnp.mean(xf, axis=-1, keepdims=True)
    xc = xf - mu
    var = jnp.mean(xc * xc, axis=-1, keepdims=True)
    return (xc * lax.rsqrt(var + EPS) * g + b).astype(x.dtype)


def gated_group_rmsnorm(y, z, g):
    h = (y * jax.nn.silu(z)).astype(jnp.float32)
    shp = h.shape
    h = h.reshape(shp[:-1] + (N_GROUPS, shp[-1] // N_GROUPS))
    h = h * lax.rsqrt(jnp.mean(h * h, axis=-1, keepdims=True) + EPS)
    return (h.reshape(shp) * g).astype(y.dtype)


def causal_depthwise_conv(x, w, b):
    k = w.shape[0]
    y = lax.conv_general_dilated(
        x, w[:, None, :], window_strides=(1,), padding=[(k - 1, 0)],
        dimension_numbers=("NWC", "WIO", "NWC"), feature_group_count=x.shape[-1])
    return y + b


def segsum(a):
    t = a.shape[-1]
    ar = jnp.broadcast_to(a[..., :, None], a.shape + (t,))
    ar = jnp.where(jnp.tril(jnp.ones((t, t), dtype=bool), -1), ar, 0.0)
    cs = jnp.cumsum(ar, axis=-2)
    return jnp.where(jnp.tril(jnp.ones((t, t), dtype=bool), 0), cs, -jnp.inf)


def ssd_chunked(x, da, bm, cm):
    b, l, h, p = x.shape
    c = l // CHUNK
    dt_ = x.dtype
    x = x.reshape(b, c, CHUNK, N_GROUPS, HEADS_PER_GROUP, p)
    bm = bm.reshape(b, c, CHUNK, N_GROUPS, D_STATE)
    cm = cm.reshape(b, c, CHUNK, N_GROUPS, D_STATE)
    a = da.reshape(b, c, CHUNK, N_GROUPS, HEADS_PER_GROUP).transpose(0, 3, 4, 1, 2)
    a_cs = jnp.cumsum(a, axis=-1)

    lmat = jnp.exp(segsum(a)).astype(dt_)
    cb = jnp.einsum("bclgn,bcsgn->bgcls", cm, bm)
    y_diag = jnp.einsum("bgrcls,bcsgrp->bclgrp", cb[:, :, None] * lmat, x)

    decay_states = jnp.exp(a_cs[..., -1:] - a_cs).astype(dt_)
    xd = x * decay_states.transpose(0, 3, 4, 1, 2)[..., None]
    states = jnp.einsum("bclgn,bclgrp->bcgrpn", bm, xd)

    chunk_tot = jnp.pad(a_cs[..., -1], ((0, 0), (0, 0), (0, 0), (1, 0)))
    decay_chunk = jnp.exp(segsum(chunk_tot))[..., :c, 1:].astype(dt_)
    states_in = jnp.einsum("bgrzj,bjgrpn->bzgrpn", decay_chunk, states)

    state_decay_out = jnp.exp(a_cs).astype(dt_).transpose(0, 3, 4, 1, 2)
    y_off = jnp.einsum("bclgn,bcgrpn->bclgrp", cm, states_in) * state_decay_out[..., None]
    return (y_diag + y_off).reshape(b, l, h, p)


def mamba2_mixer(u, w_in, conv_w, conv_b, dt_bias, a_log, d_skip, gate_norm, w_out):
    b, l, _ = u.shape
    zxbcdt = u @ w_in
    z, xbc, dt = jnp.split(zxbcdt, [D_INNER, D_INNER + CONV_DIM], axis=-1)
    xbc = jax.nn.silu(causal_depthwise_conv(xbc, conv_w, conv_b))
    xs, bm, cm = jnp.split(xbc, [D_INNER, D_INNER + N_GROUPS * D_STATE], axis=-1)
    xs = xs.reshape(b, l, N_HEADS, HEAD_DIM)
    bm = bm.reshape(b, l, N_GROUPS, D_STATE)
    cm = cm.reshape(b, l, N_GROUPS, D_STATE)
    dt = jax.nn.softplus(dt.astype(jnp.float32) + dt_bias.astype(jnp.float32))
    a = -jnp.exp(a_log.astype(jnp.float32))
    y = ssd_chunked(xs * dt[..., None].astype(xs.dtype), dt * a, bm, cm)
    y = y + xs * d_skip[:, None]
    y = gated_group_rmsnorm(y.reshape(b, l, D_INNER), z, gate_norm)
    return y @ w_out


def conformer_conv_module(u, w_pw1, b_pw1, dw_w, dw_b, ln_g, ln_b, w_pw2, b_pw2):
    h = u @ w_pw1 + b_pw1
    a, g = jnp.split(h, 2, axis=-1)
    h = a * jax.nn.sigmoid(g)
    h = causal_depthwise_conv(h, dw_w, dw_b)
    h = jax.nn.silu(layernorm(h, ln_g, ln_b))
    return h @ w_pw2 + b_pw2


def swiglu_ffn(u, w_gate, w_up, w_down):
    return (jax.nn.silu(u @ w_gate) * (u @ w_up)) @ w_down


def _fwd_setup_inputs(seed: int = 0) -> dict:
    key = jax.random.key(seed)
    ks = iter(jax.random.split(key, 40))

    def nrm(shape, scale):
        return jax.random.normal(next(ks), shape, jnp.float32) * scale

    na, nb = N_A_LAYERS, N_B_LAYERS
    x = nrm((BATCH, SEQ, D_MODEL), 1.0)

    u = jax.random.uniform(next(ks), (na, N_HEADS), jnp.float32)
    dt0 = jnp.exp(u * (math.log(DT_MAX) - math.log(DT_MIN)) + math.log(DT_MIN))
    dt0 = jnp.maximum(dt0, 1e-4)
    ssm_dt_bias = dt0 + jnp.log(-jnp.expm1(-dt0))
    ssm_a_log = jnp.log(jax.random.uniform(next(ks), (na, N_HEADS), jnp.float32, 1.0, 16.0))

    return {
        "x": x,
        "ssm_norm": 1.0 + nrm((na, D_MODEL), 0.05),
        "ssm_w_in": nrm((na, D_MODEL, D_IN_PROJ), D_MODEL ** -0.5),
        "ssm_conv_w": nrm((na, SSM_CONV, CONV_DIM), SSM_CONV ** -0.5),
        "ssm_conv_b": nrm((na, CONV_DIM), 0.02),
        "ssm_dt_bias": ssm_dt_bias,
        "ssm_a_log": ssm_a_log,
        "ssm_d": 1.0 + nrm((na, N_HEADS), 0.1),
        "ssm_gate_norm": 1.0 + nrm((na, D_INNER), 0.05),
        "ssm_w_out": nrm((na, D_INNER, D_MODEL), D_INNER ** -0.5),
        "cv_norm": 1.0 + nrm((nb, D_MODEL), 0.05),
        "cv_w_pw1": nrm((nb, D_MODEL, 2 * D_MODEL), D_MODEL ** -0.5),
        "cv_b_pw1": nrm((nb, 2 * D_MODEL), 0.02),
        "cv_dw_w": nrm((nb, CONV_KERNEL, D_MODEL), CONV_KERNEL ** -0.5),
        "cv_dw_b": nrm((nb, D_MODEL), 0.02),
        "cv_ln_g": 1.0 + nrm((nb, D_MODEL), 0.05),
        "cv_ln_b": nrm((nb, D_MODEL), 0.02),
        "cv_w_pw2": nrm((nb, D_MODEL, D_MODEL), D_MODEL ** -0.5),
        "cv_b_pw2": nrm((nb, D_MODEL), 0.02),
        "ffn_norm": 1.0 + nrm((DEPTH, D_MODEL), 0.05),
        "ffn_w_gate": nrm((DEPTH, D_MODEL, D_FF), D_MODEL ** -0.5),
        "ffn_w_up": nrm((DEPTH, D_MODEL, D_FF), D_MODEL ** -0.5),
        "ffn_w_down": nrm((DEPTH, D_FF, D_MODEL), D_FF ** -0.5),
        "final_norm": 1.0 + nrm((D_MODEL,), 0.05),
    }


def _fwd_reference(x, ssm_norm, ssm_w_in, ssm_conv_w, ssm_conv_b, ssm_dt_bias, ssm_a_log, ssm_d,
              ssm_gate_norm, ssm_w_out, cv_norm, cv_w_pw1, cv_b_pw1, cv_dw_w, cv_dw_b, cv_ln_g,
              cv_ln_b, cv_w_pw2, cv_b_pw2, ffn_norm, ffn_w_gate, ffn_w_up, ffn_w_down, final_norm):
    h = x
    for i in range(DEPTH):
        j = i // N_MIXERS
        if i % N_MIXERS == 0:
            h = h + mamba2_mixer(rmsnorm(h, ssm_norm[j]), ssm_w_in[j], ssm_conv_w[j], ssm_conv_b[j],
                                 ssm_dt_bias[j], ssm_a_log[j], ssm_d[j], ssm_gate_norm[j], ssm_w_out[j])
        else:
            h = h + conformer_conv_module(rmsnorm(h, cv_norm[j]), cv_w_pw1[j], cv_b_pw1[j], cv_dw_w[j],
                                          cv_dw_b[j], cv_ln_g[j], cv_ln_b[j], cv_w_pw2[j], cv_b_pw2[j])
        h = h + swiglu_ffn(rmsnorm(h, ffn_norm[i]), ffn_w_gate[i], ffn_w_up[i], ffn_w_down[i])
    return rmsnorm(h, final_norm)


import jax as _jax
import jax.numpy as _jnp

TWIN_FORMAT = 'train_step'
FWD_PARAMS = ['x', 'ssm_norm', 'ssm_w_in', 'ssm_conv_w', 'ssm_conv_b', 'ssm_dt_bias', 'ssm_a_log', 'ssm_d', 'ssm_gate_norm', 'ssm_w_out', 'cv_norm', 'cv_w_pw1', 'cv_b_pw1', 'cv_dw_w', 'cv_dw_b', 'cv_ln_g', 'cv_ln_b', 'cv_w_pw2', 'cv_b_pw2', 'ffn_norm', 'ffn_w_gate', 'ffn_w_up', 'ffn_w_down', 'final_norm']
TWIN_WEIGHTS = ['ssm_norm', 'ssm_w_in', 'ssm_conv_w', 'ssm_conv_b', 'ssm_dt_bias', 'ssm_a_log', 'ssm_d', 'ssm_gate_norm', 'ssm_w_out', 'cv_norm', 'cv_w_pw1', 'cv_b_pw1', 'cv_dw_w', 'cv_dw_b', 'cv_ln_g', 'cv_ln_b', 'cv_w_pw2', 'cv_b_pw2', 'ffn_norm', 'ffn_w_gate', 'ffn_w_up', 'ffn_w_down', 'final_norm']
TWIN_DIFF_INPUT = 'x'
TWIN_INPUTS = ['x', 'ssm_norm', 'ssm_w_in', 'ssm_conv_w', 'ssm_conv_b', 'ssm_dt_bias', 'ssm_a_log', 'ssm_d', 'ssm_gate_norm', 'ssm_w_out', 'cv_norm', 'cv_w_pw1', 'cv_b_pw1', 'cv_dw_w', 'cv_dw_b', 'cv_ln_g', 'cv_ln_b', 'cv_w_pw2', 'cv_b_pw2', 'ffn_norm', 'ffn_w_gate', 'ffn_w_up', 'ffn_w_down', 'final_norm', 'loss_target', 'm_ssm_norm', 'm_ssm_w_in', 'm_ssm_conv_w', 'm_ssm_conv_b', 'm_ssm_dt_bias', 'm_ssm_a_log', 'm_ssm_d', 'm_ssm_gate_norm', 'm_ssm_w_out', 'm_cv_norm', 'm_cv_w_pw1', 'm_cv_b_pw1', 'm_cv_dw_w', 'm_cv_dw_b', 'm_cv_ln_g', 'm_cv_ln_b', 'm_cv_w_pw2', 'm_cv_b_pw2', 'm_ffn_norm', 'm_ffn_w_gate', 'm_ffn_w_up', 'm_ffn_w_down', 'm_final_norm', 'v_ssm_norm', 'v_ssm_w_in', 'v_ssm_conv_w', 'v_ssm_conv_b', 'v_ssm_dt_bias', 'v_ssm_a_log', 'v_ssm_d', 'v_ssm_gate_norm', 'v_ssm_w_out', 'v_cv_norm', 'v_cv_w_pw1', 'v_cv_b_pw1', 'v_cv_dw_w', 'v_cv_dw_b', 'v_cv_ln_g', 'v_cv_ln_b', 'v_cv_w_pw2', 'v_cv_b_pw2', 'v_ffn_norm', 'v_ffn_w_gate', 'v_ffn_w_up', 'v_ffn_w_down', 'v_final_norm']
TWIN_OUTPUTS = ['loss', 'grad_x', 'grad_ssm_norm', 'grad_ssm_w_in', 'grad_ssm_conv_w', 'grad_ssm_conv_b', 'grad_ssm_dt_bias', 'grad_ssm_a_log', 'grad_ssm_d', 'grad_ssm_gate_norm', 'grad_ssm_w_out', 'grad_cv_norm', 'grad_cv_w_pw1', 'grad_cv_b_pw1', 'grad_cv_dw_w', 'grad_cv_dw_b', 'grad_cv_ln_g', 'grad_cv_ln_b', 'grad_cv_w_pw2', 'grad_cv_b_pw2', 'grad_ffn_norm', 'grad_ffn_w_gate', 'grad_ffn_w_up', 'grad_ffn_w_down', 'grad_final_norm', 'delta_ssm_norm', 'delta_ssm_w_in', 'delta_ssm_conv_w', 'delta_ssm_conv_b', 'delta_ssm_dt_bias', 'delta_ssm_a_log', 'delta_ssm_d', 'delta_ssm_gate_norm', 'delta_ssm_w_out', 'delta_cv_norm', 'delta_cv_w_pw1', 'delta_cv_b_pw1', 'delta_cv_dw_w', 'delta_cv_dw_b', 'delta_cv_ln_g', 'delta_cv_ln_b', 'delta_cv_w_pw2', 'delta_cv_b_pw2', 'delta_ffn_norm', 'delta_ffn_w_gate', 'delta_ffn_w_up', 'delta_ffn_w_down', 'delta_final_norm', 'new_m_ssm_norm', 'new_m_ssm_w_in', 'new_m_ssm_conv_w', 'new_m_ssm_conv_b', 'new_m_ssm_dt_bias', 'new_m_ssm_a_log', 'new_m_ssm_d', 'new_m_ssm_gate_norm', 'new_m_ssm_w_out', 'new_m_cv_norm', 'new_m_cv_w_pw1', 'new_m_cv_b_pw1', 'new_m_cv_dw_w', 'new_m_cv_dw_b', 'new_m_cv_ln_g', 'new_m_cv_ln_b', 'new_m_cv_w_pw2', 'new_m_cv_b_pw2', 'new_m_ffn_norm', 'new_m_ffn_w_gate', 'new_m_ffn_w_up', 'new_m_ffn_w_down', 'new_m_final_norm', 'new_v_ssm_norm', 'new_v_ssm_w_in', 'new_v_ssm_conv_w', 'new_v_ssm_conv_b', 'new_v_ssm_dt_bias', 'new_v_ssm_a_log', 'new_v_ssm_d', 'new_v_ssm_gate_norm', 'new_v_ssm_w_out', 'new_v_cv_norm', 'new_v_cv_w_pw1', 'new_v_cv_b_pw1', 'new_v_cv_dw_w', 'new_v_cv_dw_b', 'new_v_cv_ln_g', 'new_v_cv_ln_b', 'new_v_cv_w_pw2', 'new_v_cv_b_pw2', 'new_v_ffn_norm', 'new_v_ffn_w_gate', 'new_v_ffn_w_up', 'new_v_ffn_w_down', 'new_v_final_norm']
TWIN_LEAF_KINDS = {'loss': 'loss', 'grad_x': 'grad_x', 'grad_ssm_norm': 'grad_w', 'grad_ssm_w_in': 'grad_w', 'grad_ssm_conv_w': 'grad_w', 'grad_ssm_conv_b': 'grad_w', 'grad_ssm_dt_bias': 'grad_w', 'grad_ssm_a_log': 'grad_w', 'grad_ssm_d': 'grad_w', 'grad_ssm_gate_norm': 'grad_w', 'grad_ssm_w_out': 'grad_w', 'grad_cv_norm': 'grad_w', 'grad_cv_w_pw1': 'grad_w', 'grad_cv_b_pw1': 'grad_w', 'grad_cv_dw_w': 'grad_w', 'grad_cv_dw_b': 'grad_w', 'grad_cv_ln_g': 'grad_w', 'grad_cv_ln_b': 'grad_w', 'grad_cv_w_pw2': 'grad_w', 'grad_cv_b_pw2': 'grad_w', 'grad_ffn_norm': 'grad_w', 'grad_ffn_w_gate': 'grad_w', 'grad_ffn_w_up': 'grad_w', 'grad_ffn_w_down': 'grad_w', 'grad_final_norm': 'grad_w', 'delta_ssm_norm': 'delta_w', 'delta_ssm_w_in': 'delta_w', 'delta_ssm_conv_w': 'delta_w', 'delta_ssm_conv_b': 'delta_w', 'delta_ssm_dt_bias': 'delta_w', 'delta_ssm_a_log': 'delta_w', 'delta_ssm_d': 'delta_w', 'delta_ssm_gate_norm': 'delta_w', 'delta_ssm_w_out': 'delta_w', 'delta_cv_norm': 'delta_w', 'delta_cv_w_pw1': 'delta_w', 'delta_cv_b_pw1': 'delta_w', 'delta_cv_dw_w': 'delta_w', 'delta_cv_dw_b': 'delta_w', 'delta_cv_ln_g': 'delta_w', 'delta_cv_ln_b': 'delta_w', 'delta_cv_w_pw2': 'delta_w', 'delta_cv_b_pw2': 'delta_w', 'delta_ffn_norm': 'delta_w', 'delta_ffn_w_gate': 'delta_w', 'delta_ffn_w_up': 'delta_w', 'delta_ffn_w_down': 'delta_w', 'delta_final_norm': 'delta_w', 'new_m_ssm_norm': 'new_m', 'new_m_ssm_w_in': 'new_m', 'new_m_ssm_conv_w': 'new_m', 'new_m_ssm_conv_b': 'new_m', 'new_m_ssm_dt_bias': 'new_m', 'new_m_ssm_a_log': 'new_m', 'new_m_ssm_d': 'new_m', 'new_m_ssm_gate_norm': 'new_m', 'new_m_ssm_w_out': 'new_m', 'new_m_cv_norm': 'new_m', 'new_m_cv_w_pw1': 'new_m', 'new_m_cv_b_pw1': 'new_m', 'new_m_cv_dw_w': 'new_m', 'new_m_cv_dw_b': 'new_m', 'new_m_cv_ln_g': 'new_m', 'new_m_cv_ln_b': 'new_m', 'new_m_cv_w_pw2': 'new_m', 'new_m_cv_b_pw2': 'new_m', 'new_m_ffn_norm': 'new_m', 'new_m_ffn_w_gate': 'new_m', 'new_m_ffn_w_up': 'new_m', 'new_m_ffn_w_down': 'new_m', 'new_m_final_norm': 'new_m', 'new_v_ssm_norm': 'new_v', 'new_v_ssm_w_in': 'new_v', 'new_v_ssm_conv_w': 'new_v', 'new_v_ssm_conv_b': 'new_v', 'new_v_ssm_dt_bias': 'new_v', 'new_v_ssm_a_log': 'new_v', 'new_v_ssm_d': 'new_v', 'new_v_ssm_gate_norm': 'new_v', 'new_v_ssm_w_out': 'new_v', 'new_v_cv_norm': 'new_v', 'new_v_cv_w_pw1': 'new_v', 'new_v_cv_b_pw1': 'new_v', 'new_v_cv_dw_w': 'new_v', 'new_v_cv_dw_b': 'new_v', 'new_v_cv_ln_g': 'new_v', 'new_v_cv_ln_b': 'new_v', 'new_v_cv_w_pw2': 'new_v', 'new_v_cv_b_pw2': 'new_v', 'new_v_ffn_norm': 'new_v', 'new_v_ffn_w_gate': 'new_v', 'new_v_ffn_w_up': 'new_v', 'new_v_ffn_w_down': 'new_v', 'new_v_final_norm': 'new_v'}


def _forward(args):
    return _fwd_reference(*[args[k] for k in FWD_PARAMS])


def _output_shape():
    out = _jax.eval_shape(lambda: _forward(_fwd_setup_inputs(0)))
    return out.shape, out.dtype

N_MICROBATCH = 1
ADAM_LR = 0.001
ADAM_B1 = 0.9
ADAM_B2 = 0.999
ADAM_EPS = 1e-08
ADAM_WD = 0.01
ADAM_STEP = 10
PER_EXAMPLE_BATCH_AXIS = {'x': 0, 'loss_target': 0}
SHARED_INPUTS = []
_WEIGHT_DTYPES = {'ssm_norm': _jnp.float32, 'ssm_w_in': _jnp.float32, 'ssm_conv_w': _jnp.float32, 'ssm_conv_b': _jnp.float32, 'ssm_dt_bias': _jnp.float32, 'ssm_a_log': _jnp.float32, 'ssm_d': _jnp.float32, 'ssm_gate_norm': _jnp.float32, 'ssm_w_out': _jnp.float32, 'cv_norm': _jnp.float32, 'cv_w_pw1': _jnp.float32, 'cv_b_pw1': _jnp.float32, 'cv_dw_w': _jnp.float32, 'cv_dw_b': _jnp.float32, 'cv_ln_g': _jnp.float32, 'cv_ln_b': _jnp.float32, 'cv_w_pw2': _jnp.float32, 'cv_b_pw2': _jnp.float32, 'ffn_norm': _jnp.float32, 'ffn_w_gate': _jnp.float32, 'ffn_w_up': _jnp.float32, 'ffn_w_down': _jnp.float32, 'final_norm': _jnp.float32}
MOMENT_SCALE = {'ssm_norm': 1.552894e-01, 'ssm_w_in': 6.828164e-02, 'ssm_conv_w': 6.318761e-02, 'ssm_conv_b': 8.906048e-02, 'ssm_dt_bias': 1.462807e-01, 'ssm_a_log': 1.543763e-01, 'ssm_d': 3.973929e-01, 'ssm_gate_norm': 7.512682e-02, 'ssm_w_out': 1.045498e-01, 'cv_norm': 5.588510e-02, 'cv_w_pw1': 3.903073e-02, 'cv_b_pw1': 4.898551e-02, 'cv_dw_w': 5.159156e-02, 'cv_dw_b': 1.169500e-01, 'cv_ln_g': 6.716518e-02, 'cv_ln_b': 7.083991e-02, 'cv_w_pw2': 5.396566e-02, 'cv_b_pw2': 1.325053e-01, 'ffn_norm': 6.951752e-02, 'ffn_w_gate': 3.044402e-02, 'ffn_w_up': 2.960386e-02, 'ffn_w_down': 4.900945e-02, 'final_norm': 1.602930e+01}


def _to_microbatches(a, axis):
    t = _jnp.moveaxis(a, axis, 0)
    t = t.reshape((N_MICROBATCH, t.shape[0] // N_MICROBATCH) + t.shape[1:])
    return _jnp.moveaxis(t, 1, axis + 1)


def setup_inputs(seed: int = 0) -> dict:
    inp = _fwd_setup_inputs(seed)
    key = _jax.random.fold_in(_jax.random.key(seed), 7919)
    shape, _ = _output_shape()
    out = dict(inp)
    out["loss_target"] = _jax.random.normal(_jax.random.fold_in(key, 0), shape, _jnp.float32)
    for i, name in enumerate(TWIN_WEIGHTS):
        w = inp[name].astype(_jnp.float32)
        if MOMENT_SCALE is None:
            s = _jnp.sqrt(_jnp.mean(_jnp.square(w)) + 1e-30)
        else:
            s = MOMENT_SCALE[name]
        km, kv = _jax.random.split(_jax.random.fold_in(key, i + 1))
        out[name] = w
        out["m_" + name] = s * _jax.random.normal(km, w.shape, _jnp.float32)
        out["v_" + name] = (s * s) * _jax.random.uniform(kv, w.shape, _jnp.float32, 0.5, 1.5)
    if N_MICROBATCH > 1:
        for name, axis in PER_EXAMPLE_BATCH_AXIS.items():
            out[name] = _to_microbatches(out[name], axis)
    return {'x': out['x'], 'ssm_norm': out['ssm_norm'], 'ssm_w_in': out['ssm_w_in'], 'ssm_conv_w': out['ssm_conv_w'], 'ssm_conv_b': out['ssm_conv_b'], 'ssm_dt_bias': out['ssm_dt_bias'], 'ssm_a_log': out['ssm_a_log'], 'ssm_d': out['ssm_d'], 'ssm_gate_norm': out['ssm_gate_norm'], 'ssm_w_out': out['ssm_w_out'], 'cv_norm': out['cv_norm'], 'cv_w_pw1': out['cv_w_pw1'], 'cv_b_pw1': out['cv_b_pw1'], 'cv_dw_w': out['cv_dw_w'], 'cv_dw_b': out['cv_dw_b'], 'cv_ln_g': out['cv_ln_g'], 'cv_ln_b': out['cv_ln_b'], 'cv_w_pw2': out['cv_w_pw2'], 'cv_b_pw2': out['cv_b_pw2'], 'ffn_norm': out['ffn_norm'], 'ffn_w_gate': out['ffn_w_gate'], 'ffn_w_up': out['ffn_w_up'], 'ffn_w_down': out['ffn_w_down'], 'final_norm': out['final_norm'], 'loss_target': out['loss_target'], 'm_ssm_norm': out['m_ssm_norm'], 'm_ssm_w_in': out['m_ssm_w_in'], 'm_ssm_conv_w': out['m_ssm_conv_w'], 'm_ssm_conv_b': out['m_ssm_conv_b'], 'm_ssm_dt_bias': out['m_ssm_dt_bias'], 'm_ssm_a_log': out['m_ssm_a_log'], 'm_ssm_d': out['m_ssm_d'], 'm_ssm_gate_norm': out['m_ssm_gate_norm'], 'm_ssm_w_out': out['m_ssm_w_out'], 'm_cv_norm': out['m_cv_norm'], 'm_cv_w_pw1': out['m_cv_w_pw1'], 'm_cv_b_pw1': out['m_cv_b_pw1'], 'm_cv_dw_w': out['m_cv_dw_w'], 'm_cv_dw_b': out['m_cv_dw_b'], 'm_cv_ln_g': out['m_cv_ln_g'], 'm_cv_ln_b': out['m_cv_ln_b'], 'm_cv_w_pw2': out['m_cv_w_pw2'], 'm_cv_b_pw2': out['m_cv_b_pw2'], 'm_ffn_norm': out['m_ffn_norm'], 'm_ffn_w_gate': out['m_ffn_w_gate'], 'm_ffn_w_up': out['m_ffn_w_up'], 'm_ffn_w_down': out['m_ffn_w_down'], 'm_final_norm': out['m_final_norm'], 'v_ssm_norm': out['v_ssm_norm'], 'v_ssm_w_in': out['v_ssm_w_in'], 'v_ssm_conv_w': out['v_ssm_conv_w'], 'v_ssm_conv_b': out['v_ssm_conv_b'], 'v_ssm_dt_bias': out['v_ssm_dt_bias'], 'v_ssm_a_log': out['v_ssm_a_log'], 'v_ssm_d': out['v_ssm_d'], 'v_ssm_gate_norm': out['v_ssm_gate_norm'], 'v_ssm_w_out': out['v_ssm_w_out'], 'v_cv_norm': out['v_cv_norm'], 'v_cv_w_pw1': out['v_cv_w_pw1'], 'v_cv_b_pw1': out['v_cv_b_pw1'], 'v_cv_dw_w': out['v_cv_dw_w'], 'v_cv_dw_b': out['v_cv_dw_b'], 'v_cv_ln_g': out['v_cv_ln_g'], 'v_cv_ln_b': out['v_cv_ln_b'], 'v_cv_w_pw2': out['v_cv_w_pw2'], 'v_cv_b_pw2': out['v_cv_b_pw2'], 'v_ffn_norm': out['v_ffn_norm'], 'v_ffn_w_gate': out['v_ffn_w_gate'], 'v_ffn_w_up': out['v_ffn_w_up'], 'v_ffn_w_down': out['v_ffn_w_down'], 'v_final_norm': out['v_final_norm']}


def _loss(weights, diff, rest, loss_target):
    with _jax.named_scope("forward"):
        args = {**rest, TWIN_DIFF_INPUT: diff, **{k: w.astype(_WEIGHT_DTYPES[k]) for k, w in weights.items()}}
        y = _forward(args)
    with _jax.named_scope("loss_head"):
        err = _jnp.square(y.astype(_jnp.float32) - loss_target)
        return 0.5 * _jnp.sum(_jnp.mean(err, axis=-1)) if err.ndim else 0.5 * err


def _adamw(w, g, m, v):
    m = ADAM_B1 * m + (1.0 - ADAM_B1) * g
    v = ADAM_B2 * v + (1.0 - ADAM_B2) * _jnp.square(g)
    m_hat = m / (1.0 - ADAM_B1 ** ADAM_STEP)
    v_hat = v / (1.0 - ADAM_B2 ** ADAM_STEP)
    delta = -ADAM_LR * (m_hat / (_jnp.sqrt(v_hat) + ADAM_EPS) + ADAM_WD * w)
    return delta, m, v


def reference(x, ssm_norm, ssm_w_in, ssm_conv_w, ssm_conv_b, ssm_dt_bias, ssm_a_log, ssm_d, ssm_gate_norm, ssm_w_out, cv_norm, cv_w_pw1, cv_b_pw1, cv_dw_w, cv_dw_b, cv_ln_g, cv_ln_b, cv_w_pw2, cv_b_pw2, ffn_norm, ffn_w_gate, ffn_w_up, ffn_w_down, final_norm, loss_target, m_ssm_norm, m_ssm_w_in, m_ssm_conv_w, m_ssm_conv_b, m_ssm_dt_bias, m_ssm_a_log, m_ssm_d, m_ssm_gate_norm, m_ssm_w_out, m_cv_norm, m_cv_w_pw1, m_cv_b_pw1, m_cv_dw_w, m_cv_dw_b, m_cv_ln_g, m_cv_ln_b, m_cv_w_pw2, m_cv_b_pw2, m_ffn_norm, m_ffn_w_gate, m_ffn_w_up, m_ffn_w_down, m_final_norm, v_ssm_norm, v_ssm_w_in, v_ssm_conv_w, v_ssm_conv_b, v_ssm_dt_bias, v_ssm_a_log, v_ssm_d, v_ssm_gate_norm, v_ssm_w_out, v_cv_norm, v_cv_w_pw1, v_cv_b_pw1, v_cv_dw_w, v_cv_dw_b, v_cv_ln_g, v_cv_ln_b, v_cv_w_pw2, v_cv_b_pw2, v_ffn_norm, v_ffn_w_gate, v_ffn_w_up, v_ffn_w_down, v_final_norm):
    given = dict(x=x, ssm_norm=ssm_norm, ssm_w_in=ssm_w_in, ssm_conv_w=ssm_conv_w, ssm_conv_b=ssm_conv_b, ssm_dt_bias=ssm_dt_bias, ssm_a_log=ssm_a_log, ssm_d=ssm_d, ssm_gate_norm=ssm_gate_norm, ssm_w_out=ssm_w_out, cv_norm=cv_norm, cv_w_pw1=cv_w_pw1, cv_b_pw1=cv_b_pw1, cv_dw_w=cv_dw_w, cv_dw_b=cv_dw_b, cv_ln_g=cv_ln_g, cv_ln_b=cv_ln_b, cv_w_pw2=cv_w_pw2, cv_b_pw2=cv_b_pw2, ffn_norm=ffn_norm, ffn_w_gate=ffn_w_gate, ffn_w_up=ffn_w_up, ffn_w_down=ffn_w_down, final_norm=final_norm, loss_target=loss_target, m_ssm_norm=m_ssm_norm, m_ssm_w_in=m_ssm_w_in, m_ssm_conv_w=m_ssm_conv_w, m_ssm_conv_b=m_ssm_conv_b, m_ssm_dt_bias=m_ssm_dt_bias, m_ssm_a_log=m_ssm_a_log, m_ssm_d=m_ssm_d, m_ssm_gate_norm=m_ssm_gate_norm, m_ssm_w_out=m_ssm_w_out, m_cv_norm=m_cv_norm, m_cv_w_pw1=m_cv_w_pw1, m_cv_b_pw1=m_cv_b_pw1, m_cv_dw_w=m_cv_dw_w, m_cv_dw_b=m_cv_dw_b, m_cv_ln_g=m_cv_ln_g, m_cv_ln_b=m_cv_ln_b, m_cv_w_pw2=m_cv_w_pw2, m_cv_b_pw2=m_cv_b_pw2, m_ffn_norm=m_ffn_norm, m_ffn_w_gate=m_ffn_w_gate, m_ffn_w_up=m_ffn_w_up, m_ffn_w_down=m_ffn_w_down, m_final_norm=m_final_norm, v_ssm_norm=v_ssm_norm, v_ssm_w_in=v_ssm_w_in, v_ssm_conv_w=v_ssm_conv_w, v_ssm_conv_b=v_ssm_conv_b, v_ssm_dt_bias=v_ssm_dt_bias, v_ssm_a_log=v_ssm_a_log, v_ssm_d=v_ssm_d, v_ssm_gate_norm=v_ssm_gate_norm, v_ssm_w_out=v_ssm_w_out, v_cv_norm=v_cv_norm, v_cv_w_pw1=v_cv_w_pw1, v_cv_b_pw1=v_cv_b_pw1, v_cv_dw_w=v_cv_dw_w, v_cv_dw_b=v_cv_dw_b, v_cv_ln_g=v_cv_ln_g, v_cv_ln_b=v_cv_ln_b, v_cv_w_pw2=v_cv_w_pw2, v_cv_b_pw2=v_cv_b_pw2, v_ffn_norm=v_ffn_norm, v_ffn_w_gate=v_ffn_w_gate, v_ffn_w_up=v_ffn_w_up, v_ffn_w_down=v_ffn_w_down, v_final_norm=v_final_norm)
    weights = {n: given[n] for n in TWIN_WEIGHTS}
    shared = {n: given[n] for n in SHARED_INPUTS}
    per_example = {n: given[n] for n in ['x']}
    grad_fn = _jax.value_and_grad(_loss, argnums=(0, 1))

    def one_microbatch(ex, loss_target):
        ex = dict(ex)
        diff = ex.pop(TWIN_DIFF_INPUT)
        return grad_fn(weights, diff, {**shared, **ex}, loss_target)

    if N_MICROBATCH == 1:
        loss, (grad_w, grad_x) = one_microbatch(per_example, given["loss_target"])
    else:
        def body(carry, xs):
            loss_sum, grad_sum = carry
            l_k, (gw_k, gx_k) = one_microbatch(xs[0], xs[1])
            with _jax.named_scope("update"):
                return (loss_sum + l_k, _jax.tree.map(_jnp.add, grad_sum, gw_k)), gx_k

        init = (_jnp.zeros((), _jnp.float32), _jax.tree.map(_jnp.zeros_like, weights))
        (loss, grad_w), grad_x = _jax.lax.scan(body, init, (per_example, given["loss_target"]))
    with _jax.named_scope("update"):
        delta_w, new_m, new_v = {}, {}, {}
        for n in TWIN_WEIGHTS:
            delta_w[n], new_m[n], new_v[n] = _adamw(weights[n], grad_w[n], given["m_" + n], given["v_" + n])
    return (loss, grad_x, *[grad_w[n] for n in TWIN_WEIGHTS], *[delta_w[n] for n in TWIN_WEIGHTS],
            *[new_m[n] for n in TWIN_WEIGHTS], *[new_v[n] for n in TWIN_WEIGHTS])
```

```python
import functools

import jax
import jax.numpy as jnp
from jax import lax
from jax.experimental import pallas as pl
from jax.experimental.pallas import tpu as pltpu

F32 = jnp.float32
BF16 = jnp.bfloat16
HI = lax.Precision.HIGHEST
MESH = pl.DeviceIdType.MESH

D = 1024
DI = 2048
HD = 64
NH = 32
NG = 4
GW = DI // NG
NS = 128
KC = 4
CD = DI + 2 * NG * NS
Q = 128
DFF = 2816
CK = 31
EPS = 1e-5
NHP = 128

LR, B1, B2, AEPS, WD, STEP = 0.001, 0.9, 0.999, 1e-08, 0.01, 10

VMEM_LIMIT = 56 * 1024 * 1024
MM_VMEM_BUDGET = 36 * 1024 * 1024


def _pc(body, *, name, grid, in_specs, out_specs, out_shape, scratch=()):
    return pl.pallas_call(
        body, name=name, grid=grid, in_specs=in_specs, out_specs=out_specs, out_shape=out_shape,
        scratch_shapes=list(scratch),
        compiler_params=pltpu.CompilerParams(dimension_semantics=("arbitrary",) * len(grid),
                                             vmem_limit_bytes=VMEM_LIMIT))


def _sds(shape, dtype):
    return jax.ShapeDtypeStruct(tuple(shape), dtype)


def _row(tm, c):
    return pl.BlockSpec((tm, c), lambda i: (i, 0))


def _const(shape):
    return pl.BlockSpec(tuple(shape), lambda i: (0,) * len(shape))


def _tile(n, cap):
    if n <= cap:
        return n
    best = 128
    for t in range(128, cap + 1, 128):
        if n % t == 0:
            best = t
    return best


def _sigmoid(x):
    return 1.0 / (1.0 + jnp.exp(-x))


def _dsilu(x):
    s = _sigmoid(x)
    return s * (1.0 + x * (1.0 - s))


def _mm(a, b, mode, *, name, out_dtype=F32, bias=None, res=None, tm=1024, tn=1024):
    if mode == "nn":
        (m, k), (k2, n) = a.shape, b.shape
    elif mode == "nt":
        (m, k), (n, k2) = a.shape, b.shape
    else:
        (k, m), (k2, n) = a.shape, b.shape
    assert k == k2, (a.shape, b.shape, mode)
    tm = _tile(m, tm if k <= 2048 else 512)
    so = jnp.dtype(out_dtype).itemsize + (4 if res is not None else 0)
    fits = [c for c in range(128, min(n, 2048) + 1, 128) if n % c == 0 and
            2 * (tm * k * a.dtype.itemsize + c * k * b.dtype.itemsize + tm * c * so) <= MM_VMEM_BUDGET]
    tn = n if n <= 128 else max(fits)
    dn = {"nn": (((1,), (0,)), ((), ())), "nt": (((1,), (1,)), ((), ())), "tn": (((0,), (0,)), ((), ()))}[mode]
    nb, nr = bias is not None, res is not None

    def body(*refs):
        acc = lax.dot_general(refs[0][...].astype(BF16), refs[1][...].astype(BF16), dn, preferred_element_type=F32)
        if nb:
            acc = acc + refs[2][...]
        if nr:
            acc = acc + refs[2 + nb][...]
        refs[-1][...] = acc.astype(out_dtype)

    a_spec = pl.BlockSpec((k, tm), lambda i, j: (0, i)) if mode == "tn" else pl.BlockSpec((tm, k), lambda i, j: (i, 0))
    b_spec = pl.BlockSpec((tn, k), lambda i, j: (j, 0)) if mode == "nt" else pl.BlockSpec((k, tn), lambda i, j: (0, j))
    ins, specs = [a, b], [a_spec, b_spec]
    if nb:
        ins.append(bias)
        specs.append(pl.BlockSpec((1, tn), lambda i, j: (0, j)))
    if nr:
        ins.append(res)
        specs.append(pl.BlockSpec((tm, tn), lambda i, j: (i, j)))
    return _pc(body, name=name, grid=(m // tm, n // tn), in_specs=specs,
               out_specs=pl.BlockSpec((tm, tn), lambda i, j: (i, j)), out_shape=_sds((m, n), out_dtype))(*ins)


def _rms_fwd(h, g, *, name):
    t = h.shape[0]
    tm = min(t, 512)

    def body(h_ref, g_ref, u_ref):
        x = h_ref[...]
        r = lax.rsqrt(jnp.mean(x * x, axis=-1, keepdims=True) + EPS)
        u_ref[...] = (x * r * g_ref[...]).astype(BF16)

    return _pc(body, name=name, grid=(t // tm,), in_specs=[_row(tm, D), _const((1, D))], out_specs=_row(tm, D),
               out_shape=_sds((t, D), BF16))(h, g.reshape(1, D))


def _rms_bwd(h, g, du, dres, *, name):
    t = h.shape[0]
    tm = min(t, 512)

    def body(h_ref, g_ref, du_ref, dres_ref, dh_ref, dg_ref):
        x = h_ref[...]
        r = lax.rsqrt(jnp.mean(x * x, axis=-1, keepdims=True) + EPS)
        dy = du_ref[...]
        dyg = dy * g_ref[...]
        dot = jnp.mean(dyg * x, axis=-1, keepdims=True)
        dh_ref[...] = dres_ref[...] + r * dyg - x * (r * r * r * dot)

        @pl.when(pl.program_id(0) == 0)
        def _():
            dg_ref[...] = jnp.zeros_like(dg_ref)

        dg_ref[...] += jnp.sum(dy * x * r, axis=0, keepdims=True)

    dh, dg = _pc(body, name=name, grid=(t // tm,),
                 in_specs=[_row(tm, D), _const((1, D)), _row(tm, D), _row(tm, D)],
                 out_specs=[_row(tm, D), _const((1, D))],
                 out_shape=[_sds((t, D), F32), _sds((1, D), F32)])(h, g.reshape(1, D), du, dres)
    return dh, dg.reshape(D)


def _final_loss(h, g, target, *, name):
    t = h.shape[0]
    tm = min(t, 512)

    def body(h_ref, g_ref, t_ref, dh_ref, dg_ref, l_ref):
        x = h_ref[...]
        gg = g_ref[...]
        r = lax.rsqrt(jnp.mean(x * x, axis=-1, keepdims=True) + EPS)
        err = x * r * gg - t_ref[...]
        dy = err * (1.0 / D)
        dyg = dy * gg
        dot = jnp.mean(dyg * x, axis=-1, keepdims=True)
        dh_ref[...] = r * dyg - x * (r * r * r * dot)

        @pl.when(pl.program_id(0) == 0)
        def _():
            dg_ref[...] = jnp.zeros_like(dg_ref)
            l_ref[...] = jnp.zeros_like(l_ref)

        dg_ref[...] += jnp.sum(dy * x * r, axis=0, keepdims=True)
        l_ref[...] += jnp.sum(err * err, axis=0, keepdims=True)

    dh, dg, l = _pc(body, name=name, grid=(t // tm,),
                    in_specs=[_row(tm, D), _const((1, D)), _row(tm, D)],
                    out_specs=[_row(tm, D), _const((1, D)), _const((1, D))],
                    out_shape=[_sds((t, D), F32), _sds((1, D), F32), _sds((1, D), F32)])(h, g.reshape(1, D), target)
    return l, dh, dg.reshape(D)


def _swiglu_fwd(gu, *, name):
    t = gu.shape[0]
    tm = min(t, 256)

    def body(gu_ref, a_ref):
        gate = gu_ref[:, :DFF]
        up = gu_ref[:, DFF:]
        a_ref[...] = (gate * _sigmoid(gate) * up).astype(BF16)

    return _pc(body, name=name, grid=(t // tm,), in_specs=[_row(tm, 2 * DFF)], out_specs=_row(tm, DFF),
               out_shape=_sds((t, DFF), BF16))(gu)


def _swiglu_bwd(gu, da, *, name):
    t = gu.shape[0]
    tm = min(t, 256)

    def body(gu_ref, da_ref, d_ref):
        gate = gu_ref[:, :DFF]
        up = gu_ref[:, DFF:]
        dav = da_ref[...]
        d_ref[:, :DFF] = (dav * up * _dsilu(gate)).astype(BF16)
        d_ref[:, DFF:] = (dav * gate * _sigmoid(gate)).astype(BF16)

    return _pc(body, name=name, grid=(t // tm,), in_specs=[_row(tm, 2 * DFF), _row(tm, DFF)],
               out_specs=_row(tm, 2 * DFF), out_shape=_sds((t, 2 * DFF), BF16))(gu, da)


def _halo_before(tm, halo, tc):
    return pl.BlockSpec((halo, tc), lambda j, i: (jnp.maximum(i * (tm // halo) - 1, 0), j))


def _halo_after(tm, halo, tc, t):
    return pl.BlockSpec((halo, tc), lambda j, i: (jnp.minimum((i + 1) * (tm // halo), t // halo - 1), j))


def _conv4_fwd(xp, w, b, *, name):
    t = xp.shape[0]
    tm, tc, halo = min(t, 256), 512, 8

    def body(x_ref, h_ref, w_ref, b_ref, o_ref, pad):
        i = pl.program_id(1)
        pad[pl.ds(0, halo), :] = jnp.where(i == 0, 0.0, h_ref[...])
        pad[pl.ds(halo, tm), :] = x_ref[...]
        acc = jnp.zeros((tm, tc), F32) + b_ref[...]
        for k in range(KC):
            acc = acc + pad[pl.ds(halo - (KC - 1) + k, tm), :] * w_ref[pl.ds(k, 1), :]
        o_ref[...] = acc * _sigmoid(acc)

    tile = pl.BlockSpec((tm, tc), lambda j, i: (i, j))
    return _pc(body, name=name, grid=(CD // tc, t // tm),
               in_specs=[tile, _halo_before(tm, halo, tc), pl.BlockSpec((KC, tc), lambda j, i: (0, j)),
                         pl.BlockSpec((1, tc), lambda j, i: (0, j))],
               out_specs=tile, out_shape=_sds((t, CD), F32), scratch=[pltpu.VMEM((halo + tm, tc), F32)])(xp, xp, w, b.reshape(1, CD))


def _conv4_bwd_act(xp, w, b, dout, *, name):
    t = xp.shape[0]
    tm, tc, halo = min(t, 256), 512, 8

    def body(x_ref, h_ref, w_ref, b_ref, do_ref, dp_ref, dw_ref, db_ref, pad):
        i = pl.program_id(1)
        pad[pl.ds(0, halo), :] = jnp.where(i == 0, 0.0, h_ref[...])
        pad[pl.ds(halo, tm), :] = x_ref[...]
        acc = jnp.zeros((tm, tc), F32) + b_ref[...]
        for k in range(KC):
            acc = acc + pad[pl.ds(halo - (KC - 1) + k, tm), :] * w_ref[pl.ds(k, 1), :]
        dpre = do_ref[...] * _dsilu(acc)
        dp_ref[...] = dpre

        @pl.when(i == 0)
        def _():
            dw_ref[...] = jnp.zeros_like(dw_ref)
            db_ref[...] = jnp.zeros_like(db_ref)

        db_ref[...] += jnp.sum(dpre, axis=0, keepdims=True)
        for k in range(KC):
            dw_ref[pl.ds(k, 1), :] += jnp.sum(dpre * pad[pl.ds(halo - (KC - 1) + k, tm), :], axis=0, keepdims=True)

    tile = pl.BlockSpec((tm, tc), lambda j, i: (i, j))
    wspec = pl.BlockSpec((KC, tc), lambda j, i: (0, j))
    bspec = pl.BlockSpec((1, tc), lambda j, i: (0, j))
    dp, dw, db = _pc(body, name=name, grid=(CD // tc, t // tm),
                     in_specs=[tile, _halo_before(tm, halo, tc), wspec, bspec, tile],
                     out_specs=[tile, wspec, bspec],
                     out_shape=[_sds((t, CD), F32), _sds((KC, CD), F32), _sds((1, CD), F32)],
                     scratch=[pltpu.VMEM((halo + tm, tc), F32)])(xp, xp, w, b.reshape(1, CD), dout)
    return dp, dw, db.reshape(CD)


def _conv_bwd_x(dy, w, kk, *, name, out_dtype):
    t, c = dy.shape
    halo = 8 if kk <= 8 else 32
    tm, tc = min(t, 256), min(c, 512)

    def body(y_ref, h_ref, w_ref, o_ref, pad):
        i = pl.program_id(1)
        pad[pl.ds(0, tm), :] = y_ref[...]
        pad[pl.ds(tm, halo), :] = jnp.where(i == t // tm - 1, 0.0, h_ref[...])
        acc = jnp.zeros((tm, tc), F32)
        for k in range(kk):
            acc = acc + pad[pl.ds(kk - 1 - k, tm), :] * w_ref[pl.ds(k, 1), :]
        o_ref[...] = acc.astype(out_dtype)

    tile = pl.BlockSpec((tm, tc), lambda j, i: (i, j))
    return _pc(body, name=name, grid=(c // tc, t // tm),
               in_specs=[tile, _halo_after(tm, halo, tc, t), pl.BlockSpec((kk, tc), lambda j, i: (0, j))],
               out_specs=tile, out_shape=_sds((t, c), out_dtype), scratch=[pltpu.VMEM((tm + halo, tc), F32)])(dy, dy, w)


def _glu_fwd(p, *, name):
    t = p.shape[0]
    tm = min(t, 512)

    def body(p_ref, o_ref):
        o_ref[...] = p_ref[:, :D] * _sigmoid(p_ref[:, D:])

    return _pc(body, name=name, grid=(t // tm,), in_specs=[_row(tm, 2 * D)], out_specs=_row(tm, D),
               out_shape=_sds((t, D), F32))(p)


def _glu_bwd(p, dg, *, name):
    t = p.shape[0]
    tm = min(t, 512)

    def body(p_ref, dg_ref, dp_ref, db_ref):
        a = p_ref[:, :D]
        s = _sigmoid(p_ref[:, D:])
        d = dg_ref[...]
        da = d * s
        dgate = d * a * s * (1.0 - s)
        dp_ref[:, :D] = da.astype(BF16)
        dp_ref[:, D:] = dgate.astype(BF16)

        @pl.when(pl.program_id(0) == 0)
        def _():
            db_ref[...] = jnp.zeros_like(db_ref)

        db_ref[:, :D] += jnp.sum(da, axis=0, keepdims=True)
        db_ref[:, D:] += jnp.sum(dgate, axis=0, keepdims=True)

    dp, db = _pc(body, name=name, grid=(t // tm,), in_specs=[_row(tm, 2 * D), _row(tm, D)],
                 out_specs=[_row(tm, 2 * D), _const((1, 2 * D))],
                 out_shape=[_sds((t, 2 * D), BF16), _sds((1, 2 * D), F32)])(p, dg)
    return dp, db.reshape(2 * D)


def _dwconv_ln_fwd(g, w, b, lg, lb, *, name):
    t = g.shape[0]
    tm, halo = min(t, 256), 32

    def body(x_ref, h_ref, w_ref, b_ref, lg_ref, lb_ref, c_ref, s_ref, pad):
        i = pl.program_id(0)
        pad[pl.ds(0, halo), :] = jnp.where(i == 0, 0.0, h_ref[...])
        pad[pl.ds(halo, tm), :] = x_ref[...]
        acc = jnp.zeros((tm, D), F32) + b_ref[...]
        for k in range(CK):
            acc = acc + pad[pl.ds(halo - (CK - 1) + k, tm), :] * w_ref[pl.ds(k, 1), :]
        c_ref[...] = acc
        mu = jnp.mean(acc, axis=-1, keepdims=True)
        xc = acc - mu
        r = lax.rsqrt(jnp.mean(xc * xc, axis=-1, keepdims=True) + EPS)
        n = xc * r * lg_ref[...] + lb_ref[...]
        s_ref[...] = (n * _sigmoid(n)).astype(BF16)

    c, s = _pc(body, name=name, grid=(t // tm,),
               in_specs=[_row(tm, D), pl.BlockSpec((halo, D), lambda i: (jnp.maximum(i * (tm // halo) - 1, 0), 0)),
                         _const((CK, D)), _const((1, D)), _const((1, D)), _const((1, D))],
               out_specs=[_row(tm, D), _row(tm, D)], out_shape=[_sds((t, D), F32), _sds((t, D), BF16)],
               scratch=[pltpu.VMEM((halo + tm, D), F32)])(g, g, w, b.reshape(1, D), lg.reshape(1, D), lb.reshape(1, D))
    return c, s


def _ln_silu_bwd(c, lg, lb, ds, *, name):
    t = c.shape[0]
    tm = min(t, 512)

    def body(c_ref, lg_ref, lb_ref, ds_ref, dc_ref, dlg_ref, dlb_ref):
        x = c_ref[...]
        mu = jnp.mean(x, axis=-1, keepdims=True)
        xc = x - mu
        r = lax.rsqrt(jnp.mean(xc * xc, axis=-1, keepdims=True) + EPS)
        xh = xc * r
        n = xh * lg_ref[...] + lb_ref[...]
        dn = ds_ref[...] * _dsilu(n)
        dxh = dn * lg_ref[...]
        m1 = jnp.mean(dxh, axis=-1, keepdims=True)
        m2 = jnp.mean(dxh * xh, axis=-1, keepdims=True)
        dc_ref[...] = r * (dxh - m1 - xh * m2)

        @pl.when(pl.program_id(0) == 0)
        def _():
            dlg_ref[...] = jnp.zeros_like(dlg_ref)
            dlb_ref[...] = jnp.zeros_like(dlb_ref)

        dlg_ref[...] += jnp.sum(dn * xh, axis=0, keepdims=True)
        dlb_ref[...] += jnp.sum(dn, axis=0, keepdims=True)

    dc, dlg, dlb = _pc(body, name=name, grid=(t // tm,),
                       in_specs=[_row(tm, D), _const((1, D)), _const((1, D)), _row(tm, D)],
                       out_specs=[_row(tm, D), _const((1, D)), _const((1, D))],
                       out_shape=[_sds((t, D), F32), _sds((1, D), F32), _sds((1, D), F32)])(
                           c, lg.reshape(1, D), lb.reshape(1, D), ds)
    return dc, dlg.reshape(D), dlb.reshape(D)


def _dwconv_bwd_w(g, dc, *, name):
    t = g.shape[0]
    tm, tc, halo = min(t, 256), 512, 32

    def body(x_ref, h_ref, dc_ref, dw_ref, db_ref, pad):
        i = pl.program_id(1)
        pad[pl.ds(0, halo), :] = jnp.where(i == 0, 0.0, h_ref[...])
        pad[pl.ds(halo, tm), :] = x_ref[...]
        d = dc_ref[...]

        @pl.when(i == 0)
        def _():
            dw_ref[...] = jnp.zeros_like(dw_ref)
            db_ref[...] = jnp.zeros_like(db_ref)

        db_ref[...] += jnp.sum(d, axis=0, keepdims=True)
        for k in range(CK):
            dw_ref[pl.ds(k, 1), :] += jnp.sum(d * pad[pl.ds(halo - (CK - 1) + k, tm), :], axis=0, keepdims=True)

    tile = pl.BlockSpec((tm, tc), lambda j, i: (i, j))
    dw, db = _pc(body, name=name, grid=(D // tc, t // tm),
                 in_specs=[tile, _halo_before(tm, halo, tc), tile],
                 out_specs=[pl.BlockSpec((CK, tc), lambda j, i: (0, j)), pl.BlockSpec((1, tc), lambda j, i: (0, j))],
                 out_shape=[_sds((CK, D), F32), _sds((1, D), F32)],
                 scratch=[pltpu.VMEM((halo + tm, tc), F32)])(g, g, dc)
    return dw, db.reshape(D)


def _colsum(x, *, name):
    t, c = x.shape
    tm = min(t, 512)

    def body(x_ref, o_ref):
        @pl.when(pl.program_id(0) == 0)
        def _():
            o_ref[...] = jnp.zeros_like(o_ref)

        o_ref[...] += jnp.sum(x_ref[...], axis=0, keepdims=True)

    return _pc(body, name=name, grid=(t // tm,), in_specs=[_row(tm, c)], out_specs=_const((1, c)),
               out_shape=_sds((1, c), F32))(x).reshape(c)


def _head_expand(n_lanes, first_head=0):
    h = lax.broadcasted_iota(jnp.int32, (NHP, n_lanes), 0)
    j = lax.broadcasted_iota(jnp.int32, (NHP, n_lanes), 1)
    return (h == first_head + j // HD).astype(F32)


def _ssd_pre(dtraw, dtb, alog, *, name):
    t = dtraw.shape[0]
    tm = min(t, 256)

    def body(r_ref, b_ref, al_ref, dtx_ref, da_ref):
        v = r_ref[...] + b_ref[...]
        dt = jnp.maximum(v, 0.0) + jnp.log(1.0 + jnp.exp(-jnp.abs(v)))
        dtx_ref[...] = jnp.dot(dt, _head_expand(DI), preferred_element_type=F32, precision=HI)
        da_ref[...] = dt * (-jnp.exp(al_ref[...]))

    return _pc(body, name=name, grid=(t // tm,), in_specs=[_row(tm, NHP), _const((1, NHP)), _const((1, NHP))],
               out_specs=[_row(tm, DI), _row(tm, NHP)], out_shape=[_sds((t, DI), F32), _sds((t, NHP), F32)])(dtraw, dtb, alog)


def _ssd_post(ddtx, dtraw, dtb, *, name):
    t = dtraw.shape[0]
    tm = min(t, 256)

    def body(g_ref, r_ref, b_ref, o_ref, db_ref):
        h = lax.broadcasted_iota(jnp.int32, (DI, NHP), 1)
        j = lax.broadcasted_iota(jnp.int32, (DI, NHP), 0)
        red = (h == j // HD).astype(F32)
        ddt = jnp.dot(g_ref[...], red, preferred_element_type=F32, precision=HI)
        draw = ddt * _sigmoid(r_ref[...] + b_ref[...])
        o_ref[...] = draw.astype(BF16)

        @pl.when(pl.program_id(0) == 0)
        def _():
            db_ref[...] = jnp.zeros_like(db_ref)

        db_ref[...] += jnp.sum(draw, axis=0, keepdims=True)

    return _pc(body, name=name, grid=(t // tm,), in_specs=[_row(tm, DI), _row(tm, NHP), _const((1, NHP))],
               out_specs=[_row(tm, NHP), _const((1, NHP))], out_shape=[_sds((t, NHP), BF16), _sds((1, NHP), F32)])(ddtx, dtraw, dtb)


def _chunk_common(g, da):
    ri = lax.broadcasted_iota(jnp.int32, (Q, Q), 0)
    ci = lax.broadcasted_iota(jnp.int32, (Q, Q), 1)
    tri = ri >= ci
    acs_col = jnp.dot(tri.astype(F32), da, preferred_element_type=F32, precision=HI)
    acs_x = jnp.dot(acs_col, _head_expand(GW, g * (NH // NG)), preferred_element_type=F32, precision=HI)
    last = lax.broadcasted_iota(jnp.int32, (Q, GW), 0) == Q - 1
    atot_x = jnp.sum(jnp.where(last, acs_x, 0.0), axis=0, keepdims=True)
    return tri, acs_col, acs_x, atot_x


def _decay_matrix(acs_col, head, tri):
    hk = lax.broadcasted_iota(jnp.int32, (NHP, Q), 0)
    colsel = (hk == head).astype(F32)
    colb = jnp.dot(acs_col, colsel, preferred_element_type=F32, precision=HI)
    hq = lax.broadcasted_iota(jnp.int32, (Q, NHP), 1)
    rowsel = (hq == head).astype(F32)
    rowb = lax.dot_general(rowsel, acs_col, (((1,), (1,)), ((), ())), preferred_element_type=F32, precision=HI)
    return jnp.exp(jnp.where(tri, colb - rowb, -1e30))


def _ssd_fwd(xbc, dtx, da, a_x, d_x, *, name):
    t = xbc.shape[0]
    nc = t // Q
    hpg = NH // NG

    def body(x_ref, b_ref, c_ref, dtx_ref, da_ref, a_ref, d_ref, y_ref, sin_ref, state):
        g = pl.program_id(0)

        @pl.when(pl.program_id(1) == 0)
        def _():
            state[...] = jnp.zeros_like(state)

        xs = x_ref[...]
        tri, acs_col, acs_x, atot_x = _chunk_common(g, da_ref[...])
        xd = xs * dtx_ref[...]
        xb = xd.astype(BF16)
        bb = b_ref[...]
        cb16 = c_ref[...].astype(BF16)
        cbm = lax.dot_general(cb16, bb.astype(BF16), (((1,), (1,)), ((), ())), preferred_element_type=F32)
        s_in = state[...]
        sin_ref[0] = s_in
        y = jnp.dot(cb16, s_in.astype(BF16), preferred_element_type=F32) * jnp.exp(acs_x) + xs * d_ref[...]
        lane = lax.broadcasted_iota(jnp.int32, (Q, 2 * HD), 1)
        for p in range(hpg // 2):
            xp = xb[:, 2 * HD * p:2 * HD * (p + 1)]
            yp = jnp.zeros((Q, 2 * HD), F32)
            for hh in range(2):
                lm = _decay_matrix(acs_col, g * hpg + 2 * p + hh, tri)
                m = (cbm * lm).astype(BF16)
                xm = jnp.where((lane >= HD) if hh == 1 else (lane < HD), xp, jnp.zeros_like(xp))
                yp = yp + jnp.dot(m, xm, preferred_element_type=F32)
            y_ref[:, 2 * HD * p:2 * HD * (p + 1)] = y[:, 2 * HD * p:2 * HD * (p + 1)] + yp
        dec = jnp.exp(atot_x - acs_x)
        s_new = jnp.dot(bb.T.astype(BF16), (xd * dec).astype(BF16), preferred_element_type=F32)
        state[...] = jnp.exp(atot_x) * s_in + s_new

    grp = lambda g, c: (c, g)
    return _pc(body, name=name, grid=(NG, nc),
               in_specs=[pl.BlockSpec((Q, GW), grp),
                         pl.BlockSpec((Q, NS), lambda g, c: (c, DI // NS + g)),
                         pl.BlockSpec((Q, NS), lambda g, c: (c, DI // NS + NG + g)),
                         pl.BlockSpec((Q, GW), grp),
                         pl.BlockSpec((Q, NHP), lambda g, c: (c, 0)),
                         pl.BlockSpec((1, GW), lambda g, c: (0, g)),
                         pl.BlockSpec((1, GW), lambda g, c: (0, g))],
               out_specs=[pl.BlockSpec((Q, GW), grp), pl.BlockSpec((1, NS, GW), lambda g, c: (c, 0, g))],
               out_shape=[_sds((t, DI), F32), _sds((nc, NS, DI), F32)],
               scratch=[pltpu.VMEM((NS, GW), F32)])(xbc, xbc, xbc, dtx, da, a_x, d_x)


def _ssd_bwd(xbc, dtx, da, a_x, d_x, sin, dy, *, name):
    t = xbc.shape[0]
    nc = t // Q
    hpg = NH // NG

    def body(x_ref, b_ref, c_ref, dtx_ref, da_ref, a_ref, d_ref, sin_ref, dy_ref,
             dx_ref, db_ref, dc_ref, ddt_ref, dax_ref, ddx_ref, dstate):
        g = pl.program_id(0)

        @pl.when(pl.program_id(1) == 0)
        def _():
            dstate[...] = jnp.zeros_like(dstate)
            dax_ref[...] = jnp.zeros_like(dax_ref)
            ddx_ref[...] = jnp.zeros_like(ddx_ref)

        xs = x_ref[...]
        dtxv = dtx_ref[...]
        a_xv = a_ref[...]
        gy = dy_ref[...]
        tri, acs_col, acs_x, atot_x = _chunk_common(g, da_ref[...])
        xd = xs * dtxv
        xb = xd.astype(BF16)
        gb = gy.astype(BF16)
        bb = b_ref[...]
        cc = c_ref[...]
        bb16, cc16 = bb.astype(BF16), cc.astype(BF16)
        cbm = lax.dot_general(cc16, bb16, (((1,), (1,)), ((), ())), preferred_element_type=F32)
        s_in = sin_ref[0]
        s_in16 = s_in.astype(BF16)
        ds_next = dstate[...]
        ds16 = ds_next.astype(BF16)
        ecs = jnp.exp(acs_x)
        dec = jnp.exp(atot_x - acs_x)
        etot = jnp.exp(atot_x)

        ddx_ref[...] += jnp.sum(gy * xs, axis=0, keepdims=True)
        z = jnp.dot(cc16, s_in16, preferred_element_type=F32)
        dz16 = (gy * ecs).astype(BF16)
        dacs = gy * z * ecs
        dcm = lax.dot_general(dz16, s_in16, (((1,), (1,)), ((), ())), preferred_element_type=F32)
        ds_here = jnp.dot(cc.T.astype(BF16), dz16, preferred_element_type=F32)
        u = jnp.dot(bb16, ds16, preferred_element_type=F32)
        dxd = u * dec
        w_dec = u * xd * dec
        dacs = dacs - w_dec
        datot = jnp.sum(w_dec, axis=0, keepdims=True) + jnp.sum(etot * ds_next * s_in, axis=0, keepdims=True)
        dbm = lax.dot_general((xd * dec).astype(BF16), ds16, (((1,), (1,)), ((), ())), preferred_element_type=F32)
        dstate[...] = etot * ds_next + ds_here

        lane = lax.broadcasted_iota(jnp.int32, (Q, 2 * HD), 1)
        lane_x = lax.broadcasted_iota(jnp.int32, (1, GW), 1)
        dcb = jnp.zeros((Q, Q), F32)
        dx_pairs = []
        for p in range(hpg // 2):
            sl = slice(2 * HD * p, 2 * HD * (p + 1))
            xp = xb[:, sl]
            gp = gb[:, sl]
            dxp = jnp.zeros((Q, 2 * HD), F32)
            for hh in range(2):
                r = 2 * p + hh
                lm = _decay_matrix(acs_col, g * hpg + r, tri)
                mm = cbm * lm
                half = (lane >= HD) if hh == 1 else (lane < HD)
                xm = jnp.where(half, xp, jnp.zeros_like(xp))
                gm = jnp.where(half, gp, jnp.zeros_like(gp))
                mt16 = mm.T.astype(BF16)
                dxp = dxp + jnp.dot(mt16, gm, preferred_element_type=F32)
                dm = lax.dot_general(gm, xp, (((1,), (1,)), ((), ())), preferred_element_type=F32)
                dmt = lax.dot_general(xm, gp, (((1,), (1,)), ((), ())), preferred_element_type=F32)
                dcb = dcb + dm * lm
                wdiff = jnp.sum(dm * mm - dmt * mm.T, axis=1, keepdims=True)
                dacs = dacs + wdiff * (lane_x == HD * r).astype(F32)
            dx_pairs.append(dxp)
        dxd = dxd + jnp.concatenate(dx_pairs, axis=1)
        dcm = dcm + jnp.dot(dcb.astype(BF16), bb16, preferred_element_type=F32)
        dbm = dbm + jnp.dot(dcb.T.astype(BF16), cc16, preferred_element_type=F32)
        db_ref[...] = dbm
        dc_ref[...] = dcm

        last = lax.broadcasted_iota(jnp.int32, (Q, GW), 0) == Q - 1
        dacs = dacs + jnp.where(last, datot, 0.0)
        ri = lax.broadcasted_iota(jnp.int32, (Q, Q), 0)
        ci = lax.broadcasted_iota(jnp.int32, (Q, Q), 1)
        dda = jnp.dot((ri <= ci).astype(F32), dacs, preferred_element_type=F32, precision=HI)
        ddt_ref[...] = dda * a_xv + dxd * xs
        dax_ref[...] += jnp.sum(dda * dtxv, axis=0, keepdims=True)
        dx_ref[...] = dxd * dtxv + gy * d_ref[...]

    rev = lambda g, c: (nc - 1 - c, g)
    return _pc(body, name=name, grid=(NG, nc),
               in_specs=[pl.BlockSpec((Q, GW), rev),
                         pl.BlockSpec((Q, NS), lambda g, c: (nc - 1 - c, DI // NS + g)),
                         pl.BlockSpec((Q, NS), lambda g, c: (nc - 1 - c, DI // NS + NG + g)),
                         pl.BlockSpec((Q, GW), rev),
                         pl.BlockSpec((Q, NHP), lambda g, c: (nc - 1 - c, 0)),
                         pl.BlockSpec((1, GW), lambda g, c: (0, g)),
                         pl.BlockSpec((1, GW), lambda g, c: (0, g)),
                         pl.BlockSpec((1, NS, GW), lambda g, c: (nc - 1 - c, 0, g)),
                         pl.BlockSpec((Q, GW), rev)],
               out_specs=[pl.BlockSpec((Q, GW), rev), pl.BlockSpec((Q, NS), rev), pl.BlockSpec((Q, NS), rev),
                          pl.BlockSpec((Q, GW), rev),
                          pl.BlockSpec((1, GW), lambda g, c: (0, g)), pl.BlockSpec((1, GW), lambda g, c: (0, g))],
               out_shape=[_sds((t, DI), F32), _sds((t, NG * NS), F32), _sds((t, NG * NS), F32), _sds((t, DI), F32),
                          _sds((1, DI), F32), _sds((1, DI), F32)],
               scratch=[pltpu.VMEM((NS, GW), F32)])(xbc, xbc, xbc, dtx, da, a_x, d_x, sin, dy)


def _gated_norm_fwd(y, z, gn, *, name):
    t = y.shape[0]
    tm = min(t, 256)

    def body(y_ref, z_ref, g_ref, o_ref):
        for k in range(NG):
            sl = slice(GW * k, GW * (k + 1))
            zz = z_ref[:, sl]
            h = y_ref[:, sl] * (zz * _sigmoid(zz))
            r = lax.rsqrt(jnp.mean(h * h, axis=-1, keepdims=True) + EPS)
            o_ref[:, sl] = (h * r * g_ref[:, sl]).astype(BF16)

    return _pc(body, name=name, grid=(t // tm,), in_specs=[_row(tm, DI), _row(tm, DI), _const((1, DI))],
               out_specs=_row(tm, DI), out_shape=_sds((t, DI), BF16))(y, z, gn.reshape(1, DI))


def _gated_norm_bwd(y, z, gn, dout, *, name):
    t = y.shape[0]
    tm = min(t, 256)

    def body(y_ref, z_ref, g_ref, do_ref, dy_ref, dz_ref, dg_ref):
        @pl.when(pl.program_id(0) == 0)
        def _():
            dg_ref[...] = jnp.zeros_like(dg_ref)

        for k in range(NG):
            sl = slice(GW * k, GW * (k + 1))
            zz = z_ref[:, sl]
            yy = y_ref[:, sl]
            sz = zz * _sigmoid(zz)
            h = yy * sz
            r = lax.rsqrt(jnp.mean(h * h, axis=-1, keepdims=True) + EPS)
            d = do_ref[:, sl]
            dg_ref[:, sl] += jnp.sum(d * h * r, axis=0, keepdims=True)
            dgd = d * g_ref[:, sl]
            dot = jnp.mean(dgd * h, axis=-1, keepdims=True)
            dh = r * dgd - h * (r * r * r * dot)
            dy_ref[:, sl] = dh * sz
            dz_ref[:, sl] = (dh * yy * _dsilu(zz)).astype(BF16)

    dy, dz, dg = _pc(body, name=name, grid=(t // tm,),
                     in_specs=[_row(tm, DI), _row(tm, DI), _const((1, DI)), _row(tm, DI)],
                     out_specs=[_row(tm, DI), _row(tm, DI), _const((1, DI))],
                     out_shape=[_sds((t, DI), F32), _sds((t, DI), BF16), _sds((1, DI), F32)])(y, z, gn.reshape(1, DI), dout)
    return dy, dz, dg.reshape(DI)


def _pad_heads(v):
    return jnp.pad(v.reshape(1, NH), ((0, 0), (0, NHP - NH)))


def _ssm_fwd(h, p):
    u = _rms_fwd(h, p["norm"], name="ssm_rms_fwd")
    z = _mm(u, p["wz"], "nn", name="ssm_in_z")
    xp = _mm(u, p["wxbc"], "nn", name="ssm_in_xbc")
    dtraw = _mm(u, p["wdt"], "nn", name="ssm_in_dt")
    xbc = _conv4_fwd(xp, p["conv_w"], p["conv_b"], name="ssm_conv_fwd")
    dtb, alog = _pad_heads(p["dt_bias"]), _pad_heads(p["a_log"])
    a_x = jnp.repeat(-jnp.exp(p["a_log"]), HD).reshape(1, DI)
    d_x = jnp.repeat(p["d"], HD).reshape(1, DI)
    dtx, da = _ssd_pre(dtraw, dtb, alog, name="ssd_pre")
    y, sin = _ssd_fwd(xbc, dtx, da, a_x, d_x, name="ssd_fwd")
    yn = _gated_norm_fwd(y, z, p["gate_norm"], name="ssm_gate_fwd")
    h2 = _mm(yn, p["wout"], "nn", res=h, name="ssm_out")
    return h2, (h, u, z, xp, dtraw, xbc, dtx, da, a_x, d_x, dtb, y, sin, yn)


def _ssm_bwd(dh2, p, saved):
    h, u, z, xp, dtraw, xbc, dtx, da, a_x, d_x, dtb, y, sin, yn = saved
    dyn = _mm(dh2, p["wout"], "nt", name="ssm_out_dx")
    g = {"w_out": _mm(yn, dh2, "tn", out_dtype=BF16, name="ssm_out_dw")}
    dy, dz, g["gate_norm"] = _gated_norm_bwd(y, z, p["gate_norm"], dyn, name="ssm_gate_bwd")
    dxs, dbm, dcm, ddtx, dax, ddx = _ssd_bwd(xbc, dtx, da, a_x, d_x, sin, dy, name="ssd_bwd")
    ddtraw, ddtb = _ssd_post(ddtx, dtraw, dtb, name="ssd_post")
    dpre, g["conv_w"], g["conv_b"] = _conv4_bwd_act(xp, p["conv_w"], p["conv_b"], jnp.concatenate([dxs, dbm, dcm], axis=1),
                                                    name="ssm_conv_bwd_act")
    dxp = _conv_bwd_x(dpre, p["conv_w"], KC, name="ssm_conv_bwd_x", out_dtype=BF16)
    du = _mm(dz, p["wz"], "nt", name="ssm_in_dx_z")
    du = _mm(dxp, p["wxbc"], "nt", res=du, name="ssm_in_dx_xbc")
    du = _mm(ddtraw, p["wdt"], "nt", res=du, name="ssm_in_dx_dt")
    g["w_in"] = jnp.concatenate([_mm(u, dz, "tn", out_dtype=BF16, name="ssm_in_dw_z"),
                                 _mm(u, dxp, "tn", out_dtype=BF16, name="ssm_in_dw_xbc"),
                                 _mm(u, ddtraw, "tn", out_dtype=BF16, name="ssm_in_dw_dt")[:, :NH]], axis=1)
    dh, g["norm"] = _rms_bwd(h, p["norm"], du, dh2, name="ssm_rms_bwd")
    g["dt_bias"] = ddtb[0, :NH]
    g["a_log"] = dax.reshape(NH, HD).sum(-1) * (-jnp.exp(p["a_log"]))
    g["d"] = ddx.reshape(NH, HD).sum(-1)
    return dh, g


def _cv_fwd(h, p):
    u = _rms_fwd(h, p["norm"], name="cv_rms_fwd")
    pre = _mm(u, p["wpw1"], "nn", bias=p["b_pw1"].reshape(1, 2 * D), name="cv_pw1")
    gl = _glu_fwd(pre, name="cv_glu_fwd")
    c, s = _dwconv_ln_fwd(gl, p["dw_w"], p["dw_b"], p["ln_g"], p["ln_b"], name="cv_dwconv_ln_fwd")
    h2 = _mm(s, p["wpw2"], "nn", bias=p["b_pw2"].reshape(1, D), res=h, name="cv_pw2")
    return h2, (h, u, pre, gl, c, s)


def _cv_bwd(dh2, p, saved):
    h, u, pre, gl, c, s = saved
    ds = _mm(dh2, p["wpw2"], "nt", name="cv_pw2_dx")
    g = {"w_pw2": _mm(s, dh2, "tn", out_dtype=BF16, name="cv_pw2_dw"), "b_pw2": _colsum(dh2, name="cv_pw2_db")}
    dc, g["ln_g"], g["ln_b"] = _ln_silu_bwd(c, p["ln_g"], p["ln_b"], ds, name="cv_ln_bwd")
    dgl = _conv_bwd_x(dc, p["dw_w"], CK, name="cv_dwconv_bwd_x", out_dtype=F32)
    g["dw_w"], g["dw_b"] = _dwconv_bwd_w(gl, dc, name="cv_dwconv_bwd_w")
    dpre, g["b_pw1"] = _glu_bwd(pre, dgl, name="cv_glu_bwd")
    du = _mm(dpre, p["wpw1"], "nt", name="cv_pw1_dx")
    g["w_pw1"] = _mm(u, dpre, "tn", out_dtype=BF16, name="cv_pw1_dw")
    dh, g["norm"] = _rms_bwd(h, p["norm"], du, dh2, name="cv_rms_bwd")
    return dh, g


def _ffn_fwd(h, p):
    u = _rms_fwd(h, p["norm"], name="ffn_rms_fwd")
    gu = _mm(u, p["wgu"], "nn", name="ffn_gate_up")
    a = _swiglu_fwd(gu, name="ffn_act_fwd")
    h2 = _mm(a, p["wd"], "nn", res=h, name="ffn_down")
    return h2, (h, u, gu, a)


def _ffn_bwd(dh2, p, saved):
    h, u, gu, a = saved
    da = _mm(dh2, p["wd"], "nt", name="ffn_down_dx")
    g = {"w_down": _mm(a, dh2, "tn", out_dtype=BF16, name="ffn_down_dw")}
    dgu = _swiglu_bwd(gu, da, name="ffn_act_bwd")
    du = _mm(dgu, p["wgu"], "nt", name="ffn_gate_up_dx")
    dwgu = _mm(u, dgu, "tn", out_dtype=BF16, name="ffn_gate_up_dw")
    g["w_gate"], g["w_up"] = dwgu[:, :DFF], dwgu[:, DFF:]
    dh, g["norm"] = _rms_bwd(h, p["norm"], du, dh2, name="ffn_rms_bwd")
    return dh, g


def _fwd_bwd(x, target, ssm, cv, ffn, final_norm):
    depth = len(ffn)
    h, tape = x, []
    for i in range(depth):
        h, s_mix = (_ssm_fwd if i % 2 == 0 else _cv_fwd)(h, (ssm if i % 2 == 0 else cv)[i // 2])
        h, s_ffn = _ffn_fwd(h, ffn[i])
        tape.append((s_mix, s_ffn))
    lsum, dh, g_final = _final_loss(h, final_norm, target, name="loss_head")
    g_ssm, g_cv, g_ffn = [None] * len(ssm), [None] * len(cv), [None] * depth
    for i in reversed(range(depth)):
        s_mix, s_ffn = tape[i]
        dh, g_ffn[i] = _ffn_bwd(dh, ffn[i], s_ffn)
        if i % 2 == 0:
            dh, g_ssm[i // 2] = _ssm_bwd(dh, ssm[i // 2], s_mix)
        else:
            dh, g_cv[i // 2] = _cv_bwd(dh, cv[i // 2], s_mix)
    return lsum, dh, g_ssm, g_cv, g_ffn, g_final


ANY = pl.BlockSpec(memory_space=pl.ANY)


def _place():
    x, y, c = lax.axis_index("x"), lax.axis_index("y"), lax.axis_index("c")
    return x, y, c, [(1 - x, y), (x, 1 - y), (1 - x, 1 - y)]


def _remote(src, dst, ssem, rsem, dev):
    return pltpu.make_async_remote_copy(src_ref=src, dst_ref=dst, send_sem=ssem, recv_sem=rsem, device_id=dev,
                                        device_id_type=MESH)


def _comm_call(body, name, ins, out_shape, n_sems):
    return pl.pallas_call(
        body, name=name, in_specs=[ANY] * len(ins), out_specs=[ANY] * len(out_shape), out_shape=out_shape,
        scratch_shapes=[pltpu.SemaphoreType.DMA((k,)) for k in n_sems])(*ins)


def _gather_big(ws):
    n = len(ws)

    def body(*refs):
        ins, outs = refs[:n], refs[n:2 * n]
        lsem, ssem, rsem, fssem, frsem = refs[2 * n:]
        x, y, c, chips = _place()
        me, sib = 2 * x + y, (x, y, 1 - c)

        def half(a, cc):
            lh = ws[a].shape[0] // 2
            return pl.ds(cc * lh, lh)

        local = [pltpu.make_async_copy(ins[a], outs[a].at[me], lsem.at[a]) for a in range(n)]
        for cp in local:
            cp.start()
        sends, fwds = [], []
        for a in range(n):
            for j, (px, py) in enumerate(chips):
                cp = _remote(ins[a].at[half(a, c)], outs[a].at[me, half(a, c)], ssem.at[3 * a + j], rsem.at[3 * a + j], (px, py, c))
                cp.start()
                sends.append(cp)
        for a in range(n):
            for j, (px, py) in enumerate(chips):
                blk = outs[a].at[2 * px + py, half(a, c)]
                _remote(blk, blk, ssem.at[3 * a + j], rsem.at[3 * a + j], (px, py, c)).wait_recv()
                f = _remote(blk, blk, fssem.at[3 * a + j], frsem.at[3 * a + j], sib)
                f.start()
                fwds.append(f)
        for a in range(n):
            for j, (px, py) in enumerate(chips):
                blk = outs[a].at[2 * px + py, half(a, 1 - c)]
                _remote(blk, blk, fssem.at[3 * a + j], frsem.at[3 * a + j], sib).wait_recv()
        for cp in sends + fwds:
            cp.wait_send()
        for cp in local:
            cp.wait()

    return _comm_call(body, "gather_weights", ws, [_sds((4,) + w.shape, w.dtype) for w in ws], [n, 3 * n, 3 * n, 3 * n, 3 * n])


def _exchange_pairs(gs):
    n = len(gs)

    def body(*refs):
        ins, keep, got = refs[:n], refs[n:2 * n], refs[2 * n:3 * n]
        lsem, ssem, rsem = refs[3 * n:]
        x, y, c, _ = _place()
        cps = []
        for a in range(n):
            lh = gs[a].shape[1] // 2
            cps.append(pltpu.make_async_copy(ins[a].at[:, pl.ds(c * lh, lh)], keep[a], lsem.at[a]))
            cps.append(_remote(ins[a].at[:, pl.ds((1 - c) * lh, lh)], got[a], ssem.at[a], rsem.at[a], (x, y, 1 - c)))
        for cp in cps:
            cp.start()
        for cp in cps:
            cp.wait()

    halves = [_sds((4, g.shape[1] // 2) + g.shape[2:], g.dtype) for g in gs]
    out = _comm_call(body, "reduce_pair_exchange", gs, halves + halves, [n, n, n])
    return out[:n], out[n:]


def _exchange_chips(ps):
    n = len(ps)

    def body(*refs):
        ins, outs = refs[:n], refs[n:2 * n]
        lsem, ssem, rsem = refs[2 * n:]
        x, y, c, chips = _place()
        me = 2 * x + y
        cps = []
        for a in range(n):
            cps.append(pltpu.make_async_copy(ins[a].at[me], outs[a].at[me], lsem.at[a]))
            for j, (px, py) in enumerate(chips):
                cps.append(_remote(ins[a].at[2 * px + py], outs[a].at[me], ssem.at[3 * a + j], rsem.at[3 * a + j], (px, py, c)))
        for cp in cps:
            cp.start()
        for a in range(n):
            for j, (px, py) in enumerate(chips):
                blk = outs[a].at[2 * px + py]
                _remote(blk, blk, ssem.at[3 * a + j], rsem.at[3 * a + j], (px, py, c)).wait_recv()
        for a in range(n):
            cps[4 * a].wait()
            for j in range(3):
                cps[4 * a + 1 + j].wait_send()

    return _comm_call(body, "reduce_chip_exchange", ps, [_sds(p.shape, p.dtype) for p in ps], [n, 3 * n, 3 * n])


def _exchange_final(fs):
    n = len(fs)

    def body(*refs):
        ins, outs = refs[:n], refs[n:2 * n]
        lsem, ssem, rsem = refs[2 * n:]
        x, y, c, _ = _place()
        cps = []
        for a in range(n):
            lh = fs[a].shape[0]
            mine = outs[a].at[pl.ds(c * lh, lh)]
            cps.append(pltpu.make_async_copy(ins[a], mine, lsem.at[a]))
            cps.append(_remote(ins[a], mine, ssem.at[a], rsem.at[a], (x, y, 1 - c)))
        for cp in cps:
            cp.start()
        for a in range(n):
            lh = fs[a].shape[0]
            theirs = outs[a].at[pl.ds((1 - c) * lh, lh)]
            _remote(theirs, theirs, ssem.at[a], rsem.at[a], (x, y, 1 - c)).wait_recv()
            cps[2 * a].wait()
            cps[2 * a + 1].wait_send()

    return _comm_call(body, "reduce_final_exchange", fs, [_sds((2 * f.shape[0],) + f.shape[1:], f.dtype) for f in fs], [n, n, n])


def _rows_tile(r, c, n_arrays, itemsize=4):
    cap = max(16, (16 * 1024 * 1024) // (2 * n_arrays * c * itemsize))
    best = 16 if r % 16 == 0 else r
    for t in range(16, min(r, cap) + 1, 16):
        if r % t == 0:
            best = t
    return best


def _add_pair(p, q, *, name):
    shape = p.shape
    c = shape[-1]
    r = p.size // c
    tm = _rows_tile(r, c, 3)

    def body(p_ref, q_ref, o_ref):
        o_ref[...] = (p_ref[...].astype(F32) + q_ref[...].astype(F32)).astype(BF16)

    return _pc(body, name=name, grid=(r // tm,), in_specs=[_row(tm, c), _row(tm, c)], out_specs=_row(tm, c),
               out_shape=_sds((r, c), BF16))(p.reshape(r, c), q.reshape(r, c)).reshape(shape)


def _sum_slots(rb, *, name):
    shape = rb.shape[1:]
    c = shape[-1]
    r = rb.size // (4 * c)
    tm = _rows_tile(r, c, 6)

    def body(i_ref, o_ref):
        acc = i_ref[0].astype(F32)
        for k in range(1, 4):
            acc = acc + i_ref[k].astype(F32)
        o_ref[...] = acc

    return _pc(body, name=name, grid=(r // tm,), in_specs=[pl.BlockSpec((4, tm, c), lambda i: (0, i, 0))],
               out_specs=_row(tm, c), out_shape=_sds((r, c), F32))(rb.reshape(4, r, c)).reshape(shape)


def _reduce_big(gs):
    keep, got = _exchange_pairs(gs)
    ps = [_add_pair(k, g, name="reduce_pair_add") for k, g in zip(keep, got)]
    rb = _exchange_chips(ps)
    fs = [_sum_slots(b, name="reduce_chip_sum") for b in rb]
    return _exchange_final(fs)


def _gather8(v, reduce):
    m = v.shape[0]

    def body(x_ref, *rest):
        if reduce:
            sum_ref, out_ref, send_sems, recv_sems, local_sem = rest
        else:
            out_ref, send_sems, recv_sems, local_sem = rest
        x, y, c, chips = _place()
        me, sibling = (x, y, c), (x, y, 1 - c)

        def rows(px, py, pc):
            return out_ref.at[pl.ds((4 * px + 2 * py + pc) * m, m), :]

        def copy(k, block, to, src=None):
            return _remote(rows(*block) if src is None else src, rows(*block), send_sems.at[k], recv_sems.at[k], to)

        mine = pltpu.make_async_copy(x_ref, rows(*me), local_sem)
        mine.start()
        first = [copy(0, me, sibling, src=x_ref)]
        first += [copy(1 + j, me, (*chip, c), src=x_ref) for j, chip in enumerate(chips)]
        for cp in first:
            cp.start()
        passed = [copy(4 + j, (*chip, c), sibling) for j, chip in enumerate(chips)]
        for j, chip in enumerate(chips):
            copy(1 + j, (*chip, c), me).wait_recv()
            passed[j].start()
        copy(0, sibling, me).wait_recv()
        for j, chip in enumerate(chips):
            copy(4 + j, (*chip, 1 - c), me).wait_recv()
        for cp in first + passed:
            cp.wait_send()
        mine.wait()
        if reduce:
            acc = out_ref[pl.ds(0, m), :]
            for k in range(1, 8):
                acc = acc + out_ref[pl.ds(k * m, m), :]
            sum_ref[...] = acc

    vm = pl.BlockSpec(memory_space=pltpu.VMEM)
    out_shape = [_sds((8 * m, 128), F32)]
    if reduce:
        out_shape = [_sds((m, 128), F32)] + out_shape
    out = pl.pallas_call(
        body, name="allreduce_small" if reduce else "gather_small", in_specs=[vm], out_specs=[vm] * len(out_shape),
        out_shape=out_shape,
        scratch_shapes=[pltpu.SemaphoreType.DMA((7,)), pltpu.SemaphoreType.DMA((7,)), pltpu.SemaphoreType.DMA(())],
        compiler_params=pltpu.CompilerParams(vmem_limit_bytes=VMEM_LIMIT))(v)
    return out[0]


def _adamw(w, g, m, v, *, name):
    shape = w.shape
    c = shape[-1]
    r = w.size // c
    tm = _rows_tile(r, c, 7) if r % 16 == 0 else r

    def body(w_ref, g_ref, m_ref, v_ref, d_ref, mo_ref, vo_ref):
        gg = g_ref[...]
        mm = B1 * m_ref[...] + (1.0 - B1) * gg
        vv = B2 * v_ref[...] + (1.0 - B2) * (gg * gg)
        mo_ref[...] = mm
        vo_ref[...] = vv
        m_hat = mm / (1.0 - B1 ** STEP)
        v_hat = vv / (1.0 - B2 ** STEP)
        d_ref[...] = -LR * (m_hat / (jnp.sqrt(v_hat) + AEPS) + WD * w_ref[...])

    outs = _pc(body, name=name, grid=(r // tm,), in_specs=[_row(tm, c)] * 4, out_specs=[_row(tm, c)] * 3,
               out_shape=[_sds((r, c), F32)] * 3)(*[t.reshape(r, c) for t in (w, g, m, v)])
    return [o.reshape(shape) for o in outs]


WEIGHTS = ["ssm_norm", "ssm_w_in", "ssm_conv_w", "ssm_conv_b", "ssm_dt_bias", "ssm_a_log", "ssm_d", "ssm_gate_norm",
           "ssm_w_out", "cv_norm", "cv_w_pw1", "cv_b_pw1", "cv_dw_w", "cv_dw_b", "cv_ln_g", "cv_ln_b", "cv_w_pw2",
           "cv_b_pw2", "ffn_norm", "ffn_w_gate", "ffn_w_up", "ffn_w_down", "final_norm"]
BIG = {"ssm_w_in": "col", "ssm_w_out": "row", "cv_w_pw1": "col", "cv_w_pw2": "row",
       "ffn_w_gate": "col", "ffn_w_up": "col", "ffn_w_down": "row"}
SMALL_SHARDED = {"ssm_conv_w": 2, "cv_norm": 1, "cv_b_pw1": 1, "cv_dw_w": 2, "cv_dw_b": 1, "cv_ln_g": 1, "cv_ln_b": 1,
                 "cv_b_pw2": 1}
SMALL = [n for n in WEIGHTS if n not in BIG]
LANES = 128


def _pack(arrays):
    flat = jnp.concatenate([a.reshape(-1) for a in arrays])
    rows = -(-flat.size // (8 * LANES)) * 8
    return jnp.pad(flat, (0, rows * LANES - flat.size)).reshape(rows, LANES)


def _unpack(packed, shapes):
    flat, out, off = packed.reshape(-1), [], 0
    for s in shapes:
        n = 1
        for d_ in s:
            n *= d_
        out.append(flat[off:off + n].reshape(s))
        off += n
    return out


def _unshard(parts, axis):
    return jnp.concatenate(parts, axis=axis)


def kernel(x, ssm_norm, ssm_w_in, ssm_conv_w, ssm_conv_b, ssm_dt_bias, ssm_a_log, ssm_d, ssm_gate_norm, ssm_w_out, cv_norm, cv_w_pw1, cv_b_pw1, cv_dw_w, cv_dw_b, cv_ln_g, cv_ln_b, cv_w_pw2, cv_b_pw2, ffn_norm, ffn_w_gate, ffn_w_up, ffn_w_down, final_norm, loss_target, m_ssm_norm, m_ssm_w_in, m_ssm_conv_w, m_ssm_conv_b, m_ssm_dt_bias, m_ssm_a_log, m_ssm_d, m_ssm_gate_norm, m_ssm_w_out, m_cv_norm, m_cv_w_pw1, m_cv_b_pw1, m_cv_dw_w, m_cv_dw_b, m_cv_ln_g, m_cv_ln_b, m_cv_w_pw2, m_cv_b_pw2, m_ffn_norm, m_ffn_w_gate, m_ffn_w_up, m_ffn_w_down, m_final_norm, v_ssm_norm, v_ssm_w_in, v_ssm_conv_w, v_ssm_conv_b, v_ssm_dt_bias, v_ssm_a_log, v_ssm_d, v_ssm_gate_norm, v_ssm_w_out, v_cv_norm, v_cv_w_pw1, v_cv_b_pw1, v_cv_dw_w, v_cv_dw_b, v_cv_ln_g, v_cv_ln_b, v_cv_w_pw2, v_cv_b_pw2, v_ffn_norm, v_ffn_w_gate, v_ffn_w_up, v_ffn_w_down, v_final_norm):
    a = dict(locals())
    chip = 2 * lax.axis_index("x") + lax.axis_index("y")
    n_ssm, n_cv, depth = ssm_norm.shape[0], cv_norm.shape[0], ffn_norm.shape[0]

    gw = dict(zip(BIG, _gather_big([a[n].astype(BF16) for n in BIG])))
    sh_names = list(SMALL_SHARDED)
    got = _gather8(_pack([a[n] for n in sh_names]), reduce=False)
    got = got.reshape(8, -1)[0::2]
    per_chip = [_unpack(got[s], [a[n].shape for n in sh_names]) for s in range(4)]
    full = {n: a[n] for n in SMALL}
    for k, n in enumerate(sh_names):
        full[n] = _unshard([per_chip[s][k] for s in range(4)], SMALL_SHARDED[n])

    def whole(name, layer):
        parts = gw[name][:, layer]
        if BIG[name] == "col":
            return _unshard([parts[s] for s in range(4)], 1)
        return parts.reshape(-1, parts.shape[-1])

    ssm, cv, ffn = [], [], []
    for j in range(n_ssm):
        w_in = whole("ssm_w_in", j)
        ssm.append(dict(norm=full["ssm_norm"][j], wz=w_in[:, :DI], wxbc=w_in[:, DI:DI + CD],
                        wdt=jnp.pad(w_in[:, DI + CD:], ((0, 0), (0, NHP - NH))),
                        conv_w=full["ssm_conv_w"][j], conv_b=full["ssm_conv_b"][j], dt_bias=full["ssm_dt_bias"][j],
                        a_log=full["ssm_a_log"][j], d=full["ssm_d"][j], gate_norm=full["ssm_gate_norm"][j],
                        wout=whole("ssm_w_out", j)))
    for j in range(n_cv):
        cv.append(dict(norm=full["cv_norm"][j], wpw1=whole("cv_w_pw1", j), b_pw1=full["cv_b_pw1"][j],
                       dw_w=full["cv_dw_w"][j], dw_b=full["cv_dw_b"][j], ln_g=full["cv_ln_g"][j], ln_b=full["cv_ln_b"][j],
                       wpw2=whole("cv_w_pw2", j), b_pw2=full["cv_b_pw2"][j]))
    for i in range(depth):
        ffn.append(dict(norm=full["ffn_norm"][i],
                        wgu=jnp.concatenate([whole("ffn_w_gate", i), whole("ffn_w_up", i)], axis=1),
                        wd=whole("ffn_w_down", i)))

    lsum, grad_x, g_ssm, g_cv, g_ffn, g_final = _fwd_bwd(x[0], loss_target[0], ssm, cv, ffn, full["final_norm"])
    loss = (0.5 / D) * lax.psum(jnp.sum(lsum), ("x", "y", "c"))

    def slots(name, layers):
        if BIG[name] == "col":
            return jnp.stack([g.reshape(g.shape[0], 4, g.shape[1] // 4).transpose(1, 0, 2) for g in layers], axis=1)
        return jnp.stack([g.reshape(4, g.shape[0] // 4, g.shape[1]) for g in layers], axis=1)

    local_big = {"ssm_w_in": [g["w_in"] for g in g_ssm], "ssm_w_out": [g["w_out"] for g in g_ssm],
                 "cv_w_pw1": [g["w_pw1"] for g in g_cv], "cv_w_pw2": [g["w_pw2"] for g in g_cv],
                 "ffn_w_gate": [g["w_gate"] for g in g_ffn], "ffn_w_up": [g["w_up"] for g in g_ffn],
                 "ffn_w_down": [g["w_down"] for g in g_ffn]}
    grads = dict(zip(BIG, _reduce_big([slots(n, local_big[n]) for n in BIG])))

    local_small = {"final_norm": g_final, "ffn_norm": jnp.stack([g["norm"] for g in g_ffn])}
    for n in SMALL:
        if n.startswith("ssm_"):
            local_small[n] = jnp.stack([g[n[4:]] for g in g_ssm])
        elif n.startswith("cv_"):
            local_small[n] = jnp.stack([g[n[3:]] for g in g_cv])
    summed = _unpack(_gather8(_pack([local_small[n] for n in SMALL]), reduce=True), [full[n].shape for n in SMALL])
    for n, g in zip(SMALL, summed):
        if n in SMALL_SHARDED:
            ax = SMALL_SHARDED[n]
            g = lax.dynamic_slice_in_dim(g, chip * a[n].shape[ax], a[n].shape[ax], axis=ax)
        grads[n] = g

    delta, new_m, new_v = {}, {}, {}
    for n in BIG:
        delta[n], new_m[n], new_v[n] = _adamw(a[n], grads[n], a["m_" + n], a["v_" + n], name="adamw_" + n)
    shapes = [a[n].shape for n in SMALL]
    upd = _adamw(_pack([a[n] for n in SMALL]), _pack([grads[n] for n in SMALL]), _pack([a["m_" + n] for n in SMALL]),
                 _pack([a["v_" + n] for n in SMALL]), name="adamw_small")
    for dst, packed in zip((delta, new_m, new_v), upd):
        dst.update(zip(SMALL, _unpack(packed, shapes)))

    return (loss, grad_x[None], *[grads[n] for n in WEIGHTS], *[delta[n] for n in WEIGHTS],
            *[new_m[n] for n in WEIGHTS], *[new_v[n] for n in WEIGHTS])
```

```python
import functools

import jax
import jax.numpy as jnp
from jax import lax
from jax.experimental import pallas as pl
from jax.experimental.pallas import tpu as pltpu

F32 = jnp.float32
BF16 = jnp.bfloat16
HI = lax.Precision.HIGHEST
MESH = pl.DeviceIdType.MESH

D = 1024
DI = 2048
HD = 64
NH = 32
NG = 4
GW = DI // NG
NS = 128
KC = 4
CD = DI + 2 * NG * NS
Q = 128
DFF = 2816
CK = 31
EPS = 1e-5
NHP = 128

LR, B1, B2, AEPS, WD, STEP = 0.001, 0.9, 0.999, 1e-08, 0.01, 10

VMEM_LIMIT = 56 * 1024 * 1024
MM_VMEM_BUDGET = 36 * 1024 * 1024


def _pc(body, *, name, grid, in_specs, out_specs, out_shape, scratch=()):
    return pl.pallas_call(
        body, name=name, grid=grid, in_specs=in_specs, out_specs=out_specs, out_shape=out_shape,
        scratch_shapes=list(scratch),
        compiler_params=pltpu.CompilerParams(dimension_semantics=("arbitrary",) * len(grid),
                                             vmem_limit_bytes=VMEM_LIMIT))


def _sds(shape, dtype):
    return jax.ShapeDtypeStruct(tuple(shape), dtype)


def _row(tm, c):
    return pl.BlockSpec((tm, c), lambda i: (i, 0))


def _const(shape):
    return pl.BlockSpec(tuple(shape), lambda i: (0,) * len(shape))


def _tile(n, cap):
    if n <= cap:
        return n
    best = 128
    for t in range(128, cap + 1, 128):
        if n % t == 0:
            best = t
    return best


def _sigmoid(x):
    return 1.0 / (1.0 + jnp.exp(-x))


def _dsilu(x):
    s = _sigmoid(x)
    return s * (1.0 + x * (1.0 - s))


def _mm(a, b, mode, *, name, out_dtype=F32, bias=None, res=None, tm=1024, tn=1024):
    if mode == "nn":
        (m, k), (k2, n) = a.shape, b.shape
    elif mode == "nt":
        (m, k), (n, k2) = a.shape, b.shape
    else:
        (k, m), (k2, n) = a.shape, b.shape
    assert k == k2, (a.shape, b.shape, mode)
    tm = _tile(m, tm if k <= 2048 else 512)
    so = jnp.dtype(out_dtype).itemsize + (4 if res is not None else 0)
    fits = [c for c in range(128, min(n, 2048) + 1, 128) if n % c == 0 and
            2 * (tm * k * a.dtype.itemsize + c * k * b.dtype.itemsize + tm * c * so) <= MM_VMEM_BUDGET]
    tn = n if n <= 128 else max(fits)
    dn = {"nn": (((1,), (0,)), ((), ())), "nt": (((1,), (1,)), ((), ())), "tn": (((0,), (0,)), ((), ()))}[mode]
    nb, nr = bias is not None, res is not None

    def body(*refs):
        acc = lax.dot_general(refs[0][...].astype(BF16), refs[1][...].astype(BF16), dn, preferred_element_type=F32)
        if nb:
            acc = acc + refs[2][...]
        if nr:
            acc = acc + refs[2 + nb][...]
        refs[-1][...] = acc.astype(out_dtype)

    a_spec = pl.BlockSpec((k, tm), lambda i, j: (0, i)) if mode == "tn" else pl.BlockSpec((tm, k), lambda i, j: (i, 0))
    b_spec = pl.BlockSpec((tn, k), lambda i, j: (j, 0)) if mode == "nt" else pl.BlockSpec((k, tn), lambda i, j: (0, j))
    ins, specs = [a, b], [a_spec, b_spec]
    if nb:
        ins.append(bias)
        specs.append(pl.BlockSpec((1, tn), lambda i, j: (0, j)))
    if nr:
        ins.append(res)
        specs.append(pl.BlockSpec((tm, tn), lambda i, j: (i, j)))
    return _pc(body, name=name, grid=(m // tm, n // tn), in_specs=specs,
               out_specs=pl.BlockSpec((tm, tn), lambda i, j: (i, j)), out_shape=_sds((m, n), out_dtype))(*ins)


def _rms_fwd(h, g, *, name):
    t = h.shape[0]
    tm = min(t, 512)

    def body(h_ref, g_ref, u_ref):
        x = h_ref[...]
        r = lax.rsqrt(jnp.mean(x * x, axis=-1, keepdims=True) + EPS)
        u_ref[...] = (x * r * g_ref[...]).astype(BF16)

    return _pc(body, name=name, grid=(t // tm,), in_specs=[_row(tm, D), _const((1, D))], out_specs=_row(tm, D),
               out_shape=_sds((t, D), BF16))(h, g.reshape(1, D))


def _rms_bwd(h, g, du, dres, *, name):
    t = h.shape[0]
    tm = min(t, 512)

    def body(h_ref, g_ref, du_ref, dres_ref, dh_ref, dg_ref):
        x = h_ref[...]
        r = lax.rsqrt(jnp.mean(x * x, axis=-1, keepdims=True) + EPS)
        dy = du_ref[...]
        dyg = dy * g_ref[...]
        dot = jnp.mean(dyg * x, axis=-1, keepdims=True)
        dh_ref[...] = dres_ref[...] + r * dyg - x * (r * r * r * dot)

        @pl.when(pl.program_id(0) == 0)
        def _():
            dg_ref[...] = jnp.zeros_like(dg_ref)

        dg_ref[...] += jnp.sum(dy * x * r, axis=0, keepdims=True)

    dh, dg = _pc(body, name=name, grid=(t // tm,),
                 in_specs=[_row(tm, D), _const((1, D)), _row(tm, D), _row(tm, D)],
                 out_specs=[_row(tm, D), _const((1, D))],
                 out_shape=[_sds((t, D), F32), _sds((1, D), F32)])(h, g.reshape(1, D), du, dres)
    return dh, dg.reshape(D)


def _final_loss(h, g, target, *, name):
    t = h.shape[0]
    tm = min(t, 512)

    def body(h_ref, g_ref, t_ref, dh_ref, dg_ref, l_ref):
        x = h_ref[...]
        gg = g_ref[...]
        r = lax.rsqrt(jnp.mean(x * x, axis=-1, keepdims=True) + EPS)
        err = x * r * gg - t_ref[...]
        dy = err * (1.0 / D)
        dyg = dy * gg
        dot = jnp.mean(dyg * x, axis=-1, keepdims=True)
        dh_ref[...] = r * dyg - x * (r * r * r * dot)

        @pl.when(pl.program_id(0) == 0)
        def _():
            dg_ref[...] = jnp.zeros_like(dg_ref)
            l_ref[...] = jnp.zeros_like(l_ref)

        dg_ref[...] += jnp.sum(dy * x * r, axis=0, keepdims=True)
        l_ref[...] += jnp.sum(err * err, axis=0, keepdims=True)

    dh, dg, l = _pc(body, name=name, grid=(t // tm,),
                    in_specs=[_row(tm, D), _const((1, D)), _row(tm, D)],
                    out_specs=[_row(tm, D), _const((1, D)), _const((1, D))],
                    out_shape=[_sds((t, D), F32), _sds((1, D), F32), _sds((1, D), F32)])(h, g.reshape(1, D), target)
    return l, dh, dg.reshape(D)


def _swiglu_fwd(gu, *, name):
    t = gu.shape[0]
    tm = min(t, 256)

    def body(gu_ref, a_ref):
        gate = gu_ref[:, :DFF]
        up = gu_ref[:, DFF:]
        a_ref[...] = (gate * _sigmoid(gate) * up).astype(BF16)

    return _pc(body, name=name, grid=(t // tm,), in_specs=[_row(tm, 2 * DFF)], out_specs=_row(tm, DFF),
               out_shape=_sds((t, DFF), BF16))(gu)


def _swiglu_bwd(gu, da, *, name):
    t = gu.shape[0]
    tm = min(t, 256)

    def body(gu_ref, da_ref, d_ref):
        gate = gu_ref[:, :DFF]
        up = gu_ref[:, DFF:]
        dav = da_ref[...]
        d_ref[:, :DFF] = (dav * up * _dsilu(gate)).astype(BF16)
        d_ref[:, DFF:] = (dav * gate * _sigmoid(gate)).astype(BF16)

    return _pc(body, name=name, grid=(t // tm,), in_specs=[_row(tm, 2 * DFF), _row(tm, DFF)],
               out_specs=_row(tm, 2 * DFF), out_shape=_sds((t, 2 * DFF), BF16))(gu, da)


def _halo_before(tm, halo, tc):
    return pl.BlockSpec((halo, tc), lambda j, i: (jnp.maximum(i * (tm // halo) - 1, 0), j))


def _halo_after(tm, halo, tc, t):
    return pl.BlockSpec((halo, tc), lambda j, i: (jnp.minimum((i + 1) * (tm // halo), t // halo - 1), j))


def _conv4_fwd(xp, w, b, *, name):
    t = xp.shape[0]
    tm, tc, halo = min(t, 256), 512, 8

    def body(x_ref, h_ref, w_ref, b_ref, o_ref, pad):
        i = pl.program_id(1)
        pad[pl.ds(0, halo), :] = jnp.where(i == 0, 0.0, h_ref[...])
        pad[pl.ds(halo, tm), :] = x_ref[...]
        acc = jnp.zeros((tm, tc), F32) + b_ref[...]
        for k in range(KC):
            acc = acc + pad[pl.ds(halo - (KC - 1) + k, tm), :] * w_ref[pl.ds(k, 1), :]
        o_ref[...] = acc * _sigmoid(acc)

    tile = pl.BlockSpec((tm, tc), lambda j, i: (i, j))
    return _pc(body, name=name, grid=(CD // tc, t // tm),
               in_specs=[tile, _halo_before(tm, halo, tc), pl.BlockSpec((KC, tc), lambda j, i: (0, j)),
                         pl.BlockSpec((1, tc), lambda j, i: (0, j))],
               out_specs=tile, out_shape=_sds((t, CD), F32), scratch=[pltpu.VMEM((halo + tm, tc), F32)])(xp, xp, w, b.reshape(1, CD))


def _conv4_bwd_act(xp, w, b, dout, *, name):
    t = xp.shape[0]
    tm, tc, halo = min(t, 256), 512, 8

    def body(x_ref, h_ref, w_ref, b_ref, do_ref, dp_ref, dw_ref, db_ref, pad):
        i = pl.program_id(1)
        pad[pl.ds(0, halo), :] = jnp.where(i == 0, 0.0, h_ref[...])
        pad[pl.ds(halo, tm), :] = x_ref[...]
        acc = jnp.zeros((tm, tc), F32) + b_ref[...]
        for k in range(KC):
            acc = acc + pad[pl.ds(halo - (KC - 1) + k, tm), :] * w_ref[pl.ds(k, 1), :]
        dpre = do_ref[...] * _dsilu(acc)
        dp_ref[...] = dpre

        @pl.when(i == 0)
        def _():
            dw_ref[...] = jnp.zeros_like(dw_ref)
            db_ref[...] = jnp.zeros_like(db_ref)

        db_ref[...] += jnp.sum(dpre, axis=0, keepdims=True)
        for k in range(KC):
            dw_ref[pl.ds(k, 1), :] += jnp.sum(dpre * pad[pl.ds(halo - (KC - 1) + k, tm), :], axis=0, keepdims=True)

    tile = pl.BlockSpec((tm, tc), lambda j, i: (i, j))
    wspec = pl.BlockSpec((KC, tc), lambda j, i: (0, j))
    bspec = pl.BlockSpec((1, tc), lambda j, i: (0, j))
    dp, dw, db = _pc(body, name=name, grid=(CD // tc, t // tm),
                     in_specs=[tile, _halo_before(tm, halo, tc), wspec, bspec, tile],
                     out_specs=[tile, wspec, bspec],
                     out_shape=[_sds((t, CD), F32), _sds((KC, CD), F32), _sds((1, CD), F32)],
                     scratch=[pltpu.VMEM((halo + tm, tc), F32)])(xp, xp, w, b.reshape(1, CD), dout)
    return dp, dw, db.reshape(CD)


def _conv_bwd_x(dy, w, kk, *, name, out_dtype):
    t, c = dy.shape
    halo = 8 if kk <= 8 else 32
    tm, tc = min(t, 256), min(c, 512)

    def body(y_ref, h_ref, w_ref, o_ref, pad):
        i = pl.program_id(1)
        pad[pl.ds(0, tm), :] = y_ref[...]
        pad[pl.ds(tm, halo), :] = jnp.where(i == t // tm - 1, 0.0, h_ref[...])
        acc = jnp.zeros((tm, tc), F32)
        for k in range(kk):
            acc = acc + pad[pl.ds(kk - 1 - k, tm), :] * w_ref[pl.ds(k, 1), :]
        o_ref[...] = acc.astype(out_dtype)

    tile = pl.BlockSpec((tm, tc), lambda j, i: (i, j))
    return _pc(body, name=name, grid=(c // tc, t // tm),
               in_specs=[tile, _halo_after(tm, halo, tc, t), pl.BlockSpec((kk, tc), lambda j, i: (0, j))],
               out_specs=tile, out_shape=_sds((t, c), out_dtype), scratch=[pltpu.VMEM((tm + halo, tc), F32)])(dy, dy, w)


def _glu_fwd(p, *, name):
    t = p.shape[0]
    tm = min(t, 512)

    def body(p_ref, o_ref):
        o_ref[...] = p_ref[:, :D] * _sigmoid(p_ref[:, D:])

    return _pc(body, name=name, grid=(t // tm,), in_specs=[_row(tm, 2 * D)], out_specs=_row(tm, D),
               out_shape=_sds((t, D), F32))(p)


def _glu_bwd(p, dg, *, name):
    t = p.shape[0]
    tm = min(t, 512)

    def body(p_ref, dg_ref, dp_ref, db_ref):
        a = p_ref[:, :D]
        s = _sigmoid(p_ref[:, D:])
        d = dg_ref[...]
        da = d * s
        dgate = d * a * s * (1.0 - s)
        dp_ref[:, :D] = da.astype(BF16)
        dp_ref[:, D:] = dgate.astype(BF16)

        @pl.when(pl.program_id(0) == 0)
        def _():
            db_ref[...] = jnp.zeros_like(db_ref)

        db_ref[:, :D] += jnp.sum(da, axis=0, keepdims=True)
        db_ref[:, D:] += jnp.sum(dgate, axis=0, keepdims=True)

    dp, db = _pc(body, name=name, grid=(t // tm,), in_specs=[_row(tm, 2 * D), _row(tm, D)],
                 out_specs=[_row(tm, 2 * D), _const((1, 2 * D))],
                 out_shape=[_sds((t, 2 * D), BF16), _sds((1, 2 * D), F32)])(p, dg)
    return dp, db.reshape(2 * D)


def _dwconv_ln_fwd(g, w, b, lg, lb, *, name):
    t = g.shape[0]
    tm, halo = min(t, 256), 32

    def body(x_ref, h_ref, w_ref, b_ref, lg_ref, lb_ref, c_ref, s_ref, pad):
        i = pl.program_id(0)
        pad[pl.ds(0, halo), :] = jnp.where(i == 0, 0.0, h_ref[...])
        pad[pl.ds(halo, tm), :] = x_ref[...]
        acc = jnp.zeros((tm, D), F32) + b_ref[...]
        for k in range(CK):
            acc = acc + pad[pl.ds(halo - (CK - 1) + k, tm), :] * w_ref[pl.ds(k, 1), :]
        c_ref[...] = acc
        mu = jnp.mean(acc, axis=-1, keepdims=True)
        xc = acc - mu
        r = lax.rsqrt(jnp.mean(xc * xc, axis=-1, keepdims=True) + EPS)
        n = xc * r * lg_ref[...] + lb_ref[...]
        s_ref[...] = (n * _sigmoid(n)).astype(BF16)

    c, s = _pc(body, name=name, grid=(t // tm,),
               in_specs=[_row(tm, D), pl.BlockSpec((halo, D), lambda i: (jnp.maximum(i * (tm // halo) - 1, 0), 0)),
                         _const((CK, D)), _const((1, D)), _const((1, D)), _const((1, D))],
               out_specs=[_row(tm, D), _row(tm, D)], out_shape=[_sds((t, D), F32), _sds((t, D), BF16)],
               scratch=[pltpu.VMEM((halo + tm, D), F32)])(g, g, w, b.reshape(1, D), lg.reshape(1, D), lb.reshape(1, D))
    return c, s


def _ln_silu_bwd(c, lg, lb, ds, *, name):
    t = c.shape[0]
    tm = min(t, 512)

    def body(c_ref, lg_ref, lb_ref, ds_ref, dc_ref, dlg_ref, dlb_ref):
        x = c_ref[...]
        mu = jnp.mean(x, axis=-1, keepdims=True)
        xc = x - mu
        r = lax.rsqrt(jnp.mean(xc * xc, axis=-1, keepdims=True) + EPS)
        xh = xc * r
        n = xh * lg_ref[...] + lb_ref[...]
        dn = ds_ref[...] * _dsilu(n)
        dxh = dn * lg_ref[...]
        m1 = jnp.mean(dxh, axis=-1, keepdims=True)
        m2 = jnp.mean(dxh * xh, axis=-1, keepdims=True)
        dc_ref[...] = r * (dxh - m1 - xh * m2)

        @pl.when(pl.program_id(0) == 0)
        def _():
            dlg_ref[...] = jnp.zeros_like(dlg_ref)
            dlb_ref[...] = jnp.zeros_like(dlb_ref)

        dlg_ref[...] += jnp.sum(dn * xh, axis=0, keepdims=True)
        dlb_ref[...] += jnp.sum(dn, axis=0, keepdims=True)

    dc, dlg, dlb = _pc(body, name=name, grid=(t // tm,),
                       in_specs=[_row(tm, D), _const((1, D)), _const((1, D)), _row(tm, D)],
                       out_specs=[_row(tm, D), _const((1, D)), _const((1, D))],
                       out_shape=[_sds((t, D), F32), _sds((1, D), F32), _sds((1, D), F32)])(
                           c, lg.reshape(1, D), lb.reshape(1, D), ds)
    return dc, dlg.reshape(D), dlb.reshape(D)


def _dwconv_bwd_w(g, dc, *, name):
    t = g.shape[0]
    tm, tc, halo = min(t, 256), 512, 32

    def body(x_ref, h_ref, dc_ref, dw_ref, db_ref, pad):
        i = pl.program_id(1)
        pad[pl.ds(0, halo), :] = jnp.where(i == 0, 0.0, h_ref[...])
        pad[pl.ds(halo, tm), :] = x_ref[...]
        d = dc_ref[...]

        @pl.when(i == 0)
        def _():
            dw_ref[...] = jnp.zeros_like(dw_ref)
            db_ref[...] = jnp.zeros_like(db_ref)

        db_ref[...] += jnp.sum(d, axis=0, keepdims=True)
        for k in range(CK):
            dw_ref[pl.ds(k, 1), :] += jnp.sum(d * pad[pl.ds(halo - (CK - 1) + k, tm), :], axis=0, keepdims=True)

    tile = pl.BlockSpec((tm, tc), lambda j, i: (i, j))
    dw, db = _pc(body, name=name, grid=(D // tc, t // tm),
                 in_specs=[tile, _halo_before(tm, halo, tc), tile],
                 out_specs=[pl.BlockSpec((CK, tc), lambda j, i: (0, j)), pl.BlockSpec((1, tc), lambda j, i: (0, j))],
                 out_shape=[_sds((CK, D), F32), _sds((1, D), F32)],
                 scratch=[pltpu.VMEM((halo + tm, tc), F32)])(g, g, dc)
    return dw, db.reshape(D)


def _colsum(x, *, name):
    t, c = x.shape
    tm = min(t, 512)

    def body(x_ref, o_ref):
        @pl.when(pl.program_id(0) == 0)
        def _():
            o_ref[...] = jnp.zeros_like(o_ref)

        o_ref[...] += jnp.sum(x_ref[...], axis=0, keepdims=True)

    return _pc(body, name=name, grid=(t // tm,), in_specs=[_row(tm, c)], out_specs=_const((1, c)),
               out_shape=_sds((1, c), F32))(x).reshape(c)


def _head_expand(n_lanes, first_head=0):
    h = lax.broadcasted_iota(jnp.int32, (NHP, n_lanes), 0)
    j = lax.broadcasted_iota(jnp.int32, (NHP, n_lanes), 1)
    return (h == first_head + j // HD).astype(F32)


def _ssd_pre(dtraw, dtb, alog, *, name):
    t = dtraw.shape[0]
    tm = min(t, 256)

    def body(r_ref, b_ref, al_ref, dtx_ref, da_ref):
        v = r_ref[...] + b_ref[...]
        dt = jnp.maximum(v, 0.0) + jnp.log(1.0 + jnp.exp(-jnp.abs(v)))
        dtx_ref[...] = jnp.dot(dt, _head_expand(DI), preferred_element_type=F32, precision=HI)
        da_ref[...] = dt * (-jnp.exp(al_ref[...]))

    return _pc(body, name=name, grid=(t // tm,), in_specs=[_row(tm, NHP), _const((1, NHP)), _const((1, NHP))],
               out_specs=[_row(tm, DI), _row(tm, NHP)], out_shape=[_sds((t, DI), F32), _sds((t, NHP), F32)])(dtraw, dtb, alog)


def _ssd_post(ddtx, dtraw, dtb, *, name):
    t = dtraw.shape[0]
    tm = min(t, 256)

    def body(g_ref, r_ref, b_ref, o_ref, db_ref):
        h = lax.broadcasted_iota(jnp.int32, (DI, NHP), 1)
        j = lax.broadcasted_iota(jnp.int32, (DI, NHP), 0)
        red = (h == j // HD).astype(F32)
        ddt = jnp.dot(g_ref[...], red, preferred_element_type=F32, precision=HI)
        draw = ddt * _sigmoid(r_ref[...] + b_ref[...])
        o_ref[...] = draw.astype(BF16)

        @pl.when(pl.program_id(0) == 0)
        def _():
            db_ref[...] = jnp.zeros_like(db_ref)

        db_ref[...] += jnp.sum(draw, axis=0, keepdims=True)

    return _pc(body, name=name, grid=(t // tm,), in_specs=[_row(tm, DI), _row(tm, NHP), _const((1, NHP))],
               out_specs=[_row(tm, NHP), _const((1, NHP))], out_shape=[_sds((t, NHP), BF16), _sds((1, NHP), F32)])(ddtx, dtraw, dtb)


def _chunk_common(g, da):
    ri = lax.broadcasted_iota(jnp.int32, (Q, Q), 0)
    ci = lax.broadcasted_iota(jnp.int32, (Q, Q), 1)
    tri = ri >= ci
    acs_col = jnp.dot(tri.astype(F32), da, preferred_element_type=F32, precision=HI)
    acs_x = jnp.dot(acs_col, _head_expand(GW, g * (NH // NG)), preferred_element_type=F32, precision=HI)
    last = lax.broadcasted_iota(jnp.int32, (Q, GW), 0) == Q - 1
    atot_x = jnp.sum(jnp.where(last, acs_x, 0.0), axis=0, keepdims=True)
    return tri, acs_col, acs_x, atot_x


def _decay_matrix(acs_col, head, tri):
    hk = lax.broadcasted_iota(jnp.int32, (NHP, Q), 0)
    colsel = (hk == head).astype(F32)
    colb = jnp.dot(acs_col, colsel, preferred_element_type=F32, precision=HI)
    hq = lax.broadcasted_iota(jnp.int32, (Q, NHP), 1)
    rowsel = (hq == head).astype(F32)
    rowb = lax.dot_general(rowsel, acs_col, (((1,), (1,)), ((), ())), preferred_element_type=F32, precision=HI)
    return jnp.exp(jnp.where(tri, colb - rowb, -1e30))


def _ssd_fwd(xbc, dtx, da, a_x, d_x, *, name):
    t = xbc.shape[0]
    nc = t // Q
    hpg = NH // NG

    def body(x_ref, b_ref, c_ref, dtx_ref, da_ref, a_ref, d_ref, y_ref, sin_ref, state):
        g = pl.program_id(0)

        @pl.when(pl.program_id(1) == 0)
        def _():
            state[...] = jnp.zeros_like(state)

        xs = x_ref[...]
        tri, acs_col, acs_x, atot_x = _chunk_common(g, da_ref[...])
        xd = xs * dtx_ref[...]
        xb = xd.astype(BF16)
        bb = b_ref[...]
        cb16 = c_ref[...].astype(BF16)
        cbm = lax.dot_general(cb16, bb.astype(BF16), (((1,), (1,)), ((), ())), preferred_element_type=F32)
        s_in = state[...]
        sin_ref[0] = s_in
        y = jnp.dot(cb16, s_in.astype(BF16), preferred_element_type=F32) * jnp.exp(acs_x) + xs * d_ref[...]
        lane = lax.broadcasted_iota(jnp.int32, (Q, 2 * HD), 1)
        for p in range(hpg // 2):
            xp = xb[:, 2 * HD * p:2 * HD * (p + 1)]
            yp = jnp.zeros((Q, 2 * HD), F32)
            for hh in range(2):
                lm = _decay_matrix(acs_col, g * hpg + 2 * p + hh, tri)
                m = (cbm * lm).astype(BF16)
                xm = jnp.where((lane >= HD) if hh == 1 else (lane < HD), xp, jnp.zeros_like(xp))
                yp = yp + jnp.dot(m, xm, preferred_element_type=F32)
            y_ref[:, 2 * HD * p:2 * HD * (p + 1)] = y[:, 2 * HD * p:2 * HD * (p + 1)] + yp
        dec = jnp.exp(atot_x - acs_x)
        s_new = jnp.dot(bb.T.astype(BF16), (xd * dec).astype(BF16), preferred_element_type=F32)
        state[...] = jnp.exp(atot_x) * s_in + s_new

    grp = lambda g, c: (c, g)
    return _pc(body, name=name, grid=(NG, nc),
               in_specs=[pl.BlockSpec((Q, GW), grp),
                         pl.BlockSpec((Q, NS), lambda g, c: (c, DI // NS + g)),
                         pl.BlockSpec((Q, NS), lambda g, c: (c, DI // NS + NG + g)),
                         pl.BlockSpec((Q, GW), grp),
                         pl.BlockSpec((Q, NHP), lambda g, c: (c, 0)),
                         pl.BlockSpec((1, GW), lambda g, c: (0, g)),
                         pl.BlockSpec((1, GW), lambda g, c: (0, g))],
               out_specs=[pl.BlockSpec((Q, GW), grp), pl.BlockSpec((1, NS, GW), lambda g, c: (c, 0, g))],
               out_shape=[_sds((t, DI), F32), _sds((nc, NS, DI), F32)],
               scratch=[pltpu.VMEM((NS, GW), F32)])(xbc, xbc, xbc, dtx, da, a_x, d_x)


def _ssd_bwd(xbc, dtx, da, a_x, d_x, sin, dy, *, name):
    t = xbc.shape[0]
    nc = t // Q
    hpg = NH // NG

    def body(x_ref, b_ref, c_ref, dtx_ref, da_ref, a_ref, d_ref, sin_ref, dy_ref,
             dx_ref, db_ref, dc_ref, ddt_ref, dax_ref, ddx_ref, dstate):
        g = pl.program_id(0)

        @pl.when(pl.program_id(1) == 0)
        def _():
            dstate[...] = jnp.zeros_like(dstate)
            dax_ref[...] = jnp.zeros_like(dax_ref)
            ddx_ref[...] = jnp.zeros_like(ddx_ref)

        xs = x_ref[...]
        dtxv = dtx_ref[...]
        a_xv = a_ref[...]
        gy = dy_ref[...]
        tri, acs_col, acs_x, atot_x = _chunk_common(g, da_ref[...])
        xd = xs * dtxv
        xb = xd.astype(BF16)
        gb = gy.astype(BF16)
        bb = b_ref[...]
        cc = c_ref[...]
        bb16, cc16 = bb.astype(BF16), cc.astype(BF16)
        cbm = lax.dot_general(cc16, bb16, (((1,), (1,)), ((), ())), preferred_element_type=F32)
        s_in = sin_ref[0]
        s_in16 = s_in.astype(BF16)
        ds_next = dstate[...]
        ds16 = ds_next.astype(BF16)
        ecs = jnp.exp(acs_x)
        dec = jnp.exp(atot_x - acs_x)
        etot = jnp.exp(atot_x)

        ddx_ref[...] += jnp.sum(gy * xs, axis=0, keepdims=True)
        z = jnp.dot(cc16, s_in16, preferred_element_type=F32)
        dz16 = (gy * ecs).astype(BF16)
        dacs = gy * z * ecs
        dcm = lax.dot_general(dz16, s_in16, (((1,), (1,)), ((), ())), preferred_element_type=F32)
        ds_here = jnp.dot(cc.T.astype(BF16), dz16, preferred_element_type=F32)
        u = jnp.dot(bb16, ds16, preferred_element_type=F32)
        dxd = u * dec
        w_dec = u * xd * dec
        dacs = dacs - w_dec
        datot = jnp.sum(w_dec, axis=0, keepdims=True) + jnp.sum(etot * ds_next * s_in, axis=0, keepdims=True)
        dbm = lax.dot_general((xd * dec).astype(BF16), ds16, (((1,), (1,)), ((), ())), preferred_element_type=F32)
        dstate[...] = etot * ds_next + ds_here

        lane = lax.broadcasted_iota(jnp.int32, (Q, 2 * HD), 1)
        lane_x = lax.broadcasted_iota(jnp.int32, (1, GW), 1)
        dcb = jnp.zeros((Q, Q), F32)
        dx_pairs = []
        for p in range(hpg // 2):
            sl = slice(2 * HD * p, 2 * HD * (p + 1))
            xp = xb[:, sl]
            gp = gb[:, sl]
            dxp = jnp.zeros((Q, 2 * HD), F32)
            for hh in range(2):
                r = 2 * p + hh
                lm = _decay_matrix(acs_col, g * hpg + r, tri)
                mm = cbm * lm
                half = (lane >= HD) if hh == 1 else (lane < HD)
                xm = jnp.where(half, xp, jnp.zeros_like(xp))
                gm = jnp.where(half, gp, jnp.zeros_like(gp))
                mt16 = mm.T.astype(BF16)
                dxp = dxp + jnp.dot(mt16, gm, preferred_element_type=F32)
                dm = lax.dot_general(gm, xp, (((1,), (1,)), ((), ())), preferred_element_type=F32)
                dmt = lax.dot_general(xm, gp, (((1,), (1,)), ((), ())), preferred_element_type=F32)
                dcb = dcb + dm * lm
                wdiff = jnp.sum(dm * mm - dmt * mm.T, axis=1, keepdims=True)
                dacs = dacs + wdiff * (lane_x == HD * r).astype(F32)
            dx_pairs.append(dxp)
        dxd = dxd + jnp.concatenate(dx_pairs, axis=1)
        dcm = dcm + jnp.dot(dcb.astype(BF16), bb16, preferred_element_type=F32)
        dbm = dbm + jnp.dot(dcb.T.astype(BF16), cc16, preferred_element_type=F32)
        db_ref[...] = dbm
        dc_ref[...] = dcm

        last = lax.broadcasted_iota(jnp.int32, (Q, GW), 0) == Q - 1
        dacs = dacs + jnp.where(last, datot, 0.0)
        ri = lax.broadcasted_iota(jnp.int32, (Q, Q), 0)
        ci = lax.broadcasted_iota(jnp.int32, (Q, Q), 1)
        dda = jnp.dot((ri <= ci).astype(F32), dacs, preferred_element_type=F32, precision=HI)
        ddt_ref[...] = dda * a_xv + dxd * xs
        dax_ref[...] += jnp.sum(dda * dtxv, axis=0, keepdims=True)
        dx_ref[...] = dxd * dtxv + gy * d_ref[...]

    rev = lambda g, c: (nc - 1 - c, g)
    return _pc(body, name=name, grid=(NG, nc),
               in_specs=[pl.BlockSpec((Q, GW), rev),
                         pl.BlockSpec((Q, NS), lambda g, c: (nc - 1 - c, DI // NS + g)),
                         pl.BlockSpec((Q, NS), lambda g, c: (nc - 1 - c, DI // NS + NG + g)),
                         pl.BlockSpec((Q, GW), rev),
                         pl.BlockSpec((Q, NHP), lambda g, c: (nc - 1 - c, 0)),
                         pl.BlockSpec((1, GW), lambda g, c: (0, g)),
                         pl.BlockSpec((1, GW), lambda g, c: (0, g)),
                         pl.BlockSpec((1, NS, GW), lambda g, c: (nc - 1 - c, 0, g)),
                         pl.BlockSpec((Q, GW), rev)],
               out_specs=[pl.BlockSpec((Q, GW), rev), pl.BlockSpec((Q, NS), rev), pl.BlockSpec((Q, NS), rev),
                          pl.BlockSpec((Q, GW), rev),
                          pl.BlockSpec((1, GW), lambda g, c: (0, g)), pl.BlockSpec((1, GW), lambda g, c: (0, g))],
               out_shape=[_sds((t, DI), F32), _sds((t, NG * NS), F32), _sds((t, NG * NS), F32), _sds((t, DI), F32),
                          _sds((1, DI), F32), _sds((1, DI), F32)],
               scratch=[pltpu.VMEM((NS, GW), F32)])(xbc, xbc, xbc, dtx, da, a_x, d_x, sin, dy)


def _gated_norm_fwd(y, z, gn, *, name):
    t = y.shape[0]
    tm = min(t, 256)

    def body(y_ref, z_ref, g_ref, o_ref):
        for k in range(NG):
            sl = slice(GW * k, GW * (k + 1))
            zz = z_ref[:, sl]
            h = y_ref[:, sl] * (zz * _sigmoid(zz))
            r = lax.rsqrt(jnp.mean(h * h, axis=-1, keepdims=True) + EPS)
            o_ref[:, sl] = (h * r * g_ref[:, sl]).astype(BF16)

    return _pc(body, name=name, grid=(t // tm,), in_specs=[_row(tm, DI), _row(tm, DI), _const((1, DI))],
               out_specs=_row(tm, DI), out_shape=_sds((t, DI), BF16))(y, z, gn.reshape(1, DI))


def _gated_norm_bwd(y, z, gn, dout, *, name):
    t = y.shape[0]
    tm = min(t, 256)

    def body(y_ref, z_ref, g_ref, do_ref, dy_ref, dz_ref, dg_ref):
        @pl.when(pl.program_id(0) == 0)
        def _():
            dg_ref[...] = jnp.zeros_like(dg_ref)

        for k in range(NG):
            sl = slice(GW * k, GW * (k + 1))
            zz = z_ref[:, sl]
            yy = y_ref[:, sl]
            sz = zz * _sigmoid(zz)
            h = yy * sz
            r = lax.rsqrt(jnp.mean(h * h, axis=-1, keepdims=True) + EPS)
            d = do_ref[:, sl]
            dg_ref[:, sl] += jnp.sum(d * h * r, axis=0, keepdims=True)
            dgd = d * g_ref[:, sl]
            dot = jnp.mean(dgd * h, axis=-1, keepdims=True)
            dh = r * dgd - h * (r * r * r * dot)
            dy_ref[:, sl] = dh * sz
            dz_ref[:, sl] = (dh * yy * _dsilu(zz)).astype(BF16)

    dy, dz, dg = _pc(body, name=name, grid=(t // tm,),
                     in_specs=[_row(tm, DI), _row(tm, DI), _const((1, DI)), _row(tm, DI)],
                     out_specs=[_row(tm, DI), _row(tm, DI), _const((1, DI))],
                     out_shape=[_sds((t, DI), F32), _sds((t, DI), BF16), _sds((1, DI), F32)])(y, z, gn.reshape(1, DI), dout)
    return dy, dz, dg.reshape(DI)


def _pad_heads(v):
    return jnp.pad(v.reshape(1, NH), ((0, 0), (0, NHP - NH)))


def _ssm_fwd(h, p):
    u = _rms_fwd(h, p["norm"], name="ssm_rms_fwd")
    z = _mm(u, p["wz"], "nn", name="ssm_in_z")
    xp = _mm(u, p["wxbc"], "nn", name="ssm_in_xbc")
    dtraw = _mm(u, p["wdt"], "nn", name="ssm_in_dt")
    xbc = _conv4_fwd(xp, p["conv_w"], p["conv_b"], name="ssm_conv_fwd")
    dtb, alog = _pad_heads(p["dt_bias"]), _pad_heads(p["a_log"])
    a_x = jnp.repeat(-jnp.exp(p["a_log"]), HD).reshape(1, DI)
    d_x = jnp.repeat(p["d"], HD).reshape(1, DI)
    dtx, da = _ssd_pre(dtraw, dtb, alog, name="ssd_pre")
    y, sin = _ssd_fwd(xbc, dtx, da, a_x, d_x, name="ssd_fwd")
    yn = _gated_norm_fwd(y, z, p["gate_norm"], name="ssm_gate_fwd")
    h2 = _mm(yn, p["wout"], "nn", res=h, name="ssm_out")
    return h2, (h, u, z, xp, dtraw, xbc, dtx, da, a_x, d_x, dtb, y, sin, yn)


def _ssm_bwd(dh2, p, saved):
    h, u, z, xp, dtraw, xbc, dtx, da, a_x, d_x, dtb, y, sin, yn = saved
    dyn = _mm(dh2, p["wout"], "nt", name="ssm_out_dx")
    g = {"w_out": _mm(yn, dh2, "tn", out_dtype=BF16, name="ssm_out_dw")}
    dy, dz, g["gate_norm"] = _gated_norm_bwd(y, z, p["gate_norm"], dyn, name="ssm_gate_bwd")
    dxs, dbm, dcm, ddtx, dax, ddx = _ssd_bwd(xbc, dtx, da, a_x, d_x, sin, dy, name="ssd_bwd")
    ddtraw, ddtb = _ssd_post(ddtx, dtraw, dtb, name="ssd_post")
    dpre, g["conv_w"], g["conv_b"] = _conv4_bwd_act(xp, p["conv_w"], p["conv_b"], jnp.concatenate([dxs, dbm, dcm], axis=1),
                                                    name="ssm_conv_bwd_act")
    dxp = _conv_bwd_x(dpre, p["conv_w"], KC, name="ssm_conv_bwd_x", out_dtype=BF16)
    du = _mm(dz, p["wz"], "nt", name="ssm_in_dx_z")
    du = _mm(dxp, p["wxbc"], "nt", res=du, name="ssm_in_dx_xbc")
    du = _mm(ddtraw, p["wdt"], "nt", res=du, name="ssm_in_dx_dt")
    g["w_in"] = jnp.concatenate([_mm(u, dz, "tn", out_dtype=BF16, name="ssm_in_dw_z"),
                                 _mm(u, dxp, "tn", out_dtype=BF16, name="ssm_in_dw_xbc"),
                                 _mm(u, ddtraw, "tn", out_dtype=BF16, name="ssm_in_dw_dt")[:, :NH]], axis=1)
    dh, g["norm"] = _rms_bwd(h, p["norm"], du, dh2, name="ssm_rms_bwd")
    g["dt_bias"] = ddtb[0, :NH]
    g["a_log"] = dax.reshape(NH, HD).sum(-1) * (-jnp.exp(p["a_log"]))
    g["d"] = ddx.reshape(NH, HD).sum(-1)
    return dh, g


def _cv_fwd(h, p):
    u = _rms_fwd(h, p["norm"], name="cv_rms_fwd")
    pre = _mm(u, p["wpw1"], "nn", bias=p["b_pw1"].reshape(1, 2 * D), name="cv_pw1")
    gl = _glu_fwd(pre, name="cv_glu_fwd")
    c, s = _dwconv_ln_fwd(gl, p["dw_w"], p["dw_b"], p["ln_g"], p["ln_b"], name="cv_dwconv_ln_fwd")
    h2 = _mm(s, p["wpw2"], "nn", bias=p["b_pw2"].reshape(1, D), res=h, name="cv_pw2")
    return h2, (h, u, pre, gl, c, s)


def _cv_bwd(dh2, p, saved):
    h, u, pre, gl, c, s = saved
    ds = _mm(dh2, p["wpw2"], "nt", name="cv_pw2_dx")
    g = {"w_pw2": _mm(s, dh2, "tn", out_dtype=BF16, name="cv_pw2_dw"), "b_pw2": _colsum(dh2, name="cv_pw2_db")}
    dc, g["ln_g"], g["ln_b"] = _ln_silu_bwd(c, p["ln_g"], p["ln_b"], ds, name="cv_ln_bwd")
    dgl = _conv_bwd_x(dc, p["dw_w"], CK, name="cv_dwconv_bwd_x", out_dtype=F32)
    g["dw_w"], g["dw_b"] = _dwconv_bwd_w(gl, dc, name="cv_dwconv_bwd_w")
    dpre, g["b_pw1"] = _glu_bwd(pre, dgl, name="cv_glu_bwd")
    du = _mm(dpre, p["wpw1"], "nt", name="cv_pw1_dx")
    g["w_pw1"] = _mm(u, dpre, "tn", out_dtype=BF16, name="cv_pw1_dw")
    dh, g["norm"] = _rms_bwd(h, p["norm"], du, dh2, name="cv_rms_bwd")
    return dh, g


def _ffn_fwd(h, p):
    u = _rms_fwd(h, p["norm"], name="ffn_rms_fwd")
    gu = _mm(u, p["wgu"], "nn", name="ffn_gate_up")
    a = _swiglu_fwd(gu, name="ffn_act_fwd")
    h2 = _mm(a, p["wd"], "nn", res=h, name="ffn_down")
    return h2, (h, u, gu, a)


def _ffn_bwd(dh2, p, saved):
    h, u, gu, a = saved
    da = _mm(dh2, p["wd"], "nt", name="ffn_down_dx")
    g = {"w_down": _mm(a, dh2, "tn", out_dtype=BF16, name="ffn_down_dw")}
    dgu = _swiglu_bwd(gu, da, name="ffn_act_bwd")
    du = _mm(dgu, p["wgu"], "nt", name="ffn_gate_up_dx")
    dwgu = _mm(u, dgu, "tn", out_dtype=BF16, name="ffn_gate_up_dw")
    g["w_gate"], g["w_up"] = dwgu[:, :DFF], dwgu[:, DFF:]
    dh, g["norm"] = _rms_bwd(h, p["norm"], du, dh2, name="ffn_rms_bwd")
    return dh, g


def _fwd_bwd(x, target, ssm, cv, ffn, final_norm):
    depth = len(ffn)
    h, tape = x, []
    for i in range(depth):
        h, s_mix = (_ssm_fwd if i % 2 == 0 else _cv_fwd)(h, (ssm if i % 2 == 0 else cv)[i // 2])
        h, s_ffn = _ffn_fwd(h, ffn[i])
        tape.append((s_mix, s_ffn))
    lsum, dh, g_final = _final_loss(h, final_norm, target, name="loss_head")
    g_ssm, g_cv, g_ffn = [None] * len(ssm), [None] * len(cv), [None] * depth
    for i in reversed(range(depth)):
        s_mix, s_ffn = tape[i]
        dh, g_ffn[i] = _ffn_bwd(dh, ffn[i], s_ffn)
        if i % 2 == 0:
            dh, g_ssm[i // 2] = _ssm_bwd(dh, ssm[i // 2], s_mix)
        else:
            dh, g_cv[i // 2] = _cv_bwd(dh, cv[i // 2], s_mix)
    return lsum, dh, g_ssm, g_cv, g_ffn, g_final


ANY = pl.BlockSpec(memory_space=pl.ANY)


def _place():
    x, y, c = lax.axis_index("x"), lax.axis_index("y"), lax.axis_index("c")
    return x, y, c, [(1 - x, y), (x, 1 - y), (1 - x, 1 - y)]


def _remote(src, dst, ssem, rsem, dev):
    return pltpu.make_async_remote_copy(src_ref=src, dst_ref=dst, send_sem=ssem, recv_sem=rsem, device_id=dev,
                                        device_id_type=MESH)


def _comm_call(body, name, ins, out_shape, n_sems):
    return pl.pallas_call(
        body, name=name, in_specs=[ANY] * len(ins), out_specs=[ANY] * len(out_shape), out_shape=out_shape,
        scratch_shapes=[pltpu.SemaphoreType.DMA((k,)) for k in n_sems])(*ins)


def _gather_big(ws):
    n = len(ws)

    def body(*refs):
        ins, outs = refs[:n], refs[n:2 * n]
        ssem, rsem, fssem, frsem = refs[2 * n:]
        x, y, c, chips = _place()
        me, sib = 2 * x + y, (x, y, 1 - c)

        def half(a, cc):
            lh = ws[a].shape[0] // 2
            return pl.ds(cc * lh, lh)

        sends, fwds = [], []
        for a in range(n):
            for j, (px, py) in enumerate(chips):
                cp = _remote(ins[a].at[half(a, c)], outs[a].at[me, half(a, c)], ssem.at[3 * a + j], rsem.at[3 * a + j], (px, py, c))
                cp.start()
                sends.append(cp)
        for a in range(n):
            for j, (px, py) in enumerate(chips):
                blk = outs[a].at[2 * px + py, half(a, c)]
                _remote(blk, blk, ssem.at[3 * a + j], rsem.at[3 * a + j], (px, py, c)).wait_recv()
                f = _remote(blk, blk, fssem.at[3 * a + j], frsem.at[3 * a + j], sib)
                f.start()
                fwds.append(f)
        for a in range(n):
            for j, (px, py) in enumerate(chips):
                blk = outs[a].at[2 * px + py, half(a, 1 - c)]
                _remote(blk, blk, fssem.at[3 * a + j], frsem.at[3 * a + j], sib).wait_recv()
        for cp in sends + fwds:
            cp.wait_send()

    return _comm_call(body, "gather_weights", ws, [_sds((4,) + w.shape, w.dtype) for w in ws], [3 * n, 3 * n, 3 * n, 3 * n])


def _exchange_pairs(gs):
    n = len(gs)

    def body(*refs):
        ins, got = refs[:n], refs[n:2 * n]
        ssem, rsem = refs[2 * n:]
        x, y, c, _ = _place()
        cps = []
        for a in range(n):
            lh = gs[a].shape[1] // 2
            cps.append(_remote(ins[a].at[:, pl.ds((1 - c) * lh, lh)], got[a], ssem.at[a], rsem.at[a], (x, y, 1 - c)))
        for cp in cps:
            cp.start()
        for cp in cps:
            cp.wait()

    halves = [_sds((4, g.shape[1] // 2) + g.shape[2:], g.dtype) for g in gs]
    return _comm_call(body, "reduce_pair_exchange", gs, halves, [n, n])


def _exchange_chips(ps):
    n = len(ps)

    def body(*refs):
        ins, outs = refs[:n], refs[n:2 * n]
        ssem, rsem = refs[2 * n:]
        x, y, c, chips = _place()
        me = 2 * x + y
        cps = []
        for a in range(n):
            for j, (px, py) in enumerate(chips):
                cps.append(_remote(ins[a].at[2 * px + py], outs[a].at[me], ssem.at[3 * a + j], rsem.at[3 * a + j], (px, py, c)))
        for cp in cps:
            cp.start()
        for a in range(n):
            for j, (px, py) in enumerate(chips):
                blk = outs[a].at[2 * px + py]
                _remote(blk, blk, ssem.at[3 * a + j], rsem.at[3 * a + j], (px, py, c)).wait_recv()
        for cp in cps:
            cp.wait_send()

    return _comm_call(body, "reduce_chip_exchange", ps, [_sds(p.shape, p.dtype) for p in ps], [3 * n, 3 * n])


def _exchange_final(fs):
    n = len(fs)

    def body(*refs):
        ins, outs = refs[:n], refs[n:2 * n]
        ssem, rsem = refs[2 * n:]
        x, y, c, _ = _place()
        cps = [_remote(ins[a], outs[a], ssem.at[a], rsem.at[a], (x, y, 1 - c)) for a in range(n)]
        for cp in cps:
            cp.start()
        for cp in cps:
            cp.wait()

    return _comm_call(body, "reduce_final_exchange", fs, [_sds(f.shape, f.dtype) for f in fs], [n, n])


def _rows_tile(r, c, n_arrays, itemsize=4):
    cap = max(16, (16 * 1024 * 1024) // (2 * n_arrays * c * itemsize))
    best = 16 if r % 16 == 0 else r
    for t in range(16, min(r, cap) + 1, 16):
        if r % t == 0:
            best = t
    return best


def _add_pair(g, got, core, *, name):
    _, nl, r, c = g.shape
    rows = (nl // 2) * r
    tm = _rows_tile(rows, c, 3)

    def body(core_ref, p_ref, q_ref, o_ref):
        o_ref[...] = (p_ref[...].astype(F32) + q_ref[...].astype(F32)).astype(BF16)

    spec = pl.BlockSpec((None, tm, c), lambda s, i, core_ref: (s, i, 0))
    grid_spec = pltpu.PrefetchScalarGridSpec(
        num_scalar_prefetch=1, grid=(4, rows // tm),
        in_specs=[pl.BlockSpec((None, None, tm, c), lambda s, i, core_ref: (s, core_ref[0], i, 0)), spec], out_specs=spec)
    out = pl.pallas_call(body, name=name, grid_spec=grid_spec, out_shape=_sds((4, rows, c), BF16),
                         compiler_params=pltpu.CompilerParams(dimension_semantics=("arbitrary", "arbitrary"),
                                                              vmem_limit_bytes=VMEM_LIMIT))(
                                                                  core, g.reshape(4, 2, rows, c), got.reshape(4, rows, c))
    return out.reshape(got.shape)


def _sum_slots(rb, *, name):
    shape = rb.shape[1:]
    c = shape[-1]
    r = rb.size // (4 * c)
    tm = _rows_tile(r, c, 6)

    def body(i_ref, o_ref):
        acc = i_ref[0].astype(F32)
        for k in range(1, 4):
            acc = acc + i_ref[k].astype(F32)
        o_ref[...] = acc

    return _pc(body, name=name, grid=(r // tm,), in_specs=[pl.BlockSpec((4, tm, c), lambda i: (0, i, 0))],
               out_specs=_row(tm, c), out_shape=_sds((r, c), F32))(rb.reshape(4, r, c)).reshape(shape)


def _reduce_big(gs, chip, core):
    got = _exchange_pairs(gs)
    core1 = core.reshape(1).astype(jnp.int32)
    ps = [_add_pair(g, q, core1, name="reduce_pair_add") for g, q in zip(gs, got)]
    rb = _exchange_chips(ps)
    rb = [lax.dynamic_update_index_in_dim(b, lax.dynamic_index_in_dim(p, chip, 0, keepdims=False), chip, 0)
          for b, p in zip(rb, ps)]
    fs = [_sum_slots(b, name="reduce_chip_sum") for b in rb]
    theirs = _exchange_final(fs)
    return [jnp.where(core == 0, jnp.concatenate([f, t], axis=0), jnp.concatenate([t, f], axis=0))
            for f, t in zip(fs, theirs)]


def _gather8(v, reduce):
    m = v.shape[0]

    def body(x_ref, *rest):
        if reduce:
            sum_ref, out_ref, send_sems, recv_sems, local_sem = rest
        else:
            out_ref, send_sems, recv_sems, local_sem = rest
        x, y, c, chips = _place()
        me, sibling = (x, y, c), (x, y, 1 - c)

        def rows(px, py, pc):
            return out_ref.at[pl.ds((4 * px + 2 * py + pc) * m, m), :]

        def copy(k, block, to, src=None):
            return _remote(rows(*block) if src is None else src, rows(*block), send_sems.at[k], recv_sems.at[k], to)

        mine = pltpu.make_async_copy(x_ref, rows(*me), local_sem)
        mine.start()
        first = [copy(0, me, sibling, src=x_ref)]
        first += [copy(1 + j, me, (*chip, c), src=x_ref) for j, chip in enumerate(chips)]
        for cp in first:
            cp.start()
        passed = [copy(4 + j, (*chip, c), sibling) for j, chip in enumerate(chips)]
        for j, chip in enumerate(chips):
            copy(1 + j, (*chip, c), me).wait_recv()
            passed[j].start()
        copy(0, sibling, me).wait_recv()
        for j, chip in enumerate(chips):
            copy(4 + j, (*chip, 1 - c), me).wait_recv()
        for cp in first + passed:
            cp.wait_send()
        mine.wait()
        if reduce:
            acc = out_ref[pl.ds(0, m), :]
            for k in range(1, 8):
                acc = acc + out_ref[pl.ds(k * m, m), :]
            sum_ref[...] = acc

    vm = pl.BlockSpec(memory_space=pltpu.VMEM)
    out_shape = [_sds((8 * m, 128), F32)]
    if reduce:
        out_shape = [_sds((m, 128), F32)] + out_shape
    out = pl.pallas_call(
        body, name="allreduce_small" if reduce else "gather_small", in_specs=[vm], out_specs=[vm] * len(out_shape),
        out_shape=out_shape,
        scratch_shapes=[pltpu.SemaphoreType.DMA((7,)), pltpu.SemaphoreType.DMA((7,)), pltpu.SemaphoreType.DMA(())],
        compiler_params=pltpu.CompilerParams(vmem_limit_bytes=VMEM_LIMIT))(v)
    return out[0]


def _adamw(w, g, m, v, *, name):
    shape = w.shape
    c = shape[-1]
    r = w.size // c
    tm = _rows_tile(r, c, 7) if r % 16 == 0 else r

    def body(w_ref, g_ref, m_ref, v_ref, d_ref, mo_ref, vo_ref):
        gg = g_ref[...]
        mm = B1 * m_ref[...] + (1.0 - B1) * gg
        vv = B2 * v_ref[...] + (1.0 - B2) * (gg * gg)
        mo_ref[...] = mm
        vo_ref[...] = vv
        m_hat = mm / (1.0 - B1 ** STEP)
        v_hat = vv / (1.0 - B2 ** STEP)
        d_ref[...] = -LR * (m_hat / (jnp.sqrt(v_hat) + AEPS) + WD * w_ref[...])

    outs = _pc(body, name=name, grid=(r // tm,), in_specs=[_row(tm, c)] * 4, out_specs=[_row(tm, c)] * 3,
               out_shape=[_sds((r, c), F32)] * 3)(*[t.reshape(r, c) for t in (w, g, m, v)])
    return [o.reshape(shape) for o in outs]


WEIGHTS = ["ssm_norm", "ssm_w_in", "ssm_conv_w", "ssm_conv_b", "ssm_dt_bias", "ssm_a_log", "ssm_d", "ssm_gate_norm",
           "ssm_w_out", "cv_norm", "cv_w_pw1", "cv_b_pw1", "cv_dw_w", "cv_dw_b", "cv_ln_g", "cv_ln_b", "cv_w_pw2",
           "cv_b_pw2", "ffn_norm", "ffn_w_gate", "ffn_w_up", "ffn_w_down", "final_norm"]
BIG = {"ssm_w_in": "col", "ssm_w_out": "row", "cv_w_pw1": "col", "cv_w_pw2": "row",
       "ffn_w_gate": "col", "ffn_w_up": "col", "ffn_w_down": "row"}
SMALL_SHARDED = {"ssm_conv_w": 2, "cv_norm": 1, "cv_b_pw1": 1, "cv_dw_w": 2, "cv_dw_b": 1, "cv_ln_g": 1, "cv_ln_b": 1,
                 "cv_b_pw2": 1}
SMALL = [n for n in WEIGHTS if n not in BIG]
LANES = 128


def _pack(arrays):
    flat = jnp.concatenate([a.reshape(-1) for a in arrays])
    rows = -(-flat.size // (8 * LANES)) * 8
    return jnp.pad(flat, (0, rows * LANES - flat.size)).reshape(rows, LANES)


def _unpack(packed, shapes):
    flat, out, off = packed.reshape(-1), [], 0
    for s in shapes:
        n = 1
        for d_ in s:
            n *= d_
        out.append(flat[off:off + n].reshape(s))
        off += n
    return out


def _unshard(parts, axis):
    return jnp.concatenate(parts, axis=axis)


def kernel(x, ssm_norm, ssm_w_in, ssm_conv_w, ssm_conv_b, ssm_dt_bias, ssm_a_log, ssm_d, ssm_gate_norm, ssm_w_out, cv_norm, cv_w_pw1, cv_b_pw1, cv_dw_w, cv_dw_b, cv_ln_g, cv_ln_b, cv_w_pw2, cv_b_pw2, ffn_norm, ffn_w_gate, ffn_w_up, ffn_w_down, final_norm, loss_target, m_ssm_norm, m_ssm_w_in, m_ssm_conv_w, m_ssm_conv_b, m_ssm_dt_bias, m_ssm_a_log, m_ssm_d, m_ssm_gate_norm, m_ssm_w_out, m_cv_norm, m_cv_w_pw1, m_cv_b_pw1, m_cv_dw_w, m_cv_dw_b, m_cv_ln_g, m_cv_ln_b, m_cv_w_pw2, m_cv_b_pw2, m_ffn_norm, m_ffn_w_gate, m_ffn_w_up, m_ffn_w_down, m_final_norm, v_ssm_norm, v_ssm_w_in, v_ssm_conv_w, v_ssm_conv_b, v_ssm_dt_bias, v_ssm_a_log, v_ssm_d, v_ssm_gate_norm, v_ssm_w_out, v_cv_norm, v_cv_w_pw1, v_cv_b_pw1, v_cv_dw_w, v_cv_dw_b, v_cv_ln_g, v_cv_ln_b, v_cv_w_pw2, v_cv_b_pw2, v_ffn_norm, v_ffn_w_gate, v_ffn_w_up, v_ffn_w_down, v_final_norm):
    a = dict(locals())
    chip = 2 * lax.axis_index("x") + lax.axis_index("y")
    n_ssm, n_cv, depth = ssm_norm.shape[0], cv_norm.shape[0], ffn_norm.shape[0]

    own = [a[n].astype(BF16) for n in BIG]
    gw = {n: lax.dynamic_update_index_in_dim(g, w, chip, 0) for n, g, w in zip(BIG, _gather_big(own), own)}
    sh_names = list(SMALL_SHARDED)
    got = _gather8(_pack([a[n] for n in sh_names]), reduce=False)
    got = got.reshape(8, -1)[0::2]
    per_chip = [_unpack(got[s], [a[n].shape for n in sh_names]) for s in range(4)]
    full = {n: a[n] for n in SMALL}
    for k, n in enumerate(sh_names):
        full[n] = _unshard([per_chip[s][k] for s in range(4)], SMALL_SHARDED[n])

    def whole(name, layer):
        parts = gw[name][:, layer]
        if BIG[name] == "col":
            return _unshard([parts[s] for s in range(4)], 1)
        return parts.reshape(-1, parts.shape[-1])

    ssm, cv, ffn = [], [], []
    for j in range(n_ssm):
        w_in = whole("ssm_w_in", j)
        ssm.append(dict(norm=full["ssm_norm"][j], wz=w_in[:, :DI], wxbc=w_in[:, DI:DI + CD],
                        wdt=jnp.pad(w_in[:, DI + CD:], ((0, 0), (0, NHP - NH))),
                        conv_w=full["ssm_conv_w"][j], conv_b=full["ssm_conv_b"][j], dt_bias=full["ssm_dt_bias"][j],
                        a_log=full["ssm_a_log"][j], d=full["ssm_d"][j], gate_norm=full["ssm_gate_norm"][j],
                        wout=whole("ssm_w_out", j)))
    for j in range(n_cv):
        cv.append(dict(norm=full["cv_norm"][j], wpw1=whole("cv_w_pw1", j), b_pw1=full["cv_b_pw1"][j],
                       dw_w=full["cv_dw_w"][j], dw_b=full["cv_dw_b"][j], ln_g=full["cv_ln_g"][j], ln_b=full["cv_ln_b"][j],
                       wpw2=whole("cv_w_pw2", j), b_pw2=full["cv_b_pw2"][j]))
    for i in range(depth):
        ffn.append(dict(norm=full["ffn_norm"][i],
                        wgu=jnp.concatenate([whole("ffn_w_gate", i), whole("ffn_w_up", i)], axis=1),
                        wd=whole("ffn_w_down", i)))

    lsum, grad_x, g_ssm, g_cv, g_ffn, g_final = _fwd_bwd(x[0], loss_target[0], ssm, cv, ffn, full["final_norm"])
    loss = (0.5 / D) * lax.psum(jnp.sum(lsum), ("x", "y", "c"))

    def slots(name, layers):
        if BIG[name] == "col":
            return jnp.stack([g.reshape(g.shape[0], 4, g.shape[1] // 4).transpose(1, 0, 2) for g in layers], axis=1)
        return jnp.stack([g.reshape(4, g.shape[0] // 4, g.shape[1]) for g in layers], axis=1)

    local_big = {"ssm_w_in": [g["w_in"] for g in g_ssm], "ssm_w_out": [g["w_out"] for g in g_ssm],
                 "cv_w_pw1": [g["w_pw1"] for g in g_cv], "cv_w_pw2": [g["w_pw2"] for g in g_cv],
                 "ffn_w_gate": [g["w_gate"] for g in g_ffn], "ffn_w_up": [g["w_up"] for g in g_ffn],
                 "ffn_w_down": [g["w_down"] for g in g_ffn]}
    grads = dict(zip(BIG, _reduce_big([slots(n, local_big[n]) for n in BIG], chip, lax.axis_index("c"))))

    local_small = {"final_norm": g_final, "ffn_norm": jnp.stack([g["norm"] for g in g_ffn])}
    for n in SMALL:
        if n.startswith("ssm_"):
            local_small[n] = jnp.stack([g[n[4:]] for g in g_ssm])
        elif n.startswith("cv_"):
            local_small[n] = jnp.stack([g[n[3:]] for g in g_cv])
    summed = _unpack(_gather8(_pack([local_small[n] for n in SMALL]), reduce=True), [full[n].shape for n in SMALL])
    for n, g in zip(SMALL, summed):
        if n in SMALL_SHARDED:
            ax = SMALL_SHARDED[n]
            g = lax.dynamic_slice_in_dim(g, chip * a[n].shape[ax], a[n].shape[ax], axis=ax)
        grads[n] = g

    delta, new_m, new_v = {}, {}, {}
    for n in BIG:
        delta[n], new_m[n], new_v[n] = _adamw(a[n], grads[n], a["m_" + n], a["v_" + n], name="adamw_" + n)
    shapes = [a[n].shape for n in SMALL]
    upd = _adamw(_pack([a[n] for n in SMALL]), _pack([grads[n] for n in SMALL]), _pack([a["m_" + n] for n in SMALL]),
                 _pack([a["v_" + n] for n in SMALL]), name="adamw_small")
    for dst, packed in zip((delta, new_m, new_v), upd):
        dst.update(zip(SMALL, _unpack(packed, shapes)))

    return (loss, grad_x[None], *[grads[n] for n in WEIGHTS], *[delta[n] for n in WEIGHTS],
            *[new_m[n] for n in WEIGHTS], *[new_v[n] for n in WEIGHTS])
```

```python
import functools

import jax
import jax.numpy as jnp
from jax import lax
from jax.experimental import pallas as pl
from jax.experimental.pallas import tpu as pltpu

F32 = jnp.float32
BF16 = jnp.bfloat16
HI = lax.Precision.HIGHEST
MESH = pl.DeviceIdType.MESH

D = 1024
DI = 2048
HD = 64
NH = 32
NG = 4
GW = DI // NG
NS = 128
KC = 4
CD = DI + 2 * NG * NS
Q = 128
DFF = 2816
CK = 31
EPS = 1e-5
NHP = 128

LR, B1, B2, AEPS, WD, STEP = 0.001, 0.9, 0.999, 1e-08, 0.01, 10

VMEM_LIMIT = 56 * 1024 * 1024
MM_VMEM_BUDGET = 36 * 1024 * 1024


def _pc(body, *, name, grid, in_specs, out_specs, out_shape, scratch=()):
    return pl.pallas_call(
        body, name=name, grid=grid, in_specs=in_specs, out_specs=out_specs, out_shape=out_shape,
        scratch_shapes=list(scratch),
        compiler_params=pltpu.CompilerParams(dimension_semantics=("arbitrary",) * len(grid),
                                             vmem_limit_bytes=VMEM_LIMIT))


def _sds(shape, dtype):
    return jax.ShapeDtypeStruct(tuple(shape), dtype)


def _row(tm, c):
    return pl.BlockSpec((tm, c), lambda i: (i, 0))


def _const(shape):
    return pl.BlockSpec(tuple(shape), lambda i: (0,) * len(shape))


def _tile(n, cap):
    if n <= cap:
        return n
    best = 128
    for t in range(128, cap + 1, 128):
        if n % t == 0:
            best = t
    return best


def _sigmoid(x):
    return 1.0 / (1.0 + jnp.exp(-x))


def _dsilu(x):
    s = _sigmoid(x)
    return s * (1.0 + x * (1.0 - s))


def _mm(a, b, mode, *, name, out_dtype=F32, bias=None, res=None, tm=1024, tn=1024):
    if mode == "nn":
        (m, k), (k2, n) = a.shape, b.shape
    elif mode == "nt":
        (m, k), (n, k2) = a.shape, b.shape
    else:
        (k, m), (k2, n) = a.shape, b.shape
    assert k == k2, (a.shape, b.shape, mode)
    tm = _tile(m, tm if k <= 2048 else 512)
    so = jnp.dtype(out_dtype).itemsize + (4 if res is not None else 0)
    fits = [c for c in range(128, min(n, 2048) + 1, 128) if n % c == 0 and
            2 * (tm * k * a.dtype.itemsize + c * k * b.dtype.itemsize + tm * c * so) <= MM_VMEM_BUDGET]
    tn = n if n <= 128 else max(fits)
    dn = {"nn": (((1,), (0,)), ((), ())), "nt": (((1,), (1,)), ((), ())), "tn": (((0,), (0,)), ((), ()))}[mode]
    nb, nr = bias is not None, res is not None

    def body(*refs):
        acc = lax.dot_general(refs[0][...].astype(BF16), refs[1][...].astype(BF16), dn, preferred_element_type=F32)
        if nb:
            acc = acc + refs[2][...]
        if nr:
            acc = acc + refs[2 + nb][...]
        refs[-1][...] = acc.astype(out_dtype)

    a_spec = pl.BlockSpec((k, tm), lambda i, j: (0, i)) if mode == "tn" else pl.BlockSpec((tm, k), lambda i, j: (i, 0))
    b_spec = pl.BlockSpec((tn, k), lambda i, j: (j, 0)) if mode == "nt" else pl.BlockSpec((k, tn), lambda i, j: (0, j))
    ins, specs = [a, b], [a_spec, b_spec]
    if nb:
        ins.append(bias)
        specs.append(pl.BlockSpec((1, tn), lambda i, j: (0, j)))
    if nr:
        ins.append(res)
        specs.append(pl.BlockSpec((tm, tn), lambda i, j: (i, j)))
    return _pc(body, name=name, grid=(m // tm, n // tn), in_specs=specs,
               out_specs=pl.BlockSpec((tm, tn), lambda i, j: (i, j)), out_shape=_sds((m, n), out_dtype))(*ins)


def _rms_fwd(h, g, *, name):
    t = h.shape[0]
    tm = min(t, 512)

    def body(h_ref, g_ref, u_ref):
        x = h_ref[...]
        r = lax.rsqrt(jnp.mean(x * x, axis=-1, keepdims=True) + EPS)
        u_ref[...] = (x * r * g_ref[...]).astype(BF16)

    return _pc(body, name=name, grid=(t // tm,), in_specs=[_row(tm, D), _const((1, D))], out_specs=_row(tm, D),
               out_shape=_sds((t, D), BF16))(h, g.reshape(1, D))


def _rms_bwd(h, g, du, dres, *, name):
    t = h.shape[0]
    tm = min(t, 512)

    def body(h_ref, g_ref, du_ref, dres_ref, dh_ref, dg_ref):
        x = h_ref[...]
        r = lax.rsqrt(jnp.mean(x * x, axis=-1, keepdims=True) + EPS)
        dy = du_ref[...]
        dyg = dy * g_ref[...]
        dot = jnp.mean(dyg * x, axis=-1, keepdims=True)
        dh_ref[...] = dres_ref[...] + r * dyg - x * (r * r * r * dot)

        @pl.when(pl.program_id(0) == 0)
        def _():
            dg_ref[...] = jnp.zeros_like(dg_ref)

        dg_ref[...] += jnp.sum(dy * x * r, axis=0, keepdims=True)

    dh, dg = _pc(body, name=name, grid=(t // tm,),
                 in_specs=[_row(tm, D), _const((1, D)), _row(tm, D), _row(tm, D)],
                 out_specs=[_row(tm, D), _const((1, D))],
                 out_shape=[_sds((t, D), F32), _sds((1, D), F32)])(h, g.reshape(1, D), du, dres)
    return dh, dg.reshape(D)


def _final_loss(h, g, target, *, name):
    t = h.shape[0]
    tm = min(t, 512)

    def body(h_ref, g_ref, t_ref, dh_ref, dg_ref, l_ref):
        x = h_ref[...]
        gg = g_ref[...]
        r = lax.rsqrt(jnp.mean(x * x, axis=-1, keepdims=True) + EPS)
        err = x * r * gg - t_ref[...]
        dy = err * (1.0 / D)
        dyg = dy * gg
        dot = jnp.mean(dyg * x, axis=-1, keepdims=True)
        dh_ref[...] = r * dyg - x * (r * r * r * dot)

        @pl.when(pl.program_id(0) == 0)
        def _():
            dg_ref[...] = jnp.zeros_like(dg_ref)
            l_ref[...] = jnp.zeros_like(l_ref)

        dg_ref[...] += jnp.sum(dy * x * r, axis=0, keepdims=True)
        l_ref[...] += jnp.sum(err * err, axis=0, keepdims=True)

    dh, dg, l = _pc(body, name=name, grid=(t // tm,),
                    in_specs=[_row(tm, D), _const((1, D)), _row(tm, D)],
                    out_specs=[_row(tm, D), _const((1, D)), _const((1, D))],
                    out_shape=[_sds((t, D), F32), _sds((1, D), F32), _sds((1, D), F32)])(h, g.reshape(1, D), target)
    return l, dh, dg.reshape(D)


def _swiglu_fwd(gu, *, name):
    t = gu.shape[0]
    tm = min(t, 256)

    def body(gu_ref, a_ref):
        gate = gu_ref[:, :DFF]
        up = gu_ref[:, DFF:]
        a_ref[...] = (gate * _sigmoid(gate) * up).astype(BF16)

    return _pc(body, name=name, grid=(t // tm,), in_specs=[_row(tm, 2 * DFF)], out_specs=_row(tm, DFF),
               out_shape=_sds((t, DFF), BF16))(gu)


def _swiglu_bwd(gu, da, *, name):
    t = gu.shape[0]
    tm = min(t, 256)

    def body(gu_ref, da_ref, d_ref):
        gate = gu_ref[:, :DFF]
        up = gu_ref[:, DFF:]
        dav = da_ref[...]
        d_ref[:, :DFF] = (dav * up * _dsilu(gate)).astype(BF16)
        d_ref[:, DFF:] = (dav * gate * _sigmoid(gate)).astype(BF16)

    return _pc(body, name=name, grid=(t // tm,), in_specs=[_row(tm, 2 * DFF), _row(tm, DFF)],
               out_specs=_row(tm, 2 * DFF), out_shape=_sds((t, 2 * DFF), BF16))(gu, da)


def _halo_before(tm, halo, tc):
    return pl.BlockSpec((halo, tc), lambda j, i: (jnp.maximum(i * (tm // halo) - 1, 0), j))


def _halo_after(tm, halo, tc, t):
    return pl.BlockSpec((halo, tc), lambda j, i: (jnp.minimum((i + 1) * (tm // halo), t // halo - 1), j))


def _conv4_fwd(xp, w, b, *, name):
    t = xp.shape[0]
    tm, tc, halo = min(t, 512), 1024, 8

    def body(x_ref, h_ref, w_ref, b_ref, o_ref, pad):
        i = pl.program_id(1)
        pad[pl.ds(0, halo), :] = jnp.where(i == 0, 0.0, h_ref[...])
        pad[pl.ds(halo, tm), :] = x_ref[...]
        acc = jnp.zeros((tm, tc), F32) + b_ref[...]
        for k in range(KC):
            acc = acc + pad[pl.ds(halo - (KC - 1) + k, tm), :] * w_ref[pl.ds(k, 1), :]
        o_ref[...] = acc * _sigmoid(acc)

    tile = pl.BlockSpec((tm, tc), lambda j, i: (i, j))
    return _pc(body, name=name, grid=(CD // tc, t // tm),
               in_specs=[tile, _halo_before(tm, halo, tc), pl.BlockSpec((KC, tc), lambda j, i: (0, j)),
                         pl.BlockSpec((1, tc), lambda j, i: (0, j))],
               out_specs=tile, out_shape=_sds((t, CD), F32), scratch=[pltpu.VMEM((halo + tm, tc), F32)])(xp, xp, w, b.reshape(1, CD))


def _conv4_bwd_act(xp, w, b, dout, *, name):
    t = xp.shape[0]
    tm, tc, halo = min(t, 512), 1024, 8

    def body(x_ref, h_ref, w_ref, b_ref, do_ref, dp_ref, dw_ref, db_ref, pad):
        i = pl.program_id(1)
        pad[pl.ds(0, halo), :] = jnp.where(i == 0, 0.0, h_ref[...])
        pad[pl.ds(halo, tm), :] = x_ref[...]
        acc = jnp.zeros((tm, tc), F32) + b_ref[...]
        for k in range(KC):
            acc = acc + pad[pl.ds(halo - (KC - 1) + k, tm), :] * w_ref[pl.ds(k, 1), :]
        dpre = do_ref[...] * _dsilu(acc)
        dp_ref[...] = dpre

        @pl.when(i == 0)
        def _():
            dw_ref[...] = jnp.zeros_like(dw_ref)
            db_ref[...] = jnp.zeros_like(db_ref)

        db_ref[...] += jnp.sum(dpre, axis=0, keepdims=True)
        for k in range(KC):
            dw_ref[pl.ds(k, 1), :] += jnp.sum(dpre * pad[pl.ds(halo - (KC - 1) + k, tm), :], axis=0, keepdims=True)

    tile = pl.BlockSpec((tm, tc), lambda j, i: (i, j))
    wspec = pl.BlockSpec((KC, tc), lambda j, i: (0, j))
    bspec = pl.BlockSpec((1, tc), lambda j, i: (0, j))
    dp, dw, db = _pc(body, name=name, grid=(CD // tc, t // tm),
                     in_specs=[tile, _halo_before(tm, halo, tc), wspec, bspec, tile],
                     out_specs=[tile, wspec, bspec],
                     out_shape=[_sds((t, CD), F32), _sds((KC, CD), F32), _sds((1, CD), F32)],
                     scratch=[pltpu.VMEM((halo + tm, tc), F32)])(xp, xp, w, b.reshape(1, CD), dout)
    return dp, dw, db.reshape(CD)


def _conv_bwd_x(dy, w, kk, *, name, out_dtype):
    t, c = dy.shape
    halo = 8 if kk <= 8 else 32
    tm, tc = min(t, 512), min(c, 1024)

    def body(y_ref, h_ref, w_ref, o_ref, pad):
        i = pl.program_id(1)
        pad[pl.ds(0, tm), :] = y_ref[...]
        pad[pl.ds(tm, halo), :] = jnp.where(i == t // tm - 1, 0.0, h_ref[...])
        acc = jnp.zeros((tm, tc), F32)
        for k in range(kk):
            acc = acc + pad[pl.ds(kk - 1 - k, tm), :] * w_ref[pl.ds(k, 1), :]
        o_ref[...] = acc.astype(out_dtype)

    tile = pl.BlockSpec((tm, tc), lambda j, i: (i, j))
    return _pc(body, name=name, grid=(c // tc, t // tm),
               in_specs=[tile, _halo_after(tm, halo, tc, t), pl.BlockSpec((kk, tc), lambda j, i: (0, j))],
               out_specs=tile, out_shape=_sds((t, c), out_dtype), scratch=[pltpu.VMEM((tm + halo, tc), F32)])(dy, dy, w)


def _glu_fwd(p, *, name):
    t = p.shape[0]
    tm = min(t, 512)

    def body(p_ref, o_ref):
        o_ref[...] = p_ref[:, :D] * _sigmoid(p_ref[:, D:])

    return _pc(body, name=name, grid=(t // tm,), in_specs=[_row(tm, 2 * D)], out_specs=_row(tm, D),
               out_shape=_sds((t, D), F32))(p)


def _glu_bwd(p, dg, *, name):
    t = p.shape[0]
    tm = min(t, 512)

    def body(p_ref, dg_ref, dp_ref, db_ref):
        a = p_ref[:, :D]
        s = _sigmoid(p_ref[:, D:])
        d = dg_ref[...]
        da = d * s
        dgate = d * a * s * (1.0 - s)
        dp_ref[:, :D] = da.astype(BF16)
        dp_ref[:, D:] = dgate.astype(BF16)

        @pl.when(pl.program_id(0) == 0)
        def _():
            db_ref[...] = jnp.zeros_like(db_ref)

        db_ref[:, :D] += jnp.sum(da, axis=0, keepdims=True)
        db_ref[:, D:] += jnp.sum(dgate, axis=0, keepdims=True)

    dp, db = _pc(body, name=name, grid=(t // tm,), in_specs=[_row(tm, 2 * D), _row(tm, D)],
                 out_specs=[_row(tm, 2 * D), _const((1, 2 * D))],
                 out_shape=[_sds((t, 2 * D), BF16), _sds((1, 2 * D), F32)])(p, dg)
    return dp, db.reshape(2 * D)


def _dwconv_ln_fwd(g, w, b, lg, lb, *, name):
    t = g.shape[0]
    tm, halo = min(t, 256), 32

    def body(x_ref, h_ref, w_ref, b_ref, lg_ref, lb_ref, c_ref, s_ref, pad):
        i = pl.program_id(0)
        pad[pl.ds(0, halo), :] = jnp.where(i == 0, 0.0, h_ref[...])
        pad[pl.ds(halo, tm), :] = x_ref[...]
        acc = jnp.zeros((tm, D), F32) + b_ref[...]
        for k in range(CK):
            acc = acc + pad[pl.ds(halo - (CK - 1) + k, tm), :] * w_ref[pl.ds(k, 1), :]
        c_ref[...] = acc
        mu = jnp.mean(acc, axis=-1, keepdims=True)
        xc = acc - mu
        r = lax.rsqrt(jnp.mean(xc * xc, axis=-1, keepdims=True) + EPS)
        n = xc * r * lg_ref[...] + lb_ref[...]
        s_ref[...] = (n * _sigmoid(n)).astype(BF16)

    c, s = _pc(body, name=name, grid=(t // tm,),
               in_specs=[_row(tm, D), pl.BlockSpec((halo, D), lambda i: (jnp.maximum(i * (tm // halo) - 1, 0), 0)),
                         _const((CK, D)), _const((1, D)), _const((1, D)), _const((1, D))],
               out_specs=[_row(tm, D), _row(tm, D)], out_shape=[_sds((t, D), F32), _sds((t, D), BF16)],
               scratch=[pltpu.VMEM((halo + tm, D), F32)])(g, g, w, b.reshape(1, D), lg.reshape(1, D), lb.reshape(1, D))
    return c, s


def _ln_silu_bwd(c, lg, lb, ds, *, name):
    t = c.shape[0]
    tm = min(t, 512)

    def body(c_ref, lg_ref, lb_ref, ds_ref, dc_ref, dlg_ref, dlb_ref):
        x = c_ref[...]
        mu = jnp.mean(x, axis=-1, keepdims=True)
        xc = x - mu
        r = lax.rsqrt(jnp.mean(xc * xc, axis=-1, keepdims=True) + EPS)
        xh = xc * r
        n = xh * lg_ref[...] + lb_ref[...]
        dn = ds_ref[...] * _dsilu(n)
        dxh = dn * lg_ref[...]
        m1 = jnp.mean(dxh, axis=-1, keepdims=True)
        m2 = jnp.mean(dxh * xh, axis=-1, keepdims=True)
        dc_ref[...] = r * (dxh - m1 - xh * m2)

        @pl.when(pl.program_id(0) == 0)
        def _():
            dlg_ref[...] = jnp.zeros_like(dlg_ref)
            dlb_ref[...] = jnp.zeros_like(dlb_ref)

        dlg_ref[...] += jnp.sum(dn * xh, axis=0, keepdims=True)
        dlb_ref[...] += jnp.sum(dn, axis=0, keepdims=True)

    dc, dlg, dlb = _pc(body, name=name, grid=(t // tm,),
                       in_specs=[_row(tm, D), _const((1, D)), _const((1, D)), _row(tm, D)],
                       out_specs=[_row(tm, D), _const((1, D)), _const((1, D))],
                       out_shape=[_sds((t, D), F32), _sds((1, D), F32), _sds((1, D), F32)])(
                           c, lg.reshape(1, D), lb.reshape(1, D), ds)
    return dc, dlg.reshape(D), dlb.reshape(D)


def _dwconv_bwd_w(g, dc, *, name):
    t = g.shape[0]
    tm, tc, halo = min(t, 512), 1024, 32

    def body(x_ref, h_ref, dc_ref, dw_ref, db_ref, pad):
        i = pl.program_id(1)
        pad[pl.ds(0, halo), :] = jnp.where(i == 0, 0.0, h_ref[...])
        pad[pl.ds(halo, tm), :] = x_ref[...]
        d = dc_ref[...]

        @pl.when(i == 0)
        def _():
            dw_ref[...] = jnp.zeros_like(dw_ref)
            db_ref[...] = jnp.zeros_like(db_ref)

        db_ref[...] += jnp.sum(d, axis=0, keepdims=True)
        for k in range(CK):
            dw_ref[pl.ds(k, 1), :] += jnp.sum(d * pad[pl.ds(halo - (CK - 1) + k, tm), :], axis=0, keepdims=True)

    tile = pl.BlockSpec((tm, tc), lambda j, i: (i, j))
    dw, db = _pc(body, name=name, grid=(D // tc, t // tm),
                 in_specs=[tile, _halo_before(tm, halo, tc), tile],
                 out_specs=[pl.BlockSpec((CK, tc), lambda j, i: (0, j)), pl.BlockSpec((1, tc), lambda j, i: (0, j))],
                 out_shape=[_sds((CK, D), F32), _sds((1, D), F32)],
                 scratch=[pltpu.VMEM((halo + tm, tc), F32)])(g, g, dc)
    return dw, db.reshape(D)


def _colsum(x, *, name):
    t, c = x.shape
    tm = min(t, 512)

    def body(x_ref, o_ref):
        @pl.when(pl.program_id(0) == 0)
        def _():
            o_ref[...] = jnp.zeros_like(o_ref)

        o_ref[...] += jnp.sum(x_ref[...], axis=0, keepdims=True)

    return _pc(body, name=name, grid=(t // tm,), in_specs=[_row(tm, c)], out_specs=_const((1, c)),
               out_shape=_sds((1, c), F32))(x).reshape(c)


def _head_expand(n_lanes, first_head=0):
    h = lax.broadcasted_iota(jnp.int32, (NHP, n_lanes), 0)
    j = lax.broadcasted_iota(jnp.int32, (NHP, n_lanes), 1)
    return (h == first_head + j // HD).astype(F32)


def _softplus(v):
    return jnp.maximum(v, 0.0) + jnp.log(1.0 + jnp.exp(-jnp.abs(v)))


def _chunk_tri(n, upper):
    ri = lax.broadcasted_iota(jnp.int32, (n, n), 0)
    ci = lax.broadcasted_iota(jnp.int32, (n, n), 1)
    keep = (ri <= ci) if upper else (ri >= ci)
    return (keep & (ri // Q == ci // Q)).astype(F32)


def _split3(x):
    hi = x.astype(BF16)
    r1 = x - hi.astype(F32)
    mid = r1.astype(BF16)
    return hi, mid, (r1 - mid.astype(F32)).astype(BF16)


def _dot_sel(x, sel, sel_left=False):
    s = sel.astype(BF16)
    parts = [jnp.dot(s, t, preferred_element_type=F32) if sel_left else jnp.dot(t, s, preferred_element_type=F32)
             for t in _split3(x)]
    return (parts[2] + parts[1]) + parts[0]


def _ssd_pre(dtraw, dtb, alog, *, name):
    t = dtraw.shape[0]
    tm = min(t, 256)

    def body(r_ref, b_ref, al_ref, dtx_ref, acsx_ref, acst_ref):
        dt = _softplus(r_ref[...] + b_ref[...])
        da = dt * (-jnp.exp(al_ref[...]))
        acs = _dot_sel(da, _chunk_tri(tm, False), sel_left=True)
        expand = _head_expand(DI)
        dtx_ref[...] = _dot_sel(dt, expand)
        acsx_ref[...] = _dot_sel(acs, expand)
        acst_ref[...] = acs.T

    return _pc(body, name=name, grid=(t // tm,), in_specs=[_row(tm, NHP), _const((1, NHP)), _const((1, NHP))],
               out_specs=[_row(tm, DI), _row(tm, DI), pl.BlockSpec((NHP, tm), lambda i: (0, i))],
               out_shape=[_sds((t, DI), F32), _sds((t, DI), F32), _sds((NHP, t), F32)])(dtraw, dtb, alog)


def _ssd_post(dacs, dxdt, dtraw, dtb, alog, *, name):
    t = dtraw.shape[0]
    tm = min(t, 256)

    def body(g1_ref, g2_ref, r_ref, b_ref, al_ref, o_ref, db_ref, da_ref):
        h = lax.broadcasted_iota(jnp.int32, (DI, NHP), 1)
        j = lax.broadcasted_iota(jnp.int32, (DI, NHP), 0)
        red = (h == j // HD).astype(F32)
        dda = _dot_sel(_dot_sel(g1_ref[...], red), _chunk_tri(tm, True), sel_left=True)
        v = r_ref[...] + b_ref[...]
        ddt = dda * (-jnp.exp(al_ref[...])) + _dot_sel(g2_ref[...], red)
        draw = ddt * _sigmoid(v)
        o_ref[...] = draw.astype(BF16)

        @pl.when(pl.program_id(0) == 0)
        def _():
            db_ref[...] = jnp.zeros_like(db_ref)
            da_ref[...] = jnp.zeros_like(da_ref)

        db_ref[...] += jnp.sum(draw, axis=0, keepdims=True)
        da_ref[...] += jnp.sum(dda * _softplus(v), axis=0, keepdims=True)

    return _pc(body, name=name, grid=(t // tm,),
               in_specs=[_row(tm, DI), _row(tm, DI), _row(tm, NHP), _const((1, NHP)), _const((1, NHP))],
               out_specs=[_row(tm, NHP), _const((1, NHP)), _const((1, NHP))],
               out_shape=[_sds((t, NHP), BF16), _sds((1, NHP), F32), _sds((1, NHP), F32)])(dacs, dxdt, dtraw, dtb, alog)


def _decay_pair(acs_x, acst_ref, p, tri):
    lane = lax.broadcasted_iota(jnp.int32, (Q, 2 * HD), 1)
    v = acs_x[:, 2 * HD * p:2 * HD * (p + 1)]
    swapped = pltpu.roll(v, HD, axis=1)
    out = []
    for hh in range(2):
        colb = jnp.where((lane < HD) if hh == 0 else (lane >= HD), v, swapped)
        rowb = acst_ref[pl.ds(2 * p + hh, 1), :]
        out.append(jnp.exp(jnp.where(tri, colb - rowb, -1e30)))
    return out


def _ssd_fwd(xbc, dtx, acs_x, acs_t, d_x, *, name):
    t = xbc.shape[0]
    nc = t // Q
    hpg = NH // NG

    def body(x_ref, b_ref, c_ref, dtx_ref, acsx_ref, acst_ref, d_ref, y_ref, sin_ref, state):
        @pl.when(pl.program_id(1) == 0)
        def _():
            state[...] = jnp.zeros_like(state)

        xs = x_ref[...]
        tri = lax.broadcasted_iota(jnp.int32, (Q, Q), 0) >= lax.broadcasted_iota(jnp.int32, (Q, Q), 1)
        acs_x = acsx_ref[...]
        atot_x = acsx_ref[pl.ds(Q - 1, 1), :]
        xd = xs * dtx_ref[...]
        xb = xd.astype(BF16)
        bb = b_ref[...]
        cb16 = c_ref[...].astype(BF16)
        cbm = lax.dot_general(cb16, bb.astype(BF16), (((1,), (1,)), ((), ())), preferred_element_type=F32)
        s_in = state[...]
        sin_ref[0] = s_in
        y = jnp.dot(cb16, s_in.astype(BF16), preferred_element_type=F32) * jnp.exp(acs_x) + xs * d_ref[...]
        lane = lax.broadcasted_iota(jnp.int32, (Q, 2 * HD), 1)
        for p in range(hpg // 2):
            xp = xb[:, 2 * HD * p:2 * HD * (p + 1)]
            yp = jnp.zeros((Q, 2 * HD), F32)
            for hh, lm in enumerate(_decay_pair(acs_x, acst_ref, p, tri)):
                m = (cbm * lm).astype(BF16)
                xm = jnp.where((lane >= HD) if hh == 1 else (lane < HD), xp, jnp.zeros_like(xp))
                yp = yp + jnp.dot(m, xm, preferred_element_type=F32)
            y_ref[:, 2 * HD * p:2 * HD * (p + 1)] = y[:, 2 * HD * p:2 * HD * (p + 1)] + yp
        dec = jnp.exp(atot_x - acs_x)
        s_new = jnp.dot(bb.T.astype(BF16), (xd * dec).astype(BF16), preferred_element_type=F32)
        state[...] = jnp.exp(atot_x) * s_in + s_new

    grp = lambda g, c: (c, g)
    return _pc(body, name=name, grid=(NG, nc),
               in_specs=[pl.BlockSpec((Q, GW), grp),
                         pl.BlockSpec((Q, NS), lambda g, c: (c, DI // NS + g)),
                         pl.BlockSpec((Q, NS), lambda g, c: (c, DI // NS + NG + g)),
                         pl.BlockSpec((Q, GW), grp),
                         pl.BlockSpec((Q, GW), grp),
                         pl.BlockSpec((hpg, Q), lambda g, c: (g, c)),
                         pl.BlockSpec((1, GW), lambda g, c: (0, g))],
               out_specs=[pl.BlockSpec((Q, GW), grp), pl.BlockSpec((1, NS, GW), lambda g, c: (c, 0, g))],
               out_shape=[_sds((t, DI), F32), _sds((nc, NS, DI), F32)],
               scratch=[pltpu.VMEM((NS, GW), F32)])(xbc, xbc, xbc, dtx, acs_x, acs_t, d_x)


def _ssd_bwd(xbc, dtx, acs_x, acs_t, d_x, sin, dy, *, name):
    t = xbc.shape[0]
    nc = t // Q
    hpg = NH // NG

    def body(x_ref, b_ref, c_ref, dtx_ref, acsx_ref, acst_ref, d_ref, sin_ref, dy_ref,
             dx_ref, db_ref, dc_ref, dacs_ref, dxdt_ref, ddx_ref, dstate):
        @pl.when(pl.program_id(1) == 0)
        def _():
            dstate[...] = jnp.zeros_like(dstate)
            ddx_ref[...] = jnp.zeros_like(ddx_ref)

        xs = x_ref[...]
        dtxv = dtx_ref[...]
        gy = dy_ref[...]
        tri = lax.broadcasted_iota(jnp.int32, (Q, Q), 0) >= lax.broadcasted_iota(jnp.int32, (Q, Q), 1)
        acs_x = acsx_ref[...]
        atot_x = acsx_ref[pl.ds(Q - 1, 1), :]
        xd = xs * dtxv
        xb = xd.astype(BF16)
        gb = gy.astype(BF16)
        bb = b_ref[...]
        cc = c_ref[...]
        bb16, cc16 = bb.astype(BF16), cc.astype(BF16)
        cbm = lax.dot_general(cc16, bb16, (((1,), (1,)), ((), ())), preferred_element_type=F32)
        s_in = sin_ref[0]
        s_in16 = s_in.astype(BF16)
        ds_next = dstate[...]
        ds16 = ds_next.astype(BF16)
        ecs = jnp.exp(acs_x)
        dec = jnp.exp(atot_x - acs_x)
        etot = jnp.exp(atot_x)

        ddx_ref[...] += jnp.sum(gy * xs, axis=0, keepdims=True)
        z = jnp.dot(cc16, s_in16, preferred_element_type=F32)
        dz16 = (gy * ecs).astype(BF16)
        dacs = gy * z * ecs
        dcm = lax.dot_general(dz16, s_in16, (((1,), (1,)), ((), ())), preferred_element_type=F32)
        ds_here = jnp.dot(cc.T.astype(BF16), dz16, preferred_element_type=F32)
        u = jnp.dot(bb16, ds16, preferred_element_type=F32)
        dxd = u * dec
        w_dec = u * xd * dec
        dacs = dacs - w_dec
        datot = jnp.sum(w_dec, axis=0, keepdims=True) + jnp.sum(etot * ds_next * s_in, axis=0, keepdims=True)
        dbm = lax.dot_general((xd * dec).astype(BF16), ds16, (((1,), (1,)), ((), ())), preferred_element_type=F32)
        dstate[...] = etot * ds_next + ds_here

        lane = lax.broadcasted_iota(jnp.int32, (Q, 2 * HD), 1)
        lane_x = lax.broadcasted_iota(jnp.int32, (1, GW), 1)
        dcb = jnp.zeros((Q, Q), F32)
        dx_pairs = []
        for p in range(hpg // 2):
            sl = slice(2 * HD * p, 2 * HD * (p + 1))
            xp = xb[:, sl]
            gp = gb[:, sl]
            dxp = jnp.zeros((Q, 2 * HD), F32)
            for hh, lm in enumerate(_decay_pair(acs_x, acst_ref, p, tri)):
                r = 2 * p + hh
                mm = cbm * lm
                half = (lane >= HD) if hh == 1 else (lane < HD)
                xm = jnp.where(half, xp, jnp.zeros_like(xp))
                gm = jnp.where(half, gp, jnp.zeros_like(gp))
                mt16 = mm.T.astype(BF16)
                dxp = dxp + jnp.dot(mt16, gm, preferred_element_type=F32)
                dm = lax.dot_general(gm, xp, (((1,), (1,)), ((), ())), preferred_element_type=F32)
                dmt = lax.dot_general(xm, gp, (((1,), (1,)), ((), ())), preferred_element_type=F32)
                dcb = dcb + dm * lm
                wdiff = jnp.sum(dm * mm - dmt * mm.T, axis=1, keepdims=True)
                dacs = dacs + wdiff * (lane_x == HD * r).astype(F32)
            dx_pairs.append(dxp)
        dxd = dxd + jnp.concatenate(dx_pairs, axis=1)
        dcm = dcm + jnp.dot(dcb.astype(BF16), bb16, preferred_element_type=F32)
        dbm = dbm + jnp.dot(dcb.T.astype(BF16), cc16, preferred_element_type=F32)
        db_ref[...] = dbm
        dc_ref[...] = dcm

        last = lax.broadcasted_iota(jnp.int32, (Q, GW), 0) == Q - 1
        dacs_ref[...] = dacs + jnp.where(last, datot, 0.0)
        dxdt_ref[...] = dxd * xs
        dx_ref[...] = dxd * dtxv + gy * d_ref[...]

    rev = lambda g, c: (nc - 1 - c, g)
    return _pc(body, name=name, grid=(NG, nc),
               in_specs=[pl.BlockSpec((Q, GW), rev),
                         pl.BlockSpec((Q, NS), lambda g, c: (nc - 1 - c, DI // NS + g)),
                         pl.BlockSpec((Q, NS), lambda g, c: (nc - 1 - c, DI // NS + NG + g)),
                         pl.BlockSpec((Q, GW), rev),
                         pl.BlockSpec((Q, GW), rev),
                         pl.BlockSpec((hpg, Q), lambda g, c: (g, nc - 1 - c)),
                         pl.BlockSpec((1, GW), lambda g, c: (0, g)),
                         pl.BlockSpec((1, NS, GW), lambda g, c: (nc - 1 - c, 0, g)),
                         pl.BlockSpec((Q, GW), rev)],
               out_specs=[pl.BlockSpec((Q, GW), rev), pl.BlockSpec((Q, NS), rev), pl.BlockSpec((Q, NS), rev),
                          pl.BlockSpec((Q, GW), rev), pl.BlockSpec((Q, GW), rev),
                          pl.BlockSpec((1, GW), lambda g, c: (0, g))],
               out_shape=[_sds((t, DI), F32), _sds((t, NG * NS), F32), _sds((t, NG * NS), F32), _sds((t, DI), F32),
                          _sds((t, DI), F32), _sds((1, DI), F32)],
               scratch=[pltpu.VMEM((NS, GW), F32)])(xbc, xbc, xbc, dtx, acs_x, acs_t, d_x, sin, dy)


def _gated_norm_fwd(y, z, gn, *, name):
    t = y.shape[0]
    tm = min(t, 256)

    def body(y_ref, z_ref, g_ref, o_ref):
        for k in range(NG):
            sl = slice(GW * k, GW * (k + 1))
            zz = z_ref[:, sl]
            h = y_ref[:, sl] * (zz * _sigmoid(zz))
            r = lax.rsqrt(jnp.mean(h * h, axis=-1, keepdims=True) + EPS)
            o_ref[:, sl] = (h * r * g_ref[:, sl]).astype(BF16)

    return _pc(body, name=name, grid=(t // tm,), in_specs=[_row(tm, DI), _row(tm, DI), _const((1, DI))],
               out_specs=_row(tm, DI), out_shape=_sds((t, DI), BF16))(y, z, gn.reshape(1, DI))


def _gated_norm_bwd(y, z, gn, dout, *, name):
    t = y.shape[0]
    tm = min(t, 256)

    def body(y_ref, z_ref, g_ref, do_ref, dy_ref, dz_ref, dg_ref):
        @pl.when(pl.program_id(0) == 0)
        def _():
            dg_ref[...] = jnp.zeros_like(dg_ref)

        for k in range(NG):
            sl = slice(GW * k, GW * (k + 1))
            zz = z_ref[:, sl]
            yy = y_ref[:, sl]
            sz = zz * _sigmoid(zz)
            h = yy * sz
            r = lax.rsqrt(jnp.mean(h * h, axis=-1, keepdims=True) + EPS)
            d = do_ref[:, sl]
            dg_ref[:, sl] += jnp.sum(d * h * r, axis=0, keepdims=True)
            dgd = d * g_ref[:, sl]
            dot = jnp.mean(dgd * h, axis=-1, keepdims=True)
            dh = r * dgd - h * (r * r * r * dot)
            dy_ref[:, sl] = dh * sz
            dz_ref[:, sl] = (dh * yy * _dsilu(zz)).astype(BF16)

    dy, dz, dg = _pc(body, name=name, grid=(t // tm,),
                     in_specs=[_row(tm, DI), _row(tm, DI), _const((1, DI)), _row(tm, DI)],
                     out_specs=[_row(tm, DI), _row(tm, DI), _const((1, DI))],
                     out_shape=[_sds((t, DI), F32), _sds((t, DI), BF16), _sds((1, DI), F32)])(y, z, gn.reshape(1, DI), dout)
    return dy, dz, dg.reshape(DI)


def _pad_heads(v):
    return jnp.pad(v.reshape(1, NH), ((0, 0), (0, NHP - NH)))


def _ssm_fwd(h, p):
    u = _rms_fwd(h, p["norm"], name="ssm_rms_fwd")
    z = _mm(u, p["wz"], "nn", name="ssm_in_z")
    xp = _mm(u, p["wxbc"], "nn", name="ssm_in_xbc")
    dtraw = _mm(u, p["wdt"], "nn", name="ssm_in_dt")
    xbc = _conv4_fwd(xp, p["conv_w"], p["conv_b"], name="ssm_conv_fwd")
    dtb, alog = _pad_heads(p["dt_bias"]), _pad_heads(p["a_log"])
    d_x = jnp.repeat(p["d"], HD).reshape(1, DI)
    dtx, acs_x, acs_t = _ssd_pre(dtraw, dtb, alog, name="ssd_pre")
    y, sin = _ssd_fwd(xbc, dtx, acs_x, acs_t, d_x, name="ssd_fwd")
    yn = _gated_norm_fwd(y, z, p["gate_norm"], name="ssm_gate_fwd")
    h2 = _mm(yn, p["wout"], "nn", res=h, name="ssm_out")
    return h2, (h, u, z, xp, dtraw, xbc, dtx, acs_x, acs_t, d_x, dtb, alog, y, sin, yn)


def _ssm_bwd(dh2, p, saved):
    h, u, z, xp, dtraw, xbc, dtx, acs_x, acs_t, d_x, dtb, alog, y, sin, yn = saved
    dyn = _mm(dh2, p["wout"], "nt", name="ssm_out_dx")
    g = {"w_out": _mm(yn, dh2, "tn", out_dtype=BF16, name="ssm_out_dw")}
    dy, dz, g["gate_norm"] = _gated_norm_bwd(y, z, p["gate_norm"], dyn, name="ssm_gate_bwd")
    dxs, dbm, dcm, dacs, dxdt, ddx = _ssd_bwd(xbc, dtx, acs_x, acs_t, d_x, sin, dy, name="ssd_bwd")
    ddtraw, ddtb, dav = _ssd_post(dacs, dxdt, dtraw, dtb, alog, name="ssd_post")
    dpre, g["conv_w"], g["conv_b"] = _conv4_bwd_act(xp, p["conv_w"], p["conv_b"], jnp.concatenate([dxs, dbm, dcm], axis=1),
                                                    name="ssm_conv_bwd_act")
    dxp = _conv_bwd_x(dpre, p["conv_w"], KC, name="ssm_conv_bwd_x", out_dtype=BF16)
    du = _mm(dz, p["wz"], "nt", name="ssm_in_dx_z")
    du = _mm(dxp, p["wxbc"], "nt", res=du, name="ssm_in_dx_xbc")
    du = _mm(ddtraw, p["wdt"], "nt", res=du, name="ssm_in_dx_dt")
    g["w_in"] = jnp.concatenate([_mm(u, dz, "tn", out_dtype=BF16, name="ssm_in_dw_z"),
                                 _mm(u, dxp, "tn", out_dtype=BF16, name="ssm_in_dw_xbc"),
                                 _mm(u, ddtraw, "tn", out_dtype=BF16, name="ssm_in_dw_dt")[:, :NH]], axis=1)
    dh, g["norm"] = _rms_bwd(h, p["norm"], du, dh2, name="ssm_rms_bwd")
    g["dt_bias"] = ddtb[0, :NH]
    g["a_log"] = dav[0, :NH] * (-jnp.exp(p["a_log"]))
    g["d"] = ddx.reshape(NH, HD).sum(-1)
    return dh, g


def _cv_fwd(h, p):
    u = _rms_fwd(h, p["norm"], name="cv_rms_fwd")
    pre = _mm(u, p["wpw1"], "nn", bias=p["b_pw1"].reshape(1, 2 * D), name="cv_pw1")
    gl = _glu_fwd(pre, name="cv_glu_fwd")
    c, s = _dwconv_ln_fwd(gl, p["dw_w"], p["dw_b"], p["ln_g"], p["ln_b"], name="cv_dwconv_ln_fwd")
    h2 = _mm(s, p["wpw2"], "nn", bias=p["b_pw2"].reshape(1, D), res=h, name="cv_pw2")
    return h2, (h, u, pre, gl, c, s)


def _cv_bwd(dh2, p, saved):
    h, u, pre, gl, c, s = saved
    ds = _mm(dh2, p["wpw2"], "nt", name="cv_pw2_dx")
    g = {"w_pw2": _mm(s, dh2, "tn", out_dtype=BF16, name="cv_pw2_dw"), "b_pw2": _colsum(dh2, name="cv_pw2_db")}
    dc, g["ln_g"], g["ln_b"] = _ln_silu_bwd(c, p["ln_g"], p["ln_b"], ds, name="cv_ln_bwd")
    dgl = _conv_bwd_x(dc, p["dw_w"], CK, name="cv_dwconv_bwd_x", out_dtype=F32)
    g["dw_w"], g["dw_b"] = _dwconv_bwd_w(gl, dc, name="cv_dwconv_bwd_w")
    dpre, g["b_pw1"] = _glu_bwd(pre, dgl, name="cv_glu_bwd")
    du = _mm(dpre, p["wpw1"], "nt", name="cv_pw1_dx")
    g["w_pw1"] = _mm(u, dpre, "tn", out_dtype=BF16, name="cv_pw1_dw")
    dh, g["norm"] = _rms_bwd(h, p["norm"], du, dh2, name="cv_rms_bwd")
    return dh, g


def _ffn_fwd(h, p):
    u = _rms_fwd(h, p["norm"], name="ffn_rms_fwd")
    gu = _mm(u, p["wgu"], "nn", name="ffn_gate_up")
    a = _swiglu_fwd(gu, name="ffn_act_fwd")
    h2 = _mm(a, p["wd"], "nn", res=h, name="ffn_down")
    return h2, (h, u, gu, a)


def _ffn_bwd(dh2, p, saved):
    h, u, gu, a = saved
    da = _mm(dh2, p["wd"], "nt", name="ffn_down_dx")
    g = {"w_down": _mm(a, dh2, "tn", out_dtype=BF16, name="ffn_down_dw")}
    dgu = _swiglu_bwd(gu, da, name="ffn_act_bwd")
    du = _mm(dgu, p["wgu"], "nt", name="ffn_gate_up_dx")
    dwgu = _mm(u, dgu, "tn", out_dtype=BF16, name="ffn_gate_up_dw")
    g["w_gate"], g["w_up"] = dwgu[:, :DFF], dwgu[:, DFF:]
    dh, g["norm"] = _rms_bwd(h, p["norm"], du, dh2, name="ffn_rms_bwd")
    return dh, g


def _fwd_bwd(x, target, ssm, cv, ffn, final_norm):
    depth = len(ffn)
    h, tape = x, []
    for i in range(depth):
        h, s_mix = (_ssm_fwd if i % 2 == 0 else _cv_fwd)(h, (ssm if i % 2 == 0 else cv)[i // 2])
        h, s_ffn = _ffn_fwd(h, ffn[i])
        tape.append((s_mix, s_ffn))
    lsum, dh, g_final = _final_loss(h, final_norm, target, name="loss_head")
    g_ssm, g_cv, g_ffn = [None] * len(ssm), [None] * len(cv), [None] * depth
    for i in reversed(range(depth)):
        s_mix, s_ffn = tape[i]
        dh, g_ffn[i] = _ffn_bwd(dh, ffn[i], s_ffn)
        if i % 2 == 0:
            dh, g_ssm[i // 2] = _ssm_bwd(dh, ssm[i // 2], s_mix)
        else:
            dh, g_cv[i // 2] = _cv_bwd(dh, cv[i // 2], s_mix)
    return lsum, dh, g_ssm, g_cv, g_ffn, g_final


ANY = pl.BlockSpec(memory_space=pl.ANY)


def _place():
    x, y, c = lax.axis_index("x"), lax.axis_index("y"), lax.axis_index("c")
    return x, y, c, [(1 - x, y), (x, 1 - y), (1 - x, 1 - y)]


def _remote(src, dst, ssem, rsem, dev):
    return pltpu.make_async_remote_copy(src_ref=src, dst_ref=dst, send_sem=ssem, recv_sem=rsem, device_id=dev,
                                        device_id_type=MESH)


def _comm_call(body, name, ins, out_shape, n_sems):
    return pl.pallas_call(
        body, name=name, in_specs=[ANY] * len(ins), out_specs=[ANY] * len(out_shape), out_shape=out_shape,
        scratch_shapes=[pltpu.SemaphoreType.DMA((k,)) for k in n_sems])(*ins)


def _gather_big(ws):
    n = len(ws)

    def body(*refs):
        ins, outs = refs[:n], refs[n:2 * n]
        ssem, rsem, fssem, frsem = refs[2 * n:]
        x, y, c, chips = _place()
        me, sib = 2 * x + y, (x, y, 1 - c)

        def half(a, cc):
            lh = ws[a].shape[0] // 2
            return pl.ds(cc * lh, lh)

        sends, fwds = [], []
        for a in range(n):
            for j, (px, py) in enumerate(chips):
                cp = _remote(ins[a].at[half(a, c)], outs[a].at[me, half(a, c)], ssem.at[3 * a + j], rsem.at[3 * a + j], (px, py, c))
                cp.start()
                sends.append(cp)
        for a in range(n):
            for j, (px, py) in enumerate(chips):
                blk = outs[a].at[2 * px + py, half(a, c)]
                _remote(blk, blk, ssem.at[3 * a + j], rsem.at[3 * a + j], (px, py, c)).wait_recv()
                f = _remote(blk, blk, fssem.at[3 * a + j], frsem.at[3 * a + j], sib)
                f.start()
                fwds.append(f)
        for a in range(n):
            for j, (px, py) in enumerate(chips):
                blk = outs[a].at[2 * px + py, half(a, 1 - c)]
                _remote(blk, blk, fssem.at[3 * a + j], frsem.at[3 * a + j], sib).wait_recv()
        for cp in sends + fwds:
            cp.wait_send()

    return _comm_call(body, "gather_weights", ws, [_sds((4,) + w.shape, w.dtype) for w in ws], [3 * n, 3 * n, 3 * n, 3 * n])


def _exchange_pairs(gs):
    n = len(gs)

    def body(*refs):
        ins, got = refs[:n], refs[n:2 * n]
        ssem, rsem = refs[2 * n:]
        x, y, c, _ = _place()
        cps = []
        for a in range(n):
            lh = gs[a].shape[1] // 2
            cps.append(_remote(ins[a].at[:, pl.ds((1 - c) * lh, lh)], got[a], ssem.at[a], rsem.at[a], (x, y, 1 - c)))
        for cp in cps:
            cp.start()
        for cp in cps:
            cp.wait()

    halves = [_sds((4, g.shape[1] // 2) + g.shape[2:], g.dtype) for g in gs]
    return _comm_call(body, "reduce_pair_exchange", gs, halves, [n, n])


def _exchange_chips(ps):
    n = len(ps)

    def body(*refs):
        ins, outs = refs[:n], refs[n:2 * n]
        ssem, rsem = refs[2 * n:]
        x, y, c, chips = _place()
        me = 2 * x + y
        cps = []
        for a in range(n):
            for j, (px, py) in enumerate(chips):
                cps.append(_remote(ins[a].at[2 * px + py], outs[a].at[me], ssem.at[3 * a + j], rsem.at[3 * a + j], (px, py, c)))
        for cp in cps:
            cp.start()
        for a in range(n):
            for j, (px, py) in enumerate(chips):
                blk = outs[a].at[2 * px + py]
                _remote(blk, blk, ssem.at[3 * a + j], rsem.at[3 * a + j], (px, py, c)).wait_recv()
        for cp in cps:
            cp.wait_send()

    return _comm_call(body, "reduce_chip_exchange", ps, [_sds(p.shape, p.dtype) for p in ps], [3 * n, 3 * n])


def _exchange_final(fs):
    n = len(fs)

    def body(*refs):
        ins, outs = refs[:n], refs[n:2 * n]
        ssem, rsem = refs[2 * n:]
        x, y, c, _ = _place()
        cps = [_remote(ins[a], outs[a], ssem.at[a], rsem.at[a], (x, y, 1 - c)) for a in range(n)]
        for cp in cps:
            cp.start()
        for cp in cps:
            cp.wait()

    return _comm_call(body, "reduce_final_exchange", fs, [_sds(f.shape, f.dtype) for f in fs], [n, n])


def _rows_tile(r, c, n_arrays, itemsize=4):
    cap = max(16, (16 * 1024 * 1024) // (2 * n_arrays * c * itemsize))
    best = 16 if r % 16 == 0 else r
    for t in range(16, min(r, cap) + 1, 16):
        if r % t == 0:
            best = t
    return best


def _add_pair(g, got, core, *, name):
    _, nl, r, c = g.shape
    rows = (nl // 2) * r
    tm = _rows_tile(rows, c, 3)

    def body(core_ref, p_ref, q_ref, o_ref):
        o_ref[...] = (p_ref[...].astype(F32) + q_ref[...].astype(F32)).astype(BF16)

    spec = pl.BlockSpec((None, tm, c), lambda s, i, core_ref: (s, i, 0))
    grid_spec = pltpu.PrefetchScalarGridSpec(
        num_scalar_prefetch=1, grid=(4, rows // tm),
        in_specs=[pl.BlockSpec((None, None, tm, c), lambda s, i, core_ref: (s, core_ref[0], i, 0)), spec], out_specs=spec)
    out = pl.pallas_call(body, name=name, grid_spec=grid_spec, out_shape=_sds((4, rows, c), BF16),
                         compiler_params=pltpu.CompilerParams(dimension_semantics=("arbitrary", "arbitrary"),
                                                              vmem_limit_bytes=VMEM_LIMIT))(
                                                                  core, g.reshape(4, 2, rows, c), got.reshape(4, rows, c))
    return out.reshape(got.shape)


def _sum_slots(rb, *, name):
    shape = rb.shape[1:]
    c = shape[-1]
    r = rb.size // (4 * c)
    tm = _rows_tile(r, c, 6)

    def body(i_ref, o_ref):
        acc = i_ref[0].astype(F32)
        for k in range(1, 4):
            acc = acc + i_ref[k].astype(F32)
        o_ref[...] = acc

    return _pc(body, name=name, grid=(r // tm,), in_specs=[pl.BlockSpec((4, tm, c), lambda i: (0, i, 0))],
               out_specs=_row(tm, c), out_shape=_sds((r, c), F32))(rb.reshape(4, r, c)).reshape(shape)


def _reduce_big(gs, chip, core):
    got = _exchange_pairs(gs)
    core1 = core.reshape(1).astype(jnp.int32)
    ps = [_add_pair(g, q, core1, name="reduce_pair_add") for g, q in zip(gs, got)]
    rb = _exchange_chips(ps)
    rb = [lax.dynamic_update_index_in_dim(b, lax.dynamic_index_in_dim(p, chip, 0, keepdims=False), chip, 0)
          for b, p in zip(rb, ps)]
    fs = [_sum_slots(b, name="reduce_chip_sum") for b in rb]
    theirs = _exchange_final(fs)
    return [jnp.where(core == 0, jnp.concatenate([f, t], axis=0), jnp.concatenate([t, f], axis=0))
            for f, t in zip(fs, theirs)]


def _gather8(v, reduce):
    m = v.shape[0]

    def body(x_ref, *rest):
        if reduce:
            sum_ref, out_ref, send_sems, recv_sems, local_sem = rest
        else:
            out_ref, send_sems, recv_sems, local_sem = rest
        x, y, c, chips = _place()
        me, sibling = (x, y, c), (x, y, 1 - c)

        def rows(px, py, pc):
            return out_ref.at[pl.ds((4 * px + 2 * py + pc) * m, m), :]

        def copy(k, block, to, src=None):
            return _remote(rows(*block) if src is None else src, rows(*block), send_sems.at[k], recv_sems.at[k], to)

        mine = pltpu.make_async_copy(x_ref, rows(*me), local_sem)
        mine.start()
        first = [copy(0, me, sibling, src=x_ref)]
        first += [copy(1 + j, me, (*chip, c), src=x_ref) for j, chip in enumerate(chips)]
        for cp in first:
            cp.start()
        passed = [copy(4 + j, (*chip, c), sibling) for j, chip in enumerate(chips)]
        for j, chip in enumerate(chips):
            copy(1 + j, (*chip, c), me).wait_recv()
            passed[j].start()
        copy(0, sibling, me).wait_recv()
        for j, chip in enumerate(chips):
            copy(4 + j, (*chip, 1 - c), me).wait_recv()
        for cp in first + passed:
            cp.wait_send()
        mine.wait()
        if reduce:
            acc = out_ref[pl.ds(0, m), :]
            for k in range(1, 8):
                acc = acc + out_ref[pl.ds(k * m, m), :]
            sum_ref[...] = acc

    vm = pl.BlockSpec(memory_space=pltpu.VMEM)
    out_shape = [_sds((8 * m, 128), F32)]
    if reduce:
        out_shape = [_sds((m, 128), F32)] + out_shape
    out = pl.pallas_call(
        body, name="allreduce_small" if reduce else "gather_small", in_specs=[vm], out_specs=[vm] * len(out_shape),
        out_shape=out_shape,
        scratch_shapes=[pltpu.SemaphoreType.DMA((7,)), pltpu.SemaphoreType.DMA((7,)), pltpu.SemaphoreType.DMA(())],
        compiler_params=pltpu.CompilerParams(vmem_limit_bytes=VMEM_LIMIT))(v)
    return out[0]


def _adamw(w, g, m, v, *, name):
    shape = w.shape
    c = shape[-1]
    r = w.size // c
    tm = _rows_tile(r, c, 7) if r % 16 == 0 else r

    def body(w_ref, g_ref, m_ref, v_ref, d_ref, mo_ref, vo_ref):
        gg = g_ref[...]
        mm = B1 * m_ref[...] + (1.0 - B1) * gg
        vv = B2 * v_ref[...] + (1.0 - B2) * (gg * gg)
        mo_ref[...] = mm
        vo_ref[...] = vv
        m_hat = mm / (1.0 - B1 ** STEP)
        v_hat = vv / (1.0 - B2 ** STEP)
        d_ref[...] = -LR * (m_hat / (jnp.sqrt(v_hat) + AEPS) + WD * w_ref[...])

    outs = _pc(body, name=name, grid=(r // tm,), in_specs=[_row(tm, c)] * 4, out_specs=[_row(tm, c)] * 3,
               out_shape=[_sds((r, c), F32)] * 3)(*[t.reshape(r, c) for t in (w, g, m, v)])
    return [o.reshape(shape) for o in outs]


WEIGHTS = ["ssm_norm", "ssm_w_in", "ssm_conv_w", "ssm_conv_b", "ssm_dt_bias", "ssm_a_log", "ssm_d", "ssm_gate_norm",
           "ssm_w_out", "cv_norm", "cv_w_pw1", "cv_b_pw1", "cv_dw_w", "cv_dw_b", "cv_ln_g", "cv_ln_b", "cv_w_pw2",
           "cv_b_pw2", "ffn_norm", "ffn_w_gate", "ffn_w_up", "ffn_w_down", "final_norm"]
BIG = {"ssm_w_in": "col", "ssm_w_out": "row", "cv_w_pw1": "col", "cv_w_pw2": "row",
       "ffn_w_gate": "col", "ffn_w_up": "col", "ffn_w_down": "row"}
SMALL_SHARDED = {"ssm_conv_w": 2, "cv_norm": 1, "cv_b_pw1": 1, "cv_dw_w": 2, "cv_dw_b": 1, "cv_ln_g": 1, "cv_ln_b": 1,
                 "cv_b_pw2": 1}
SMALL = [n for n in WEIGHTS if n not in BIG]
LANES = 128


def _pack(arrays):
    flat = jnp.concatenate([a.reshape(-1) for a in arrays])
    rows = -(-flat.size // (8 * LANES)) * 8
    return jnp.pad(flat, (0, rows * LANES - flat.size)).reshape(rows, LANES)


def _unpack(packed, shapes):
    flat, out, off = packed.reshape(-1), [], 0
    for s in shapes:
        n = 1
        for d_ in s:
            n *= d_
        out.append(flat[off:off + n].reshape(s))
        off += n
    return out


def _unshard(parts, axis):
    return jnp.concatenate(parts, axis=axis)


def kernel(x, ssm_norm, ssm_w_in, ssm_conv_w, ssm_conv_b, ssm_dt_bias, ssm_a_log, ssm_d, ssm_gate_norm, ssm_w_out, cv_norm, cv_w_pw1, cv_b_pw1, cv_dw_w, cv_dw_b, cv_ln_g, cv_ln_b, cv_w_pw2, cv_b_pw2, ffn_norm, ffn_w_gate, ffn_w_up, ffn_w_down, final_norm, loss_target, m_ssm_norm, m_ssm_w_in, m_ssm_conv_w, m_ssm_conv_b, m_ssm_dt_bias, m_ssm_a_log, m_ssm_d, m_ssm_gate_norm, m_ssm_w_out, m_cv_norm, m_cv_w_pw1, m_cv_b_pw1, m_cv_dw_w, m_cv_dw_b, m_cv_ln_g, m_cv_ln_b, m_cv_w_pw2, m_cv_b_pw2, m_ffn_norm, m_ffn_w_gate, m_ffn_w_up, m_ffn_w_down, m_final_norm, v_ssm_norm, v_ssm_w_in, v_ssm_conv_w, v_ssm_conv_b, v_ssm_dt_bias, v_ssm_a_log, v_ssm_d, v_ssm_gate_norm, v_ssm_w_out, v_cv_norm, v_cv_w_pw1, v_cv_b_pw1, v_cv_dw_w, v_cv_dw_b, v_cv_ln_g, v_cv_ln_b, v_cv_w_pw2, v_cv_b_pw2, v_ffn_norm, v_ffn_w_gate, v_ffn_w_up, v_ffn_w_down, v_final_norm):
    a = dict(locals())
    chip = 2 * lax.axis_index("x") + lax.axis_index("y")
    n_ssm, n_cv, depth = ssm_norm.shape[0], cv_norm.shape[0], ffn_norm.shape[0]

    own = {n: a[n].astype(BF16) for n in BIG}
    gw = dict(zip(BIG, _gather_big([own[n] for n in BIG])))
    sh_names = list(SMALL_SHARDED)
    got = _gather8(_pack([a[n] for n in sh_names]), reduce=False)
    got = got.reshape(8, -1)[0::2]
    per_chip = [_unpack(got[s], [a[n].shape for n in sh_names]) for s in range(4)]
    full = {n: a[n] for n in SMALL}
    for k, n in enumerate(sh_names):
        full[n] = _unshard([per_chip[s][k] for s in range(4)], SMALL_SHARDED[n])

    def whole(layer, *names):
        parts = [jnp.where(chip == s, own[n][layer], gw[n][s, layer]) for n in names for s in range(4)]
        return jnp.concatenate(parts, axis=1 if BIG[names[0]] == "col" else 0)

    ssm, cv, ffn = [], [], []
    for j in range(n_ssm):
        w_in = whole(j, "ssm_w_in")
        ssm.append(dict(norm=full["ssm_norm"][j], wz=w_in[:, :DI], wxbc=w_in[:, DI:DI + CD],
                        wdt=jnp.pad(w_in[:, DI + CD:], ((0, 0), (0, NHP - NH))),
                        conv_w=full["ssm_conv_w"][j], conv_b=full["ssm_conv_b"][j], dt_bias=full["ssm_dt_bias"][j],
                        a_log=full["ssm_a_log"][j], d=full["ssm_d"][j], gate_norm=full["ssm_gate_norm"][j],
                        wout=whole(j, "ssm_w_out")))
    for j in range(n_cv):
        cv.append(dict(norm=full["cv_norm"][j], wpw1=whole(j, "cv_w_pw1"), b_pw1=full["cv_b_pw1"][j],
                       dw_w=full["cv_dw_w"][j], dw_b=full["cv_dw_b"][j], ln_g=full["cv_ln_g"][j], ln_b=full["cv_ln_b"][j],
                       wpw2=whole(j, "cv_w_pw2"), b_pw2=full["cv_b_pw2"][j]))
    for i in range(depth):
        ffn.append(dict(norm=full["ffn_norm"][i],
                        wgu=whole(i, "ffn_w_gate", "ffn_w_up"), wd=whole(i, "ffn_w_down")))

    lsum, grad_x, g_ssm, g_cv, g_ffn, g_final = _fwd_bwd(x[0], loss_target[0], ssm, cv, ffn, full["final_norm"])
    loss = (0.5 / D) * lax.psum(jnp.sum(lsum), ("x", "y", "c"))

    def slots(name, layers):
        if BIG[name] == "col":
            return jnp.stack([g.reshape(g.shape[0], 4, g.shape[1] // 4).transpose(1, 0, 2) for g in layers], axis=1)
        return jnp.stack([g.reshape(4, g.shape[0] // 4, g.shape[1]) for g in layers], axis=1)

    local_big = {"ssm_w_in": [g["w_in"] for g in g_ssm], "ssm_w_out": [g["w_out"] for g in g_ssm],
                 "cv_w_pw1": [g["w_pw1"] for g in g_cv], "cv_w_pw2": [g["w_pw2"] for g in g_cv],
                 "ffn_w_gate": [g["w_gate"] for g in g_ffn], "ffn_w_up": [g["w_up"] for g in g_ffn],
                 "ffn_w_down": [g["w_down"] for g in g_ffn]}
    grads = dict(zip(BIG, _reduce_big([slots(n, local_big[n]) for n in BIG], chip, lax.axis_index("c"))))

    local_small = {"final_norm": g_final, "ffn_norm": jnp.stack([g["norm"] for g in g_ffn])}
    for n in SMALL:
        if n.startswith("ssm_"):
            local_small[n] = jnp.stack([g[n[4:]] for g in g_ssm])
        elif n.startswith("cv_"):
            local_small[n] = jnp.stack([g[n[3:]] for g in g_cv])
    summed = _unpack(_gather8(_pack([local_small[n] for n in SMALL]), reduce=True), [full[n].shape for n in SMALL])
    for n, g in zip(SMALL, summed):
        if n in SMALL_SHARDED:
            ax = SMALL_SHARDED[n]
            g = lax.dynamic_slice_in_dim(g, chip * a[n].shape[ax], a[n].shape[ax], axis=ax)
        grads[n] = g

    delta, new_m, new_v = {}, {}, {}
    for n in BIG:
        delta[n], new_m[n], new_v[n] = _adamw(a[n], grads[n], a["m_" + n], a["v_" + n], name="adamw_" + n)
    shapes = [a[n].shape for n in SMALL]
    upd = _adamw(_pack([a[n] for n in SMALL]), _pack([grads[n] for n in SMALL]), _pack([a["m_" + n] for n in SMALL]),
                 _pack([a["v_" + n] for n in SMALL]), name="adamw_small")
    for dst, packed in zip((delta, new_m, new_v), upd):
        dst.update(zip(SMALL, _unpack(packed, shapes)))

    return (loss, grad_x[None], *[grads[n] for n in WEIGHTS], *[delta[n] for n in WEIGHTS],
            *[new_m[n] for n in WEIGHTS], *[new_v[n] for n in WEIGHTS])
```

```python
import functools

import jax
import jax.numpy as jnp
from jax import lax
from jax.experimental import pallas as pl
from jax.experimental.pallas import tpu as pltpu

F32 = jnp.float32
BF16 = jnp.bfloat16
HI = lax.Precision.HIGHEST
MESH = pl.DeviceIdType.MESH

D = 1024
DI = 2048
HD = 64
NH = 32
NG = 4
GW = DI // NG
NS = 128
KC = 4
CD = DI + 2 * NG * NS
Q = 128
DFF = 2816
CK = 31
EPS = 1e-5
NHP = 128

LR, B1, B2, AEPS, WD, STEP = 0.001, 0.9, 0.999, 1e-08, 0.01, 10

VMEM_LIMIT = 56 * 1024 * 1024
MM_VMEM_BUDGET = 36 * 1024 * 1024


def _pc(body, *, name, grid, in_specs, out_specs, out_shape, scratch=()):
    return pl.pallas_call(
        body, name=name, grid=grid, in_specs=in_specs, out_specs=out_specs, out_shape=out_shape,
        scratch_shapes=list(scratch),
        compiler_params=pltpu.CompilerParams(dimension_semantics=("arbitrary",) * len(grid),
                                             vmem_limit_bytes=VMEM_LIMIT))


def _sds(shape, dtype):
    return jax.ShapeDtypeStruct(tuple(shape), dtype)


def _row(tm, c):
    return pl.BlockSpec((tm, c), lambda i: (i, 0))


def _const(shape):
    return pl.BlockSpec(tuple(shape), lambda i: (0,) * len(shape))


def _tile(n, cap):
    if n <= cap:
        return n
    best = 128
    for t in range(128, cap + 1, 128):
        if n % t == 0:
            best = t
    return best


def _sigmoid(x):
    return 1.0 / (1.0 + jnp.exp(-x))


def _dsilu(x):
    s = _sigmoid(x)
    return s * (1.0 + x * (1.0 - s))


def _mm(a, b, mode, *, name, out_dtype=F32, bias=None, res=None, tm=1024, tn=1024):
    if mode == "nn":
        (m, k), (k2, n) = a.shape, b.shape
    elif mode == "nt":
        (m, k), (n, k2) = a.shape, b.shape
    else:
        (k, m), (k2, n) = a.shape, b.shape
    assert k == k2, (a.shape, b.shape, mode)
    tm = _tile(m, tm if k <= 2048 else 512)
    so = jnp.dtype(out_dtype).itemsize + (4 if res is not None else 0)
    fits = [c for c in range(128, min(n, 2048) + 1, 128) if n % c == 0 and
            2 * (tm * k * a.dtype.itemsize + c * k * b.dtype.itemsize + tm * c * so) <= MM_VMEM_BUDGET]
    tn = n if n <= 128 else max(fits)
    dn = {"nn": (((1,), (0,)), ((), ())), "nt": (((1,), (1,)), ((), ())), "tn": (((0,), (0,)), ((), ()))}[mode]
    nb, nr = bias is not None, res is not None

    def body(*refs):
        acc = lax.dot_general(refs[0][...].astype(BF16), refs[1][...].astype(BF16), dn, preferred_element_type=F32)
        if nb:
            acc = acc + refs[2][...]
        if nr:
            acc = acc + refs[2 + nb][...]
        refs[-1][...] = acc.astype(out_dtype)

    a_spec = pl.BlockSpec((k, tm), lambda i, j: (0, i)) if mode == "tn" else pl.BlockSpec((tm, k), lambda i, j: (i, 0))
    b_spec = pl.BlockSpec((tn, k), lambda i, j: (j, 0)) if mode == "nt" else pl.BlockSpec((k, tn), lambda i, j: (0, j))
    ins, specs = [a, b], [a_spec, b_spec]
    if nb:
        ins.append(bias)
        specs.append(pl.BlockSpec((1, tn), lambda i, j: (0, j)))
    if nr:
        ins.append(res)
        specs.append(pl.BlockSpec((tm, tn), lambda i, j: (i, j)))
    return _pc(body, name=name, grid=(m // tm, n // tn), in_specs=specs,
               out_specs=pl.BlockSpec((tm, tn), lambda i, j: (i, j)), out_shape=_sds((m, n), out_dtype))(*ins)


def _rms_fwd(h, g, *, name):
    t = h.shape[0]
    tm = min(t, 512)

    def body(h_ref, g_ref, u_ref):
        x = h_ref[...]
        r = lax.rsqrt(jnp.mean(x * x, axis=-1, keepdims=True) + EPS)
        u_ref[...] = (x * r * g_ref[...]).astype(BF16)

    return _pc(body, name=name, grid=(t // tm,), in_specs=[_row(tm, D), _const((1, D))], out_specs=_row(tm, D),
               out_shape=_sds((t, D), BF16))(h, g.reshape(1, D))


def _rms_bwd(h, g, du, dres, *, name):
    t = h.shape[0]
    tm = min(t, 512)

    def body(h_ref, g_ref, du_ref, dres_ref, dh_ref, dg_ref):
        x = h_ref[...]
        r = lax.rsqrt(jnp.mean(x * x, axis=-1, keepdims=True) + EPS)
        dy = du_ref[...]
        dyg = dy * g_ref[...]
        dot = jnp.mean(dyg * x, axis=-1, keepdims=True)
        dh_ref[...] = dres_ref[...] + r * dyg - x * (r * r * r * dot)

        @pl.when(pl.program_id(0) == 0)
        def _():
            dg_ref[...] = jnp.zeros_like(dg_ref)

        dg_ref[...] += jnp.sum(dy * x * r, axis=0, keepdims=True)

    dh, dg = _pc(body, name=name, grid=(t // tm,),
                 in_specs=[_row(tm, D), _const((1, D)), _row(tm, D), _row(tm, D)],
                 out_specs=[_row(tm, D), _const((1, D))],
                 out_shape=[_sds((t, D), F32), _sds((1, D), F32)])(h, g.reshape(1, D), du, dres)
    return dh, dg.reshape(D)


def _final_loss(h, g, target, *, name):
    t = h.shape[0]
    tm = min(t, 512)

    def body(h_ref, g_ref, t_ref, dh_ref, dg_ref, l_ref):
        x = h_ref[...]
        gg = g_ref[...]
        r = lax.rsqrt(jnp.mean(x * x, axis=-1, keepdims=True) + EPS)
        err = x * r * gg - t_ref[...]
        dy = err * (1.0 / D)
        dyg = dy * gg
        dot = jnp.mean(dyg * x, axis=-1, keepdims=True)
        dh_ref[...] = r * dyg - x * (r * r * r * dot)

        @pl.when(pl.program_id(0) == 0)
        def _():
            dg_ref[...] = jnp.zeros_like(dg_ref)
            l_ref[...] = jnp.zeros_like(l_ref)

        dg_ref[...] += jnp.sum(dy * x * r, axis=0, keepdims=True)
        l_ref[...] += jnp.sum(err * err, axis=0, keepdims=True)

    dh, dg, l = _pc(body, name=name, grid=(t // tm,),
                    in_specs=[_row(tm, D), _const((1, D)), _row(tm, D)],
                    out_specs=[_row(tm, D), _const((1, D)), _const((1, D))],
                    out_shape=[_sds((t, D), F32), _sds((1, D), F32), _sds((1, D), F32)])(h, g.reshape(1, D), target)
    return l, dh, dg.reshape(D)


def _swiglu_fwd(gu, *, name):
    t = gu.shape[0]
    tm = min(t, 256)

    def body(gu_ref, a_ref):
        gate = gu_ref[:, :DFF]
        up = gu_ref[:, DFF:]
        a_ref[...] = (gate * _sigmoid(gate) * up).astype(BF16)

    return _pc(body, name=name, grid=(t // tm,), in_specs=[_row(tm, 2 * DFF)], out_specs=_row(tm, DFF),
               out_shape=_sds((t, DFF), BF16))(gu)


def _swiglu_bwd(gu, da, *, name):
    t = gu.shape[0]
    tm = min(t, 256)

    def body(gu_ref, da_ref, d_ref):
        gate = gu_ref[:, :DFF]
        up = gu_ref[:, DFF:]
        dav = da_ref[...]
        d_ref[:, :DFF] = (dav * up * _dsilu(gate)).astype(BF16)
        d_ref[:, DFF:] = (dav * gate * _sigmoid(gate)).astype(BF16)

    return _pc(body, name=name, grid=(t // tm,), in_specs=[_row(tm, 2 * DFF), _row(tm, DFF)],
               out_specs=_row(tm, 2 * DFF), out_shape=_sds((t, 2 * DFF), BF16))(gu, da)


def _halo_before(tm, halo, tc):
    return pl.BlockSpec((halo, tc), lambda j, i: (jnp.maximum(i * (tm // halo) - 1, 0), j))


def _halo_after(tm, halo, tc, t):
    return pl.BlockSpec((halo, tc), lambda j, i: (jnp.minimum((i + 1) * (tm // halo), t // halo - 1), j))


def _conv4_fwd(xp, w, b, *, name):
    t = xp.shape[0]
    tm, tc, halo = min(t, 512), 1024, 8

    def body(x_ref, h_ref, w_ref, b_ref, o_ref, pad):
        i = pl.program_id(1)
        pad[pl.ds(0, halo), :] = jnp.where(i == 0, 0.0, h_ref[...])
        pad[pl.ds(halo, tm), :] = x_ref[...]
        acc = jnp.zeros((tm, tc), F32) + b_ref[...]
        for k in range(KC):
            acc = acc + pad[pl.ds(halo - (KC - 1) + k, tm), :] * w_ref[pl.ds(k, 1), :]
        o_ref[...] = acc * _sigmoid(acc)

    tile = pl.BlockSpec((tm, tc), lambda j, i: (i, j))
    return _pc(body, name=name, grid=(CD // tc, t // tm),
               in_specs=[tile, _halo_before(tm, halo, tc), pl.BlockSpec((KC, tc), lambda j, i: (0, j)),
                         pl.BlockSpec((1, tc), lambda j, i: (0, j))],
               out_specs=tile, out_shape=_sds((t, CD), F32), scratch=[pltpu.VMEM((halo + tm, tc), F32)])(xp, xp, w, b.reshape(1, CD))


def _conv4_bwd_act(xp, w, b, dout, *, name):
    t = xp.shape[0]
    tm, tc, halo = min(t, 512), 1024, 8

    def body(x_ref, h_ref, w_ref, b_ref, do_ref, dp_ref, dw_ref, db_ref, pad):
        i = pl.program_id(1)
        pad[pl.ds(0, halo), :] = jnp.where(i == 0, 0.0, h_ref[...])
        pad[pl.ds(halo, tm), :] = x_ref[...]
        acc = jnp.zeros((tm, tc), F32) + b_ref[...]
        for k in range(KC):
            acc = acc + pad[pl.ds(halo - (KC - 1) + k, tm), :] * w_ref[pl.ds(k, 1), :]
        dpre = do_ref[...] * _dsilu(acc)
        dp_ref[...] = dpre

        @pl.when(i == 0)
        def _():
            dw_ref[...] = jnp.zeros_like(dw_ref)
            db_ref[...] = jnp.zeros_like(db_ref)

        db_ref[...] += jnp.sum(dpre, axis=0, keepdims=True)
        for k in range(KC):
            dw_ref[pl.ds(k, 1), :] += jnp.sum(dpre * pad[pl.ds(halo - (KC - 1) + k, tm), :], axis=0, keepdims=True)

    tile = pl.BlockSpec((tm, tc), lambda j, i: (i, j))
    wspec = pl.BlockSpec((KC, tc), lambda j, i: (0, j))
    bspec = pl.BlockSpec((1, tc), lambda j, i: (0, j))
    dp, dw, db = _pc(body, name=name, grid=(CD // tc, t // tm),
                     in_specs=[tile, _halo_before(tm, halo, tc), wspec, bspec, tile],
                     out_specs=[tile, wspec, bspec],
                     out_shape=[_sds((t, CD), F32), _sds((KC, CD), F32), _sds((1, CD), F32)],
                     scratch=[pltpu.VMEM((halo + tm, tc), F32)])(xp, xp, w, b.reshape(1, CD), dout)
    return dp, dw, db.reshape(CD)


def _conv_bwd_x(dy, w, kk, *, name, out_dtype):
    t, c = dy.shape
    halo = 8 if kk <= 8 else 32
    tm, tc = (min(t, 512), min(c, 1024)) if kk <= 8 else (min(t, 256), min(c, 512))

    def body(y_ref, h_ref, w_ref, o_ref, pad):
        i = pl.program_id(1)
        pad[pl.ds(0, tm), :] = y_ref[...]
        pad[pl.ds(tm, halo), :] = jnp.where(i == t // tm - 1, 0.0, h_ref[...])
        acc = jnp.zeros((tm, tc), F32)
        for k in range(kk):
            acc = acc + pad[pl.ds(kk - 1 - k, tm), :] * w_ref[pl.ds(k, 1), :]
        o_ref[...] = acc.astype(out_dtype)

    tile = pl.BlockSpec((tm, tc), lambda j, i: (i, j))
    return _pc(body, name=name, grid=(c // tc, t // tm),
               in_specs=[tile, _halo_after(tm, halo, tc, t), pl.BlockSpec((kk, tc), lambda j, i: (0, j))],
               out_specs=tile, out_shape=_sds((t, c), out_dtype), scratch=[pltpu.VMEM((tm + halo, tc), F32)])(dy, dy, w)


def _glu_fwd(p, *, name):
    t = p.shape[0]
    tm = min(t, 512)

    def body(p_ref, o_ref):
        o_ref[...] = p_ref[:, :D] * _sigmoid(p_ref[:, D:])

    return _pc(body, name=name, grid=(t // tm,), in_specs=[_row(tm, 2 * D)], out_specs=_row(tm, D),
               out_shape=_sds((t, D), F32))(p)


def _glu_bwd(p, dg, *, name):
    t = p.shape[0]
    tm = min(t, 512)

    def body(p_ref, dg_ref, dp_ref, db_ref):
        a = p_ref[:, :D]
        s = _sigmoid(p_ref[:, D:])
        d = dg_ref[...]
        da = d * s
        dgate = d * a * s * (1.0 - s)
        dp_ref[:, :D] = da.astype(BF16)
        dp_ref[:, D:] = dgate.astype(BF16)

        @pl.when(pl.program_id(0) == 0)
        def _():
            db_ref[...] = jnp.zeros_like(db_ref)

        db_ref[:, :D] += jnp.sum(da, axis=0, keepdims=True)
        db_ref[:, D:] += jnp.sum(dgate, axis=0, keepdims=True)

    dp, db = _pc(body, name=name, grid=(t // tm,), in_specs=[_row(tm, 2 * D), _row(tm, D)],
                 out_specs=[_row(tm, 2 * D), _const((1, 2 * D))],
                 out_shape=[_sds((t, 2 * D), BF16), _sds((1, 2 * D), F32)])(p, dg)
    return dp, db.reshape(2 * D)


def _dwconv_ln_fwd(g, w, b, lg, lb, *, name):
    t = g.shape[0]
    tm, halo = min(t, 256), 32

    def body(x_ref, h_ref, w_ref, b_ref, lg_ref, lb_ref, c_ref, s_ref, pad):
        i = pl.program_id(0)
        pad[pl.ds(0, halo), :] = jnp.where(i == 0, 0.0, h_ref[...])
        pad[pl.ds(halo, tm), :] = x_ref[...]
        acc = jnp.zeros((tm, D), F32) + b_ref[...]
        for k in range(CK):
            acc = acc + pad[pl.ds(halo - (CK - 1) + k, tm), :] * w_ref[pl.ds(k, 1), :]
        c_ref[...] = acc
        mu = jnp.mean(acc, axis=-1, keepdims=True)
        xc = acc - mu
        r = lax.rsqrt(jnp.mean(xc * xc, axis=-1, keepdims=True) + EPS)
        n = xc * r * lg_ref[...] + lb_ref[...]
        s_ref[...] = (n * _sigmoid(n)).astype(BF16)

    c, s = _pc(body, name=name, grid=(t // tm,),
               in_specs=[_row(tm, D), pl.BlockSpec((halo, D), lambda i: (jnp.maximum(i * (tm // halo) - 1, 0), 0)),
                         _const((CK, D)), _const((1, D)), _const((1, D)), _const((1, D))],
               out_specs=[_row(tm, D), _row(tm, D)], out_shape=[_sds((t, D), F32), _sds((t, D), BF16)],
               scratch=[pltpu.VMEM((halo + tm, D), F32)])(g, g, w, b.reshape(1, D), lg.reshape(1, D), lb.reshape(1, D))
    return c, s


def _ln_silu_bwd(c, lg, lb, ds, *, name):
    t = c.shape[0]
    tm = min(t, 512)

    def body(c_ref, lg_ref, lb_ref, ds_ref, dc_ref, dlg_ref, dlb_ref):
        x = c_ref[...]
        mu = jnp.mean(x, axis=-1, keepdims=True)
        xc = x - mu
        r = lax.rsqrt(jnp.mean(xc * xc, axis=-1, keepdims=True) + EPS)
        xh = xc * r
        n = xh * lg_ref[...] + lb_ref[...]
        dn = ds_ref[...] * _dsilu(n)
        dxh = dn * lg_ref[...]
        m1 = jnp.mean(dxh, axis=-1, keepdims=True)
        m2 = jnp.mean(dxh * xh, axis=-1, keepdims=True)
        dc_ref[...] = r * (dxh - m1 - xh * m2)

        @pl.when(pl.program_id(0) == 0)
        def _():
            dlg_ref[...] = jnp.zeros_like(dlg_ref)
            dlb_ref[...] = jnp.zeros_like(dlb_ref)

        dlg_ref[...] += jnp.sum(dn * xh, axis=0, keepdims=True)
        dlb_ref[...] += jnp.sum(dn, axis=0, keepdims=True)

    dc, dlg, dlb = _pc(body, name=name, grid=(t // tm,),
                       in_specs=[_row(tm, D), _const((1, D)), _const((1, D)), _row(tm, D)],
                       out_specs=[_row(tm, D), _const((1, D)), _const((1, D))],
                       out_shape=[_sds((t, D), F32), _sds((1, D), F32), _sds((1, D), F32)])(
                           c, lg.reshape(1, D), lb.reshape(1, D), ds)
    return dc, dlg.reshape(D), dlb.reshape(D)


def _dwconv_bwd_w(g, dc, *, name):
    t = g.shape[0]
    tm, tc, halo = min(t, 256), 512, 32

    def body(x_ref, h_ref, dc_ref, dw_ref, db_ref, pad):
        i = pl.program_id(1)
        pad[pl.ds(0, halo), :] = jnp.where(i == 0, 0.0, h_ref[...])
        pad[pl.ds(halo, tm), :] = x_ref[...]
        d = dc_ref[...]

        @pl.when(i == 0)
        def _():
            dw_ref[...] = jnp.zeros_like(dw_ref)
            db_ref[...] = jnp.zeros_like(db_ref)

        db_ref[...] += jnp.sum(d, axis=0, keepdims=True)
        for k in range(CK):
            dw_ref[pl.ds(k, 1), :] += jnp.sum(d * pad[pl.ds(halo - (CK - 1) + k, tm), :], axis=0, keepdims=True)

    tile = pl.BlockSpec((tm, tc), lambda j, i: (i, j))
    dw, db = _pc(body, name=name, grid=(D // tc, t // tm),
                 in_specs=[tile, _halo_before(tm, halo, tc), tile],
                 out_specs=[pl.BlockSpec((CK, tc), lambda j, i: (0, j)), pl.BlockSpec((1, tc), lambda j, i: (0, j))],
                 out_shape=[_sds((CK, D), F32), _sds((1, D), F32)],
                 scratch=[pltpu.VMEM((halo + tm, tc), F32)])(g, g, dc)
    return dw, db.reshape(D)


def _colsum(x, *, name):
    t, c = x.shape
    tm = min(t, 512)

    def body(x_ref, o_ref):
        @pl.when(pl.program_id(0) == 0)
        def _():
            o_ref[...] = jnp.zeros_like(o_ref)

        o_ref[...] += jnp.sum(x_ref[...], axis=0, keepdims=True)

    return _pc(body, name=name, grid=(t // tm,), in_specs=[_row(tm, c)], out_specs=_const((1, c)),
               out_shape=_sds((1, c), F32))(x).reshape(c)


def _head_expand(n_lanes, first_head=0):
    h = lax.broadcasted_iota(jnp.int32, (NHP, n_lanes), 0)
    j = lax.broadcasted_iota(jnp.int32, (NHP, n_lanes), 1)
    return (h == first_head + j // HD).astype(F32)


def _softplus(v):
    return jnp.maximum(v, 0.0) + jnp.log(1.0 + jnp.exp(-jnp.abs(v)))


def _chunk_tri(n, upper):
    ri = lax.broadcasted_iota(jnp.int32, (n, n), 0)
    ci = lax.broadcasted_iota(jnp.int32, (n, n), 1)
    keep = (ri <= ci) if upper else (ri >= ci)
    return (keep & (ri // Q == ci // Q)).astype(F32)


def _split3(x):
    hi = x.astype(BF16)
    r1 = x - hi.astype(F32)
    mid = r1.astype(BF16)
    return hi, mid, (r1 - mid.astype(F32)).astype(BF16)


def _dot_sel(x, sel, sel_left=False):
    s = sel.astype(BF16)
    parts = [jnp.dot(s, t, preferred_element_type=F32) if sel_left else jnp.dot(t, s, preferred_element_type=F32)
             for t in _split3(x)]
    return (parts[2] + parts[1]) + parts[0]


def _ssd_pre(dtraw, dtb, alog, *, name):
    t = dtraw.shape[0]
    tm = min(t, 256)

    def body(r_ref, b_ref, al_ref, dtx_ref, acsx_ref, acst_ref):
        dt = _softplus(r_ref[...] + b_ref[...])
        da = dt * (-jnp.exp(al_ref[...]))
        acs = _dot_sel(da, _chunk_tri(tm, False), sel_left=True)
        expand = _head_expand(DI)
        dtx_ref[...] = _dot_sel(dt, expand)
        acsx_ref[...] = _dot_sel(acs, expand)
        acst_ref[...] = acs.T

    return _pc(body, name=name, grid=(t // tm,), in_specs=[_row(tm, NHP), _const((1, NHP)), _const((1, NHP))],
               out_specs=[_row(tm, DI), _row(tm, DI), pl.BlockSpec((NHP, tm), lambda i: (0, i))],
               out_shape=[_sds((t, DI), F32), _sds((t, DI), F32), _sds((NHP, t), F32)])(dtraw, dtb, alog)


def _ssd_post(dacs, dxdt, dtraw, dtb, alog, *, name):
    t = dtraw.shape[0]
    tm = min(t, 256)

    def body(g1_ref, g2_ref, r_ref, b_ref, al_ref, o_ref, db_ref, da_ref):
        h = lax.broadcasted_iota(jnp.int32, (DI, NHP), 1)
        j = lax.broadcasted_iota(jnp.int32, (DI, NHP), 0)
        red = (h == j // HD).astype(F32)
        dda = _dot_sel(_dot_sel(g1_ref[...], red), _chunk_tri(tm, True), sel_left=True)
        v = r_ref[...] + b_ref[...]
        ddt = dda * (-jnp.exp(al_ref[...])) + _dot_sel(g2_ref[...], red)
        draw = ddt * _sigmoid(v)
        o_ref[...] = draw.astype(BF16)

        @pl.when(pl.program_id(0) == 0)
        def _():
            db_ref[...] = jnp.zeros_like(db_ref)
            da_ref[...] = jnp.zeros_like(da_ref)

        db_ref[...] += jnp.sum(draw, axis=0, keepdims=True)
        da_ref[...] += jnp.sum(dda * _softplus(v), axis=0, keepdims=True)

    return _pc(body, name=name, grid=(t // tm,),
               in_specs=[_row(tm, DI), _row(tm, DI), _row(tm, NHP), _const((1, NHP)), _const((1, NHP))],
               out_specs=[_row(tm, NHP), _const((1, NHP)), _const((1, NHP))],
               out_shape=[_sds((t, NHP), BF16), _sds((1, NHP), F32), _sds((1, NHP), F32)])(dacs, dxdt, dtraw, dtb, alog)


def _decay_pair(acs_x, acst_ref, p, tri):
    lane = lax.broadcasted_iota(jnp.int32, (Q, 2 * HD), 1)
    v = acs_x[:, 2 * HD * p:2 * HD * (p + 1)]
    swapped = pltpu.roll(v, HD, axis=1)
    out = []
    for hh in range(2):
        colb = jnp.where((lane < HD) if hh == 0 else (lane >= HD), v, swapped)
        rowb = acst_ref[pl.ds(2 * p + hh, 1), :]
        out.append(jnp.exp(jnp.where(tri, colb - rowb, -1e30)))
    return out


def _ssd_fwd(xbc, dtx, acs_x, acs_t, d_x, *, name):
    t = xbc.shape[0]
    nc = t // Q
    hpg = NH // NG

    def body(x_ref, b_ref, c_ref, dtx_ref, acsx_ref, acst_ref, d_ref, y_ref, sin_ref, state):
        @pl.when(pl.program_id(1) == 0)
        def _():
            state[...] = jnp.zeros_like(state)

        xs = x_ref[...]
        tri = lax.broadcasted_iota(jnp.int32, (Q, Q), 0) >= lax.broadcasted_iota(jnp.int32, (Q, Q), 1)
        acs_x = acsx_ref[...]
        atot_x = acsx_ref[pl.ds(Q - 1, 1), :]
        xd = xs * dtx_ref[...]
        xb = xd.astype(BF16)
        bb = b_ref[...]
        cb16 = c_ref[...].astype(BF16)
        cbm = lax.dot_general(cb16, bb.astype(BF16), (((1,), (1,)), ((), ())), preferred_element_type=F32)
        s_in = state[...]
        sin_ref[0] = s_in
        y = jnp.dot(cb16, s_in.astype(BF16), preferred_element_type=F32) * jnp.exp(acs_x) + xs * d_ref[...]
        lane = lax.broadcasted_iota(jnp.int32, (Q, 2 * HD), 1)
        for p in range(hpg // 2):
            xp = xb[:, 2 * HD * p:2 * HD * (p + 1)]
            yp = jnp.zeros((Q, 2 * HD), F32)
            for hh, lm in enumerate(_decay_pair(acs_x, acst_ref, p, tri)):
                m = (cbm * lm).astype(BF16)
                xm = jnp.where((lane >= HD) if hh == 1 else (lane < HD), xp, jnp.zeros_like(xp))
                yp = yp + jnp.dot(m, xm, preferred_element_type=F32)
            y_ref[:, 2 * HD * p:2 * HD * (p + 1)] = y[:, 2 * HD * p:2 * HD * (p + 1)] + yp
        dec = jnp.exp(atot_x - acs_x)
        s_new = jnp.dot(bb.T.astype(BF16), (xd * dec).astype(BF16), preferred_element_type=F32)
        state[...] = jnp.exp(atot_x) * s_in + s_new

    grp = lambda g, c: (c, g)
    return _pc(body, name=name, grid=(NG, nc),
               in_specs=[pl.BlockSpec((Q, GW), grp),
                         pl.BlockSpec((Q, NS), lambda g, c: (c, DI // NS + g)),
                         pl.BlockSpec((Q, NS), lambda g, c: (c, DI // NS + NG + g)),
                         pl.BlockSpec((Q, GW), grp),
                         pl.BlockSpec((Q, GW), grp),
                         pl.BlockSpec((hpg, Q), lambda g, c: (g, c)),
                         pl.BlockSpec((1, GW), lambda g, c: (0, g))],
               out_specs=[pl.BlockSpec((Q, GW), grp), pl.BlockSpec((1, NS, GW), lambda g, c: (c, 0, g))],
               out_shape=[_sds((t, DI), F32), _sds((nc, NS, DI), F32)],
               scratch=[pltpu.VMEM((NS, GW), F32)])(xbc, xbc, xbc, dtx, acs_x, acs_t, d_x)


def _ssd_bwd(xbc, dtx, acs_x, acs_t, d_x, sin, dy, *, name):
    t = xbc.shape[0]
    nc = t // Q
    hpg = NH // NG

    def body(x_ref, b_ref, c_ref, dtx_ref, acsx_ref, acst_ref, d_ref, sin_ref, dy_ref,
             dx_ref, db_ref, dc_ref, dacs_ref, dxdt_ref, ddx_ref, dstate):
        @pl.when(pl.program_id(1) == 0)
        def _():
            dstate[...] = jnp.zeros_like(dstate)
            ddx_ref[...] = jnp.zeros_like(ddx_ref)

        xs = x_ref[...]
        dtxv = dtx_ref[...]
        gy = dy_ref[...]
        tri = lax.broadcasted_iota(jnp.int32, (Q, Q), 0) >= lax.broadcasted_iota(jnp.int32, (Q, Q), 1)
        acs_x = acsx_ref[...]
        atot_x = acsx_ref[pl.ds(Q - 1, 1), :]
        xd = xs * dtxv
        xb = xd.astype(BF16)
        gb = gy.astype(BF16)
        bb = b_ref[...]
        cc = c_ref[...]
        bb16, cc16 = bb.astype(BF16), cc.astype(BF16)
        cbm = lax.dot_general(cc16, bb16, (((1,), (1,)), ((), ())), preferred_element_type=F32)
        s_in = sin_ref[0]
        s_in16 = s_in.astype(BF16)
        ds_next = dstate[...]
        ds16 = ds_next.astype(BF16)
        ecs = jnp.exp(acs_x)
        dec = jnp.exp(atot_x - acs_x)
        etot = jnp.exp(atot_x)

        ddx_ref[...] += jnp.sum(gy * xs, axis=0, keepdims=True)
        z = jnp.dot(cc16, s_in16, preferred_element_type=F32)
        dz16 = (gy * ecs).astype(BF16)
        dacs = gy * z * ecs
        dcm = lax.dot_general(dz16, s_in16, (((1,), (1,)), ((), ())), preferred_element_type=F32)
        ds_here = jnp.dot(cc.T.astype(BF16), dz16, preferred_element_type=F32)
        u = jnp.dot(bb16, ds16, preferred_element_type=F32)
        dxd = u * dec
        w_dec = u * xd * dec
        dacs = dacs - w_dec
        datot = jnp.sum(w_dec, axis=0, keepdims=True) + jnp.sum(etot * ds_next * s_in, axis=0, keepdims=True)
        dbm = lax.dot_general((xd * dec).astype(BF16), ds16, (((1,), (1,)), ((), ())), preferred_element_type=F32)
        dstate[...] = etot * ds_next + ds_here

        lane = lax.broadcasted_iota(jnp.int32, (Q, 2 * HD), 1)
        lane_x = lax.broadcasted_iota(jnp.int32, (1, GW), 1)
        dcb = jnp.zeros((Q, Q), F32)
        dx_pairs = []
        for p in range(hpg // 2):
            sl = slice(2 * HD * p, 2 * HD * (p + 1))
            xp = xb[:, sl]
            gp = gb[:, sl]
            dxp = jnp.zeros((Q, 2 * HD), F32)
            for hh, lm in enumerate(_decay_pair(acs_x, acst_ref, p, tri)):
                r = 2 * p + hh
                mm = cbm * lm
                half = (lane >= HD) if hh == 1 else (lane < HD)
                xm = jnp.where(half, xp, jnp.zeros_like(xp))
                gm = jnp.where(half, gp, jnp.zeros_like(gp))
                mt16 = mm.T.astype(BF16)
                dxp = dxp + jnp.dot(mt16, gm, preferred_element_type=F32)
                dm = lax.dot_general(gm, xp, (((1,), (1,)), ((), ())), preferred_element_type=F32)
                dmt = lax.dot_general(xm, gp, (((1,), (1,)), ((), ())), preferred_element_type=F32)
                dcb = dcb + dm * lm
                wdiff = jnp.sum(dm * mm - dmt * mm.T, axis=1, keepdims=True)
                dacs = dacs + wdiff * (lane_x == HD * r).astype(F32)
            dx_pairs.append(dxp)
        dxd = dxd + jnp.concatenate(dx_pairs, axis=1)
        dcm = dcm + jnp.dot(dcb.astype(BF16), bb16, preferred_element_type=F32)
        dbm = dbm + jnp.dot(dcb.T.astype(BF16), cc16, preferred_element_type=F32)
        db_ref[...] = dbm
        dc_ref[...] = dcm

        last = lax.broadcasted_iota(jnp.int32, (Q, GW), 0) == Q - 1
        dacs_ref[...] = dacs + jnp.where(last, datot, 0.0)
        dxdt_ref[...] = dxd * xs
        dx_ref[...] = dxd * dtxv + gy * d_ref[...]

    rev = lambda g, c: (nc - 1 - c, g)
    return _pc(body, name=name, grid=(NG, nc),
               in_specs=[pl.BlockSpec((Q, GW), rev),
                         pl.BlockSpec((Q, NS), lambda g, c: (nc - 1 - c, DI // NS + g)),
                         pl.BlockSpec((Q, NS), lambda g, c: (nc - 1 - c, DI // NS + NG + g)),
                         pl.BlockSpec((Q, GW), rev),
                         pl.BlockSpec((Q, GW), rev),
                         pl.BlockSpec((hpg, Q), lambda g, c: (g, nc - 1 - c)),
                         pl.BlockSpec((1, GW), lambda g, c: (0, g)),
                         pl.BlockSpec((1, NS, GW), lambda g, c: (nc - 1 - c, 0, g)),
                         pl.BlockSpec((Q, GW), rev)],
               out_specs=[pl.BlockSpec((Q, GW), rev), pl.BlockSpec((Q, NS), rev), pl.BlockSpec((Q, NS), rev),
                          pl.BlockSpec((Q, GW), rev), pl.BlockSpec((Q, GW), rev),
                          pl.BlockSpec((1, GW), lambda g, c: (0, g))],
               out_shape=[_sds((t, DI), F32), _sds((t, NG * NS), F32), _sds((t, NG * NS), F32), _sds((t, DI), F32),
                          _sds((t, DI), F32), _sds((1, DI), F32)],
               scratch=[pltpu.VMEM((NS, GW), F32)])(xbc, xbc, xbc, dtx, acs_x, acs_t, d_x, sin, dy)


def _gated_norm_fwd(y, z, gn, *, name):
    t = y.shape[0]
    tm = min(t, 256)

    def body(y_ref, z_ref, g_ref, o_ref):
        for k in range(NG):
            sl = slice(GW * k, GW * (k + 1))
            zz = z_ref[:, sl]
            h = y_ref[:, sl] * (zz * _sigmoid(zz))
            r = lax.rsqrt(jnp.mean(h * h, axis=-1, keepdims=True) + EPS)
            o_ref[:, sl] = (h * r * g_ref[:, sl]).astype(BF16)

    return _pc(body, name=name, grid=(t // tm,), in_specs=[_row(tm, DI), _row(tm, DI), _const((1, DI))],
               out_specs=_row(tm, DI), out_shape=_sds((t, DI), BF16))(y, z, gn.reshape(1, DI))


def _gated_norm_bwd(y, z, gn, dout, *, name):
    t = y.shape[0]
    tm = min(t, 256)

    def body(y_ref, z_ref, g_ref, do_ref, dy_ref, dz_ref, dg_ref):
        @pl.when(pl.program_id(0) == 0)
        def _():
            dg_ref[...] = jnp.zeros_like(dg_ref)

        for k in range(NG):
            sl = slice(GW * k, GW * (k + 1))
            zz = z_ref[:, sl]
            yy = y_ref[:, sl]
            sz = zz * _sigmoid(zz)
            h = yy * sz
            r = lax.rsqrt(jnp.mean(h * h, axis=-1, keepdims=True) + EPS)
            d = do_ref[:, sl]
            dg_ref[:, sl] += jnp.sum(d * h * r, axis=0, keepdims=True)
            dgd = d * g_ref[:, sl]
            dot = jnp.mean(dgd * h, axis=-1, keepdims=True)
            dh = r * dgd - h * (r * r * r * dot)
            dy_ref[:, sl] = dh * sz
            dz_ref[:, sl] = (dh * yy * _dsilu(zz)).astype(BF16)

    dy, dz, dg = _pc(body, name=name, grid=(t // tm,),
                     in_specs=[_row(tm, DI), _row(tm, DI), _const((1, DI)), _row(tm, DI)],
                     out_specs=[_row(tm, DI), _row(tm, DI), _const((1, DI))],
                     out_shape=[_sds((t, DI), F32), _sds((t, DI), BF16), _sds((1, DI), F32)])(y, z, gn.reshape(1, DI), dout)
    return dy, dz, dg.reshape(DI)


def _pad_heads(v):
    return jnp.pad(v.reshape(1, NH), ((0, 0), (0, NHP - NH)))


def _ssm_fwd(h, p):
    u = _rms_fwd(h, p["norm"], name="ssm_rms_fwd")
    z = _mm(u, p["wz"], "nn", name="ssm_in_z")
    xp = _mm(u, p["wxbc"], "nn", name="ssm_in_xbc")
    dtraw = _mm(u, p["wdt"], "nn", name="ssm_in_dt")
    xbc = _conv4_fwd(xp, p["conv_w"], p["conv_b"], name="ssm_conv_fwd")
    dtb, alog = _pad_heads(p["dt_bias"]), _pad_heads(p["a_log"])
    d_x = jnp.repeat(p["d"], HD).reshape(1, DI)
    dtx, acs_x, acs_t = _ssd_pre(dtraw, dtb, alog, name="ssd_pre")
    y, sin = _ssd_fwd(xbc, dtx, acs_x, acs_t, d_x, name="ssd_fwd")
    yn = _gated_norm_fwd(y, z, p["gate_norm"], name="ssm_gate_fwd")
    h2 = _mm(yn, p["wout"], "nn", res=h, name="ssm_out")
    return h2, (h, u, z, xp, dtraw, xbc, dtx, acs_x, acs_t, d_x, dtb, alog, y, sin, yn)


def _ssm_bwd(dh2, p, saved):
    h, u, z, xp, dtraw, xbc, dtx, acs_x, acs_t, d_x, dtb, alog, y, sin, yn = saved
    dyn = _mm(dh2, p["wout"], "nt", name="ssm_out_dx")
    g = {"w_out": _mm(yn, dh2, "tn", out_dtype=BF16, name="ssm_out_dw")}
    dy, dz, g["gate_norm"] = _gated_norm_bwd(y, z, p["gate_norm"], dyn, name="ssm_gate_bwd")
    dxs, dbm, dcm, dacs, dxdt, ddx = _ssd_bwd(xbc, dtx, acs_x, acs_t, d_x, sin, dy, name="ssd_bwd")
    ddtraw, ddtb, dav = _ssd_post(dacs, dxdt, dtraw, dtb, alog, name="ssd_post")
    dpre, g["conv_w"], g["conv_b"] = _conv4_bwd_act(xp, p["conv_w"], p["conv_b"], jnp.concatenate([dxs, dbm, dcm], axis=1),
                                                    name="ssm_conv_bwd_act")
    dxp = _conv_bwd_x(dpre, p["conv_w"], KC, name="ssm_conv_bwd_x", out_dtype=BF16)
    du = _mm(dz, p["wz"], "nt", name="ssm_in_dx_z")
    du = _mm(dxp, p["wxbc"], "nt", res=du, name="ssm_in_dx_xbc")
    du = _mm(ddtraw, p["wdt"], "nt", res=du, name="ssm_in_dx_dt")
    g["w_in"] = jnp.concatenate([_mm(u, dz, "tn", out_dtype=BF16, name="ssm_in_dw_z"),
                                 _mm(u, dxp, "tn", out_dtype=BF16, name="ssm_in_dw_xbc"),
                                 _mm(u, ddtraw, "tn", out_dtype=BF16, name="ssm_in_dw_dt")[:, :NH]], axis=1)
    dh, g["norm"] = _rms_bwd(h, p["norm"], du, dh2, name="ssm_rms_bwd")
    g["dt_bias"] = ddtb[0, :NH]
    g["a_log"] = dav[0, :NH] * (-jnp.exp(p["a_log"]))
    g["d"] = ddx.reshape(NH, HD).sum(-1)
    return dh, g


def _cv_fwd(h, p):
    u = _rms_fwd(h, p["norm"], name="cv_rms_fwd")
    pre = _mm(u, p["wpw1"], "nn", bias=p["b_pw1"].reshape(1, 2 * D), name="cv_pw1")
    gl = _glu_fwd(pre, name="cv_glu_fwd")
    c, s = _dwconv_ln_fwd(gl, p["dw_w"], p["dw_b"], p["ln_g"], p["ln_b"], name="cv_dwconv_ln_fwd")
    h2 = _mm(s, p["wpw2"], "nn", bias=p["b_pw2"].reshape(1, D), res=h, name="cv_pw2")
    return h2, (h, u, pre, gl, c, s)


def _cv_bwd(dh2, p, saved):
    h, u, pre, gl, c, s = saved
    ds = _mm(dh2, p["wpw2"], "nt", name="cv_pw2_dx")
    g = {"w_pw2": _mm(s, dh2, "tn", out_dtype=BF16, name="cv_pw2_dw"), "b_pw2": _colsum(dh2, name="cv_pw2_db")}
    dc, g["ln_g"], g["ln_b"] = _ln_silu_bwd(c, p["ln_g"], p["ln_b"], ds, name="cv_ln_bwd")
    dgl = _conv_bwd_x(dc, p["dw_w"], CK, name="cv_dwconv_bwd_x", out_dtype=F32)
    g["dw_w"], g["dw_b"] = _dwconv_bwd_w(gl, dc, name="cv_dwconv_bwd_w")
    dpre, g["b_pw1"] = _glu_bwd(pre, dgl, name="cv_glu_bwd")
    du = _mm(dpre, p["wpw1"], "nt", name="cv_pw1_dx")
    g["w_pw1"] = _mm(u, dpre, "tn", out_dtype=BF16, name="cv_pw1_dw")
    dh, g["norm"] = _rms_bwd(h, p["norm"], du, dh2, name="cv_rms_bwd")
    return dh, g


def _ffn_fwd(h, p):
    u = _rms_fwd(h, p["norm"], name="ffn_rms_fwd")
    gu = _mm(u, p["wgu"], "nn", name="ffn_gate_up")
    a = _swiglu_fwd(gu, name="ffn_act_fwd")
    h2 = _mm(a, p["wd"], "nn", res=h, name="ffn_down")
    return h2, (h, u, gu, a)


def _ffn_bwd(dh2, p, saved):
    h, u, gu, a = saved
    da = _mm(dh2, p["wd"], "nt", name="ffn_down_dx")
    g = {"w_down": _mm(a, dh2, "tn", out_dtype=BF16, name="ffn_down_dw")}
    dgu = _swiglu_bwd(gu, da, name="ffn_act_bwd")
    du = _mm(dgu, p["wgu"], "nt", name="ffn_gate_up_dx")
    dwgu = _mm(u, dgu, "tn", out_dtype=BF16, name="ffn_gate_up_dw")
    g["w_gate"], g["w_up"] = dwgu[:, :DFF], dwgu[:, DFF:]
    dh, g["norm"] = _rms_bwd(h, p["norm"], du, dh2, name="ffn_rms_bwd")
    return dh, g


def _fwd_bwd(x, target, depth, layer_weights, final_norm, layer_done=None):
    h, tape, weights = x, [], []
    for i in range(depth):
        mix_w, ffn_w = layer_weights(i, h)
        h, s_mix = (_ssm_fwd if i % 2 == 0 else _cv_fwd)(h, mix_w)
        h, s_ffn = _ffn_fwd(h, ffn_w)
        tape.append((s_mix, s_ffn))
        weights.append((mix_w, ffn_w))
    lsum, dh, g_final = _final_loss(h, final_norm, target, name="loss_head")
    g_mix, g_ffn = [None] * depth, [None] * depth
    for i in reversed(range(depth)):
        s_mix, s_ffn = tape[i]
        dh, g_ffn[i] = _ffn_bwd(dh, weights[i][1], s_ffn)
        dh, g_mix[i] = (_ssm_bwd if i % 2 == 0 else _cv_bwd)(dh, weights[i][0], s_mix)
        if layer_done is not None:
            dh = layer_done(i, g_mix[i], g_ffn[i], dh)
    return lsum, dh, g_mix, g_ffn, g_final


ANY = pl.BlockSpec(memory_space=pl.ANY)


def _place():
    x, y, c = lax.axis_index("x"), lax.axis_index("y"), lax.axis_index("c")
    return x, y, c, [(1 - x, y), (x, 1 - y), (1 - x, 1 - y)]


def _remote(src, dst, ssem, rsem, dev):
    return pltpu.make_async_remote_copy(src_ref=src, dst_ref=dst, send_sem=ssem, recv_sem=rsem, device_id=dev,
                                        device_id_type=MESH)


def _comm_call(body, name, ins, out_shape, n_sems):
    return pl.pallas_call(
        body, name=name, in_specs=[ANY] * len(ins), out_specs=[ANY] * len(out_shape), out_shape=out_shape,
        scratch_shapes=[pltpu.SemaphoreType.DMA((k,)) for k in n_sems])(*ins)


HBM = pl.BlockSpec(memory_space=pltpu.HBM)
SEM = pl.BlockSpec(memory_space=pltpu.SEMAPHORE)
EFFECT = pltpu.SideEffectType.DATAFLOW_SIDE_EFFECTING


def _gather_copies(src, land, ssem, rsem):
    x, y, c, chips = _place()
    me, pairs = 2 * x + y, []
    for a in range(len(src)):
        h = src[a].shape[0] // 2
        rows = pl.ds(c * h, h)
        for j, (px, py) in enumerate(chips):
            sems = (ssem.at[3 * a + j], rsem.at[3 * a + j], (px, py, c))
            pairs.append((_remote(src[a].at[rows], land[a].at[me, rows], *sems),
                          _remote(src[a].at[rows], land[a].at[2 * px + py, rows], *sems)))
    return pairs


def _chip_copies(src, land, ssem, rsem):
    x, y, c, chips = _place()
    me, pairs = 2 * x + y, []
    for a in range(len(src)):
        for j, (px, py) in enumerate(chips):
            sems = (ssem.at[3 * a + j], rsem.at[3 * a + j], (px, py, c))
            pairs.append((_remote(src[a].at[2 * px + py], land[a].at[me], *sems),
                          _remote(src[a].at[2 * px + py], land[a].at[2 * px + py], *sems)))
    return pairs


def _split_start(copies, srcs, land_shapes, dep, name):
    n = len(srcs)

    def body(*refs):
        for send, _ in copies(refs[:n], refs[n:2 * n], refs[2 * n + 1], refs[2 * n + 2]):
            send.start()
        refs[-1][...] = jnp.zeros_like(refs[-1])

    lands = [pltpu.with_memory_space_constraint(lax.empty(s, w.dtype), pltpu.HBM) for s, w in zip(land_shapes, srcs)]
    out = pl.pallas_call(
        body, name=name,
        out_shape=(pltpu.SemaphoreType.DMA((3 * n,)), pltpu.SemaphoreType.DMA((3 * n,)),
                   *[pltpu.HBM(w.shape, w.dtype) for w in srcs], *[pltpu.HBM(w.shape, w.dtype) for w in lands],
                   _sds((8, 128), F32)),
        in_specs=[HBM] * (2 * n) + [ANY], out_specs=(SEM, SEM, *[HBM] * (2 * n), pl.BlockSpec(memory_space=pltpu.VMEM)),
        input_output_aliases={a: 2 + a for a in range(2 * n)},
        compiler_params=pltpu.CompilerParams(has_side_effects=EFFECT))(
            *[pltpu.with_memory_space_constraint(w, pltpu.HBM) for w in srcs], *lands, dep)
    return out[0], out[1], list(out[2:2 + n]), list(out[2 + n:2 + 2 * n]), out[-1]


def _split_wait(copies, ssem, rsem, srcs, lands, after, name):
    n = len(srcs)

    def body(*refs):
        for send, recv in copies(refs[:n], refs[n:2 * n], refs[2 * n], refs[2 * n + 1]):
            send.wait_send()
            recv.wait_recv()

    out = pl.pallas_call(
        body, name=name,
        out_shape=(*[pltpu.HBM(w.shape, w.dtype) for w in srcs], *[pltpu.HBM(w.shape, w.dtype) for w in lands]),
        in_specs=[HBM] * (2 * n) + [SEM, SEM, ANY], out_specs=tuple([HBM] * (2 * n)),
        input_output_aliases={a: a for a in range(2 * n)},
        compiler_params=pltpu.CompilerParams(has_side_effects=EFFECT))(*srcs, *lands, ssem, rsem, after)
    return list(out[:n]), list(out[n:])


def _gather_start(ws, dep, tag):
    return _split_start(_gather_copies, ws, [(4,) + w.shape for w in ws], dep, "gather_start_" + tag)


def _gather_wait(ssem, rsem, srcs, lands, after, tag):
    return _split_wait(_gather_copies, ssem, rsem, srcs, lands, after, "gather_wait_" + tag)


def _gather_forward(lands, tag):
    n = len(lands)

    def body(*refs):
        land = refs[n:2 * n]
        ssem, rsem = refs[2 * n:]
        x, y, c, chips = _place()
        sib = (x, y, 1 - c)
        cps = []
        for a in range(n):
            h = lands[a].shape[1] // 2
            for j, (px, py) in enumerate(chips):
                blk = land[a].at[2 * px + py, pl.ds(c * h, h)]
                cps.append(_remote(blk, blk, ssem.at[3 * a + j], rsem.at[3 * a + j], sib))
        for cp in cps:
            cp.start()
        for a in range(n):
            h = lands[a].shape[1] // 2
            for j, (px, py) in enumerate(chips):
                blk = land[a].at[2 * px + py, pl.ds((1 - c) * h, h)]
                _remote(blk, blk, ssem.at[3 * a + j], rsem.at[3 * a + j], sib).wait_recv()
        for cp in cps:
            cp.wait_send()

    return pl.pallas_call(
        body, name="gather_forward_" + tag, in_specs=[ANY] * n, out_specs=[ANY] * n,
        out_shape=[_sds(w.shape, w.dtype) for w in lands], input_output_aliases={a: a for a in range(n)},
        scratch_shapes=[pltpu.SemaphoreType.DMA((3 * n,)), pltpu.SemaphoreType.DMA((3 * n,))])(*lands)


def _exchange_pairs(gs, tag):
    n = len(gs)

    def body(*refs):
        ins, got = refs[:n], refs[n:2 * n]
        ssem, rsem = refs[2 * n:]
        x, y, c, _ = _place()
        cps = []
        for a in range(n):
            lh = gs[a].shape[1] // 2
            cps.append(_remote(ins[a].at[:, pl.ds((1 - c) * lh, lh)], got[a], ssem.at[a], rsem.at[a], (x, y, 1 - c)))
        for cp in cps:
            cp.start()
        for cp in cps:
            cp.wait()

    halves = [_sds((4, g.shape[1] // 2) + g.shape[2:], g.dtype) for g in gs]
    return _comm_call(body, "reduce_pair_exchange_" + tag, gs, halves, [n, n])


def _exchange_final(fs, tag):
    n = len(fs)

    def body(*refs):
        ins, outs = refs[:n], refs[n:2 * n]
        ssem, rsem = refs[2 * n:]
        x, y, c, _ = _place()
        cps = [_remote(ins[a], outs[a], ssem.at[a], rsem.at[a], (x, y, 1 - c)) for a in range(n)]
        for cp in cps:
            cp.start()
        for cp in cps:
            cp.wait()

    return _comm_call(body, "reduce_final_exchange_" + tag, fs, [_sds(f.shape, f.dtype) for f in fs], [n, n])


def _rows_tile(r, c, n_arrays, itemsize=4):
    cap = max(16, (16 * 1024 * 1024) // (2 * n_arrays * c * itemsize))
    best = 16 if r % 16 == 0 else r
    for t in range(16, min(r, cap) + 1, 16):
        if r % t == 0:
            best = t
    return best


def _add_pair(g, got, core, *, name):
    _, r, c = g.shape
    rows = r // 2
    tm = _rows_tile(rows, c, 3)

    def body(core_ref, p_ref, q_ref, o_ref):
        o_ref[...] = (p_ref[...].astype(F32) + q_ref[...].astype(F32)).astype(BF16)

    spec = pl.BlockSpec((None, tm, c), lambda s, i, core_ref: (s, i, 0))
    grid_spec = pltpu.PrefetchScalarGridSpec(
        num_scalar_prefetch=1, grid=(4, rows // tm),
        in_specs=[pl.BlockSpec((None, None, tm, c), lambda s, i, core_ref: (s, core_ref[0], i, 0)), spec], out_specs=spec)
    out = pl.pallas_call(body, name=name, grid_spec=grid_spec, out_shape=_sds((4, rows, c), BF16),
                         compiler_params=pltpu.CompilerParams(dimension_semantics=("arbitrary", "arbitrary"),
                                                              vmem_limit_bytes=VMEM_LIMIT))(
                                                                  core, g.reshape(4, 2, rows, c), got.reshape(4, rows, c))
    return out.reshape(got.shape)


def _sum_slots(rb, *, name):
    shape = rb.shape[1:]
    c = shape[-1]
    r = rb.size // (4 * c)
    tm = _rows_tile(r, c, 6)

    def body(i_ref, o_ref):
        acc = i_ref[0].astype(F32)
        for k in range(1, 4):
            acc = acc + i_ref[k].astype(F32)
        o_ref[...] = acc

    return _pc(body, name=name, grid=(r // tm,), in_specs=[pl.BlockSpec((4, tm, c), lambda i: (0, i, 0))],
               out_specs=_row(tm, c), out_shape=_sds((r, c), F32))(rb.reshape(4, r, c)).reshape(shape)


def _reduce_start(gs, core, tag):
    got = _exchange_pairs(gs, tag)
    core1 = core.reshape(1).astype(jnp.int32)
    ps = [_add_pair(g, q, core1, name="reduce_pair_add") for g, q in zip(gs, got)]
    ssem, rsem, ps, rbs, token = _split_start(_chip_copies, ps, [p.shape for p in ps], got[0], "reduce_start_" + tag)
    return (ssem, rsem, ps, rbs), token


def _reduce_finish(state, after, chip, core, tag):
    ps, rbs = _split_wait(_chip_copies, *state, after, "reduce_wait_" + tag)
    rbs = [lax.dynamic_update_index_in_dim(b, lax.dynamic_index_in_dim(p, chip, 0, keepdims=False), chip, 0)
           for b, p in zip(rbs, ps)]
    fs = [_sum_slots(b, name="reduce_chip_sum") for b in rbs]
    theirs = _exchange_final(fs, tag)
    return [jnp.where(core == 0, jnp.concatenate([f, t], axis=0), jnp.concatenate([t, f], axis=0))
            for f, t in zip(fs, theirs)]


def _gather8(v, reduce):
    m = v.shape[0]

    def body(x_ref, *rest):
        if reduce:
            sum_ref, out_ref, send_sems, recv_sems, local_sem = rest
        else:
            out_ref, send_sems, recv_sems, local_sem = rest
        x, y, c, chips = _place()
        me, sibling = (x, y, c), (x, y, 1 - c)

        def rows(px, py, pc):
            return out_ref.at[pl.ds((4 * px + 2 * py + pc) * m, m), :]

        def copy(k, block, to, src=None):
            return _remote(rows(*block) if src is None else src, rows(*block), send_sems.at[k], recv_sems.at[k], to)

        mine = pltpu.make_async_copy(x_ref, rows(*me), local_sem)
        mine.start()
        first = [copy(0, me, sibling, src=x_ref)]
        first += [copy(1 + j, me, (*chip, c), src=x_ref) for j, chip in enumerate(chips)]
        for cp in first:
            cp.start()
        passed = [copy(4 + j, (*chip, c), sibling) for j, chip in enumerate(chips)]
        for j, chip in enumerate(chips):
            copy(1 + j, (*chip, c), me).wait_recv()
            passed[j].start()
        copy(0, sibling, me).wait_recv()
        for j, chip in enumerate(chips):
            copy(4 + j, (*chip, 1 - c), me).wait_recv()
        for cp in first + passed:
            cp.wait_send()
        mine.wait()
        if reduce:
            acc = out_ref[pl.ds(0, m), :]
            for k in range(1, 8):
                acc = acc + out_ref[pl.ds(k * m, m), :]
            sum_ref[...] = acc

    vm = pl.BlockSpec(memory_space=pltpu.VMEM)
    out_shape = [_sds((8 * m, 128), F32)]
    if reduce:
        out_shape = [_sds((m, 128), F32)] + out_shape
    out = pl.pallas_call(
        body, name="allreduce_small" if reduce else "gather_small", in_specs=[vm], out_specs=[vm] * len(out_shape),
        out_shape=out_shape,
        scratch_shapes=[pltpu.SemaphoreType.DMA((7,)), pltpu.SemaphoreType.DMA((7,)), pltpu.SemaphoreType.DMA(())],
        compiler_params=pltpu.CompilerParams(vmem_limit_bytes=VMEM_LIMIT))(v)
    return out[0]


def _adamw(w, g, m, v, *, name):
    shape = w.shape
    c = shape[-1]
    r = w.size // c
    tm = _rows_tile(r, c, 7) if r % 16 == 0 else r

    def body(w_ref, g_ref, m_ref, v_ref, d_ref, mo_ref, vo_ref):
        gg = g_ref[...]
        mm = B1 * m_ref[...] + (1.0 - B1) * gg
        vv = B2 * v_ref[...] + (1.0 - B2) * (gg * gg)
        mo_ref[...] = mm
        vo_ref[...] = vv
        m_hat = mm / (1.0 - B1 ** STEP)
        v_hat = vv / (1.0 - B2 ** STEP)
        d_ref[...] = -LR * (m_hat / (jnp.sqrt(v_hat) + AEPS) + WD * w_ref[...])

    outs = _pc(body, name=name, grid=(r // tm,), in_specs=[_row(tm, c)] * 4, out_specs=[_row(tm, c)] * 3,
               out_shape=[_sds((r, c), F32)] * 3)(*[t.reshape(r, c) for t in (w, g, m, v)])
    return [o.reshape(shape) for o in outs]


WEIGHTS = ["ssm_norm", "ssm_w_in", "ssm_conv_w", "ssm_conv_b", "ssm_dt_bias", "ssm_a_log", "ssm_d", "ssm_gate_norm",
           "ssm_w_out", "cv_norm", "cv_w_pw1", "cv_b_pw1", "cv_dw_w", "cv_dw_b", "cv_ln_g", "cv_ln_b", "cv_w_pw2",
           "cv_b_pw2", "ffn_norm", "ffn_w_gate", "ffn_w_up", "ffn_w_down", "final_norm"]
BIG = {"ssm_w_in": "col", "ssm_w_out": "row", "cv_w_pw1": "col", "cv_w_pw2": "row",
       "ffn_w_gate": "col", "ffn_w_up": "col", "ffn_w_down": "row"}
SMALL_SHARDED = {"ssm_conv_w": 2, "cv_norm": 1, "cv_b_pw1": 1, "cv_dw_w": 2, "cv_dw_b": 1, "cv_ln_g": 1, "cv_ln_b": 1,
                 "cv_b_pw2": 1}
SMALL = [n for n in WEIGHTS if n not in BIG]
LANES = 128


def _pack(arrays):
    flat = jnp.concatenate([a.reshape(-1) for a in arrays])
    rows = -(-flat.size // (8 * LANES)) * 8
    return jnp.pad(flat, (0, rows * LANES - flat.size)).reshape(rows, LANES)


def _unpack(packed, shapes):
    flat, out, off = packed.reshape(-1), [], 0
    for s in shapes:
        n = 1
        for d_ in s:
            n *= d_
        out.append(flat[off:off + n].reshape(s))
        off += n
    return out


def _unshard(parts, axis):
    return jnp.concatenate(parts, axis=axis)


def kernel(x, ssm_norm, ssm_w_in, ssm_conv_w, ssm_conv_b, ssm_dt_bias, ssm_a_log, ssm_d, ssm_gate_norm, ssm_w_out, cv_norm, cv_w_pw1, cv_b_pw1, cv_dw_w, cv_dw_b, cv_ln_g, cv_ln_b, cv_w_pw2, cv_b_pw2, ffn_norm, ffn_w_gate, ffn_w_up, ffn_w_down, final_norm, loss_target, m_ssm_norm, m_ssm_w_in, m_ssm_conv_w, m_ssm_conv_b, m_ssm_dt_bias, m_ssm_a_log, m_ssm_d, m_ssm_gate_norm, m_ssm_w_out, m_cv_norm, m_cv_w_pw1, m_cv_b_pw1, m_cv_dw_w, m_cv_dw_b, m_cv_ln_g, m_cv_ln_b, m_cv_w_pw2, m_cv_b_pw2, m_ffn_norm, m_ffn_w_gate, m_ffn_w_up, m_ffn_w_down, m_final_norm, v_ssm_norm, v_ssm_w_in, v_ssm_conv_w, v_ssm_conv_b, v_ssm_dt_bias, v_ssm_a_log, v_ssm_d, v_ssm_gate_norm, v_ssm_w_out, v_cv_norm, v_cv_w_pw1, v_cv_b_pw1, v_cv_dw_w, v_cv_dw_b, v_cv_ln_g, v_cv_ln_b, v_cv_w_pw2, v_cv_b_pw2, v_ffn_norm, v_ffn_w_gate, v_ffn_w_up, v_ffn_w_down, v_final_norm):
    a = dict(locals())
    chip = 2 * lax.axis_index("x") + lax.axis_index("y")
    n_ssm, n_cv, depth = ssm_norm.shape[0], cv_norm.shape[0], ffn_norm.shape[0]

    own = {n: a[n].astype(BF16) for n in BIG}
    sh_names = list(SMALL_SHARDED)
    got = _gather8(_pack([a[n] for n in sh_names]), reduce=False)
    got = got.reshape(8, -1)[0::2]
    per_chip = [_unpack(got[s], [a[n].shape for n in sh_names]) for s in range(4)]
    full = {n: a[n] for n in SMALL}
    for k, n in enumerate(sh_names):
        full[n] = _unshard([per_chip[s][k] for s in range(4)], SMALL_SHARDED[n])

    def matrices_of(i):
        mixer = ["ssm_w_in", "ssm_w_out"] if i % 2 == 0 else ["cv_w_pw1", "cv_w_pw2"]
        return [(n, i // 2) for n in mixer] + [(n, i) for n in ("ffn_w_gate", "ffn_w_up", "ffn_w_down")]

    in_flight = {}

    def start_gather(i, dep):
        in_flight[i] = _gather_start([own[n][l] for n, l in matrices_of(i)], dep, str(i))
        return in_flight[i][-1]

    def layer_weights(i, h):
        if i == 0:
            start_gather(0, h)
        ssem, rsem, srcs, lands, _ = in_flight.pop(i)
        srcs, lands = _gather_wait(ssem, rsem, srcs, lands, h, str(i))
        lands = _gather_forward(lands, str(i))
        token = start_gather(i + 1, lands[0])[0, 0] if i + 1 < depth else 0.0
        got = {n: (src, land) for (n, _), src, land in zip(matrices_of(i), srcs, lands)}

        def whole(*names):
            parts = [jnp.where(chip == s, got[n][0], got[n][1][s]) for n in names for s in range(4)]
            return jnp.concatenate(parts, axis=1 if BIG[names[0]] == "col" else 0)

        j = i // 2
        if i % 2 == 0:
            w_in = whole("ssm_w_in")
            mix = dict(norm=full["ssm_norm"][j] + token, wz=w_in[:, :DI], wxbc=w_in[:, DI:DI + CD],
                       wdt=jnp.pad(w_in[:, DI + CD:], ((0, 0), (0, NHP - NH))),
                       conv_w=full["ssm_conv_w"][j], conv_b=full["ssm_conv_b"][j], dt_bias=full["ssm_dt_bias"][j],
                       a_log=full["ssm_a_log"][j], d=full["ssm_d"][j], gate_norm=full["ssm_gate_norm"][j],
                       wout=whole("ssm_w_out"))
        else:
            mix = dict(norm=full["cv_norm"][j] + token, wpw1=whole("cv_w_pw1"), b_pw1=full["cv_b_pw1"][j],
                       dw_w=full["cv_dw_w"][j], dw_b=full["cv_dw_b"][j], ln_g=full["cv_ln_g"][j], ln_b=full["cv_ln_b"][j],
                       wpw2=whole("cv_w_pw2"), b_pw2=full["cv_b_pw2"][j])
        return mix, dict(norm=full["ffn_norm"][i], wgu=whole("ffn_w_gate", "ffn_w_up"), wd=whole("ffn_w_down"))

    core = lax.axis_index("c")
    reducing = {}

    def layer_done(i, g_mixer, g_ffn_i, dh):
        slots = []
        for n, _ in matrices_of(i):
            g = (g_ffn_i if n.startswith("ffn_") else g_mixer)[n.split("_", 1)[1]]
            if BIG[n] == "col":
                slots.append(g.reshape(g.shape[0], 4, g.shape[1] // 4).transpose(1, 0, 2))
            else:
                slots.append(g.reshape(4, g.shape[0] // 4, g.shape[1]))
        reducing[i], token = _reduce_start(slots, core, str(i))
        return dh + token[0, 0]

    lsum, grad_x, g_mix, g_ffn, g_final = _fwd_bwd(x[0], loss_target[0], depth, layer_weights, full["final_norm"],
                                                   layer_done)
    g_ssm, g_cv = g_mix[0::2], g_mix[1::2]
    loss = (0.5 / D) * lax.psum(jnp.sum(lsum), ("x", "y", "c"))

    local_small = {"final_norm": g_final, "ffn_norm": jnp.stack([g["norm"] for g in g_ffn])}
    for n in SMALL:
        if n.startswith("ssm_"):
            local_small[n] = jnp.stack([g[n[4:]] for g in g_ssm])
        elif n.startswith("cv_"):
            local_small[n] = jnp.stack([g[n[3:]] for g in g_cv])
    summed = _unpack(_gather8(_pack([local_small[n] for n in SMALL]), reduce=True), [full[n].shape for n in SMALL])
    grads = {}
    for n, g in zip(SMALL, summed):
        if n in SMALL_SHARDED:
            ax = SMALL_SHARDED[n]
            g = lax.dynamic_slice_in_dim(g, chip * a[n].shape[ax], a[n].shape[ax], axis=ax)
        grads[n] = g

    per_layer, after = {n: [None] * a[n].shape[0] for n in BIG}, grad_x
    for i in reversed(range(depth)):
        done = _reduce_finish(reducing.pop(i), after, chip, core, str(i))
        for (n, l), g in zip(matrices_of(i), done):
            per_layer[n][l] = g
        after = done[0]
    for n in BIG:
        grads[n] = jnp.stack(per_layer[n])

    delta, new_m, new_v = {}, {}, {}
    for n in BIG:
        delta[n], new_m[n], new_v[n] = _adamw(a[n], grads[n], a["m_" + n], a["v_" + n], name="adamw_" + n)
    shapes = [a[n].shape for n in SMALL]
    upd = _adamw(_pack([a[n] for n in SMALL]), _pack([grads[n] for n in SMALL]), _pack([a["m_" + n] for n in SMALL]),
                 _pack([a["v_" + n] for n in SMALL]), name="adamw_small")
    for dst, packed in zip((delta, new_m, new_v), upd):
        dst.update(zip(SMALL, _unpack(packed, shapes)))

    return (loss, grad_x[None], *[grads[n] for n in WEIGHTS], *[delta[n] for n in WEIGHTS],
            *[new_m[n] for n in WEIGHTS], *[new_v[n] for n in WEIGHTS])
```

```python
import functools

import jax
import jax.numpy as jnp
from jax import lax
from jax.experimental import pallas as pl
from jax.experimental.pallas import tpu as pltpu

F32 = jnp.float32
BF16 = jnp.bfloat16
HI = lax.Precision.HIGHEST
MESH = pl.DeviceIdType.MESH

D = 1024
DI = 2048
HD = 64
NH = 32
NG = 4
GW = DI // NG
NS = 128
KC = 4
CD = DI + 2 * NG * NS
Q = 128
DFF = 2816
CK = 31
EPS = 1e-5
NHP = 128

LR, B1, B2, AEPS, WD, STEP = 0.001, 0.9, 0.999, 1e-08, 0.01, 10

VMEM_LIMIT = 56 * 1024 * 1024
MM_VMEM_BUDGET = 36 * 1024 * 1024


def _pc(body, *, name, grid, in_specs, out_specs, out_shape, scratch=()):
    return pl.pallas_call(
        body, name=name, grid=grid, in_specs=in_specs, out_specs=out_specs, out_shape=out_shape,
        scratch_shapes=list(scratch),
        compiler_params=pltpu.CompilerParams(dimension_semantics=("arbitrary",) * len(grid),
                                             vmem_limit_bytes=VMEM_LIMIT))


def _sds(shape, dtype):
    return jax.ShapeDtypeStruct(tuple(shape), dtype)


def _row(tm, c):
    return pl.BlockSpec((tm, c), lambda i: (i, 0))


def _const(shape):
    return pl.BlockSpec(tuple(shape), lambda i: (0,) * len(shape))


def _tile(n, cap):
    if n <= cap:
        return n
    best = 128
    for t in range(128, cap + 1, 128):
        if n % t == 0:
            best = t
    return best


def _sigmoid(x):
    return 1.0 / (1.0 + jnp.exp(-x))


def _dsilu(x):
    s = _sigmoid(x)
    return s * (1.0 + x * (1.0 - s))


def _mm(a, b, mode, *, name, out_dtype=F32, bias=None, res=None, tm=1024, after=None):
    if mode == "nn":
        (m, k), (k2, n) = a.shape, b.shape
    elif mode == "nt":
        (m, k), (n, k2) = a.shape, b.shape
    else:
        (k, m), (k2, n) = a.shape, b.shape
    assert k == k2, (a.shape, b.shape, mode)
    tm = _tile(m, tm if k <= 2048 else 512)
    so = jnp.dtype(out_dtype).itemsize + (4 if res is not None else 0)
    fits = [c for c in range(128, min(n, 2048) + 1, 128) if n % c == 0 and
            2 * (tm * k * a.dtype.itemsize + c * k * b.dtype.itemsize + tm * c * so) <= MM_VMEM_BUDGET]
    tn = n if n <= 128 else max(fits)
    dn = {"nn": (((1,), (0,)), ((), ())), "nt": (((1,), (1,)), ((), ())), "tn": (((0,), (0,)), ((), ()))}[mode]
    nb, nr = bias is not None, res is not None

    def body(*refs):
        acc = lax.dot_general(refs[0][...].astype(BF16), refs[1][...].astype(BF16), dn, preferred_element_type=F32)
        if nb:
            acc = acc + refs[2][...]
        if nr:
            acc = acc + refs[2 + nb][...]
        refs[-1][...] = acc.astype(out_dtype)

    a_spec = pl.BlockSpec((k, tm), lambda i, j: (0, i)) if mode == "tn" else pl.BlockSpec((tm, k), lambda i, j: (i, 0))
    b_spec = pl.BlockSpec((tn, k), lambda i, j: (j, 0)) if mode == "nt" else pl.BlockSpec((k, tn), lambda i, j: (0, j))
    ins, specs = [a, b], [a_spec, b_spec]
    if nb:
        ins.append(bias)
        specs.append(pl.BlockSpec((1, tn), lambda i, j: (0, j)))
    if nr:
        ins.append(res)
        specs.append(pl.BlockSpec((tm, tn), lambda i, j: (i, j)))
    if after is not None:
        ins.append(after)
        specs.append(pl.BlockSpec((8, 128), lambda i, j: (0, 0)))
    return _pc(body, name=name, grid=(m // tm, n // tn), in_specs=specs,
               out_specs=pl.BlockSpec((tm, tn), lambda i, j: (i, j)), out_shape=_sds((m, n), out_dtype))(*ins)


def _rms_fwd(h, g, *, name):
    t = h.shape[0]
    tm = min(t, 512)

    def body(h_ref, g_ref, u_ref):
        x = h_ref[...]
        r = lax.rsqrt(jnp.mean(x * x, axis=-1, keepdims=True) + EPS)
        u_ref[...] = (x * r * g_ref[...]).astype(BF16)

    return _pc(body, name=name, grid=(t // tm,), in_specs=[_row(tm, D), _const((1, D))], out_specs=_row(tm, D),
               out_shape=_sds((t, D), BF16))(h, g.reshape(1, D))


def _rms_bwd(h, g, du, dres, *, name):
    t = h.shape[0]
    tm = min(t, 512)

    def body(h_ref, g_ref, du_ref, dres_ref, dh_ref, dg_ref):
        x = h_ref[...]
        r = lax.rsqrt(jnp.mean(x * x, axis=-1, keepdims=True) + EPS)
        dy = du_ref[...]
        dyg = dy * g_ref[...]
        dot = jnp.mean(dyg * x, axis=-1, keepdims=True)
        dh_ref[...] = dres_ref[...] + r * dyg - x * (r * r * r * dot)

        @pl.when(pl.program_id(0) == 0)
        def _():
            dg_ref[...] = jnp.zeros_like(dg_ref)

        dg_ref[...] += jnp.sum(dy * x * r, axis=0, keepdims=True)

    dh, dg = _pc(body, name=name, grid=(t // tm,),
                 in_specs=[_row(tm, D), _const((1, D)), _row(tm, D), _row(tm, D)],
                 out_specs=[_row(tm, D), _const((1, D))],
                 out_shape=[_sds((t, D), F32), _sds((1, D), F32)])(h, g.reshape(1, D), du, dres)
    return dh, dg.reshape(D)


def _final_loss(h, g, target, *, name):
    t = h.shape[0]
    tm = min(t, 512)

    def body(h_ref, g_ref, t_ref, dh_ref, dg_ref, l_ref):
        x = h_ref[...]
        gg = g_ref[...]
        r = lax.rsqrt(jnp.mean(x * x, axis=-1, keepdims=True) + EPS)
        err = x * r * gg - t_ref[...]
        dy = err * (1.0 / D)
        dyg = dy * gg
        dot = jnp.mean(dyg * x, axis=-1, keepdims=True)
        dh_ref[...] = r * dyg - x * (r * r * r * dot)

        @pl.when(pl.program_id(0) == 0)
        def _():
            dg_ref[...] = jnp.zeros_like(dg_ref)
            l_ref[...] = jnp.zeros_like(l_ref)

        dg_ref[...] += jnp.sum(dy * x * r, axis=0, keepdims=True)
        l_ref[...] += jnp.sum(err * err, axis=0, keepdims=True)

    dh, dg, l = _pc(body, name=name, grid=(t // tm,),
                    in_specs=[_row(tm, D), _const((1, D)), _row(tm, D)],
                    out_specs=[_row(tm, D), _const((1, D)), _const((1, D))],
                    out_shape=[_sds((t, D), F32), _sds((1, D), F32), _sds((1, D), F32)])(h, g.reshape(1, D), target)
    return l, dh, dg.reshape(D)


def _swiglu_fwd(gu, *, name):
    t = gu.shape[0]
    tm = min(t, 256)

    def body(gu_ref, a_ref):
        gate = gu_ref[:, :DFF]
        up = gu_ref[:, DFF:]
        a_ref[...] = (gate * _sigmoid(gate) * up).astype(BF16)

    return _pc(body, name=name, grid=(t // tm,), in_specs=[_row(tm, 2 * DFF)], out_specs=_row(tm, DFF),
               out_shape=_sds((t, DFF), BF16))(gu)


def _swiglu_bwd(gu, da, *, name):
    t = gu.shape[0]
    tm = min(t, 256)

    def body(gu_ref, da_ref, d_ref):
        gate = gu_ref[:, :DFF]
        up = gu_ref[:, DFF:]
        dav = da_ref[...]
        d_ref[:, :DFF] = (dav * up * _dsilu(gate)).astype(BF16)
        d_ref[:, DFF:] = (dav * gate * _sigmoid(gate)).astype(BF16)

    return _pc(body, name=name, grid=(t // tm,), in_specs=[_row(tm, 2 * DFF), _row(tm, DFF)],
               out_specs=_row(tm, 2 * DFF), out_shape=_sds((t, 2 * DFF), BF16))(gu, da)


def _halo_before(tm, halo, tc):
    return pl.BlockSpec((halo, tc), lambda j, i: (jnp.maximum(i * (tm // halo) - 1, 0), j))


def _halo_after(tm, halo, tc, t):
    return pl.BlockSpec((halo, tc), lambda j, i: (jnp.minimum((i + 1) * (tm // halo), t // halo - 1), j))


def _conv4_fwd(xp, w, b, *, name):
    t = xp.shape[0]
    tm, tc, halo = min(t, 512), 1024, 8

    def body(x_ref, h_ref, w_ref, b_ref, o_ref, pad):
        i = pl.program_id(1)
        pad[pl.ds(0, halo), :] = jnp.where(i == 0, 0.0, h_ref[...])
        pad[pl.ds(halo, tm), :] = x_ref[...]
        acc = jnp.zeros((tm, tc), F32) + b_ref[...]
        for k in range(KC):
            acc = acc + pad[pl.ds(halo - (KC - 1) + k, tm), :] * w_ref[pl.ds(k, 1), :]
        o_ref[...] = acc * _sigmoid(acc)

    tile = pl.BlockSpec((tm, tc), lambda j, i: (i, j))
    return _pc(body, name=name, grid=(CD // tc, t // tm),
               in_specs=[tile, _halo_before(tm, halo, tc), pl.BlockSpec((KC, tc), lambda j, i: (0, j)),
                         pl.BlockSpec((1, tc), lambda j, i: (0, j))],
               out_specs=tile, out_shape=_sds((t, CD), F32), scratch=[pltpu.VMEM((halo + tm, tc), F32)])(xp, xp, w, b.reshape(1, CD))


def _conv4_bwd_act(xp, w, b, dout, *, name):
    t = xp.shape[0]
    tm, tc, halo = min(t, 512), 1024, 8

    def body(x_ref, h_ref, w_ref, b_ref, do_ref, dp_ref, dw_ref, db_ref, pad):
        i = pl.program_id(1)
        pad[pl.ds(0, halo), :] = jnp.where(i == 0, 0.0, h_ref[...])
        pad[pl.ds(halo, tm), :] = x_ref[...]
        acc = jnp.zeros((tm, tc), F32) + b_ref[...]
        for k in range(KC):
            acc = acc + pad[pl.ds(halo - (KC - 1) + k, tm), :] * w_ref[pl.ds(k, 1), :]
        dpre = do_ref[...] * _dsilu(acc)
        dp_ref[...] = dpre

        @pl.when(i == 0)
        def _():
            dw_ref[...] = jnp.zeros_like(dw_ref)
            db_ref[...] = jnp.zeros_like(db_ref)

        db_ref[...] += jnp.sum(dpre, axis=0, keepdims=True)
        for k in range(KC):
            dw_ref[pl.ds(k, 1), :] += jnp.sum(dpre * pad[pl.ds(halo - (KC - 1) + k, tm), :], axis=0, keepdims=True)

    tile = pl.BlockSpec((tm, tc), lambda j, i: (i, j))
    wspec = pl.BlockSpec((KC, tc), lambda j, i: (0, j))
    bspec = pl.BlockSpec((1, tc), lambda j, i: (0, j))
    dp, dw, db = _pc(body, name=name, grid=(CD // tc, t // tm),
                     in_specs=[tile, _halo_before(tm, halo, tc), wspec, bspec, tile],
                     out_specs=[tile, wspec, bspec],
                     out_shape=[_sds((t, CD), F32), _sds((KC, CD), F32), _sds((1, CD), F32)],
                     scratch=[pltpu.VMEM((halo + tm, tc), F32)])(xp, xp, w, b.reshape(1, CD), dout)
    return dp, dw, db.reshape(CD)


def _conv_bwd_x(dy, w, kk, *, name, out_dtype):
    t, c = dy.shape
    halo = 8 if kk <= 8 else 32
    tm, tc = (min(t, 512), min(c, 1024)) if kk <= 8 else (min(t, 256), min(c, 512))

    def body(y_ref, h_ref, w_ref, o_ref, pad):
        i = pl.program_id(1)
        pad[pl.ds(0, tm), :] = y_ref[...]
        pad[pl.ds(tm, halo), :] = jnp.where(i == t // tm - 1, 0.0, h_ref[...])
        acc = jnp.zeros((tm, tc), F32)
        for k in range(kk):
            acc = acc + pad[pl.ds(kk - 1 - k, tm), :] * w_ref[pl.ds(k, 1), :]
        o_ref[...] = acc.astype(out_dtype)

    tile = pl.BlockSpec((tm, tc), lambda j, i: (i, j))
    return _pc(body, name=name, grid=(c // tc, t // tm),
               in_specs=[tile, _halo_after(tm, halo, tc, t), pl.BlockSpec((kk, tc), lambda j, i: (0, j))],
               out_specs=tile, out_shape=_sds((t, c), out_dtype), scratch=[pltpu.VMEM((tm + halo, tc), F32)])(dy, dy, w)


def _glu_fwd(p, *, name):
    t = p.shape[0]
    tm = min(t, 512)

    def body(p_ref, o_ref):
        o_ref[...] = p_ref[:, :D] * _sigmoid(p_ref[:, D:])

    return _pc(body, name=name, grid=(t // tm,), in_specs=[_row(tm, 2 * D)], out_specs=_row(tm, D),
               out_shape=_sds((t, D), F32))(p)


def _glu_bwd(p, dg, *, name):
    t = p.shape[0]
    tm = min(t, 512)

    def body(p_ref, dg_ref, dp_ref, db_ref):
        a = p_ref[:, :D]
        s = _sigmoid(p_ref[:, D:])
        d = dg_ref[...]
        da = d * s
        dgate = d * a * s * (1.0 - s)
        dp_ref[:, :D] = da.astype(BF16)
        dp_ref[:, D:] = dgate.astype(BF16)

        @pl.when(pl.program_id(0) == 0)
        def _():
            db_ref[...] = jnp.zeros_like(db_ref)

        db_ref[:, :D] += jnp.sum(da, axis=0, keepdims=True)
        db_ref[:, D:] += jnp.sum(dgate, axis=0, keepdims=True)

    dp, db = _pc(body, name=name, grid=(t // tm,), in_specs=[_row(tm, 2 * D), _row(tm, D)],
                 out_specs=[_row(tm, 2 * D), _const((1, 2 * D))],
                 out_shape=[_sds((t, 2 * D), BF16), _sds((1, 2 * D), F32)])(p, dg)
    return dp, db.reshape(2 * D)


def _dwconv_ln_fwd(g, w, b, lg, lb, *, name):
    t = g.shape[0]
    tm, halo = min(t, 256), 32

    def body(x_ref, h_ref, w_ref, b_ref, lg_ref, lb_ref, c_ref, s_ref, pad):
        i = pl.program_id(0)
        pad[pl.ds(0, halo), :] = jnp.where(i == 0, 0.0, h_ref[...])
        pad[pl.ds(halo, tm), :] = x_ref[...]
        acc = jnp.zeros((tm, D), F32) + b_ref[...]
        for k in range(CK):
            acc = acc + pad[pl.ds(halo - (CK - 1) + k, tm), :] * w_ref[pl.ds(k, 1), :]
        c_ref[...] = acc
        mu = jnp.mean(acc, axis=-1, keepdims=True)
        xc = acc - mu
        r = lax.rsqrt(jnp.mean(xc * xc, axis=-1, keepdims=True) + EPS)
        n = xc * r * lg_ref[...] + lb_ref[...]
        s_ref[...] = (n * _sigmoid(n)).astype(BF16)

    c, s = _pc(body, name=name, grid=(t // tm,),
               in_specs=[_row(tm, D), pl.BlockSpec((halo, D), lambda i: (jnp.maximum(i * (tm // halo) - 1, 0), 0)),
                         _const((CK, D)), _const((1, D)), _const((1, D)), _const((1, D))],
               out_specs=[_row(tm, D), _row(tm, D)], out_shape=[_sds((t, D), F32), _sds((t, D), BF16)],
               scratch=[pltpu.VMEM((halo + tm, D), F32)])(g, g, w, b.reshape(1, D), lg.reshape(1, D), lb.reshape(1, D))
    return c, s


def _ln_silu_bwd(c, lg, lb, ds, *, name):
    t = c.shape[0]
    tm = min(t, 512)

    def body(c_ref, lg_ref, lb_ref, ds_ref, dc_ref, dlg_ref, dlb_ref):
        x = c_ref[...]
        mu = jnp.mean(x, axis=-1, keepdims=True)
        xc = x - mu
        r = lax.rsqrt(jnp.mean(xc * xc, axis=-1, keepdims=True) + EPS)
        xh = xc * r
        n = xh * lg_ref[...] + lb_ref[...]
        dn = ds_ref[...] * _dsilu(n)
        dxh = dn * lg_ref[...]
        m1 = jnp.mean(dxh, axis=-1, keepdims=True)
        m2 = jnp.mean(dxh * xh, axis=-1, keepdims=True)
        dc_ref[...] = r * (dxh - m1 - xh * m2)

        @pl.when(pl.program_id(0) == 0)
        def _():
            dlg_ref[...] = jnp.zeros_like(dlg_ref)
            dlb_ref[...] = jnp.zeros_like(dlb_ref)

        dlg_ref[...] += jnp.sum(dn * xh, axis=0, keepdims=True)
        dlb_ref[...] += jnp.sum(dn, axis=0, keepdims=True)

    dc, dlg, dlb = _pc(body, name=name, grid=(t // tm,),
                       in_specs=[_row(tm, D), _const((1, D)), _const((1, D)), _row(tm, D)],
                       out_specs=[_row(tm, D), _const((1, D)), _const((1, D))],
                       out_shape=[_sds((t, D), F32), _sds((1, D), F32), _sds((1, D), F32)])(
                           c, lg.reshape(1, D), lb.reshape(1, D), ds)
    return dc, dlg.reshape(D), dlb.reshape(D)


def _dwconv_bwd_w(g, dc, *, name):
    t = g.shape[0]
    tm, tc, halo = min(t, 256), 512, 32

    def body(x_ref, h_ref, dc_ref, dw_ref, db_ref, pad):
        i = pl.program_id(1)
        pad[pl.ds(0, halo), :] = jnp.where(i == 0, 0.0, h_ref[...])
        pad[pl.ds(halo, tm), :] = x_ref[...]
        d = dc_ref[...]

        @pl.when(i == 0)
        def _():
            dw_ref[...] = jnp.zeros_like(dw_ref)
            db_ref[...] = jnp.zeros_like(db_ref)

        db_ref[...] += jnp.sum(d, axis=0, keepdims=True)
        for k in range(CK):
            dw_ref[pl.ds(k, 1), :] += jnp.sum(d * pad[pl.ds(halo - (CK - 1) + k, tm), :], axis=0, keepdims=True)

    tile = pl.BlockSpec((tm, tc), lambda j, i: (i, j))
    dw, db = _pc(body, name=name, grid=(D // tc, t // tm),
                 in_specs=[tile, _halo_before(tm, halo, tc), tile],
                 out_specs=[pl.BlockSpec((CK, tc), lambda j, i: (0, j)), pl.BlockSpec((1, tc), lambda j, i: (0, j))],
                 out_shape=[_sds((CK, D), F32), _sds((1, D), F32)],
                 scratch=[pltpu.VMEM((halo + tm, tc), F32)])(g, g, dc)
    return dw, db.reshape(D)


def _colsum(x, *, name):
    t, c = x.shape
    tm = min(t, 512)

    def body(x_ref, o_ref):
        @pl.when(pl.program_id(0) == 0)
        def _():
            o_ref[...] = jnp.zeros_like(o_ref)

        o_ref[...] += jnp.sum(x_ref[...], axis=0, keepdims=True)

    return _pc(body, name=name, grid=(t // tm,), in_specs=[_row(tm, c)], out_specs=_const((1, c)),
               out_shape=_sds((1, c), F32))(x).reshape(c)


def _head_expand(n_lanes, first_head=0):
    h = lax.broadcasted_iota(jnp.int32, (NHP, n_lanes), 0)
    j = lax.broadcasted_iota(jnp.int32, (NHP, n_lanes), 1)
    return (h == first_head + j // HD).astype(F32)


def _softplus(v):
    return jnp.maximum(v, 0.0) + jnp.log(1.0 + jnp.exp(-jnp.abs(v)))


def _chunk_tri(n, upper):
    ri = lax.broadcasted_iota(jnp.int32, (n, n), 0)
    ci = lax.broadcasted_iota(jnp.int32, (n, n), 1)
    keep = (ri <= ci) if upper else (ri >= ci)
    return (keep & (ri // Q == ci // Q)).astype(F32)


def _split3(x):
    hi = x.astype(BF16)
    r1 = x - hi.astype(F32)
    mid = r1.astype(BF16)
    return hi, mid, (r1 - mid.astype(F32)).astype(BF16)


def _dot_sel(x, sel, sel_left=False):
    s = sel.astype(BF16)
    parts = [jnp.dot(s, t, preferred_element_type=F32) if sel_left else jnp.dot(t, s, preferred_element_type=F32)
             for t in _split3(x)]
    return (parts[2] + parts[1]) + parts[0]


def _ssd_pre(dtraw, dtb, alog, *, name):
    t = dtraw.shape[0]
    tm = min(t, 256)

    def body(r_ref, b_ref, al_ref, dtx_ref, acsx_ref, acst_ref):
        dt = _softplus(r_ref[...] + b_ref[...])
        da = dt * (-jnp.exp(al_ref[...]))
        acs = _dot_sel(da, _chunk_tri(tm, False), sel_left=True)
        expand = _head_expand(DI)
        dtx_ref[...] = _dot_sel(dt, expand)
        acsx_ref[...] = _dot_sel(acs, expand)
        acst_ref[...] = acs.T

    return _pc(body, name=name, grid=(t // tm,), in_specs=[_row(tm, NHP), _const((1, NHP)), _const((1, NHP))],
               out_specs=[_row(tm, DI), _row(tm, DI), pl.BlockSpec((NHP, tm), lambda i: (0, i))],
               out_shape=[_sds((t, DI), F32), _sds((t, DI), F32), _sds((NHP, t), F32)])(dtraw, dtb, alog)


def _ssd_post(dacs, dxdt, dtraw, dtb, alog, *, name):
    t = dtraw.shape[0]
    tm = min(t, 256)

    def body(g1_ref, g2_ref, r_ref, b_ref, al_ref, o_ref, db_ref, da_ref):
        h = lax.broadcasted_iota(jnp.int32, (DI, NHP), 1)
        j = lax.broadcasted_iota(jnp.int32, (DI, NHP), 0)
        red = (h == j // HD).astype(F32)
        dda = _dot_sel(_dot_sel(g1_ref[...], red), _chunk_tri(tm, True), sel_left=True)
        v = r_ref[...] + b_ref[...]
        ddt = dda * (-jnp.exp(al_ref[...])) + _dot_sel(g2_ref[...], red)
        draw = ddt * _sigmoid(v)
        o_ref[...] = draw.astype(BF16)

        @pl.when(pl.program_id(0) == 0)
        def _():
            db_ref[...] = jnp.zeros_like(db_ref)
            da_ref[...] = jnp.zeros_like(da_ref)

        db_ref[...] += jnp.sum(draw, axis=0, keepdims=True)
        da_ref[...] += jnp.sum(dda * _softplus(v), axis=0, keepdims=True)

    return _pc(body, name=name, grid=(t // tm,),
               in_specs=[_row(tm, DI), _row(tm, DI), _row(tm, NHP), _const((1, NHP)), _const((1, NHP))],
               out_specs=[_row(tm, NHP), _const((1, NHP)), _const((1, NHP))],
               out_shape=[_sds((t, NHP), BF16), _sds((1, NHP), F32), _sds((1, NHP), F32)])(dacs, dxdt, dtraw, dtb, alog)


def _decay_pair(acs_x, acst_ref, p, tri):
    lane = lax.broadcasted_iota(jnp.int32, (Q, 2 * HD), 1)
    v = acs_x[:, 2 * HD * p:2 * HD * (p + 1)]
    swapped = pltpu.roll(v, HD, axis=1)
    out = []
    for hh in range(2):
        colb = jnp.where((lane < HD) if hh == 0 else (lane >= HD), v, swapped)
        rowb = acst_ref[pl.ds(2 * p + hh, 1), :]
        out.append(jnp.exp(jnp.where(tri, colb - rowb, -1e30)))
    return out


def _ssd_fwd(xbc, dtx, acs_x, acs_t, d_x, *, name):
    t = xbc.shape[0]
    nc = t // Q
    hpg = NH // NG

    def body(x_ref, b_ref, c_ref, dtx_ref, acsx_ref, acst_ref, d_ref, y_ref, sin_ref, state):
        @pl.when(pl.program_id(1) == 0)
        def _():
            state[...] = jnp.zeros_like(state)

        xs = x_ref[...]
        tri = lax.broadcasted_iota(jnp.int32, (Q, Q), 0) >= lax.broadcasted_iota(jnp.int32, (Q, Q), 1)
        acs_x = acsx_ref[...]
        atot_x = acsx_ref[pl.ds(Q - 1, 1), :]
        xd = xs * dtx_ref[...]
        xb = xd.astype(BF16)
        bb = b_ref[...]
        cb16 = c_ref[...].astype(BF16)
        cbm = lax.dot_general(cb16, bb.astype(BF16), (((1,), (1,)), ((), ())), preferred_element_type=F32)
        s_in = state[...]
        sin_ref[0] = s_in
        y = jnp.dot(cb16, s_in.astype(BF16), preferred_element_type=F32) * jnp.exp(acs_x) + xs * d_ref[...]
        lane = lax.broadcasted_iota(jnp.int32, (Q, 2 * HD), 1)
        for p in range(hpg // 2):
            xp = xb[:, 2 * HD * p:2 * HD * (p + 1)]
            yp = jnp.zeros((Q, 2 * HD), F32)
            for hh, lm in enumerate(_decay_pair(acs_x, acst_ref, p, tri)):
                m = (cbm * lm).astype(BF16)
                xm = jnp.where((lane >= HD) if hh == 1 else (lane < HD), xp, jnp.zeros_like(xp))
                yp = yp + jnp.dot(m, xm, preferred_element_type=F32)
            y_ref[:, 2 * HD * p:2 * HD * (p + 1)] = y[:, 2 * HD * p:2 * HD * (p + 1)] + yp
        dec = jnp.exp(atot_x - acs_x)
        s_new = jnp.dot(bb.T.astype(BF16), (xd * dec).astype(BF16), preferred_element_type=F32)
        state[...] = jnp.exp(atot_x) * s_in + s_new

    grp = lambda g, c: (c, g)
    return _pc(body, name=name, grid=(NG, nc),
               in_specs=[pl.BlockSpec((Q, GW), grp),
                         pl.BlockSpec((Q, NS), lambda g, c: (c, DI // NS + g)),
                         pl.BlockSpec((Q, NS), lambda g, c: (c, DI // NS + NG + g)),
                         pl.BlockSpec((Q, GW), grp),
                         pl.BlockSpec((Q, GW), grp),
                         pl.BlockSpec((hpg, Q), lambda g, c: (g, c)),
                         pl.BlockSpec((1, GW), lambda g, c: (0, g))],
               out_specs=[pl.BlockSpec((Q, GW), grp), pl.BlockSpec((1, NS, GW), lambda g, c: (c, 0, g))],
               out_shape=[_sds((t, DI), F32), _sds((nc, NS, DI), F32)],
               scratch=[pltpu.VMEM((NS, GW), F32)])(xbc, xbc, xbc, dtx, acs_x, acs_t, d_x)


def _ssd_bwd(xbc, dtx, acs_x, acs_t, d_x, sin, dy, *, name):
    t = xbc.shape[0]
    nc = t // Q
    hpg = NH // NG

    def body(x_ref, b_ref, c_ref, dtx_ref, acsx_ref, acst_ref, d_ref, sin_ref, dy_ref,
             dx_ref, db_ref, dc_ref, dacs_ref, dxdt_ref, ddx_ref, dstate):
        @pl.when(pl.program_id(1) == 0)
        def _():
            dstate[...] = jnp.zeros_like(dstate)
            ddx_ref[...] = jnp.zeros_like(ddx_ref)

        xs = x_ref[...]
        dtxv = dtx_ref[...]
        gy = dy_ref[...]
        tri = lax.broadcasted_iota(jnp.int32, (Q, Q), 0) >= lax.broadcasted_iota(jnp.int32, (Q, Q), 1)
        acs_x = acsx_ref[...]
        atot_x = acsx_ref[pl.ds(Q - 1, 1), :]
        xd = xs * dtxv
        xb = xd.astype(BF16)
        gb = gy.astype(BF16)
        bb = b_ref[...]
        cc = c_ref[...]
        bb16, cc16 = bb.astype(BF16), cc.astype(BF16)
        cbm = lax.dot_general(cc16, bb16, (((1,), (1,)), ((), ())), preferred_element_type=F32)
        s_in = sin_ref[0]
        s_in16 = s_in.astype(BF16)
        ds_next = dstate[...]
        ds16 = ds_next.astype(BF16)
        ecs = jnp.exp(acs_x)
        dec = jnp.exp(atot_x - acs_x)
        etot = jnp.exp(atot_x)

        ddx_ref[...] += jnp.sum(gy * xs, axis=0, keepdims=True)
        z = jnp.dot(cc16, s_in16, preferred_element_type=F32)
        dz16 = (gy * ecs).astype(BF16)
        dacs = gy * z * ecs
        dcm = lax.dot_general(dz16, s_in16, (((1,), (1,)), ((), ())), preferred_element_type=F32)
        ds_here = jnp.dot(cc.T.astype(BF16), dz16, preferred_element_type=F32)
        u = jnp.dot(bb16, ds16, preferred_element_type=F32)
        dxd = u * dec
        w_dec = u * xd * dec
        dacs = dacs - w_dec
        datot = jnp.sum(w_dec, axis=0, keepdims=True) + jnp.sum(etot * ds_next * s_in, axis=0, keepdims=True)
        dbm = lax.dot_general((xd * dec).astype(BF16), ds16, (((1,), (1,)), ((), ())), preferred_element_type=F32)
        dstate[...] = etot * ds_next + ds_here

        lane = lax.broadcasted_iota(jnp.int32, (Q, 2 * HD), 1)
        lane_x = lax.broadcasted_iota(jnp.int32, (1, GW), 1)
        dcb = jnp.zeros((Q, Q), F32)
        dx_pairs = []
        for p in range(hpg // 2):
            sl = slice(2 * HD * p, 2 * HD * (p + 1))
            xp = xb[:, sl]
            gp = gb[:, sl]
            dxp = jnp.zeros((Q, 2 * HD), F32)
            for hh, lm in enumerate(_decay_pair(acs_x, acst_ref, p, tri)):
                r = 2 * p + hh
                mm = cbm * lm
                half = (lane >= HD) if hh == 1 else (lane < HD)
                xm = jnp.where(half, xp, jnp.zeros_like(xp))
                gm = jnp.where(half, gp, jnp.zeros_like(gp))
                mt16 = mm.T.astype(BF16)
                dxp = dxp + jnp.dot(mt16, gm, preferred_element_type=F32)
                dm = lax.dot_general(gm, xp, (((1,), (1,)), ((), ())), preferred_element_type=F32)
                dmt = lax.dot_general(xm, gp, (((1,), (1,)), ((), ())), preferred_element_type=F32)
                dcb = dcb + dm * lm
                wdiff = jnp.sum(dm * mm - dmt * mm.T, axis=1, keepdims=True)
                dacs = dacs + wdiff * (lane_x == HD * r).astype(F32)
            dx_pairs.append(dxp)
        dxd = dxd + jnp.concatenate(dx_pairs, axis=1)
        dcm = dcm + jnp.dot(dcb.astype(BF16), bb16, preferred_element_type=F32)
        dbm = dbm + jnp.dot(dcb.T.astype(BF16), cc16, preferred_element_type=F32)
        db_ref[...] = dbm
        dc_ref[...] = dcm

        last = lax.broadcasted_iota(jnp.int32, (Q, GW), 0) == Q - 1
        dacs_ref[...] = dacs + jnp.where(last, datot, 0.0)
        dxdt_ref[...] = dxd * xs
        dx_ref[...] = dxd * dtxv + gy * d_ref[...]

    rev = lambda g, c: (nc - 1 - c, g)
    return _pc(body, name=name, grid=(NG, nc),
               in_specs=[pl.BlockSpec((Q, GW), rev),
                         pl.BlockSpec((Q, NS), lambda g, c: (nc - 1 - c, DI // NS + g)),
                         pl.BlockSpec((Q, NS), lambda g, c: (nc - 1 - c, DI // NS + NG + g)),
                         pl.BlockSpec((Q, GW), rev),
                         pl.BlockSpec((Q, GW), rev),
                         pl.BlockSpec((hpg, Q), lambda g, c: (g, nc - 1 - c)),
                         pl.BlockSpec((1, GW), lambda g, c: (0, g)),
                         pl.BlockSpec((1, NS, GW), lambda g, c: (nc - 1 - c, 0, g)),
                         pl.BlockSpec((Q, GW), rev)],
               out_specs=[pl.BlockSpec((Q, GW), rev), pl.BlockSpec((Q, NS), rev), pl.BlockSpec((Q, NS), rev),
                          pl.BlockSpec((Q, GW), rev), pl.BlockSpec((Q, GW), rev),
                          pl.BlockSpec((1, GW), lambda g, c: (0, g))],
               out_shape=[_sds((t, DI), F32), _sds((t, NG * NS), F32), _sds((t, NG * NS), F32), _sds((t, DI), F32),
                          _sds((t, DI), F32), _sds((1, DI), F32)],
               scratch=[pltpu.VMEM((NS, GW), F32)])(xbc, xbc, xbc, dtx, acs_x, acs_t, d_x, sin, dy)


def _gated_norm_fwd(y, z, gn, *, name):
    t = y.shape[0]
    tm = min(t, 256)

    def body(y_ref, z_ref, g_ref, o_ref):
        for k in range(NG):
            sl = slice(GW * k, GW * (k + 1))
            zz = z_ref[:, sl]
            h = y_ref[:, sl] * (zz * _sigmoid(zz))
            r = lax.rsqrt(jnp.mean(h * h, axis=-1, keepdims=True) + EPS)
            o_ref[:, sl] = (h * r * g_ref[:, sl]).astype(BF16)

    return _pc(body, name=name, grid=(t // tm,), in_specs=[_row(tm, DI), _row(tm, DI), _const((1, DI))],
               out_specs=_row(tm, DI), out_shape=_sds((t, DI), BF16))(y, z, gn.reshape(1, DI))


def _gated_norm_bwd(y, z, gn, dout, *, name):
    t = y.shape[0]
    tm = min(t, 256)

    def body(y_ref, z_ref, g_ref, do_ref, dy_ref, dz_ref, dg_ref):
        @pl.when(pl.program_id(0) == 0)
        def _():
            dg_ref[...] = jnp.zeros_like(dg_ref)

        for k in range(NG):
            sl = slice(GW * k, GW * (k + 1))
            zz = z_ref[:, sl]
            yy = y_ref[:, sl]
            sz = zz * _sigmoid(zz)
            h = yy * sz
            r = lax.rsqrt(jnp.mean(h * h, axis=-1, keepdims=True) + EPS)
            d = do_ref[:, sl]
            dg_ref[:, sl] += jnp.sum(d * h * r, axis=0, keepdims=True)
            dgd = d * g_ref[:, sl]
            dot = jnp.mean(dgd * h, axis=-1, keepdims=True)
            dh = r * dgd - h * (r * r * r * dot)
            dy_ref[:, sl] = dh * sz
            dz_ref[:, sl] = (dh * yy * _dsilu(zz)).astype(BF16)

    dy, dz, dg = _pc(body, name=name, grid=(t // tm,),
                     in_specs=[_row(tm, DI), _row(tm, DI), _const((1, DI)), _row(tm, DI)],
                     out_specs=[_row(tm, DI), _row(tm, DI), _const((1, DI))],
                     out_shape=[_sds((t, DI), F32), _sds((t, DI), BF16), _sds((1, DI), F32)])(y, z, gn.reshape(1, DI), dout)
    return dy, dz, dg.reshape(DI)


def _pad_heads(v):
    return jnp.pad(v.reshape(1, NH), ((0, 0), (0, NHP - NH)))


def _ssm_fwd(h, p):
    u = _rms_fwd(h, p["norm"], name="ssm_rms_fwd")
    z = _mm(u, p["wz"], "nn", name="ssm_in_z")
    xp = _mm(u, p["wxbc"], "nn", name="ssm_in_xbc")
    dtraw = _mm(u, p["wdt"], "nn", name="ssm_in_dt")
    xbc = _conv4_fwd(xp, p["conv_w"], p["conv_b"], name="ssm_conv_fwd")
    dtb, alog = _pad_heads(p["dt_bias"]), _pad_heads(p["a_log"])
    d_x = jnp.repeat(p["d"], HD).reshape(1, DI)
    dtx, acs_x, acs_t = _ssd_pre(dtraw, dtb, alog, name="ssd_pre")
    y, sin = _ssd_fwd(xbc, dtx, acs_x, acs_t, d_x, name="ssd_fwd")
    yn = _gated_norm_fwd(y, z, p["gate_norm"], name="ssm_gate_fwd")
    h2 = _mm(yn, p["wout"], "nn", res=h, name="ssm_out")
    return h2, (h, u, z, xp, dtraw, xbc, dtx, acs_x, acs_t, d_x, dtb, alog, y, sin, yn)


def _ssm_bwd(dh2, p, saved):
    h, u, z, xp, dtraw, xbc, dtx, acs_x, acs_t, d_x, dtb, alog, y, sin, yn = saved
    dyn = _mm(dh2, p["wout"], "nt", name="ssm_out_dx")
    g = {"w_out": _mm(yn, dh2, "tn", out_dtype=BF16, name="ssm_out_dw")}
    dy, dz, g["gate_norm"] = _gated_norm_bwd(y, z, p["gate_norm"], dyn, name="ssm_gate_bwd")
    dxs, dbm, dcm, dacs, dxdt, ddx = _ssd_bwd(xbc, dtx, acs_x, acs_t, d_x, sin, dy, name="ssd_bwd")
    ddtraw, ddtb, dav = _ssd_post(dacs, dxdt, dtraw, dtb, alog, name="ssd_post")
    dpre, g["conv_w"], g["conv_b"] = _conv4_bwd_act(xp, p["conv_w"], p["conv_b"], jnp.concatenate([dxs, dbm, dcm], axis=1),
                                                    name="ssm_conv_bwd_act")
    dxp = _conv_bwd_x(dpre, p["conv_w"], KC, name="ssm_conv_bwd_x", out_dtype=BF16)
    du = _mm(dz, p["wz"], "nt", name="ssm_in_dx_z")
    du = _mm(dxp, p["wxbc"], "nt", res=du, name="ssm_in_dx_xbc")
    du = _mm(ddtraw, p["wdt"], "nt", res=du, name="ssm_in_dx_dt")
    g["w_in"] = jnp.concatenate([_mm(u, dz, "tn", out_dtype=BF16, name="ssm_in_dw_z"),
                                 _mm(u, dxp, "tn", out_dtype=BF16, name="ssm_in_dw_xbc"),
                                 _mm(u, ddtraw, "tn", out_dtype=BF16, name="ssm_in_dw_dt")[:, :NH]], axis=1)
    dh, g["norm"] = _rms_bwd(h, p["norm"], du, dh2, name="ssm_rms_bwd")
    g["dt_bias"] = ddtb[0, :NH]
    g["a_log"] = dav[0, :NH] * (-jnp.exp(p["a_log"]))
    g["d"] = ddx.reshape(NH, HD).sum(-1)
    return dh, g


def _cv_fwd(h, p):
    u = _rms_fwd(h, p["norm"], name="cv_rms_fwd")
    pre = _mm(u, p["wpw1"], "nn", bias=p["b_pw1"].reshape(1, 2 * D), name="cv_pw1")
    gl = _glu_fwd(pre, name="cv_glu_fwd")
    c, s = _dwconv_ln_fwd(gl, p["dw_w"], p["dw_b"], p["ln_g"], p["ln_b"], name="cv_dwconv_ln_fwd")
    h2 = _mm(s, p["wpw2"], "nn", bias=p["b_pw2"].reshape(1, D), res=h, name="cv_pw2")
    return h2, (h, u, pre, gl, c, s)


def _cv_bwd(dh2, p, saved):
    h, u, pre, gl, c, s = saved
    ds = _mm(dh2, p["wpw2"], "nt", name="cv_pw2_dx")
    g = {"w_pw2": _mm(s, dh2, "tn", out_dtype=BF16, name="cv_pw2_dw"), "b_pw2": _colsum(dh2, name="cv_pw2_db")}
    dc, g["ln_g"], g["ln_b"] = _ln_silu_bwd(c, p["ln_g"], p["ln_b"], ds, name="cv_ln_bwd")
    dgl = _conv_bwd_x(dc, p["dw_w"], CK, name="cv_dwconv_bwd_x", out_dtype=F32)
    g["dw_w"], g["dw_b"] = _dwconv_bwd_w(gl, dc, name="cv_dwconv_bwd_w")
    dpre, g["b_pw1"] = _glu_bwd(pre, dgl, name="cv_glu_bwd")
    du = _mm(dpre, p["wpw1"], "nt", name="cv_pw1_dx")
    g["w_pw1"] = _mm(u, dpre, "tn", out_dtype=BF16, name="cv_pw1_dw")
    dh, g["norm"] = _rms_bwd(h, p["norm"], du, dh2, name="cv_rms_bwd")
    return dh, g


def _ffn_fwd(h, p):
    u = _rms_fwd(h, p["norm"], name="ffn_rms_fwd")
    gu = _mm(u, p["wgu_t"], "nt", name="ffn_gate_up")
    a = _swiglu_fwd(gu, name="ffn_act_fwd")
    h2 = _mm(a, p["wd"], "nn", res=h, name="ffn_down")
    return h2, (h, u, gu, a)


def _ffn_bwd(dh2, p, saved, after=None):
    h, u, gu, a = saved
    da = _mm(dh2, p["wd"], "nt", name="ffn_down_dx", after=after)
    g = {"w_down": _mm(a, dh2, "tn", out_dtype=BF16, name="ffn_down_dw")}
    dgu = _swiglu_bwd(gu, da, name="ffn_act_bwd")
    du = _mm(dgu, p["wgu_t"], "nn", name="ffn_gate_up_dx")
    dwgu_t = _mm(dgu, u, "tn", out_dtype=BF16, name="ffn_gate_up_dw")
    g["w_gate"], g["w_up"] = dwgu_t[:DFF], dwgu_t[DFF:]
    dh, g["norm"] = _rms_bwd(h, p["norm"], du, dh2, name="ffn_rms_bwd")
    return dh, g


def _fwd_bwd(x, target, depth, mixer_weights, ffn_weights, final_norm, layer_done=None):
    h, tape, weights = x, [], []
    for i in range(depth):
        mix_w = mixer_weights(i, h)
        h, s_mix = (_ssm_fwd if i % 2 == 0 else _cv_fwd)(h, mix_w)
        ffn_w = ffn_weights(i, h)
        h, s_ffn = _ffn_fwd(h, ffn_w)
        tape.append((s_mix, s_ffn))
        weights.append((mix_w, ffn_w))
    lsum, dh, g_final = _final_loss(h, final_norm, target, name="loss_head")
    g_mix, g_ffn, token = [None] * depth, [None] * depth, None
    for i in reversed(range(depth)):
        s_mix, s_ffn = tape[i]
        dh, g_ffn[i] = _ffn_bwd(dh, weights[i][1], s_ffn, after=token)
        dh, g_mix[i] = (_ssm_bwd if i % 2 == 0 else _cv_bwd)(dh, weights[i][0], s_mix)
        if layer_done is not None:
            token = layer_done(i, g_mix[i], g_ffn[i], g_final)
    return lsum, dh, g_mix, g_ffn, g_final


ANY = pl.BlockSpec(memory_space=pl.ANY)


def _place():
    x, y, c = lax.axis_index("x"), lax.axis_index("y"), lax.axis_index("c")
    return x, y, c, [(1 - x, y), (x, 1 - y), (1 - x, 1 - y)]


def _remote(src, dst, ssem, rsem, dev):
    return pltpu.make_async_remote_copy(src_ref=src, dst_ref=dst, send_sem=ssem, recv_sem=rsem, device_id=dev,
                                        device_id_type=MESH)


def _comm_call(body, name, ins, out_shape, n_sems):
    return pl.pallas_call(
        body, name=name, in_specs=[ANY] * len(ins), out_specs=[ANY] * len(out_shape), out_shape=out_shape,
        scratch_shapes=[pltpu.SemaphoreType.DMA((k,)) for k in n_sems])(*ins)


HBM = pl.BlockSpec(memory_space=pltpu.HBM)
SEM = pl.BlockSpec(memory_space=pltpu.SEMAPHORE)
EFFECT = pltpu.SideEffectType.DATAFLOW_SIDE_EFFECTING


def _gather_copies(src, land, ssem, rsem):
    x, y, c, chips = _place()
    me, pairs = 2 * x + y, []
    for a in range(len(src)):
        h = src[a].shape[0] // 2
        rows = pl.ds(c * h, h)
        for j, (px, py) in enumerate(chips):
            sems = (ssem.at[3 * a + j], rsem.at[3 * a + j], (px, py, c))
            pairs.append((_remote(src[a].at[rows], land[a].at[me, rows], *sems),
                          _remote(src[a].at[rows], land[a].at[2 * px + py, rows], *sems)))
    return pairs


def _chip_copies(src, land, ssem, rsem):
    x, y, c, chips = _place()
    me, pairs = 2 * x + y, []
    for a in range(len(src)):
        for j, (px, py) in enumerate(chips):
            sems = (ssem.at[3 * a + j], rsem.at[3 * a + j], (px, py, c))
            pairs.append((_remote(src[a].at[2 * px + py], land[a].at[me], *sems),
                          _remote(src[a].at[2 * px + py], land[a].at[2 * px + py], *sems)))
    return pairs


def _split_start(copies, srcs, land_shapes, dep, name):
    n = len(srcs)

    def body(*refs):
        for send, _ in copies(refs[:n], refs[n:2 * n], refs[2 * n + 1], refs[2 * n + 2]):
            send.start()
        refs[-1][...] = jnp.zeros_like(refs[-1])

    lands = [pltpu.with_memory_space_constraint(lax.empty(s, w.dtype), pltpu.HBM) for s, w in zip(land_shapes, srcs)]
    out = pl.pallas_call(
        body, name=name,
        out_shape=(pltpu.SemaphoreType.DMA((3 * n,)), pltpu.SemaphoreType.DMA((3 * n,)),
                   *[pltpu.HBM(w.shape, w.dtype) for w in srcs], *[pltpu.HBM(w.shape, w.dtype) for w in lands],
                   _sds((8, 128), F32)),
        in_specs=[HBM] * (2 * n) + [ANY], out_specs=(SEM, SEM, *[HBM] * (2 * n), pl.BlockSpec(memory_space=pltpu.VMEM)),
        input_output_aliases={a: 2 + a for a in range(2 * n)},
        compiler_params=pltpu.CompilerParams(has_side_effects=EFFECT))(
            *[pltpu.with_memory_space_constraint(w, pltpu.HBM) for w in srcs], *lands, dep)
    return out[0], out[1], list(out[2:2 + n]), list(out[2 + n:2 + 2 * n]), out[-1]


def _split_wait(copies, ssem, rsem, srcs, lands, after, name):
    n = len(srcs)

    def body(*refs):
        for send, recv in copies(refs[:n], refs[n:2 * n], refs[2 * n], refs[2 * n + 1]):
            send.wait_send()
            recv.wait_recv()

    out = pl.pallas_call(
        body, name=name,
        out_shape=(*[pltpu.HBM(w.shape, w.dtype) for w in srcs], *[pltpu.HBM(w.shape, w.dtype) for w in lands]),
        in_specs=[HBM] * (2 * n) + [SEM, SEM, ANY], out_specs=tuple([HBM] * (2 * n)),
        input_output_aliases={a: a for a in range(2 * n)},
        compiler_params=pltpu.CompilerParams(has_side_effects=EFFECT))(*srcs, *lands, ssem, rsem, after)
    return list(out[:n]), list(out[n:])


def _gather_start(ws, dep, tag):
    return _split_start(_gather_copies, ws, [(4,) + w.shape for w in ws], dep, "gather_start_" + tag)


def _gather_wait(ssem, rsem, srcs, lands, after, tag):
    return _split_wait(_gather_copies, ssem, rsem, srcs, lands, after, "gather_wait_" + tag)


def _gather_forward(lands, tag):
    n = len(lands)

    def body(*refs):
        land = refs[n:2 * n]
        ssem, rsem = refs[2 * n:]
        x, y, c, chips = _place()
        sib = (x, y, 1 - c)
        cps = []
        for a in range(n):
            h = lands[a].shape[1] // 2
            for j, (px, py) in enumerate(chips):
                blk = land[a].at[2 * px + py, pl.ds(c * h, h)]
                cps.append(_remote(blk, blk, ssem.at[3 * a + j], rsem.at[3 * a + j], sib))
        for cp in cps:
            cp.start()
        for a in range(n):
            h = lands[a].shape[1] // 2
            for j, (px, py) in enumerate(chips):
                blk = land[a].at[2 * px + py, pl.ds((1 - c) * h, h)]
                _remote(blk, blk, ssem.at[3 * a + j], rsem.at[3 * a + j], sib).wait_recv()
        for cp in cps:
            cp.wait_send()

    return pl.pallas_call(
        body, name="gather_forward_" + tag, in_specs=[ANY] * n, out_specs=[ANY] * n,
        out_shape=[_sds(w.shape, w.dtype) for w in lands], input_output_aliases={a: a for a in range(n)},
        scratch_shapes=[pltpu.SemaphoreType.DMA((3 * n,)), pltpu.SemaphoreType.DMA((3 * n,))])(*lands)


def _exchange_pairs(gs, tag, dep=None):
    n = len(gs)
    extra = [] if dep is None else [dep]

    def body(*refs):
        ins, got = refs[:n], refs[n + len(extra):2 * n + len(extra)]
        ssem, rsem = refs[2 * n + len(extra):]
        x, y, c, _ = _place()
        cps = []
        for a in range(n):
            lh = gs[a].shape[1] // 2
            cps.append(_remote(ins[a].at[:, pl.ds((1 - c) * lh, lh)], got[a], ssem.at[a], rsem.at[a], (x, y, 1 - c)))
        for cp in cps:
            cp.start()
        for cp in cps:
            cp.wait()

    halves = [_sds((4, g.shape[1] // 2) + g.shape[2:], g.dtype) for g in gs]
    return _comm_call(body, "reduce_pair_exchange_" + tag, list(gs) + extra, halves, [n, n])


def _exchange_final(fs, tag):
    n = len(fs)

    def body(*refs):
        ins, outs = refs[:n], refs[n:2 * n]
        ssem, rsem = refs[2 * n:]
        x, y, c, _ = _place()
        cps = [_remote(ins[a], outs[a], ssem.at[a], rsem.at[a], (x, y, 1 - c)) for a in range(n)]
        for cp in cps:
            cp.start()
        for cp in cps:
            cp.wait()

    return _comm_call(body, "reduce_final_exchange_" + tag, fs, [_sds(f.shape, f.dtype) for f in fs], [n, n])


def _rows_tile(r, c, n_arrays, itemsize=4):
    cap = max(16, (16 * 1024 * 1024) // (2 * n_arrays * c * itemsize))
    best = 16 if r % 16 == 0 else r
    for t in range(16, min(r, cap) + 1, 16):
        if r % t == 0:
            best = t
    return best


REDUCE_ROW_BLOCKS = 2


def _add_pairs(gs, gots, core, *, name):
    n = len(gs)

    def body(core_ref, *refs):
        for k in range(n):
            refs[2 * n + k][...] = (refs[k][...].astype(F32) + refs[n + k][...].astype(F32)).astype(BF16)

    halves, wholes, shapes = [], [], []
    for g in gs:
        _, r, c = g.shape
        tm = r // (2 * REDUCE_ROW_BLOCKS)
        wholes.append(pl.BlockSpec((None, None, tm, c), lambda s, i, core_ref: (s, core_ref[0], i, 0)))
        halves.append(pl.BlockSpec((None, tm, c), lambda s, i, core_ref: (s, i, 0)))
        shapes.append(_sds((4, r // 2, c), BF16))
    grid_spec = pltpu.PrefetchScalarGridSpec(num_scalar_prefetch=1, grid=(4, REDUCE_ROW_BLOCKS),
                                             in_specs=wholes + halves, out_specs=halves)
    return pl.pallas_call(body, name=name, grid_spec=grid_spec, out_shape=shapes,
                          compiler_params=pltpu.CompilerParams(dimension_semantics=("arbitrary", "arbitrary"),
                                                               vmem_limit_bytes=VMEM_LIMIT))(
                                                                   core, *[g.reshape(4, 2, g.shape[1] // 2, g.shape[2]) for g in gs], *gots)


def _sum_slots(rbs, *, name):
    n = len(rbs)

    def body(*refs):
        for k in range(n):
            acc = refs[k][0].astype(F32)
            for s in range(1, 4):
                acc = acc + refs[k][s].astype(F32)
            refs[n + k][...] = acc

    ins, outs, shapes = [], [], []
    for rb in rbs:
        _, rows, c = rb.shape
        tm = rows // REDUCE_ROW_BLOCKS
        ins.append(pl.BlockSpec((4, tm, c), lambda i: (0, i, 0)))
        outs.append(_row(tm, c))
        shapes.append(_sds((rows, c), F32))
    return _pc(body, name=name, grid=(REDUCE_ROW_BLOCKS,), in_specs=ins, out_specs=outs, out_shape=shapes)(*rbs)


def _reduce_start(gs, core, tag, dep=None):
    got = _exchange_pairs(gs, tag, dep)
    core1 = core.reshape(1).astype(jnp.int32)
    ps = _add_pairs(gs, got, core1, name="reduce_pair_add_" + tag)
    ssem, rsem, ps, rbs, token = _split_start(_chip_copies, ps, [p.shape for p in ps], got[0], "reduce_start_" + tag)
    return (ssem, rsem, ps, rbs), token


def _reduce_finish(state, after, chip, core, tag):
    ps, rbs = _split_wait(_chip_copies, *state, after, "reduce_wait_" + tag)
    rbs = [lax.dynamic_update_index_in_dim(b, lax.dynamic_index_in_dim(p, chip, 0, keepdims=False), chip, 0)
           for b, p in zip(rbs, ps)]
    fs = _sum_slots(rbs, name="reduce_chip_sum_" + tag)
    theirs = _exchange_final(fs, tag)
    return [jnp.where(core == 0, jnp.concatenate([f, t], axis=0), jnp.concatenate([t, f], axis=0))
            for f, t in zip(fs, theirs)]


def _gather8(v, reduce):
    m = v.shape[0]

    def body(x_ref, *rest):
        if reduce:
            sum_ref, out_ref, send_sems, recv_sems, local_sem = rest
        else:
            out_ref, send_sems, recv_sems, local_sem = rest
        x, y, c, chips = _place()
        me, sibling = (x, y, c), (x, y, 1 - c)

        def rows(px, py, pc):
            return out_ref.at[pl.ds((4 * px + 2 * py + pc) * m, m), :]

        def copy(k, block, to, src=None):
            return _remote(rows(*block) if src is None else src, rows(*block), send_sems.at[k], recv_sems.at[k], to)

        mine = pltpu.make_async_copy(x_ref, rows(*me), local_sem)
        mine.start()
        first = [copy(0, me, sibling, src=x_ref)]
        first += [copy(1 + j, me, (*chip, c), src=x_ref) for j, chip in enumerate(chips)]
        for cp in first:
            cp.start()
        passed = [copy(4 + j, (*chip, c), sibling) for j, chip in enumerate(chips)]
        for j, chip in enumerate(chips):
            copy(1 + j, (*chip, c), me).wait_recv()
            passed[j].start()
        copy(0, sibling, me).wait_recv()
        for j, chip in enumerate(chips):
            copy(4 + j, (*chip, 1 - c), me).wait_recv()
        for cp in first + passed:
            cp.wait_send()
        mine.wait()
        if reduce:
            acc = out_ref[pl.ds(0, m), :]
            for k in range(1, 8):
                acc = acc + out_ref[pl.ds(k * m, m), :]
            sum_ref[...] = acc

    vm = pl.BlockSpec(memory_space=pltpu.VMEM)
    out_shape = [_sds((8 * m, 128), F32)]
    if reduce:
        out_shape = [_sds((m, 128), F32)] + out_shape
    out = pl.pallas_call(
        body, name="allreduce_small" if reduce else "gather_small", in_specs=[vm], out_specs=[vm] * len(out_shape),
        out_shape=out_shape,
        scratch_shapes=[pltpu.SemaphoreType.DMA((7,)), pltpu.SemaphoreType.DMA((7,)), pltpu.SemaphoreType.DMA(())],
        compiler_params=pltpu.CompilerParams(vmem_limit_bytes=VMEM_LIMIT))(v)
    return out[0]


def _adamw(w, g, m, v, *, name):
    shape = w.shape
    c = shape[-1]
    r = w.size // c
    tm = _rows_tile(r, c, 7) if r % 16 == 0 else r

    def body(w_ref, g_ref, m_ref, v_ref, d_ref, mo_ref, vo_ref):
        gg = g_ref[...]
        mm = B1 * m_ref[...] + (1.0 - B1) * gg
        vv = B2 * v_ref[...] + (1.0 - B2) * (gg * gg)
        mo_ref[...] = mm
        vo_ref[...] = vv
        m_hat = mm / (1.0 - B1 ** STEP)
        v_hat = vv / (1.0 - B2 ** STEP)
        d_ref[...] = -LR * (m_hat / (jnp.sqrt(v_hat) + AEPS) + WD * w_ref[...])

    outs = _pc(body, name=name, grid=(r // tm,), in_specs=[_row(tm, c)] * 4, out_specs=[_row(tm, c)] * 3,
               out_shape=[_sds((r, c), F32)] * 3)(*[t.reshape(r, c) for t in (w, g, m, v)])
    return [o.reshape(shape) for o in outs]


WEIGHTS = ["ssm_norm", "ssm_w_in", "ssm_conv_w", "ssm_conv_b", "ssm_dt_bias", "ssm_a_log", "ssm_d", "ssm_gate_norm",
           "ssm_w_out", "cv_norm", "cv_w_pw1", "cv_b_pw1", "cv_dw_w", "cv_dw_b", "cv_ln_g", "cv_ln_b", "cv_w_pw2",
           "cv_b_pw2", "ffn_norm", "ffn_w_gate", "ffn_w_up", "ffn_w_down", "final_norm"]
TRANSPOSED = ("ffn_w_gate", "ffn_w_up")
BIG = {"ssm_w_in": "col", "ssm_w_out": "row", "cv_w_pw1": "col", "cv_w_pw2": "row",
       "ffn_w_gate": "row", "ffn_w_up": "row", "ffn_w_down": "row"}
SMALL_SHARDED = {"ssm_conv_w": 2, "cv_norm": 1, "cv_b_pw1": 1, "cv_dw_w": 2, "cv_dw_b": 1, "cv_ln_g": 1, "cv_ln_b": 1,
                 "cv_b_pw2": 1}
SMALL = [n for n in WEIGHTS if n not in BIG]
LANES = 128


def _pack(arrays):
    flat = jnp.concatenate([a.reshape(-1) for a in arrays])
    rows = -(-flat.size // (8 * LANES)) * 8
    return jnp.pad(flat, (0, rows * LANES - flat.size)).reshape(rows, LANES)


def _unpack(packed, shapes):
    flat, out, off = packed.reshape(-1), [], 0
    for s in shapes:
        n = 1
        for d_ in s:
            n *= d_
        out.append(flat[off:off + n].reshape(s))
        off += n
    return out


def _unshard(parts, axis):
    return jnp.concatenate(parts, axis=axis)


def kernel(x, ssm_norm, ssm_w_in, ssm_conv_w, ssm_conv_b, ssm_dt_bias, ssm_a_log, ssm_d, ssm_gate_norm, ssm_w_out, cv_norm, cv_w_pw1, cv_b_pw1, cv_dw_w, cv_dw_b, cv_ln_g, cv_ln_b, cv_w_pw2, cv_b_pw2, ffn_norm, ffn_w_gate, ffn_w_up, ffn_w_down, final_norm, loss_target, m_ssm_norm, m_ssm_w_in, m_ssm_conv_w, m_ssm_conv_b, m_ssm_dt_bias, m_ssm_a_log, m_ssm_d, m_ssm_gate_norm, m_ssm_w_out, m_cv_norm, m_cv_w_pw1, m_cv_b_pw1, m_cv_dw_w, m_cv_dw_b, m_cv_ln_g, m_cv_ln_b, m_cv_w_pw2, m_cv_b_pw2, m_ffn_norm, m_ffn_w_gate, m_ffn_w_up, m_ffn_w_down, m_final_norm, v_ssm_norm, v_ssm_w_in, v_ssm_conv_w, v_ssm_conv_b, v_ssm_dt_bias, v_ssm_a_log, v_ssm_d, v_ssm_gate_norm, v_ssm_w_out, v_cv_norm, v_cv_w_pw1, v_cv_b_pw1, v_cv_dw_w, v_cv_dw_b, v_cv_ln_g, v_cv_ln_b, v_cv_w_pw2, v_cv_b_pw2, v_ffn_norm, v_ffn_w_gate, v_ffn_w_up, v_ffn_w_down, v_final_norm):
    a = dict(locals())
    chip = 2 * lax.axis_index("x") + lax.axis_index("y")
    n_ssm, n_cv, depth = ssm_norm.shape[0], cv_norm.shape[0], ffn_norm.shape[0]

    def own(n, layer):
        w = a[n][layer].astype(BF16)
        return w.T if n in TRANSPOSED else w

    sh_names = list(SMALL_SHARDED)
    got = _gather8(_pack([a[n] for n in sh_names]), reduce=False)
    got = got.reshape(8, -1)[0::2]
    per_chip = [_unpack(got[s], [a[n].shape for n in sh_names]) for s in range(4)]
    full = {n: a[n] for n in SMALL}
    for k, n in enumerate(sh_names):
        full[n] = _unshard([per_chip[s][k] for s in range(4)], SMALL_SHARDED[n])

    def matrices_of(i, part="layer"):
        mixer = [(n, i // 2) for n in (("ssm_w_in", "ssm_w_out") if i % 2 == 0 else ("cv_w_pw1", "cv_w_pw2"))]
        ffn_part = [(n, i) for n in ("ffn_w_gate", "ffn_w_up", "ffn_w_down")]
        return {"mixer": mixer, "ffn": ffn_part, "layer": mixer + ffn_part}[part]

    stages = [matrices_of(0, "mixer"), matrices_of(0, "ffn")] + [matrices_of(i) for i in range(1, depth)]
    in_flight, landed = {}, {}

    def matrix(n, layer, h):
        if not in_flight and not landed:
            dep = h
            for k, stage in enumerate(stages):
                in_flight[k] = _gather_start([own(m, l) for m, l in stage], dep, str(k))
                dep = in_flight[k][-1]
            in_flight["token"] = dep[0, 0]
        if (n, layer) not in landed:
            k = next(k for k, st in enumerate(stages) if (n, layer) in st)
            ssem, rsem, srcs, lands, _ = in_flight.pop(k)
            srcs, lands = _gather_wait(ssem, rsem, srcs, lands, h, str(k))
            lands = _gather_forward(lands, str(k))
            for key, src, land in zip(stages[k], srcs, lands):
                landed[key] = (src, land, in_flight["token"])
        return landed[(n, layer)]

    def whole(h, *keys):
        got = [matrix(n, l, h) for n, l in keys]
        parts = [jnp.where(chip == s, src, land[s]) for src, land, _ in got for s in range(4)]
        return jnp.concatenate(parts, axis=1 if BIG[keys[0][0]] == "col" else 0), got[0][2]

    def mixer_weights(i, h):
        j = i // 2
        if i % 2 == 0:
            (w_in, token), (w_out, _) = whole(h, ("ssm_w_in", j)), whole(h, ("ssm_w_out", j))
            return dict(norm=full["ssm_norm"][j] + token, wz=w_in[:, :DI], wxbc=w_in[:, DI:DI + CD],
                        wdt=jnp.pad(w_in[:, DI + CD:], ((0, 0), (0, NHP - NH))),
                        conv_w=full["ssm_conv_w"][j], conv_b=full["ssm_conv_b"][j], dt_bias=full["ssm_dt_bias"][j],
                        a_log=full["ssm_a_log"][j], d=full["ssm_d"][j], gate_norm=full["ssm_gate_norm"][j], wout=w_out)
        (w_pw1, token), (w_pw2, _) = whole(h, ("cv_w_pw1", j)), whole(h, ("cv_w_pw2", j))
        return dict(norm=full["cv_norm"][j] + token, wpw1=w_pw1, b_pw1=full["cv_b_pw1"][j], dw_w=full["cv_dw_w"][j],
                    dw_b=full["cv_dw_b"][j], ln_g=full["cv_ln_g"][j], ln_b=full["cv_ln_b"][j], wpw2=w_pw2,
                    b_pw2=full["cv_b_pw2"][j])

    def ffn_weights(i, h):
        (wgu_t, token), (wd, _) = whole(h, ("ffn_w_gate", i), ("ffn_w_up", i)), whole(h, ("ffn_w_down", i))
        return dict(norm=full["ffn_norm"][i] + token, wgu_t=wgu_t, wd=wd)

    core = lax.axis_index("c")
    reducing, seen = {}, {}

    def reduce_small(g_final):
        g_mixers, g_ffns = [seen[i][0] for i in range(depth)], [seen[i][1] for i in range(depth)]
        local_small = {"final_norm": g_final, "ffn_norm": jnp.stack([g["norm"] for g in g_ffns])}
        for n in SMALL:
            if n.startswith("ssm_"):
                local_small[n] = jnp.stack([g[n[4:]] for g in g_mixers[0::2]])
            elif n.startswith("cv_"):
                local_small[n] = jnp.stack([g[n[3:]] for g in g_mixers[1::2]])
        return _gather8(_pack([local_small[n] for n in SMALL]), reduce=True)

    def layer_done(i, g_mixer, g_ffn_i, g_final):
        seen[i] = (g_mixer, g_ffn_i)
        slots = []
        for n, _ in matrices_of(i):
            g = (g_ffn_i if n.startswith("ffn_") else g_mixer)[n.split("_", 1)[1]]
            if BIG[n] == "col":
                slots.append(g.reshape(g.shape[0], 4, g.shape[1] // 4).transpose(1, 0, 2))
            else:
                slots.append(g.reshape(4, g.shape[0] // 4, g.shape[1]))
        dep = None
        if i == 0:
            dep = reducing["small"] = reduce_small(g_final)
        reducing[i], reducing["token"] = _reduce_start(slots, core, str(i), dep)
        return reducing["token"]

    lsum, grad_x, g_mix, g_ffn, g_final = _fwd_bwd(x[0], loss_target[0], depth, mixer_weights, ffn_weights,
                                                   full["final_norm"], layer_done)
    loss = (0.5 / D) * lax.psum(jnp.sum(lsum), ("x", "y", "c"))

    grads = {}
    for n, g in zip(SMALL, _unpack(reducing.pop("small"), [full[n].shape for n in SMALL])):
        if n in SMALL_SHARDED:
            ax = SMALL_SHARDED[n]
            g = lax.dynamic_slice_in_dim(g, chip * a[n].shape[ax], a[n].shape[ax], axis=ax)
        grads[n] = g
    last_start = reducing.pop("token")

    per_layer, after = {n: [None] * a[n].shape[0] for n in BIG}, last_start
    for i in reversed(range(depth)):
        done = _reduce_finish(reducing.pop(i), after, chip, core, str(i))
        for (n, l), g in zip(matrices_of(i), done):
            per_layer[n][l] = g
        after = done[0]
    for n in BIG:
        grads[n] = jnp.stack(per_layer[n])
        if n in TRANSPOSED:
            grads[n] = grads[n].transpose(0, 2, 1)

    delta, new_m, new_v = {}, {}, {}
    for n in BIG:
        delta[n], new_m[n], new_v[n] = _adamw(a[n], grads[n], a["m_" + n], a["v_" + n], name="adamw_" + n)
    shapes = [a[n].shape for n in SMALL]
    upd = _adamw(_pack([a[n] for n in SMALL]), _pack([grads[n] for n in SMALL]), _pack([a["m_" + n] for n in SMALL]),
                 _pack([a["v_" + n] for n in SMALL]), name="adamw_small")
    for dst, packed in zip((delta, new_m, new_v), upd):
        dst.update(zip(SMALL, _unpack(packed, shapes)))

    return (loss, grad_x[None], *[grads[n] for n in WEIGHTS], *[delta[n] for n in WEIGHTS],
            *[new_m[n] for n in WEIGHTS], *[new_v[n] for n in WEIGHTS])
```

```python
import functools

import jax
import jax.numpy as jnp
from jax import lax
from jax.experimental import pallas as pl
from jax.experimental.pallas import tpu as pltpu

F32 = jnp.float32
BF16 = jnp.bfloat16
HI = lax.Precision.HIGHEST
MESH = pl.DeviceIdType.MESH

D = 1024
DI = 2048
HD = 64
NH = 32
NG = 4
GW = DI // NG
NS = 128
KC = 4
CD = DI + 2 * NG * NS
Q = 128
DFF = 2816
CK = 31
EPS = 1e-5
NHP = 128

LR, B1, B2, AEPS, WD, STEP = 0.001, 0.9, 0.999, 1e-08, 0.01, 10

VMEM_LIMIT = 56 * 1024 * 1024
MM_VMEM_BUDGET = 36 * 1024 * 1024


def _pc(body, *, name, grid, in_specs, out_specs, out_shape, scratch=()):
    return pl.pallas_call(
        body, name=name, grid=grid, in_specs=in_specs, out_specs=out_specs, out_shape=out_shape,
        scratch_shapes=list(scratch),
        compiler_params=pltpu.CompilerParams(dimension_semantics=("arbitrary",) * len(grid),
                                             vmem_limit_bytes=VMEM_LIMIT))


def _sds(shape, dtype):
    return jax.ShapeDtypeStruct(tuple(shape), dtype)


def _row(tm, c):
    return pl.BlockSpec((tm, c), lambda i: (i, 0))


def _const(shape):
    return pl.BlockSpec(tuple(shape), lambda i: (0,) * len(shape))


def _tile(n, cap):
    if n <= cap:
        return n
    best = 128
    for t in range(128, cap + 1, 128):
        if n % t == 0:
            best = t
    return best


def _sigmoid(x):
    return 1.0 / (1.0 + jnp.exp(-x))


def _dsilu(x):
    s = _sigmoid(x)
    return s * (1.0 + x * (1.0 - s))


def _mm(a, b, mode, *, name, out_dtype=F32, bias=None, res=None, tm=1024, after=None):
    if mode == "nn":
        (m, k), (k2, n) = a.shape, b.shape
    elif mode == "nt":
        (m, k), (n, k2) = a.shape, b.shape
    else:
        (k, m), (k2, n) = a.shape, b.shape
    assert k == k2, (a.shape, b.shape, mode)
    tm = _tile(m, tm if k <= 2048 else 512)
    so = jnp.dtype(out_dtype).itemsize + (4 if res is not None else 0)
    fits = [c for c in range(128, min(n, 2048) + 1, 128) if n % c == 0 and
            2 * (tm * k * a.dtype.itemsize + c * k * b.dtype.itemsize + tm * c * so) <= MM_VMEM_BUDGET]
    tn = n if n <= 128 else max(fits)
    dn = {"nn": (((1,), (0,)), ((), ())), "nt": (((1,), (1,)), ((), ())), "tn": (((0,), (0,)), ((), ()))}[mode]
    nb, nr = bias is not None, res is not None

    def body(*refs):
        acc = lax.dot_general(refs[0][...].astype(BF16), refs[1][...].astype(BF16), dn, preferred_element_type=F32)
        if nb:
            acc = acc + refs[2][...]
        if nr:
            acc = acc + refs[2 + nb][...]
        refs[-1][...] = acc.astype(out_dtype)

    a_spec = pl.BlockSpec((k, tm), lambda i, j: (0, i)) if mode == "tn" else pl.BlockSpec((tm, k), lambda i, j: (i, 0))
    b_spec = pl.BlockSpec((tn, k), lambda i, j: (j, 0)) if mode == "nt" else pl.BlockSpec((k, tn), lambda i, j: (0, j))
    ins, specs = [a, b], [a_spec, b_spec]
    if nb:
        ins.append(bias)
        specs.append(pl.BlockSpec((1, tn), lambda i, j: (0, j)))
    if nr:
        ins.append(res)
        specs.append(pl.BlockSpec((tm, tn), lambda i, j: (i, j)))
    if after is not None:
        ins.append(after)
        specs.append(pl.BlockSpec((8, 128), lambda i, j: (0, 0)))
    return _pc(body, name=name, grid=(m // tm, n // tn), in_specs=specs,
               out_specs=pl.BlockSpec((tm, tn), lambda i, j: (i, j)), out_shape=_sds((m, n), out_dtype))(*ins)


def _rms_fwd(h, g, *, name):
    t = h.shape[0]
    tm = min(t, 512)

    def body(h_ref, g_ref, u_ref):
        x = h_ref[...]
        r = lax.rsqrt(jnp.mean(x * x, axis=-1, keepdims=True) + EPS)
        u_ref[...] = (x * r * g_ref[...]).astype(BF16)

    return _pc(body, name=name, grid=(t // tm,), in_specs=[_row(tm, D), _const((1, D))], out_specs=_row(tm, D),
               out_shape=_sds((t, D), BF16))(h, g.reshape(1, D))


def _rms_bwd(h, g, du, dres, *, name):
    t = h.shape[0]
    tm = min(t, 512)

    def body(h_ref, g_ref, du_ref, dres_ref, dh_ref, dg_ref):
        x = h_ref[...]
        r = lax.rsqrt(jnp.mean(x * x, axis=-1, keepdims=True) + EPS)
        dy = du_ref[...]
        dyg = dy * g_ref[...]
        dot = jnp.mean(dyg * x, axis=-1, keepdims=True)
        dh_ref[...] = dres_ref[...] + r * dyg - x * (r * r * r * dot)

        @pl.when(pl.program_id(0) == 0)
        def _():
            dg_ref[...] = jnp.zeros_like(dg_ref)

        dg_ref[...] += jnp.sum(dy * x * r, axis=0, keepdims=True)

    dh, dg = _pc(body, name=name, grid=(t // tm,),
                 in_specs=[_row(tm, D), _const((1, D)), _row(tm, D), _row(tm, D)],
                 out_specs=[_row(tm, D), _const((1, D))],
                 out_shape=[_sds((t, D), F32), _sds((1, D), F32)])(h, g.reshape(1, D), du, dres)
    return dh, dg.reshape(D)


def _final_loss(h, g, target, *, name):
    t = h.shape[0]
    tm = min(t, 512)

    def body(h_ref, g_ref, t_ref, dh_ref, dg_ref, l_ref):
        x = h_ref[...]
        gg = g_ref[...]
        r = lax.rsqrt(jnp.mean(x * x, axis=-1, keepdims=True) + EPS)
        err = x * r * gg - t_ref[...]
        dy = err * (1.0 / D)
        dyg = dy * gg
        dot = jnp.mean(dyg * x, axis=-1, keepdims=True)
        dh_ref[...] = r * dyg - x * (r * r * r * dot)

        @pl.when(pl.program_id(0) == 0)
        def _():
            dg_ref[...] = jnp.zeros_like(dg_ref)
            l_ref[...] = jnp.zeros_like(l_ref)

        dg_ref[...] += jnp.sum(dy * x * r, axis=0, keepdims=True)
        l_ref[...] += jnp.sum(err * err, axis=0, keepdims=True)

    dh, dg, l = _pc(body, name=name, grid=(t // tm,),
                    in_specs=[_row(tm, D), _const((1, D)), _row(tm, D)],
                    out_specs=[_row(tm, D), _const((1, D)), _const((1, D))],
                    out_shape=[_sds((t, D), F32), _sds((1, D), F32), _sds((1, D), F32)])(h, g.reshape(1, D), target)
    return l, dh, dg.reshape(D)


NT = (((1,), (1,)), ((), ()))
FFN_COLS = 1408


def _ffn_up_act(u, wg_t, wu_t, *, name):
    t = u.shape[0]
    tm, tn = min(t, 512), FFN_COLS

    def body(u_ref, wg_ref, wu_ref, g_ref, up_ref, a_ref):
        uu = u_ref[...]
        gate = lax.dot_general(uu, wg_ref[...], NT, preferred_element_type=F32)
        up = lax.dot_general(uu, wu_ref[...], NT, preferred_element_type=F32)
        g_ref[...] = gate
        up_ref[...] = up
        a_ref[...] = (gate * _sigmoid(gate) * up).astype(BF16)

    wspec = pl.BlockSpec((tn, D), lambda i, j: (j, 0))
    tile = pl.BlockSpec((tm, tn), lambda i, j: (i, j))
    return _pc(body, name=name, grid=(t // tm, DFF // tn),
               in_specs=[pl.BlockSpec((tm, D), lambda i, j: (i, 0)), wspec, wspec], out_specs=[tile, tile, tile],
               out_shape=[_sds((t, DFF), F32), _sds((t, DFF), F32), _sds((t, DFF), BF16)])(u, wg_t, wu_t)


def _ffn_down_dx_act(dh2, wd, gate, up, *, name, after=None):
    t = dh2.shape[0]
    tm, tn = min(t, 512), FFN_COLS

    def body(d_ref, w_ref, g_ref, up_ref, *rest):
        dg_ref, du_ref = rest[-2:]
        da = lax.dot_general(d_ref[...].astype(BF16), w_ref[...], NT, preferred_element_type=F32)
        gg = g_ref[...]
        dg_ref[...] = (da * up_ref[...] * _dsilu(gg)).astype(BF16)
        du_ref[...] = (da * gg * _sigmoid(gg)).astype(BF16)

    tile = pl.BlockSpec((tm, tn), lambda i, j: (i, j))
    ins = [dh2, wd, gate, up]
    specs = [pl.BlockSpec((tm, D), lambda i, j: (i, 0)), pl.BlockSpec((tn, D), lambda i, j: (j, 0)), tile, tile]
    if after is not None:
        ins.append(after)
        specs.append(pl.BlockSpec((8, 128), lambda i, j: (0, 0)))
    return _pc(body, name=name, grid=(t // tm, DFF // tn), in_specs=specs, out_specs=[tile, tile],
               out_shape=[_sds((t, DFF), BF16), _sds((t, DFF), BF16)])(*ins)


def _dx_rms_bwd(pairs, h, g, dres, *, name):
    t = h.shape[0]
    tm = min(t, 256)
    n = len(pairs)
    dns = [NT if mode == "nt" else (((1,), (0,)), ((), ())) for _, _, mode in pairs]

    def body(*refs):
        h_ref, g_ref, dres_ref, dh_ref, dg_ref = refs[2 * n:]
        dy = None
        for k in range(n):
            part = lax.dot_general(refs[k][...].astype(BF16), refs[n + k][...], dns[k], preferred_element_type=F32)
            dy = part if dy is None else dy + part
        x = h_ref[...]
        r = lax.rsqrt(jnp.mean(x * x, axis=-1, keepdims=True) + EPS)
        dyg = dy * g_ref[...]
        dot = jnp.mean(dyg * x, axis=-1, keepdims=True)
        dh_ref[...] = dres_ref[...] + r * dyg - x * (r * r * r * dot)

        @pl.when(pl.program_id(0) == 0)
        def _():
            dg_ref[...] = jnp.zeros_like(dg_ref)

        dg_ref[...] += jnp.sum(dy * x * r, axis=0, keepdims=True)

    a_specs = [_row(tm, a.shape[1]) for a, _, _ in pairs]
    b_specs = [_const(b.shape) for _, b, _ in pairs]
    dh, dg = _pc(body, name=name, grid=(t // tm,),
                 in_specs=a_specs + b_specs + [_row(tm, D), _const((1, D)), _row(tm, D)],
                 out_specs=[_row(tm, D), _const((1, D))], out_shape=[_sds((t, D), F32), _sds((1, D), F32)])(
                     *[a for a, _, _ in pairs], *[b for _, b, _ in pairs], h, g.reshape(1, D), dres)
    return dh, dg.reshape(D)


def _halo_before(tm, halo, tc):
    return pl.BlockSpec((halo, tc), lambda j, i: (jnp.maximum(i * (tm // halo) - 1, 0), j))


def _halo_after(tm, halo, tc, t):
    return pl.BlockSpec((halo, tc), lambda j, i: (jnp.minimum((i + 1) * (tm // halo), t // halo - 1), j))


def _conv4_fwd(xp, w, b, *, name):
    t = xp.shape[0]
    tm, tc, halo = min(t, 512), 1024, 8

    def body(x_ref, h_ref, w_ref, b_ref, o_ref, pad):
        i = pl.program_id(1)
        pad[pl.ds(0, halo), :] = jnp.where(i == 0, 0.0, h_ref[...])
        pad[pl.ds(halo, tm), :] = x_ref[...]
        acc = jnp.zeros((tm, tc), F32) + b_ref[...]
        for k in range(KC):
            acc = acc + pad[pl.ds(halo - (KC - 1) + k, tm), :] * w_ref[pl.ds(k, 1), :]
        o_ref[...] = acc * _sigmoid(acc)

    tile = pl.BlockSpec((tm, tc), lambda j, i: (i, j))
    return _pc(body, name=name, grid=(CD // tc, t // tm),
               in_specs=[tile, _halo_before(tm, halo, tc), pl.BlockSpec((KC, tc), lambda j, i: (0, j)),
                         pl.BlockSpec((1, tc), lambda j, i: (0, j))],
               out_specs=tile, out_shape=_sds((t, CD), F32), scratch=[pltpu.VMEM((halo + tm, tc), F32)])(xp, xp, w, b.reshape(1, CD))


def _conv4_bwd_act(xp, w, b, dout, *, name):
    t = xp.shape[0]
    tm, tc, halo = min(t, 512), 1024, 8

    def body(x_ref, h_ref, w_ref, b_ref, do_ref, dp_ref, dw_ref, db_ref, pad):
        i = pl.program_id(1)
        pad[pl.ds(0, halo), :] = jnp.where(i == 0, 0.0, h_ref[...])
        pad[pl.ds(halo, tm), :] = x_ref[...]
        acc = jnp.zeros((tm, tc), F32) + b_ref[...]
        for k in range(KC):
            acc = acc + pad[pl.ds(halo - (KC - 1) + k, tm), :] * w_ref[pl.ds(k, 1), :]
        dpre = do_ref[...] * _dsilu(acc)
        dp_ref[...] = dpre

        @pl.when(i == 0)
        def _():
            dw_ref[...] = jnp.zeros_like(dw_ref)
            db_ref[...] = jnp.zeros_like(db_ref)

        db_ref[...] += jnp.sum(dpre, axis=0, keepdims=True)
        for k in range(KC):
            dw_ref[pl.ds(k, 1), :] += jnp.sum(dpre * pad[pl.ds(halo - (KC - 1) + k, tm), :], axis=0, keepdims=True)

    tile = pl.BlockSpec((tm, tc), lambda j, i: (i, j))
    wspec = pl.BlockSpec((KC, tc), lambda j, i: (0, j))
    bspec = pl.BlockSpec((1, tc), lambda j, i: (0, j))
    dp, dw, db = _pc(body, name=name, grid=(CD // tc, t // tm),
                     in_specs=[tile, _halo_before(tm, halo, tc), wspec, bspec, tile],
                     out_specs=[tile, wspec, bspec],
                     out_shape=[_sds((t, CD), F32), _sds((KC, CD), F32), _sds((1, CD), F32)],
                     scratch=[pltpu.VMEM((halo + tm, tc), F32)])(xp, xp, w, b.reshape(1, CD), dout)
    return dp, dw, db.reshape(CD)


def _conv_bwd_x(dy, w, kk, *, name, out_dtype):
    t, c = dy.shape
    halo = 8 if kk <= 8 else 32
    tm, tc = (min(t, 512), min(c, 1024)) if kk <= 8 else (min(t, 256), min(c, 512))

    def body(y_ref, h_ref, w_ref, o_ref, pad):
        i = pl.program_id(1)
        pad[pl.ds(0, tm), :] = y_ref[...]
        pad[pl.ds(tm, halo), :] = jnp.where(i == t // tm - 1, 0.0, h_ref[...])
        acc = jnp.zeros((tm, tc), F32)
        for k in range(kk):
            acc = acc + pad[pl.ds(kk - 1 - k, tm), :] * w_ref[pl.ds(k, 1), :]
        o_ref[...] = acc.astype(out_dtype)

    tile = pl.BlockSpec((tm, tc), lambda j, i: (i, j))
    return _pc(body, name=name, grid=(c // tc, t // tm),
               in_specs=[tile, _halo_after(tm, halo, tc, t), pl.BlockSpec((kk, tc), lambda j, i: (0, j))],
               out_specs=tile, out_shape=_sds((t, c), out_dtype), scratch=[pltpu.VMEM((tm + halo, tc), F32)])(dy, dy, w)


def _glu_fwd(p, *, name):
    t = p.shape[0]
    tm = min(t, 512)

    def body(p_ref, o_ref):
        o_ref[...] = p_ref[:, :D] * _sigmoid(p_ref[:, D:])

    return _pc(body, name=name, grid=(t // tm,), in_specs=[_row(tm, 2 * D)], out_specs=_row(tm, D),
               out_shape=_sds((t, D), F32))(p)


def _glu_bwd(p, dg, *, name):
    t = p.shape[0]
    tm = min(t, 512)

    def body(p_ref, dg_ref, dp_ref, db_ref):
        a = p_ref[:, :D]
        s = _sigmoid(p_ref[:, D:])
        d = dg_ref[...]
        da = d * s
        dgate = d * a * s * (1.0 - s)
        dp_ref[:, :D] = da.astype(BF16)
        dp_ref[:, D:] = dgate.astype(BF16)

        @pl.when(pl.program_id(0) == 0)
        def _():
            db_ref[...] = jnp.zeros_like(db_ref)

        db_ref[:, :D] += jnp.sum(da, axis=0, keepdims=True)
        db_ref[:, D:] += jnp.sum(dgate, axis=0, keepdims=True)

    dp, db = _pc(body, name=name, grid=(t // tm,), in_specs=[_row(tm, 2 * D), _row(tm, D)],
                 out_specs=[_row(tm, 2 * D), _const((1, 2 * D))],
                 out_shape=[_sds((t, 2 * D), BF16), _sds((1, 2 * D), F32)])(p, dg)
    return dp, db.reshape(2 * D)


def _dwconv_ln_fwd(g, w, b, lg, lb, *, name):
    t = g.shape[0]
    tm, halo = min(t, 256), 32

    def body(x_ref, h_ref, w_ref, b_ref, lg_ref, lb_ref, c_ref, s_ref, pad):
        i = pl.program_id(0)
        pad[pl.ds(0, halo), :] = jnp.where(i == 0, 0.0, h_ref[...])
        pad[pl.ds(halo, tm), :] = x_ref[...]
        acc = jnp.zeros((tm, D), F32) + b_ref[...]
        for k in range(CK):
            acc = acc + pad[pl.ds(halo - (CK - 1) + k, tm), :] * w_ref[pl.ds(k, 1), :]
        c_ref[...] = acc
        mu = jnp.mean(acc, axis=-1, keepdims=True)
        xc = acc - mu
        r = lax.rsqrt(jnp.mean(xc * xc, axis=-1, keepdims=True) + EPS)
        n = xc * r * lg_ref[...] + lb_ref[...]
        s_ref[...] = (n * _sigmoid(n)).astype(BF16)

    c, s = _pc(body, name=name, grid=(t // tm,),
               in_specs=[_row(tm, D), pl.BlockSpec((halo, D), lambda i: (jnp.maximum(i * (tm // halo) - 1, 0), 0)),
                         _const((CK, D)), _const((1, D)), _const((1, D)), _const((1, D))],
               out_specs=[_row(tm, D), _row(tm, D)], out_shape=[_sds((t, D), F32), _sds((t, D), BF16)],
               scratch=[pltpu.VMEM((halo + tm, D), F32)])(g, g, w, b.reshape(1, D), lg.reshape(1, D), lb.reshape(1, D))
    return c, s


def _ln_silu_bwd(c, lg, lb, ds, *, name):
    t = c.shape[0]
    tm = min(t, 512)

    def body(c_ref, lg_ref, lb_ref, ds_ref, dc_ref, dlg_ref, dlb_ref):
        x = c_ref[...]
        mu = jnp.mean(x, axis=-1, keepdims=True)
        xc = x - mu
        r = lax.rsqrt(jnp.mean(xc * xc, axis=-1, keepdims=True) + EPS)
        xh = xc * r
        n = xh * lg_ref[...] + lb_ref[...]
        dn = ds_ref[...] * _dsilu(n)
        dxh = dn * lg_ref[...]
        m1 = jnp.mean(dxh, axis=-1, keepdims=True)
        m2 = jnp.mean(dxh * xh, axis=-1, keepdims=True)
        dc_ref[...] = r * (dxh - m1 - xh * m2)

        @pl.when(pl.program_id(0) == 0)
        def _():
            dlg_ref[...] = jnp.zeros_like(dlg_ref)
            dlb_ref[...] = jnp.zeros_like(dlb_ref)

        dlg_ref[...] += jnp.sum(dn * xh, axis=0, keepdims=True)
        dlb_ref[...] += jnp.sum(dn, axis=0, keepdims=True)

    dc, dlg, dlb = _pc(body, name=name, grid=(t // tm,),
                       in_specs=[_row(tm, D), _const((1, D)), _const((1, D)), _row(tm, D)],
                       out_specs=[_row(tm, D), _const((1, D)), _const((1, D))],
                       out_shape=[_sds((t, D), F32), _sds((1, D), F32), _sds((1, D), F32)])(
                           c, lg.reshape(1, D), lb.reshape(1, D), ds)
    return dc, dlg.reshape(D), dlb.reshape(D)


def _dwconv_bwd_w(g, dc, *, name):
    t = g.shape[0]
    tm, tc, halo = min(t, 256), 512, 32

    def body(x_ref, h_ref, dc_ref, dw_ref, db_ref, pad):
        i = pl.program_id(1)
        pad[pl.ds(0, halo), :] = jnp.where(i == 0, 0.0, h_ref[...])
        pad[pl.ds(halo, tm), :] = x_ref[...]
        d = dc_ref[...]

        @pl.when(i == 0)
        def _():
            dw_ref[...] = jnp.zeros_like(dw_ref)
            db_ref[...] = jnp.zeros_like(db_ref)

        db_ref[...] += jnp.sum(d, axis=0, keepdims=True)
        for k in range(CK):
            dw_ref[pl.ds(k, 1), :] += jnp.sum(d * pad[pl.ds(halo - (CK - 1) + k, tm), :], axis=0, keepdims=True)

    tile = pl.BlockSpec((tm, tc), lambda j, i: (i, j))
    dw, db = _pc(body, name=name, grid=(D // tc, t // tm),
                 in_specs=[tile, _halo_before(tm, halo, tc), tile],
                 out_specs=[pl.BlockSpec((CK, tc), lambda j, i: (0, j)), pl.BlockSpec((1, tc), lambda j, i: (0, j))],
                 out_shape=[_sds((CK, D), F32), _sds((1, D), F32)],
                 scratch=[pltpu.VMEM((halo + tm, tc), F32)])(g, g, dc)
    return dw, db.reshape(D)


def _colsum(x, *, name):
    t, c = x.shape
    tm = min(t, 512)

    def body(x_ref, o_ref):
        @pl.when(pl.program_id(0) == 0)
        def _():
            o_ref[...] = jnp.zeros_like(o_ref)

        o_ref[...] += jnp.sum(x_ref[...], axis=0, keepdims=True)

    return _pc(body, name=name, grid=(t // tm,), in_specs=[_row(tm, c)], out_specs=_const((1, c)),
               out_shape=_sds((1, c), F32))(x).reshape(c)


def _head_expand(n_lanes, first_head=0):
    h = lax.broadcasted_iota(jnp.int32, (NHP, n_lanes), 0)
    j = lax.broadcasted_iota(jnp.int32, (NHP, n_lanes), 1)
    return (h == first_head + j // HD).astype(F32)


def _softplus(v):
    return jnp.maximum(v, 0.0) + jnp.log(1.0 + jnp.exp(-jnp.abs(v)))


def _chunk_tri(n, upper):
    ri = lax.broadcasted_iota(jnp.int32, (n, n), 0)
    ci = lax.broadcasted_iota(jnp.int32, (n, n), 1)
    keep = (ri <= ci) if upper else (ri >= ci)
    return (keep & (ri // Q == ci // Q)).astype(F32)


def _split3(x):
    hi = x.astype(BF16)
    r1 = x - hi.astype(F32)
    mid = r1.astype(BF16)
    return hi, mid, (r1 - mid.astype(F32)).astype(BF16)


def _dot_sel(x, sel, sel_left=False):
    s = sel.astype(BF16)
    parts = [jnp.dot(s, t, preferred_element_type=F32) if sel_left else jnp.dot(t, s, preferred_element_type=F32)
             for t in _split3(x)]
    return (parts[2] + parts[1]) + parts[0]


def _ssd_pre(dtraw, dtb, alog, *, name):
    t = dtraw.shape[0]
    tm = min(t, 256)

    def body(r_ref, b_ref, al_ref, dtx_ref, acsx_ref, acst_ref):
        dt = _softplus(r_ref[...] + b_ref[...])
        da = dt * (-jnp.exp(al_ref[...]))
        acs = _dot_sel(da, _chunk_tri(tm, False), sel_left=True)
        expand = _head_expand(DI)
        dtx_ref[...] = _dot_sel(dt, expand)
        acsx_ref[...] = _dot_sel(acs, expand)
        acst_ref[...] = acs.T

    return _pc(body, name=name, grid=(t // tm,), in_specs=[_row(tm, NHP), _const((1, NHP)), _const((1, NHP))],
               out_specs=[_row(tm, DI), _row(tm, DI), pl.BlockSpec((NHP, tm), lambda i: (0, i))],
               out_shape=[_sds((t, DI), F32), _sds((t, DI), F32), _sds((NHP, t), F32)])(dtraw, dtb, alog)


def _ssd_post(dacs, dxdt, dtraw, dtb, alog, *, name):
    t = dtraw.shape[0]
    tm = min(t, 256)

    def body(g1_ref, g2_ref, r_ref, b_ref, al_ref, o_ref, db_ref, da_ref):
        h = lax.broadcasted_iota(jnp.int32, (DI, NHP), 1)
        j = lax.broadcasted_iota(jnp.int32, (DI, NHP), 0)
        red = (h == j // HD).astype(F32)
        dda = _dot_sel(_dot_sel(g1_ref[...], red), _chunk_tri(tm, True), sel_left=True)
        v = r_ref[...] + b_ref[...]
        ddt = dda * (-jnp.exp(al_ref[...])) + _dot_sel(g2_ref[...], red)
        draw = ddt * _sigmoid(v)
        o_ref[...] = draw.astype(BF16)

        @pl.when(pl.program_id(0) == 0)
        def _():
            db_ref[...] = jnp.zeros_like(db_ref)
            da_ref[...] = jnp.zeros_like(da_ref)

        db_ref[...] += jnp.sum(draw, axis=0, keepdims=True)
        da_ref[...] += jnp.sum(dda * _softplus(v), axis=0, keepdims=True)

    return _pc(body, name=name, grid=(t // tm,),
               in_specs=[_row(tm, DI), _row(tm, DI), _row(tm, NHP), _const((1, NHP)), _const((1, NHP))],
               out_specs=[_row(tm, NHP), _const((1, NHP)), _const((1, NHP))],
               out_shape=[_sds((t, NHP), BF16), _sds((1, NHP), F32), _sds((1, NHP), F32)])(dacs, dxdt, dtraw, dtb, alog)


def _decay_pair(acs_x, acst_ref, p, tri):
    lane = lax.broadcasted_iota(jnp.int32, (Q, 2 * HD), 1)
    v = acs_x[:, 2 * HD * p:2 * HD * (p + 1)]
    swapped = pltpu.roll(v, HD, axis=1)
    out = []
    for hh in range(2):
        colb = jnp.where((lane < HD) if hh == 0 else (lane >= HD), v, swapped)
        rowb = acst_ref[pl.ds(2 * p + hh, 1), :]
        out.append(jnp.exp(jnp.where(tri, colb - rowb, -1e30)))
    return out


def _ssd_fwd(xbc, dtx, acs_x, acs_t, d_x, *, name):
    t = xbc.shape[0]
    nc = t // Q
    hpg = NH // NG

    def body(x_ref, b_ref, c_ref, dtx_ref, acsx_ref, acst_ref, d_ref, y_ref, sin_ref, state):
        @pl.when(pl.program_id(1) == 0)
        def _():
            state[...] = jnp.zeros_like(state)

        xs = x_ref[...]
        tri = lax.broadcasted_iota(jnp.int32, (Q, Q), 0) >= lax.broadcasted_iota(jnp.int32, (Q, Q), 1)
        acs_x = acsx_ref[...]
        atot_x = acsx_ref[pl.ds(Q - 1, 1), :]
        xd = xs * dtx_ref[...]
        xb = xd.astype(BF16)
        bb = b_ref[...]
        cb16 = c_ref[...].astype(BF16)
        cbm = lax.dot_general(cb16, bb.astype(BF16), (((1,), (1,)), ((), ())), preferred_element_type=F32)
        s_in = state[...]
        sin_ref[0] = s_in
        y = jnp.dot(cb16, s_in.astype(BF16), preferred_element_type=F32) * jnp.exp(acs_x) + xs * d_ref[...]
        lane = lax.broadcasted_iota(jnp.int32, (Q, 2 * HD), 1)
        for p in range(hpg // 2):
            xp = xb[:, 2 * HD * p:2 * HD * (p + 1)]
            yp = jnp.zeros((Q, 2 * HD), F32)
            for hh, lm in enumerate(_decay_pair(acs_x, acst_ref, p, tri)):
                m = (cbm * lm).astype(BF16)
                xm = jnp.where((lane >= HD) if hh == 1 else (lane < HD), xp, jnp.zeros_like(xp))
                yp = yp + jnp.dot(m, xm, preferred_element_type=F32)
            y_ref[:, 2 * HD * p:2 * HD * (p + 1)] = y[:, 2 * HD * p:2 * HD * (p + 1)] + yp
        dec = jnp.exp(atot_x - acs_x)
        s_new = jnp.dot(bb.T.astype(BF16), (xd * dec).astype(BF16), preferred_element_type=F32)
        state[...] = jnp.exp(atot_x) * s_in + s_new

    grp = lambda g, c: (c, g)
    return _pc(body, name=name, grid=(NG, nc),
               in_specs=[pl.BlockSpec((Q, GW), grp),
                         pl.BlockSpec((Q, NS), lambda g, c: (c, DI // NS + g)),
                         pl.BlockSpec((Q, NS), lambda g, c: (c, DI // NS + NG + g)),
                         pl.BlockSpec((Q, GW), grp),
                         pl.BlockSpec((Q, GW), grp),
                         pl.BlockSpec((hpg, Q), lambda g, c: (g, c)),
                         pl.BlockSpec((1, GW), lambda g, c: (0, g))],
               out_specs=[pl.BlockSpec((Q, GW), grp), pl.BlockSpec((1, NS, GW), lambda g, c: (c, 0, g))],
               out_shape=[_sds((t, DI), F32), _sds((nc, NS, DI), F32)],
               scratch=[pltpu.VMEM((NS, GW), F32)])(xbc, xbc, xbc, dtx, acs_x, acs_t, d_x)


def _ssd_bwd(xbc, dtx, acs_x, acs_t, d_x, sin, dy, *, name):
    t = xbc.shape[0]
    nc = t // Q
    hpg = NH // NG

    def body(x_ref, b_ref, c_ref, dtx_ref, acsx_ref, acst_ref, d_ref, sin_ref, dy_ref,
             dx_ref, db_ref, dc_ref, dacs_ref, dxdt_ref, ddx_ref, dstate):
        @pl.when(pl.program_id(1) == 0)
        def _():
            dstate[...] = jnp.zeros_like(dstate)
            ddx_ref[...] = jnp.zeros_like(ddx_ref)

        xs = x_ref[...]
        dtxv = dtx_ref[...]
        gy = dy_ref[...]
        tri = lax.broadcasted_iota(jnp.int32, (Q, Q), 0) >= lax.broadcasted_iota(jnp.int32, (Q, Q), 1)
        acs_x = acsx_ref[...]
        atot_x = acsx_ref[pl.ds(Q - 1, 1), :]
        xd = xs * dtxv
        xb = xd.astype(BF16)
        gb = gy.astype(BF16)
        bb = b_ref[...]
        cc = c_ref[...]
        bb16, cc16 = bb.astype(BF16), cc.astype(BF16)
        cbm = lax.dot_general(cc16, bb16, (((1,), (1,)), ((), ())), preferred_element_type=F32)
        s_in = sin_ref[0]
        s_in16 = s_in.astype(BF16)
        ds_next = dstate[...]
        ds16 = ds_next.astype(BF16)
        ecs = jnp.exp(acs_x)
        dec = jnp.exp(atot_x - acs_x)
        etot = jnp.exp(atot_x)

        ddx_ref[...] += jnp.sum(gy * xs, axis=0, keepdims=True)
        z = jnp.dot(cc16, s_in16, preferred_element_type=F32)
        dz16 = (gy * ecs).astype(BF16)
        dacs = gy * z * ecs
        dcm = lax.dot_general(dz16, s_in16, (((1,), (1,)), ((), ())), preferred_element_type=F32)
        ds_here = jnp.dot(cc.T.astype(BF16), dz16, preferred_element_type=F32)
        u = jnp.dot(bb16, ds16, preferred_element_type=F32)
        dxd = u * dec
        w_dec = u * xd * dec
        dacs = dacs - w_dec
        datot = jnp.sum(w_dec, axis=0, keepdims=True) + jnp.sum(etot * ds_next * s_in, axis=0, keepdims=True)
        dbm = lax.dot_general((xd * dec).astype(BF16), ds16, (((1,), (1,)), ((), ())), preferred_element_type=F32)
        dstate[...] = etot * ds_next + ds_here

        lane = lax.broadcasted_iota(jnp.int32, (Q, 2 * HD), 1)
        lane_x = lax.broadcasted_iota(jnp.int32, (1, GW), 1)
        dcb = jnp.zeros((Q, Q), F32)
        dx_pairs = []
        for p in range(hpg // 2):
            sl = slice(2 * HD * p, 2 * HD * (p + 1))
            xp = xb[:, sl]
            gp = gb[:, sl]
            dxp = jnp.zeros((Q, 2 * HD), F32)
            for hh, lm in enumerate(_decay_pair(acs_x, acst_ref, p, tri)):
                r = 2 * p + hh
                mm = cbm * lm
                half = (lane >= HD) if hh == 1 else (lane < HD)
                xm = jnp.where(half, xp, jnp.zeros_like(xp))
                gm = jnp.where(half, gp, jnp.zeros_like(gp))
                mt16 = mm.T.astype(BF16)
                dxp = dxp + jnp.dot(mt16, gm, preferred_element_type=F32)
                dm = lax.dot_general(gm, xp, (((1,), (1,)), ((), ())), preferred_element_type=F32)
                dmt = lax.dot_general(xm, gp, (((1,), (1,)), ((), ())), preferred_element_type=F32)
                dcb = dcb + dm * lm
                wdiff = jnp.sum(dm * mm - dmt * mm.T, axis=1, keepdims=True)
                dacs = dacs + wdiff * (lane_x == HD * r).astype(F32)
            dx_pairs.append(dxp)
        dxd = dxd + jnp.concatenate(dx_pairs, axis=1)
        dcm = dcm + jnp.dot(dcb.astype(BF16), bb16, preferred_element_type=F32)
        dbm = dbm + jnp.dot(dcb.T.astype(BF16), cc16, preferred_element_type=F32)
        db_ref[...] = dbm
        dc_ref[...] = dcm

        last = lax.broadcasted_iota(jnp.int32, (Q, GW), 0) == Q - 1
        dacs_ref[...] = dacs + jnp.where(last, datot, 0.0)
        dxdt_ref[...] = dxd * xs
        dx_ref[...] = dxd * dtxv + gy * d_ref[...]

    rev = lambda g, c: (nc - 1 - c, g)
    return _pc(body, name=name, grid=(NG, nc),
               in_specs=[pl.BlockSpec((Q, GW), rev),
                         pl.BlockSpec((Q, NS), lambda g, c: (nc - 1 - c, DI // NS + g)),
                         pl.BlockSpec((Q, NS), lambda g, c: (nc - 1 - c, DI // NS + NG + g)),
                         pl.BlockSpec((Q, GW), rev),
                         pl.BlockSpec((Q, GW), rev),
                         pl.BlockSpec((hpg, Q), lambda g, c: (g, nc - 1 - c)),
                         pl.BlockSpec((1, GW), lambda g, c: (0, g)),
                         pl.BlockSpec((1, NS, GW), lambda g, c: (nc - 1 - c, 0, g)),
                         pl.BlockSpec((Q, GW), rev)],
               out_specs=[pl.BlockSpec((Q, GW), rev), pl.BlockSpec((Q, NS), rev), pl.BlockSpec((Q, NS), rev),
                          pl.BlockSpec((Q, GW), rev), pl.BlockSpec((Q, GW), rev),
                          pl.BlockSpec((1, GW), lambda g, c: (0, g))],
               out_shape=[_sds((t, DI), F32), _sds((t, NG * NS), F32), _sds((t, NG * NS), F32), _sds((t, DI), F32),
                          _sds((t, DI), F32), _sds((1, DI), F32)],
               scratch=[pltpu.VMEM((NS, GW), F32)])(xbc, xbc, xbc, dtx, acs_x, acs_t, d_x, sin, dy)


def _gated_norm_fwd(y, z, gn, *, name):
    t = y.shape[0]
    tm = min(t, 256)

    def body(y_ref, z_ref, g_ref, o_ref):
        for k in range(NG):
            sl = slice(GW * k, GW * (k + 1))
            zz = z_ref[:, sl]
            h = y_ref[:, sl] * (zz * _sigmoid(zz))
            r = lax.rsqrt(jnp.mean(h * h, axis=-1, keepdims=True) + EPS)
            o_ref[:, sl] = (h * r * g_ref[:, sl]).astype(BF16)

    return _pc(body, name=name, grid=(t // tm,), in_specs=[_row(tm, DI), _row(tm, DI), _const((1, DI))],
               out_specs=_row(tm, DI), out_shape=_sds((t, DI), BF16))(y, z, gn.reshape(1, DI))


def _gated_norm_bwd(y, z, gn, dout, *, name):
    t = y.shape[0]
    tm = min(t, 256)

    def body(y_ref, z_ref, g_ref, do_ref, dy_ref, dz_ref, dg_ref):
        @pl.when(pl.program_id(0) == 0)
        def _():
            dg_ref[...] = jnp.zeros_like(dg_ref)

        for k in range(NG):
            sl = slice(GW * k, GW * (k + 1))
            zz = z_ref[:, sl]
            yy = y_ref[:, sl]
            sz = zz * _sigmoid(zz)
            h = yy * sz
            r = lax.rsqrt(jnp.mean(h * h, axis=-1, keepdims=True) + EPS)
            d = do_ref[:, sl]
            dg_ref[:, sl] += jnp.sum(d * h * r, axis=0, keepdims=True)
            dgd = d * g_ref[:, sl]
            dot = jnp.mean(dgd * h, axis=-1, keepdims=True)
            dh = r * dgd - h * (r * r * r * dot)
            dy_ref[:, sl] = dh * sz
            dz_ref[:, sl] = (dh * yy * _dsilu(zz)).astype(BF16)

    dy, dz, dg = _pc(body, name=name, grid=(t // tm,),
                     in_specs=[_row(tm, DI), _row(tm, DI), _const((1, DI)), _row(tm, DI)],
                     out_specs=[_row(tm, DI), _row(tm, DI), _const((1, DI))],
                     out_shape=[_sds((t, DI), F32), _sds((t, DI), BF16), _sds((1, DI), F32)])(y, z, gn.reshape(1, DI), dout)
    return dy, dz, dg.reshape(DI)


def _pad_heads(v):
    return jnp.pad(v.reshape(1, NH), ((0, 0), (0, NHP - NH)))


def _ssm_fwd(h, p):
    u = _rms_fwd(h, p["norm"], name="ssm_rms_fwd")
    z = _mm(u, p["wz"], "nn", name="ssm_in_z")
    xp = _mm(u, p["wxbc"], "nn", name="ssm_in_xbc")
    dtraw = _mm(u, p["wdt"], "nn", name="ssm_in_dt")
    xbc = _conv4_fwd(xp, p["conv_w"], p["conv_b"], name="ssm_conv_fwd")
    dtb, alog = _pad_heads(p["dt_bias"]), _pad_heads(p["a_log"])
    d_x = jnp.repeat(p["d"], HD).reshape(1, DI)
    dtx, acs_x, acs_t = _ssd_pre(dtraw, dtb, alog, name="ssd_pre")
    y, sin = _ssd_fwd(xbc, dtx, acs_x, acs_t, d_x, name="ssd_fwd")
    yn = _gated_norm_fwd(y, z, p["gate_norm"], name="ssm_gate_fwd")
    h2 = _mm(yn, p["wout"], "nn", res=h, name="ssm_out")
    return h2, (h, u, z, xp, dtraw, xbc, dtx, acs_x, acs_t, d_x, dtb, alog, y, sin, yn)


def _ssm_bwd(dh2, p, saved):
    h, u, z, xp, dtraw, xbc, dtx, acs_x, acs_t, d_x, dtb, alog, y, sin, yn = saved
    dyn = _mm(dh2, p["wout"], "nt", name="ssm_out_dx")
    g = {"w_out": _mm(yn, dh2, "tn", out_dtype=BF16, name="ssm_out_dw")}
    dy, dz, g["gate_norm"] = _gated_norm_bwd(y, z, p["gate_norm"], dyn, name="ssm_gate_bwd")
    dxs, dbm, dcm, dacs, dxdt, ddx = _ssd_bwd(xbc, dtx, acs_x, acs_t, d_x, sin, dy, name="ssd_bwd")
    ddtraw, ddtb, dav = _ssd_post(dacs, dxdt, dtraw, dtb, alog, name="ssd_post")
    dpre, g["conv_w"], g["conv_b"] = _conv4_bwd_act(xp, p["conv_w"], p["conv_b"], jnp.concatenate([dxs, dbm, dcm], axis=1),
                                                    name="ssm_conv_bwd_act")
    dxp = _conv_bwd_x(dpre, p["conv_w"], KC, name="ssm_conv_bwd_x", out_dtype=BF16)
    dh, g["norm"] = _dx_rms_bwd([(dz, p["wz"], "nt"), (dxp, p["wxbc"], "nt"), (ddtraw, p["wdt"], "nt")], h, p["norm"], dh2,
                                name="ssm_in_dx_rms_bwd")
    g["w_in"] = jnp.concatenate([_mm(u, dz, "tn", out_dtype=BF16, name="ssm_in_dw_z"),
                                 _mm(u, dxp, "tn", out_dtype=BF16, name="ssm_in_dw_xbc"),
                                 _mm(u, ddtraw, "tn", out_dtype=BF16, name="ssm_in_dw_dt")[:, :NH]], axis=1)
    g["dt_bias"] = ddtb[0, :NH]
    g["a_log"] = dav[0, :NH] * (-jnp.exp(p["a_log"]))
    g["d"] = ddx.reshape(NH, HD).sum(-1)
    return dh, g


def _cv_fwd(h, p):
    u = _rms_fwd(h, p["norm"], name="cv_rms_fwd")
    pre = _mm(u, p["wpw1"], "nn", bias=p["b_pw1"].reshape(1, 2 * D), name="cv_pw1")
    gl = _glu_fwd(pre, name="cv_glu_fwd")
    c, s = _dwconv_ln_fwd(gl, p["dw_w"], p["dw_b"], p["ln_g"], p["ln_b"], name="cv_dwconv_ln_fwd")
    h2 = _mm(s, p["wpw2"], "nn", bias=p["b_pw2"].reshape(1, D), res=h, name="cv_pw2")
    return h2, (h, u, pre, gl, c, s)


def _cv_bwd(dh2, p, saved):
    h, u, pre, gl, c, s = saved
    ds = _mm(dh2, p["wpw2"], "nt", name="cv_pw2_dx")
    g = {"w_pw2": _mm(s, dh2, "tn", out_dtype=BF16, name="cv_pw2_dw"), "b_pw2": _colsum(dh2, name="cv_pw2_db")}
    dc, g["ln_g"], g["ln_b"] = _ln_silu_bwd(c, p["ln_g"], p["ln_b"], ds, name="cv_ln_bwd")
    dgl = _conv_bwd_x(dc, p["dw_w"], CK, name="cv_dwconv_bwd_x", out_dtype=F32)
    g["dw_w"], g["dw_b"] = _dwconv_bwd_w(gl, dc, name="cv_dwconv_bwd_w")
    dpre, g["b_pw1"] = _glu_bwd(pre, dgl, name="cv_glu_bwd")
    g["w_pw1"] = _mm(u, dpre, "tn", out_dtype=BF16, name="cv_pw1_dw")
    dh, g["norm"] = _dx_rms_bwd([(dpre, p["wpw1"], "nt")], h, p["norm"], dh2, name="cv_pw1_dx_rms_bwd")
    return dh, g


def _ffn_fwd(h, p):
    u = _rms_fwd(h, p["norm"], name="ffn_rms_fwd")
    gate, up, a = _ffn_up_act(u, p["wg_t"], p["wu_t"], name="ffn_gate_up_act")
    h2 = _mm(a, p["wd"], "nn", res=h, name="ffn_down")
    return h2, (h, u, gate, up, a)


def _ffn_bwd(dh2, p, saved, after=None):
    h, u, gate, up, a = saved
    dgate, dup = _ffn_down_dx_act(dh2, p["wd"], gate, up, name="ffn_down_dx_act", after=after)
    g = {"w_down": _mm(a, dh2, "tn", out_dtype=BF16, name="ffn_down_dw"),
         "w_gate": _mm(dgate, u, "tn", out_dtype=BF16, name="ffn_gate_dw"),
         "w_up": _mm(dup, u, "tn", out_dtype=BF16, name="ffn_up_dw")}
    dh, g["norm"] = _dx_rms_bwd([(dgate, p["wg_t"], "nn"), (dup, p["wu_t"], "nn")], h, p["norm"], dh2, name="ffn_dx_rms_bwd")
    return dh, g


def _fwd_bwd(x, target, depth, mixer_weights, ffn_weights, final_norm, layer_done=None):
    h, tape, weights = x, [], []
    for i in range(depth):
        mix_w = mixer_weights(i, h)
        h, s_mix = (_ssm_fwd if i % 2 == 0 else _cv_fwd)(h, mix_w)
        ffn_w = ffn_weights(i, h)
        h, s_ffn = _ffn_fwd(h, ffn_w)
        tape.append((s_mix, s_ffn))
        weights.append((mix_w, ffn_w))
    lsum, dh, g_final = _final_loss(h, final_norm, target, name="loss_head")
    g_mix, g_ffn, token = [None] * depth, [None] * depth, None
    for i in reversed(range(depth)):
        s_mix, s_ffn = tape[i]
        dh, g_ffn[i] = _ffn_bwd(dh, weights[i][1], s_ffn, after=token)
        dh, g_mix[i] = (_ssm_bwd if i % 2 == 0 else _cv_bwd)(dh, weights[i][0], s_mix)
        if layer_done is not None:
            token = layer_done(i, g_mix[i], g_ffn[i], g_final)
    return lsum, dh, g_mix, g_ffn, g_final


ANY = pl.BlockSpec(memory_space=pl.ANY)


def _place():
    x, y, c = lax.axis_index("x"), lax.axis_index("y"), lax.axis_index("c")
    return x, y, c, [(1 - x, y), (x, 1 - y), (1 - x, 1 - y)]


def _remote(src, dst, ssem, rsem, dev):
    return pltpu.make_async_remote_copy(src_ref=src, dst_ref=dst, send_sem=ssem, recv_sem=rsem, device_id=dev,
                                        device_id_type=MESH)


def _comm_call(body, name, ins, out_shape, n_sems):
    return pl.pallas_call(
        body, name=name, in_specs=[ANY] * len(ins), out_specs=[ANY] * len(out_shape), out_shape=out_shape,
        scratch_shapes=[pltpu.SemaphoreType.DMA((k,)) for k in n_sems])(*ins)


HBM = pl.BlockSpec(memory_space=pltpu.HBM)
SEM = pl.BlockSpec(memory_space=pltpu.SEMAPHORE)
EFFECT = pltpu.SideEffectType.DATAFLOW_SIDE_EFFECTING


def _gather_copies(src, land, ssem, rsem):
    x, y, c, chips = _place()
    me, pairs = 2 * x + y, []
    for a in range(len(src)):
        h = src[a].shape[0] // 2
        rows = pl.ds(c * h, h)
        for j, (px, py) in enumerate(chips):
            sems = (ssem.at[3 * a + j], rsem.at[3 * a + j], (px, py, c))
            pairs.append((_remote(src[a].at[rows], land[a].at[me, rows], *sems),
                          _remote(src[a].at[rows], land[a].at[2 * px + py, rows], *sems)))
    return pairs


def _chip_copies(src, land, ssem, rsem):
    x, y, c, chips = _place()
    me, pairs = 2 * x + y, []
    for a in range(len(src)):
        for j, (px, py) in enumerate(chips):
            sems = (ssem.at[3 * a + j], rsem.at[3 * a + j], (px, py, c))
            pairs.append((_remote(src[a].at[2 * px + py], land[a].at[me], *sems),
                          _remote(src[a].at[2 * px + py], land[a].at[2 * px + py], *sems)))
    return pairs


def _split_start(copies, srcs, land_shapes, dep, name):
    n = len(srcs)

    def body(*refs):
        for send, _ in copies(refs[:n], refs[n:2 * n], refs[2 * n + 1], refs[2 * n + 2]):
            send.start()
        refs[-1][...] = jnp.zeros_like(refs[-1])

    lands = [pltpu.with_memory_space_constraint(lax.empty(s, w.dtype), pltpu.HBM) for s, w in zip(land_shapes, srcs)]
    out = pl.pallas_call(
        body, name=name,
        out_shape=(pltpu.SemaphoreType.DMA((3 * n,)), pltpu.SemaphoreType.DMA((3 * n,)),
                   *[pltpu.HBM(w.shape, w.dtype) for w in srcs], *[pltpu.HBM(w.shape, w.dtype) for w in lands],
                   _sds((8, 128), F32)),
        in_specs=[HBM] * (2 * n) + [ANY], out_specs=(SEM, SEM, *[HBM] * (2 * n), pl.BlockSpec(memory_space=pltpu.VMEM)),
        input_output_aliases={a: 2 + a for a in range(2 * n)},
        compiler_params=pltpu.CompilerParams(has_side_effects=EFFECT))(
            *[pltpu.with_memory_space_constraint(w, pltpu.HBM) for w in srcs], *lands, dep)
    return out[0], out[1], list(out[2:2 + n]), list(out[2 + n:2 + 2 * n]), out[-1]


def _split_wait(copies, ssem, rsem, srcs, lands, after, name):
    n = len(srcs)

    def body(*refs):
        for send, recv in copies(refs[:n], refs[n:2 * n], refs[2 * n], refs[2 * n + 1]):
            send.wait_send()
            recv.wait_recv()

    out = pl.pallas_call(
        body, name=name,
        out_shape=(*[pltpu.HBM(w.shape, w.dtype) for w in srcs], *[pltpu.HBM(w.shape, w.dtype) for w in lands]),
        in_specs=[HBM] * (2 * n) + [SEM, SEM, ANY], out_specs=tuple([HBM] * (2 * n)),
        input_output_aliases={a: a for a in range(2 * n)},
        compiler_params=pltpu.CompilerParams(has_side_effects=EFFECT))(*srcs, *lands, ssem, rsem, after)
    return list(out[:n]), list(out[n:])


def _gather_start(ws, dep, tag):
    return _split_start(_gather_copies, ws, [(4,) + w.shape for w in ws], dep, "gather_start_" + tag)


def _gather_wait(ssem, rsem, srcs, lands, after, tag):
    return _split_wait(_gather_copies, ssem, rsem, srcs, lands, after, "gather_wait_" + tag)


def _gather_forward(lands, tag):
    n = len(lands)

    def body(*refs):
        land = refs[n:2 * n]
        ssem, rsem = refs[2 * n:]
        x, y, c, chips = _place()
        sib = (x, y, 1 - c)
        cps = []
        for a in range(n):
            h = lands[a].shape[1] // 2
            for j, (px, py) in enumerate(chips):
                blk = land[a].at[2 * px + py, pl.ds(c * h, h)]
                cps.append(_remote(blk, blk, ssem.at[3 * a + j], rsem.at[3 * a + j], sib))
        for cp in cps:
            cp.start()
        for a in range(n):
            h = lands[a].shape[1] // 2
            for j, (px, py) in enumerate(chips):
                blk = land[a].at[2 * px + py, pl.ds((1 - c) * h, h)]
                _remote(blk, blk, ssem.at[3 * a + j], rsem.at[3 * a + j], sib).wait_recv()
        for cp in cps:
            cp.wait_send()

    return pl.pallas_call(
        body, name="gather_forward_" + tag, in_specs=[ANY] * n, out_specs=[ANY] * n,
        out_shape=[_sds(w.shape, w.dtype) for w in lands], input_output_aliases={a: a for a in range(n)},
        scratch_shapes=[pltpu.SemaphoreType.DMA((3 * n,)), pltpu.SemaphoreType.DMA((3 * n,))])(*lands)


def _exchange_pairs(gs, tag, dep=None):
    n = len(gs)
    extra = [] if dep is None else [dep]

    def body(*refs):
        ins, got = refs[:n], refs[n + len(extra):2 * n + len(extra)]
        ssem, rsem = refs[2 * n + len(extra):]
        x, y, c, _ = _place()
        cps = []
        for a in range(n):
            lh = gs[a].shape[1] // 2
            cps.append(_remote(ins[a].at[:, pl.ds((1 - c) * lh, lh)], got[a], ssem.at[a], rsem.at[a], (x, y, 1 - c)))
        for cp in cps:
            cp.start()
        for cp in cps:
            cp.wait()

    halves = [_sds((4, g.shape[1] // 2) + g.shape[2:], g.dtype) for g in gs]
    return _comm_call(body, "reduce_pair_exchange_" + tag, list(gs) + extra, halves, [n, n])


def _exchange_final(fs, tag):
    n = len(fs)

    def body(*refs):
        ins, outs = refs[:n], refs[n:2 * n]
        ssem, rsem = refs[2 * n:]
        x, y, c, _ = _place()
        cps = [_remote(ins[a], outs[a], ssem.at[a], rsem.at[a], (x, y, 1 - c)) for a in range(n)]
        for cp in cps:
            cp.start()
        for cp in cps:
            cp.wait()

    return _comm_call(body, "reduce_final_exchange_" + tag, fs, [_sds(f.shape, f.dtype) for f in fs], [n, n])


def _rows_tile(r, c, n_arrays):
    cap = max(8, (16 * 1024 * 1024) // (2 * n_arrays * c * 4))
    fits = [t for t in range(8, min(r, cap) + 1, 8) if r % t == 0]
    return max(fits) if fits else r


REDUCE_ROW_BLOCKS = 2


def _add_pairs(gs, gots, core, *, name):
    n = len(gs)

    def body(core_ref, *refs):
        for k in range(n):
            refs[2 * n + k][...] = (refs[k][...].astype(F32) + refs[n + k][...].astype(F32)).astype(BF16)

    halves, wholes, shapes = [], [], []
    for g in gs:
        _, r, c = g.shape
        tm = r // (2 * REDUCE_ROW_BLOCKS)
        wholes.append(pl.BlockSpec((None, None, tm, c), lambda s, i, core_ref: (s, core_ref[0], i, 0)))
        halves.append(pl.BlockSpec((None, tm, c), lambda s, i, core_ref: (s, i, 0)))
        shapes.append(_sds((4, r // 2, c), BF16))
    grid_spec = pltpu.PrefetchScalarGridSpec(num_scalar_prefetch=1, grid=(4, REDUCE_ROW_BLOCKS),
                                             in_specs=wholes + halves, out_specs=halves)
    return pl.pallas_call(body, name=name, grid_spec=grid_spec, out_shape=shapes,
                          compiler_params=pltpu.CompilerParams(dimension_semantics=("arbitrary", "arbitrary"),
                                                               vmem_limit_bytes=VMEM_LIMIT))(
                                                                   core, *[g.reshape(4, 2, g.shape[1] // 2, g.shape[2]) for g in gs], *gots)


def _sum_slots(rbs, *, name):
    n = len(rbs)

    def body(*refs):
        for k in range(n):
            acc = refs[k][0].astype(F32)
            for s in range(1, 4):
                acc = acc + refs[k][s].astype(F32)
            refs[n + k][...] = acc

    ins, outs, shapes = [], [], []
    for rb in rbs:
        _, rows, c = rb.shape
        tm = rows // REDUCE_ROW_BLOCKS
        ins.append(pl.BlockSpec((4, tm, c), lambda i: (0, i, 0)))
        outs.append(_row(tm, c))
        shapes.append(_sds((rows, c), F32))
    return _pc(body, name=name, grid=(REDUCE_ROW_BLOCKS,), in_specs=ins, out_specs=outs, out_shape=shapes)(*rbs)


def _reduce_start(gs, core, tag, dep=None):
    got = _exchange_pairs(gs, tag, dep)
    core1 = core.reshape(1).astype(jnp.int32)
    ps = _add_pairs(gs, got, core1, name="reduce_pair_add_" + tag)
    ssem, rsem, ps, rbs, token = _split_start(_chip_copies, ps, [p.shape for p in ps], got[0], "reduce_start_" + tag)
    return (ssem, rsem, ps, rbs), token


def _reduce_finish(state, after, chip, core, tag):
    ps, rbs = _split_wait(_chip_copies, *state, after, "reduce_wait_" + tag)
    rbs = [lax.dynamic_update_index_in_dim(b, lax.dynamic_index_in_dim(p, chip, 0, keepdims=False), chip, 0)
           for b, p in zip(rbs, ps)]
    fs = _sum_slots(rbs, name="reduce_chip_sum_" + tag)
    theirs = _exchange_final(fs, tag)
    return [jnp.where(core == 0, jnp.concatenate([f, t], axis=0), jnp.concatenate([t, f], axis=0))
            for f, t in zip(fs, theirs)]


def _gather8(v, reduce):
    m = v.shape[0]

    def body(x_ref, *rest):
        if reduce:
            sum_ref, out_ref, send_sems, recv_sems, local_sem = rest
        else:
            out_ref, send_sems, recv_sems, local_sem = rest
        x, y, c, chips = _place()
        me, sibling = (x, y, c), (x, y, 1 - c)

        def rows(px, py, pc):
            return out_ref.at[pl.ds((4 * px + 2 * py + pc) * m, m), :]

        def copy(k, block, to, src=None):
            return _remote(rows(*block) if src is None else src, rows(*block), send_sems.at[k], recv_sems.at[k], to)

        mine = pltpu.make_async_copy(x_ref, rows(*me), local_sem)
        mine.start()
        first = [copy(0, me, sibling, src=x_ref)]
        first += [copy(1 + j, me, (*chip, c), src=x_ref) for j, chip in enumerate(chips)]
        for cp in first:
            cp.start()
        passed = [copy(4 + j, (*chip, c), sibling) for j, chip in enumerate(chips)]
        for j, chip in enumerate(chips):
            copy(1 + j, (*chip, c), me).wait_recv()
            passed[j].start()
        copy(0, sibling, me).wait_recv()
        for j, chip in enumerate(chips):
            copy(4 + j, (*chip, 1 - c), me).wait_recv()
        for cp in first + passed:
            cp.wait_send()
        mine.wait()
        if reduce:
            acc = out_ref[pl.ds(0, m), :]
            for k in range(1, 8):
                acc = acc + out_ref[pl.ds(k * m, m), :]
            sum_ref[...] = acc

    vm = pl.BlockSpec(memory_space=pltpu.VMEM)
    out_shape = [_sds((8 * m, 128), F32)]
    if reduce:
        out_shape = [_sds((m, 128), F32)] + out_shape
    out = pl.pallas_call(
        body, name="allreduce_small" if reduce else "gather_small", in_specs=[vm], out_specs=[vm] * len(out_shape),
        out_shape=out_shape,
        scratch_shapes=[pltpu.SemaphoreType.DMA((7,)), pltpu.SemaphoreType.DMA((7,)), pltpu.SemaphoreType.DMA(())],
        compiler_params=pltpu.CompilerParams(vmem_limit_bytes=VMEM_LIMIT))(v)
    return out[0]


def _adamw(w, g, m, v, *, name):
    if w.ndim == 2:
        return [o[0] for o in _adamw(w[None], g[None], m[None], v[None], name=name)]
    nl, r, c = w.shape
    tm = _rows_tile(r, c, 7)
    blk = pl.BlockSpec((None, tm, c), lambda l, i: (l, i, 0))

    def body(w_ref, g_ref, m_ref, v_ref, d_ref, mo_ref, vo_ref):
        gg = g_ref[...]
        mm = B1 * m_ref[...] + (1.0 - B1) * gg
        vv = B2 * v_ref[...] + (1.0 - B2) * (gg * gg)
        mo_ref[...] = mm
        vo_ref[...] = vv
        m_hat = mm / (1.0 - B1 ** STEP)
        v_hat = vv / (1.0 - B2 ** STEP)
        d_ref[...] = -LR * (m_hat / (jnp.sqrt(v_hat) + AEPS) + WD * w_ref[...])

    return _pc(body, name=name, grid=(nl, r // tm), in_specs=[blk] * 4, out_specs=[blk] * 3,
               out_shape=[_sds((nl, r, c), F32)] * 3)(w, g, m, v)


WEIGHTS = ["ssm_norm", "ssm_w_in", "ssm_conv_w", "ssm_conv_b", "ssm_dt_bias", "ssm_a_log", "ssm_d", "ssm_gate_norm",
           "ssm_w_out", "cv_norm", "cv_w_pw1", "cv_b_pw1", "cv_dw_w", "cv_dw_b", "cv_ln_g", "cv_ln_b", "cv_w_pw2",
           "cv_b_pw2", "ffn_norm", "ffn_w_gate", "ffn_w_up", "ffn_w_down", "final_norm"]
TRANSPOSED = ("ffn_w_gate", "ffn_w_up")
ADAMW_TRANSPOSED = ("ssm_w_in",) + TRANSPOSED
BIG = {"ssm_w_in": "col", "ssm_w_out": "row", "cv_w_pw1": "col", "cv_w_pw2": "row",
       "ffn_w_gate": "row", "ffn_w_up": "row", "ffn_w_down": "row"}
SMALL_SHARDED = {"ssm_conv_w": 2, "cv_norm": 1, "cv_b_pw1": 1, "cv_dw_w": 2, "cv_dw_b": 1, "cv_ln_g": 1, "cv_ln_b": 1,
                 "cv_b_pw2": 1}
SMALL = [n for n in WEIGHTS if n not in BIG]
LANES = 128


def _pack(arrays):
    flat = jnp.concatenate([a.reshape(-1) for a in arrays])
    rows = -(-flat.size // (8 * LANES)) * 8
    return jnp.pad(flat, (0, rows * LANES - flat.size)).reshape(rows, LANES)


def _unpack(packed, shapes):
    flat, out, off = packed.reshape(-1), [], 0
    for s in shapes:
        n = 1
        for d_ in s:
            n *= d_
        out.append(flat[off:off + n].reshape(s))
        off += n
    return out


def _unshard(parts, axis):
    return jnp.concatenate(parts, axis=axis)


def kernel(x, ssm_norm, ssm_w_in, ssm_conv_w, ssm_conv_b, ssm_dt_bias, ssm_a_log, ssm_d, ssm_gate_norm, ssm_w_out, cv_norm, cv_w_pw1, cv_b_pw1, cv_dw_w, cv_dw_b, cv_ln_g, cv_ln_b, cv_w_pw2, cv_b_pw2, ffn_norm, ffn_w_gate, ffn_w_up, ffn_w_down, final_norm, loss_target, m_ssm_norm, m_ssm_w_in, m_ssm_conv_w, m_ssm_conv_b, m_ssm_dt_bias, m_ssm_a_log, m_ssm_d, m_ssm_gate_norm, m_ssm_w_out, m_cv_norm, m_cv_w_pw1, m_cv_b_pw1, m_cv_dw_w, m_cv_dw_b, m_cv_ln_g, m_cv_ln_b, m_cv_w_pw2, m_cv_b_pw2, m_ffn_norm, m_ffn_w_gate, m_ffn_w_up, m_ffn_w_down, m_final_norm, v_ssm_norm, v_ssm_w_in, v_ssm_conv_w, v_ssm_conv_b, v_ssm_dt_bias, v_ssm_a_log, v_ssm_d, v_ssm_gate_norm, v_ssm_w_out, v_cv_norm, v_cv_w_pw1, v_cv_b_pw1, v_cv_dw_w, v_cv_dw_b, v_cv_ln_g, v_cv_ln_b, v_cv_w_pw2, v_cv_b_pw2, v_ffn_norm, v_ffn_w_gate, v_ffn_w_up, v_ffn_w_down, v_final_norm):
    a = dict(locals())
    chip = 2 * lax.axis_index("x") + lax.axis_index("y")
    n_ssm, n_cv, depth = ssm_norm.shape[0], cv_norm.shape[0], ffn_norm.shape[0]

    def own(n, layer):
        w = a[n][layer].astype(BF16)
        return w.T if n in TRANSPOSED else w

    sh_names = list(SMALL_SHARDED)
    got = _gather8(_pack([a[n] for n in sh_names]), reduce=False)
    got = got.reshape(8, -1)[0::2]
    per_chip = [_unpack(got[s], [a[n].shape for n in sh_names]) for s in range(4)]
    full = {n: a[n] for n in SMALL}
    for k, n in enumerate(sh_names):
        full[n] = _unshard([per_chip[s][k] for s in range(4)], SMALL_SHARDED[n])

    def matrices_of(i, part="layer"):
        mixer = [(n, i // 2) for n in (("ssm_w_in", "ssm_w_out") if i % 2 == 0 else ("cv_w_pw1", "cv_w_pw2"))]
        ffn_part = [(n, i) for n in ("ffn_w_gate", "ffn_w_up", "ffn_w_down")]
        return {"mixer": mixer, "ffn": ffn_part, "layer": mixer + ffn_part}[part]

    stages = [matrices_of(0, "mixer"), matrices_of(0, "ffn")] + [matrices_of(i) for i in range(1, depth)]
    in_flight, landed = {}, {}

    def matrix(n, layer, h):
        if not in_flight and not landed:
            dep = h
            for k, stage in enumerate(stages):
                in_flight[k] = _gather_start([own(m, l) for m, l in stage], dep, str(k))
                dep = in_flight[k][-1]
            in_flight["token"] = dep[0, 0]
        if (n, layer) not in landed:
            k = next(k for k, st in enumerate(stages) if (n, layer) in st)
            ssem, rsem, srcs, lands, _ = in_flight.pop(k)
            srcs, lands = _gather_wait(ssem, rsem, srcs, lands, h, str(k))
            lands = _gather_forward(lands, str(k))
            for key, src, land in zip(stages[k], srcs, lands):
                landed[key] = (src, land, in_flight["token"])
        return landed[(n, layer)]

    def whole(h, *keys):
        got = [matrix(n, l, h) for n, l in keys]
        if BIG[keys[0][0]] == "row" and len(keys) == 1:
            src, land, token = got[0]
            return lax.dynamic_update_index_in_dim(land, src, chip, 0).reshape(-1, land.shape[-1]), token
        parts = [jnp.where(chip == s, src, land[s]) for src, land, _ in got for s in range(4)]
        return jnp.concatenate(parts, axis=1 if BIG[keys[0][0]] == "col" else 0), got[0][2]

    def mixer_weights(i, h):
        j = i // 2
        if i % 2 == 0:
            (w_in, token), (w_out, _) = whole(h, ("ssm_w_in", j)), whole(h, ("ssm_w_out", j))
            return dict(norm=full["ssm_norm"][j] + token, wz=w_in[:, :DI], wxbc=w_in[:, DI:DI + CD],
                        wdt=jnp.pad(w_in[:, DI + CD:], ((0, 0), (0, NHP - NH))),
                        conv_w=full["ssm_conv_w"][j], conv_b=full["ssm_conv_b"][j], dt_bias=full["ssm_dt_bias"][j],
                        a_log=full["ssm_a_log"][j], d=full["ssm_d"][j], gate_norm=full["ssm_gate_norm"][j], wout=w_out)
        (w_pw1, token), (w_pw2, _) = whole(h, ("cv_w_pw1", j)), whole(h, ("cv_w_pw2", j))
        return dict(norm=full["cv_norm"][j] + token, wpw1=w_pw1, b_pw1=full["cv_b_pw1"][j], dw_w=full["cv_dw_w"][j],
                    dw_b=full["cv_dw_b"][j], ln_g=full["cv_ln_g"][j], ln_b=full["cv_ln_b"][j], wpw2=w_pw2,
                    b_pw2=full["cv_b_pw2"][j])

    def ffn_weights(i, h):
        (wg_t, token), (wu_t, _), (wd, _) = (whole(h, (n, i)) for n in ("ffn_w_gate", "ffn_w_up", "ffn_w_down"))
        return dict(norm=full["ffn_norm"][i] + token, wg_t=wg_t, wu_t=wu_t, wd=wd)

    core = lax.axis_index("c")
    reducing, seen = {}, {}

    def reduce_small(g_final):
        g_mixers, g_ffns = [seen[i][0] for i in range(depth)], [seen[i][1] for i in range(depth)]
        local_small = {"final_norm": g_final, "ffn_norm": jnp.stack([g["norm"] for g in g_ffns])}
        for n in SMALL:
            if n.startswith("ssm_"):
                local_small[n] = jnp.stack([g[n[4:]] for g in g_mixers[0::2]])
            elif n.startswith("cv_"):
                local_small[n] = jnp.stack([g[n[3:]] for g in g_mixers[1::2]])
        return _gather8(_pack([local_small[n] for n in SMALL]), reduce=True)

    def layer_done(i, g_mixer, g_ffn_i, g_final):
        seen[i] = (g_mixer, g_ffn_i)
        slots = []
        for n, _ in matrices_of(i):
            g = (g_ffn_i if n.startswith("ffn_") else g_mixer)[n.split("_", 1)[1]]
            if BIG[n] == "col":
                slots.append(g.reshape(g.shape[0], 4, g.shape[1] // 4).transpose(1, 0, 2))
            else:
                slots.append(g.reshape(4, g.shape[0] // 4, g.shape[1]))
        dep = None
        if i == 0:
            dep = reducing["small"] = reduce_small(g_final)
        reducing[i], reducing["token"] = _reduce_start(slots, core, str(i), dep)
        return reducing["token"]

    lsum, grad_x, g_mix, g_ffn, g_final = _fwd_bwd(x[0], loss_target[0], depth, mixer_weights, ffn_weights,
                                                   full["final_norm"], layer_done)
    loss = (0.5 / D) * lax.psum(jnp.sum(lsum), ("x", "y", "c"))

    grads = {}
    for n, g in zip(SMALL, _unpack(reducing.pop("small"), [full[n].shape for n in SMALL])):
        if n in SMALL_SHARDED:
            ax = SMALL_SHARDED[n]
            g = lax.dynamic_slice_in_dim(g, chip * a[n].shape[ax], a[n].shape[ax], axis=ax)
        grads[n] = g
    last_start = reducing.pop("token")

    per_layer, after = {n: [None] * a[n].shape[0] for n in BIG}, last_start
    for i in reversed(range(depth)):
        done = _reduce_finish(reducing.pop(i), after, chip, core, str(i))
        for (n, l), g in zip(matrices_of(i), done):
            per_layer[n][l] = g
        after = done[0]
    grads_t = {}
    for n in BIG:
        if n in TRANSPOSED:
            grads_t[n] = jnp.stack(per_layer[n])
        else:
            grads[n] = jnp.stack(per_layer[n])

    delta, new_m, new_v = {}, {}, {}
    for n in BIG:
        if n in ADAMW_TRANSPOSED:
            g_t = grads_t[n] if n in grads_t else grads[n].transpose(0, 2, 1)
            outs = _adamw(a[n].transpose(0, 2, 1), g_t, a["m_" + n].transpose(0, 2, 1), a["v_" + n].transpose(0, 2, 1),
                          name="adamw_" + n)
            grads[n] = g_t.transpose(0, 2, 1)
            delta[n], new_m[n], new_v[n] = [o.transpose(0, 2, 1) for o in outs]
        else:
            delta[n], new_m[n], new_v[n] = _adamw(a[n], grads[n], a["m_" + n], a["v_" + n], name="adamw_" + n)
    shapes = [a[n].shape for n in SMALL]
    upd = _adamw(_pack([a[n] for n in SMALL]), _pack([grads[n] for n in SMALL]), _pack([a["m_" + n] for n in SMALL]),
                 _pack([a["v_" + n] for n in SMALL]), name="adamw_small")
    for dst, packed in zip((delta, new_m, new_v), upd):
        dst.update(zip(SMALL, _unpack(packed, shapes)))

    return (loss, grad_x[None], *[grads[n] for n in WEIGHTS], *[delta[n] for n in WEIGHTS],
            *[new_m[n] for n in WEIGHTS], *[new_v[n] for n in WEIGHTS])
```

```python
import functools

import jax
import jax.numpy as jnp
from jax import lax
from jax.experimental import pallas as pl
from jax.experimental.pallas import tpu as pltpu

F32 = jnp.float32
BF16 = jnp.bfloat16
HI = lax.Precision.HIGHEST
MESH = pl.DeviceIdType.MESH

D = 1024
DI = 2048
HD = 64
NH = 32
NG = 4
GW = DI // NG
NS = 128
KC = 4
CD = DI + 2 * NG * NS
Q = 128
DFF = 2816
CK = 31
EPS = 1e-5
NHP = 128

LR, B1, B2, AEPS, WD, STEP = 0.001, 0.9, 0.999, 1e-08, 0.01, 10

VMEM_LIMIT = 56 * 1024 * 1024
MM_VMEM_BUDGET = 36 * 1024 * 1024


def _pc(body, *, name, grid, in_specs, out_specs, out_shape, scratch=()):
    return pl.pallas_call(
        body, name=name, grid=grid, in_specs=in_specs, out_specs=out_specs, out_shape=out_shape,
        scratch_shapes=list(scratch),
        compiler_params=pltpu.CompilerParams(dimension_semantics=("arbitrary",) * len(grid),
                                             vmem_limit_bytes=VMEM_LIMIT))


def _sds(shape, dtype):
    return jax.ShapeDtypeStruct(tuple(shape), dtype)


def _row(tm, c):
    return pl.BlockSpec((tm, c), lambda i: (i, 0))


def _const(shape):
    return pl.BlockSpec(tuple(shape), lambda i: (0,) * len(shape))


def _tile(n, cap):
    if n <= cap:
        return n
    best = 128
    for t in range(128, cap + 1, 128):
        if n % t == 0:
            best = t
    return best


def _sigmoid(x):
    return 1.0 / (1.0 + jnp.exp(-x))


def _dsilu(x):
    s = _sigmoid(x)
    return s * (1.0 + x * (1.0 - s))


def _mm(a, b, mode, *, name, out_dtype=F32, bias=None, res=None, tm=1024, after=None, norm_g=None):
    if mode == "nn":
        (m, k), (k2, n) = a.shape, b.shape
    elif mode == "nt":
        (m, k), (n, k2) = a.shape, b.shape
    else:
        (k, m), (k2, n) = a.shape, b.shape
    assert k == k2, (a.shape, b.shape, mode)
    tm = _tile(m, tm if k <= 2048 else 512)
    so = jnp.dtype(out_dtype).itemsize + (4 if res is not None else 0)
    fits = [c for c in range(128, min(n, 2048) + 1, 128) if n % c == 0 and
            2 * (tm * k * a.dtype.itemsize + c * k * b.dtype.itemsize + tm * c * so) <= MM_VMEM_BUDGET]
    tn = n if n <= 128 else max(fits)
    dn = {"nn": (((1,), (0,)), ((), ())), "nt": (((1,), (1,)), ((), ())), "tn": (((0,), (0,)), ((), ()))}[mode]
    nb, nr, nn = bias is not None, res is not None, norm_g is not None
    assert not nn or tn == n == D, (n, tn)

    def body(*refs):
        acc = lax.dot_general(refs[0][...].astype(BF16), refs[1][...].astype(BF16), dn, preferred_element_type=F32)
        if nb:
            acc = acc + refs[2][...]
        if nr:
            acc = acc + refs[2 + nb][...]
        if nn:
            r = lax.rsqrt(jnp.mean(acc * acc, axis=-1, keepdims=True) + EPS)
            refs[-1][...] = (acc * r * refs[2 + nb + nr][...]).astype(BF16)
            refs[-2][...] = acc.astype(out_dtype)
        else:
            refs[-1][...] = acc.astype(out_dtype)

    a_spec = pl.BlockSpec((k, tm), lambda i, j: (0, i)) if mode == "tn" else pl.BlockSpec((tm, k), lambda i, j: (i, 0))
    b_spec = pl.BlockSpec((tn, k), lambda i, j: (j, 0)) if mode == "nt" else pl.BlockSpec((k, tn), lambda i, j: (0, j))
    ins, specs = [a, b], [a_spec, b_spec]
    if nb:
        ins.append(bias)
        specs.append(pl.BlockSpec((1, tn), lambda i, j: (0, j)))
    if nr:
        ins.append(res)
        specs.append(pl.BlockSpec((tm, tn), lambda i, j: (i, j)))
    if nn:
        ins.append(norm_g.reshape(1, D))
        specs.append(pl.BlockSpec((1, D), lambda i, j: (0, 0)))
    if after is not None:
        ins.append(after)
        specs.append(pl.BlockSpec((8, 128), lambda i, j: (0, 0)))
    tile = pl.BlockSpec((tm, tn), lambda i, j: (i, j))
    if nn:
        return _pc(body, name=name, grid=(m // tm, n // tn), in_specs=specs, out_specs=[tile, tile],
                   out_shape=[_sds((m, n), out_dtype), _sds((m, n), BF16)])(*ins)
    return _pc(body, name=name, grid=(m // tm, n // tn), in_specs=specs, out_specs=tile, out_shape=_sds((m, n), out_dtype))(*ins)


def _rms_fwd(h, g, *, name):
    t = h.shape[0]
    tm = min(t, 512)

    def body(h_ref, g_ref, u_ref):
        x = h_ref[...]
        r = lax.rsqrt(jnp.mean(x * x, axis=-1, keepdims=True) + EPS)
        u_ref[...] = (x * r * g_ref[...]).astype(BF16)

    return _pc(body, name=name, grid=(t // tm,), in_specs=[_row(tm, D), _const((1, D))], out_specs=_row(tm, D),
               out_shape=_sds((t, D), BF16))(h, g.reshape(1, D))


def _rms_bwd(h, g, du, dres, *, name):
    t = h.shape[0]
    tm = min(t, 512)

    def body(h_ref, g_ref, du_ref, dres_ref, dh_ref, dg_ref):
        x = h_ref[...]
        r = lax.rsqrt(jnp.mean(x * x, axis=-1, keepdims=True) + EPS)
        dy = du_ref[...]
        dyg = dy * g_ref[...]
        dot = jnp.mean(dyg * x, axis=-1, keepdims=True)
        dh_ref[...] = dres_ref[...] + r * dyg - x * (r * r * r * dot)

        @pl.when(pl.program_id(0) == 0)
        def _():
            dg_ref[...] = jnp.zeros_like(dg_ref)

        dg_ref[...] += jnp.sum(dy * x * r, axis=0, keepdims=True)

    dh, dg = _pc(body, name=name, grid=(t // tm,),
                 in_specs=[_row(tm, D), _const((1, D)), _row(tm, D), _row(tm, D)],
                 out_specs=[_row(tm, D), _const((1, D))],
                 out_shape=[_sds((t, D), F32), _sds((1, D), F32)])(h, g.reshape(1, D), du, dres)
    return dh, dg.reshape(D)


def _final_loss(h, g, target, *, name):
    t = h.shape[0]
    tm = min(t, 512)

    def body(h_ref, g_ref, t_ref, dh_ref, dg_ref, l_ref):
        x = h_ref[...]
        gg = g_ref[...]
        r = lax.rsqrt(jnp.mean(x * x, axis=-1, keepdims=True) + EPS)
        err = x * r * gg - t_ref[...]
        dy = err * (1.0 / D)
        dyg = dy * gg
        dot = jnp.mean(dyg * x, axis=-1, keepdims=True)
        dh_ref[...] = r * dyg - x * (r * r * r * dot)

        @pl.when(pl.program_id(0) == 0)
        def _():
            dg_ref[...] = jnp.zeros_like(dg_ref)
            l_ref[...] = jnp.zeros_like(l_ref)

        dg_ref[...] += jnp.sum(dy * x * r, axis=0, keepdims=True)
        l_ref[...] += jnp.sum(err * err, axis=0, keepdims=True)

    dh, dg, l = _pc(body, name=name, grid=(t // tm,),
                    in_specs=[_row(tm, D), _const((1, D)), _row(tm, D)],
                    out_specs=[_row(tm, D), _const((1, D)), _const((1, D))],
                    out_shape=[_sds((t, D), F32), _sds((1, D), F32), _sds((1, D), F32)])(h, g.reshape(1, D), target)
    return l, dh, dg.reshape(D)


NT = (((1,), (1,)), ((), ()))
FFN_COLS = 1408


def _ffn_up_act(u, wg_t, wu_t, *, name):
    t = u.shape[0]
    tm, tn = min(t, 512), FFN_COLS

    def body(u_ref, wg_ref, wu_ref, g_ref, up_ref, a_ref):
        uu = u_ref[...]
        gate = lax.dot_general(uu, wg_ref[...], NT, preferred_element_type=F32)
        up = lax.dot_general(uu, wu_ref[...], NT, preferred_element_type=F32)
        g_ref[...] = gate
        up_ref[...] = up
        a_ref[...] = (gate * _sigmoid(gate) * up).astype(BF16)

    wspec = pl.BlockSpec((tn, D), lambda i, j: (j, 0))
    tile = pl.BlockSpec((tm, tn), lambda i, j: (i, j))
    return _pc(body, name=name, grid=(t // tm, DFF // tn),
               in_specs=[pl.BlockSpec((tm, D), lambda i, j: (i, 0)), wspec, wspec], out_specs=[tile, tile, tile],
               out_shape=[_sds((t, DFF), F32), _sds((t, DFF), F32), _sds((t, DFF), BF16)])(u, wg_t, wu_t)


def _ffn_down_dx_act(dh2, wd, gate, up, *, name, after=None):
    t = dh2.shape[0]
    tm, tn = min(t, 512), FFN_COLS

    def body(d_ref, w_ref, g_ref, up_ref, *rest):
        dg_ref, du_ref = rest[-2:]
        da = lax.dot_general(d_ref[...].astype(BF16), w_ref[...], NT, preferred_element_type=F32)
        gg = g_ref[...]
        sg = _sigmoid(gg)
        dg_ref[...] = (da * up_ref[...] * (sg * (1.0 + gg * (1.0 - sg)))).astype(BF16)
        du_ref[...] = (da * gg * sg).astype(BF16)

    tile = pl.BlockSpec((tm, tn), lambda i, j: (i, j))
    ins = [dh2, wd, gate, up]
    specs = [pl.BlockSpec((tm, D), lambda i, j: (i, 0)), pl.BlockSpec((tn, D), lambda i, j: (j, 0)), tile, tile]
    if after is not None:
        ins.append(after)
        specs.append(pl.BlockSpec((8, 128), lambda i, j: (0, 0)))
    return _pc(body, name=name, grid=(t // tm, DFF // tn), in_specs=specs, out_specs=[tile, tile],
               out_shape=[_sds((t, DFF), BF16), _sds((t, DFF), BF16)])(*ins)


def _dx_rms_bwd(pairs, h, g, dres, *, name):
    t = h.shape[0]
    tm = min(t, 256)
    n = len(pairs)
    dns = [NT if mode == "nt" else (((1,), (0,)), ((), ())) for _, _, mode in pairs]

    def body(*refs):
        h_ref, g_ref, dres_ref, dh_ref, dg_ref = refs[2 * n:]

        @pl.when(pl.program_id(0) == 0)
        def _():
            dg_ref[...] = jnp.zeros_like(dg_ref)

        dy = None
        for k in range(n):
            part = lax.dot_general(refs[k][...].astype(BF16), refs[n + k][...], dns[k], preferred_element_type=F32)
            dy = part if dy is None else dy + part
        x = h_ref[...]
        r = lax.rsqrt(jnp.mean(x * x, axis=-1, keepdims=True) + EPS)
        dyg = dy * g_ref[...]
        dot = jnp.mean(dyg * x, axis=-1, keepdims=True)
        dh_ref[...] = dres_ref[...] + r * dyg - x * (r * r * r * dot)
        dg_ref[...] += jnp.sum(dy * x * r, axis=0, keepdims=True)

    a_specs = [_row(tm, a.shape[1]) for a, _, _ in pairs]
    b_specs = [_const(b.shape) for _, b, _ in pairs]
    dh, dg = _pc(body, name=name, grid=(t // tm,),
                 in_specs=a_specs + b_specs + [_row(tm, D), _const((1, D)), _row(tm, D)],
                 out_specs=[_row(tm, D), _const((1, D))], out_shape=[_sds((t, D), F32), _sds((1, D), F32)])(
                     *[a for a, _, _ in pairs], *[b for _, b, _ in pairs], h, g.reshape(1, D), dres)
    return dh, dg.reshape(D)


ROWS = 8


def _halo_before(tm, halo, tc):
    return pl.BlockSpec((halo, tc), lambda j, i: (jnp.maximum(i * (tm // halo) - 1, 0), j))


def _halo_after(tm, halo, tc, t):
    return pl.BlockSpec((halo, tc), lambda j, i: (jnp.minimum((i + 1) * (tm // halo), t // halo - 1), j))


def _conv4_fwd(xp, w, b, *, name):
    t = xp.shape[0]
    tm, tc, halo = min(t, 512), 1024, 8

    def body(x_ref, h_ref, w_ref, b_ref, o_ref, pad):
        i = pl.program_id(1)
        pad[pl.ds(0, halo), :] = jnp.where(i == 0, 0.0, h_ref[...])
        pad[pl.ds(halo, tm), :] = x_ref[...]
        acc = jnp.zeros((tm, tc), F32) + b_ref[...]
        for k in range(KC):
            acc = acc + pad[pl.ds(halo - (KC - 1) + k, tm), :] * w_ref[pl.ds(k, 1), :]
        o_ref[...] = acc * _sigmoid(acc)

    tile = pl.BlockSpec((tm, tc), lambda j, i: (i, j))
    return _pc(body, name=name, grid=(CD // tc, t // tm),
               in_specs=[tile, _halo_before(tm, halo, tc), pl.BlockSpec((KC, tc), lambda j, i: (0, j)),
                         pl.BlockSpec((1, tc), lambda j, i: (0, j))],
               out_specs=tile, out_shape=_sds((t, CD), F32), scratch=[pltpu.VMEM((halo + tm, tc), F32)])(xp, xp, w, b.reshape(1, CD))


def _conv4_bwd_act(xp, w, b, dxs, dbm, dcm, *, name):
    t = xp.shape[0]
    tm, tc, halo = min(t, 512), 1024, 8
    nx = DI // tc
    assert CD // tc == nx + 1 and dbm.shape[1] + dcm.shape[1] == tc

    def body(x_ref, h_ref, w_ref, b_ref, dx_ref, dbm_ref, dcm_ref, dp_ref, dw_ref, db_ref, pad):
        i = pl.program_id(1)
        pad[pl.ds(0, halo), :] = jnp.where(i == 0, 0.0, h_ref[...])
        pad[pl.ds(halo, tm), :] = x_ref[...]
        acc = jnp.zeros((tm, tc), F32) + b_ref[...]
        for k in range(KC):
            acc = acc + pad[pl.ds(halo - (KC - 1) + k, tm), :] * w_ref[pl.ds(k, 1), :]
        dout = jnp.where(pl.program_id(0) < nx, dx_ref[...], jnp.concatenate([dbm_ref[...], dcm_ref[...]], axis=1))
        dpre = dout * _dsilu(acc)
        dp_ref[...] = dpre

        @pl.when(i == 0)
        def _():
            dw_ref[...] = jnp.zeros_like(dw_ref)
            db_ref[...] = jnp.zeros_like(db_ref)

        db_ref[...] += jnp.sum(dpre, axis=0, keepdims=True)
        for k in range(KC):
            dw_ref[pl.ds(k, 1), :] += jnp.sum(dpre * pad[pl.ds(halo - (KC - 1) + k, tm), :], axis=0, keepdims=True)

    tile = pl.BlockSpec((tm, tc), lambda j, i: (i, j))
    wspec = pl.BlockSpec((KC, tc), lambda j, i: (0, j))
    bspec = pl.BlockSpec((1, tc), lambda j, i: (0, j))
    dp, dw, db = _pc(body, name=name, grid=(CD // tc, t // tm),
                     in_specs=[tile, _halo_before(tm, halo, tc), wspec, bspec,
                               pl.BlockSpec((tm, tc), lambda j, i: (i, jnp.minimum(j, nx - 1))),
                               pl.BlockSpec((tm, tc // 2), lambda j, i: (i, 0)), pl.BlockSpec((tm, tc // 2), lambda j, i: (i, 0))],
                     out_specs=[tile, wspec, bspec],
                     out_shape=[_sds((t, CD), F32), _sds((KC, CD), F32), _sds((1, CD), F32)],
                     scratch=[pltpu.VMEM((halo + tm, tc), F32)])(xp, xp, w, b.reshape(1, CD), dxs, dbm, dcm)
    return dp, dw, db.reshape(CD)


def _conv_bwd_x(dy, w, kk, *, name, out_dtype):
    t, c = dy.shape
    halo = 8 if kk <= 8 else 32
    tm, tc = (min(t, 512), min(c, 1024)) if kk <= 8 else (min(t, 256), min(c, 512))
    rows = 2 * ROWS if kk > 8 else tm

    def body(y_ref, h_ref, w_ref, o_ref, pad):
        i = pl.program_id(1)
        pad[pl.ds(0, tm), :] = y_ref[...]
        pad[pl.ds(tm, halo), :] = jnp.where(i == t // tm - 1, 0.0, h_ref[...])
        for r in range(0, tm, rows):
            acc = pad[pl.ds(kk - 1 + r, rows), :] * w_ref[pl.ds(0, 1), :]
            for k in range(1, kk):
                acc = acc + pad[pl.ds(kk - 1 - k + r, rows), :] * w_ref[pl.ds(k, 1), :]
            o_ref[pl.ds(r, rows), :] = acc.astype(out_dtype)

    tile = pl.BlockSpec((tm, tc), lambda j, i: (i, j))
    return _pc(body, name=name, grid=(c // tc, t // tm),
               in_specs=[tile, _halo_after(tm, halo, tc, t), pl.BlockSpec((kk, tc), lambda j, i: (0, j))],
               out_specs=tile, out_shape=_sds((t, c), out_dtype), scratch=[pltpu.VMEM((tm + halo, tc), F32)])(dy, dy, w)


def _glu_fwd(p, *, name):
    t = p.shape[0]
    tm = min(t, 512)

    def body(p_ref, o_ref):
        o_ref[...] = p_ref[:, :D] * _sigmoid(p_ref[:, D:])

    return _pc(body, name=name, grid=(t // tm,), in_specs=[_row(tm, 2 * D)], out_specs=_row(tm, D),
               out_shape=_sds((t, D), F32))(p)


def _glu_bwd(p, dg, *, name):
    t = p.shape[0]
    tm = min(t, 512)

    def body(p_ref, dg_ref, dp_ref, db_ref):
        a = p_ref[:, :D]
        s = _sigmoid(p_ref[:, D:])
        d = dg_ref[...]
        da = d * s
        dgate = d * a * s * (1.0 - s)
        dp_ref[:, :D] = da.astype(BF16)
        dp_ref[:, D:] = dgate.astype(BF16)

        @pl.when(pl.program_id(0) == 0)
        def _():
            db_ref[...] = jnp.zeros_like(db_ref)

        db_ref[:, :D] += jnp.sum(da, axis=0, keepdims=True)
        db_ref[:, D:] += jnp.sum(dgate, axis=0, keepdims=True)

    dp, db = _pc(body, name=name, grid=(t // tm,), in_specs=[_row(tm, 2 * D), _row(tm, D)],
                 out_specs=[_row(tm, 2 * D), _const((1, 2 * D))],
                 out_shape=[_sds((t, 2 * D), BF16), _sds((1, 2 * D), F32)])(p, dg)
    return dp, db.reshape(2 * D)


def _dwconv_ln_fwd(g, w, b, lg, lb, *, name):
    t = g.shape[0]
    tm, halo = min(t, 256), 32

    def body(x_ref, h_ref, w_ref, b_ref, lg_ref, lb_ref, c_ref, s_ref, pad):
        i = pl.program_id(0)
        pad[pl.ds(0, halo), :] = jnp.where(i == 0, 0.0, h_ref[...])
        pad[pl.ds(halo, tm), :] = x_ref[...]
        acc = jnp.zeros((tm, D), F32) + b_ref[...]
        for k in range(CK):
            acc = acc + pad[pl.ds(halo - (CK - 1) + k, tm), :] * w_ref[pl.ds(k, 1), :]
        c_ref[...] = acc
        mu = jnp.mean(acc, axis=-1, keepdims=True)
        xc = acc - mu
        r = lax.rsqrt(jnp.mean(xc * xc, axis=-1, keepdims=True) + EPS)
        n = xc * r * lg_ref[...] + lb_ref[...]
        s_ref[...] = (n * _sigmoid(n)).astype(BF16)

    c, s = _pc(body, name=name, grid=(t // tm,),
               in_specs=[_row(tm, D), pl.BlockSpec((halo, D), lambda i: (jnp.maximum(i * (tm // halo) - 1, 0), 0)),
                         _const((CK, D)), _const((1, D)), _const((1, D)), _const((1, D))],
               out_specs=[_row(tm, D), _row(tm, D)], out_shape=[_sds((t, D), F32), _sds((t, D), BF16)],
               scratch=[pltpu.VMEM((halo + tm, D), F32)])(g, g, w, b.reshape(1, D), lg.reshape(1, D), lb.reshape(1, D))
    return c, s


def _ln_silu_bwd(c, lg, lb, ds, *, name):
    t = c.shape[0]
    tm = min(t, 512)

    def body(c_ref, lg_ref, lb_ref, ds_ref, dc_ref, dlg_ref, dlb_ref):
        x = c_ref[...]
        mu = jnp.mean(x, axis=-1, keepdims=True)
        xc = x - mu
        r = lax.rsqrt(jnp.mean(xc * xc, axis=-1, keepdims=True) + EPS)
        xh = xc * r
        n = xh * lg_ref[...] + lb_ref[...]
        dn = ds_ref[...] * _dsilu(n)
        dxh = dn * lg_ref[...]
        m1 = jnp.mean(dxh, axis=-1, keepdims=True)
        m2 = jnp.mean(dxh * xh, axis=-1, keepdims=True)
        dc_ref[...] = r * (dxh - m1 - xh * m2)

        @pl.when(pl.program_id(0) == 0)
        def _():
            dlg_ref[...] = jnp.zeros_like(dlg_ref)
            dlb_ref[...] = jnp.zeros_like(dlb_ref)

        dlg_ref[...] += jnp.sum(dn * xh, axis=0, keepdims=True)
        dlb_ref[...] += jnp.sum(dn, axis=0, keepdims=True)

    dc, dlg, dlb = _pc(body, name=name, grid=(t // tm,),
                       in_specs=[_row(tm, D), _const((1, D)), _const((1, D)), _row(tm, D)],
                       out_specs=[_row(tm, D), _const((1, D)), _const((1, D))],
                       out_shape=[_sds((t, D), F32), _sds((1, D), F32), _sds((1, D), F32)])(
                           c, lg.reshape(1, D), lb.reshape(1, D), ds)
    return dc, dlg.reshape(D), dlb.reshape(D)


def _dwconv_bwd_w(g, dc, *, name):
    t = g.shape[0]
    tm, tc, halo = min(t, 256), 512, 32

    def body(x_ref, h_ref, dc_ref, dw_ref, db_ref, pad):
        i = pl.program_id(1)
        pad[pl.ds(0, halo), :] = jnp.where(i == 0, 0.0, h_ref[...])
        pad[pl.ds(halo, tm), :] = x_ref[...]
        @pl.when(i == 0)
        def _():
            dw_ref[...] = jnp.zeros_like(dw_ref)
            db_ref[...] = jnp.zeros_like(db_ref)

        d = dc_ref[...]
        db_ref[...] += jnp.sum(d, axis=0, keepdims=True)
        for k in range(CK):
            dw_ref[pl.ds(k, 1), :] += jnp.sum(d * pad[pl.ds(halo - (CK - 1) + k, tm), :], axis=0, keepdims=True)

    tile = pl.BlockSpec((tm, tc), lambda j, i: (i, j))
    dw, db = _pc(body, name=name, grid=(D // tc, t // tm),
                 in_specs=[tile, _halo_before(tm, halo, tc), tile],
                 out_specs=[pl.BlockSpec((CK, tc), lambda j, i: (0, j)), pl.BlockSpec((1, tc), lambda j, i: (0, j))],
                 out_shape=[_sds((CK, D), F32), _sds((1, D), F32)],
                 scratch=[pltpu.VMEM((halo + tm, tc), F32)])(g, g, dc)
    return dw, db.reshape(D)


def _colsum(x, *, name):
    t, c = x.shape
    tm = min(t, 512)

    def body(x_ref, o_ref):
        @pl.when(pl.program_id(0) == 0)
        def _():
            o_ref[...] = jnp.zeros_like(o_ref)

        o_ref[...] += jnp.sum(x_ref[...], axis=0, keepdims=True)

    return _pc(body, name=name, grid=(t // tm,), in_specs=[_row(tm, c)], out_specs=_const((1, c)),
               out_shape=_sds((1, c), F32))(x).reshape(c)


def _head_expand(n_lanes, first_head=0):
    h = lax.broadcasted_iota(jnp.int32, (NHP, n_lanes), 0)
    j = lax.broadcasted_iota(jnp.int32, (NHP, n_lanes), 1)
    return (h == first_head + j // HD).astype(F32)


def _softplus(v):
    return jnp.maximum(v, 0.0) + jnp.log(1.0 + jnp.exp(-jnp.abs(v)))


def _chunk_tri(n, upper):
    ri = lax.broadcasted_iota(jnp.int32, (n, n), 0)
    ci = lax.broadcasted_iota(jnp.int32, (n, n), 1)
    keep = (ri <= ci) if upper else (ri >= ci)
    return (keep & (ri // Q == ci // Q)).astype(F32)


def _split3(x):
    hi = x.astype(BF16)
    r1 = x - hi.astype(F32)
    mid = r1.astype(BF16)
    return hi, mid, (r1 - mid.astype(F32)).astype(BF16)


def _dot_sel(x, sel, sel_left=False):
    s = sel.astype(BF16)
    parts = [jnp.dot(s, t, preferred_element_type=F32) if sel_left else jnp.dot(t, s, preferred_element_type=F32)
             for t in _split3(x)]
    return (parts[2] + parts[1]) + parts[0]


def _ssd_pre(dtraw, dtb, alog, *, name):
    t = dtraw.shape[0]
    tm = min(t, 256)

    def body(r_ref, b_ref, al_ref, dtx_ref, acsx_ref, acst_ref):
        dt = _softplus(r_ref[...] + b_ref[...])
        da = dt * (-jnp.exp(al_ref[...]))
        acs = _dot_sel(da, _chunk_tri(tm, False), sel_left=True)
        expand = _head_expand(DI)
        dtx_ref[...] = _dot_sel(dt, expand)
        acsx_ref[...] = _dot_sel(acs, expand)
        acst_ref[...] = acs.T

    return _pc(body, name=name, grid=(t // tm,), in_specs=[_row(tm, NHP), _const((1, NHP)), _const((1, NHP))],
               out_specs=[_row(tm, DI), _row(tm, DI), pl.BlockSpec((NHP, tm), lambda i: (0, i))],
               out_shape=[_sds((t, DI), F32), _sds((t, DI), F32), _sds((NHP, t), F32)])(dtraw, dtb, alog)


def _ssd_post(dacs, dxdt, dtraw, dtb, alog, *, name):
    t = dtraw.shape[0]
    tm = min(t, 256)

    def body(g1_ref, g2_ref, r_ref, b_ref, al_ref, o_ref, db_ref, da_ref):
        h = lax.broadcasted_iota(jnp.int32, (DI, NHP), 1)
        j = lax.broadcasted_iota(jnp.int32, (DI, NHP), 0)
        red = (h == j // HD).astype(F32)
        dda = _dot_sel(_dot_sel(g1_ref[...], red), _chunk_tri(tm, True), sel_left=True)
        v = r_ref[...] + b_ref[...]
        ddt = dda * (-jnp.exp(al_ref[...])) + _dot_sel(g2_ref[...], red)
        draw = ddt * _sigmoid(v)
        o_ref[...] = draw.astype(BF16)

        @pl.when(pl.program_id(0) == 0)
        def _():
            db_ref[...] = jnp.zeros_like(db_ref)
            da_ref[...] = jnp.zeros_like(da_ref)

        db_ref[...] += jnp.sum(draw, axis=0, keepdims=True)
        da_ref[...] += jnp.sum(dda * _softplus(v), axis=0, keepdims=True)

    return _pc(body, name=name, grid=(t // tm,),
               in_specs=[_row(tm, DI), _row(tm, DI), _row(tm, NHP), _const((1, NHP)), _const((1, NHP))],
               out_specs=[_row(tm, NHP), _const((1, NHP)), _const((1, NHP))],
               out_shape=[_sds((t, NHP), BF16), _sds((1, NHP), F32), _sds((1, NHP), F32)])(dacs, dxdt, dtraw, dtb, alog)


def _decay_pair(acs_x, acst_ref, p, tri):
    lane = lax.broadcasted_iota(jnp.int32, (Q, 2 * HD), 1)
    v = acs_x[:, 2 * HD * p:2 * HD * (p + 1)]
    swapped = pltpu.roll(v, HD, axis=1)
    out = []
    for hh in range(2):
        colb = jnp.where((lane < HD) if hh == 0 else (lane >= HD), v, swapped)
        rowb = acst_ref[pl.ds(2 * p + hh, 1), :]
        out.append(jnp.exp(jnp.where(tri, colb - rowb, -1e30)))
    return out


def _ssd_fwd(xbc, dtx, acs_x, acs_t, d_x, *, name):
    t = xbc.shape[0]
    nc = t // Q
    hpg = NH // NG

    def body(x_ref, b_ref, c_ref, dtx_ref, acsx_ref, acst_ref, d_ref, y_ref, sin_ref, state):
        @pl.when(pl.program_id(1) == 0)
        def _():
            state[...] = jnp.zeros_like(state)

        xs = x_ref[...]
        tri = lax.broadcasted_iota(jnp.int32, (Q, Q), 0) >= lax.broadcasted_iota(jnp.int32, (Q, Q), 1)
        acs_x = acsx_ref[...]
        atot_x = acsx_ref[pl.ds(Q - 1, 1), :]
        xd = xs * dtx_ref[...]
        xb = xd.astype(BF16)
        bb = b_ref[...]
        cb16 = c_ref[...].astype(BF16)
        cbm = lax.dot_general(cb16, bb.astype(BF16), (((1,), (1,)), ((), ())), preferred_element_type=F32)
        s_in = state[...]
        sin_ref[0] = s_in
        y = jnp.dot(cb16, s_in.astype(BF16), preferred_element_type=F32) * jnp.exp(acs_x) + xs * d_ref[...]
        lane = lax.broadcasted_iota(jnp.int32, (Q, 2 * HD), 1)
        for p in range(hpg // 2):
            xp = xb[:, 2 * HD * p:2 * HD * (p + 1)]
            yp = jnp.zeros((Q, 2 * HD), F32)
            for hh, lm in enumerate(_decay_pair(acs_x, acst_ref, p, tri)):
                m = (cbm * lm).astype(BF16)
                xm = jnp.where((lane >= HD) if hh == 1 else (lane < HD), xp, jnp.zeros_like(xp))
                yp = yp + jnp.dot(m, xm, preferred_element_type=F32)
            y_ref[:, 2 * HD * p:2 * HD * (p + 1)] = y[:, 2 * HD * p:2 * HD * (p + 1)] + yp
        dec = jnp.exp(atot_x - acs_x)
        s_new = jnp.dot(bb.T.astype(BF16), (xd * dec).astype(BF16), preferred_element_type=F32)
        state[...] = jnp.exp(atot_x) * s_in + s_new

    grp = lambda g, c: (c, g)
    return _pc(body, name=name, grid=(NG, nc),
               in_specs=[pl.BlockSpec((Q, GW), grp),
                         pl.BlockSpec((Q, NS), lambda g, c: (c, DI // NS + g)),
                         pl.BlockSpec((Q, NS), lambda g, c: (c, DI // NS + NG + g)),
                         pl.BlockSpec((Q, GW), grp),
                         pl.BlockSpec((Q, GW), grp),
                         pl.BlockSpec((hpg, Q), lambda g, c: (g, c)),
                         pl.BlockSpec((1, GW), lambda g, c: (0, g))],
               out_specs=[pl.BlockSpec((Q, GW), grp), pl.BlockSpec((1, NS, GW), lambda g, c: (c, 0, g))],
               out_shape=[_sds((t, DI), F32), _sds((nc, NS, DI), F32)],
               scratch=[pltpu.VMEM((NS, GW), F32)])(xbc, xbc, xbc, dtx, acs_x, acs_t, d_x)


def _ssd_bwd(xbc, dtx, acs_x, acs_t, d_x, sin, dy, *, name):
    t = xbc.shape[0]
    nc = t // Q
    hpg = NH // NG

    def body(x_ref, b_ref, c_ref, dtx_ref, acsx_ref, acst_ref, d_ref, sin_ref, dy_ref,
             dx_ref, db_ref, dc_ref, dacs_ref, dxdt_ref, ddx_ref, dstate):
        @pl.when(pl.program_id(1) == 0)
        def _():
            dstate[...] = jnp.zeros_like(dstate)
            ddx_ref[...] = jnp.zeros_like(ddx_ref)

        xs = x_ref[...]
        dtxv = dtx_ref[...]
        gy = dy_ref[...]
        tri = lax.broadcasted_iota(jnp.int32, (Q, Q), 0) >= lax.broadcasted_iota(jnp.int32, (Q, Q), 1)
        acs_x = acsx_ref[...]
        atot_x = acsx_ref[pl.ds(Q - 1, 1), :]
        xd = xs * dtxv
        xb = xd.astype(BF16)
        gb = gy.astype(BF16)
        bb = b_ref[...]
        cc = c_ref[...]
        bb16, cc16 = bb.astype(BF16), cc.astype(BF16)
        cbm = lax.dot_general(cc16, bb16, (((1,), (1,)), ((), ())), preferred_element_type=F32)
        s_in = sin_ref[0]
        s_in16 = s_in.astype(BF16)
        ds_next = dstate[...]
        ds16 = ds_next.astype(BF16)
        ecs = jnp.exp(acs_x)
        dec = jnp.exp(atot_x - acs_x)
        etot = jnp.exp(atot_x)

        ddx_ref[...] += jnp.sum(gy * xs, axis=0, keepdims=True)
        z = jnp.dot(cc16, s_in16, preferred_element_type=F32)
        dz16 = (gy * ecs).astype(BF16)
        dacs = gy * z * ecs
        dcm = lax.dot_general(dz16, s_in16, (((1,), (1,)), ((), ())), preferred_element_type=F32)
        ds_here = jnp.dot(cc.T.astype(BF16), dz16, preferred_element_type=F32)
        u = jnp.dot(bb16, ds16, preferred_element_type=F32)
        dxd = u * dec
        w_dec = u * xd * dec
        dacs = dacs - w_dec
        datot = jnp.sum(w_dec, axis=0, keepdims=True) + jnp.sum(etot * ds_next * s_in, axis=0, keepdims=True)
        dbm = lax.dot_general((xd * dec).astype(BF16), ds16, (((1,), (1,)), ((), ())), preferred_element_type=F32)
        dstate[...] = etot * ds_next + ds_here

        lane = lax.broadcasted_iota(jnp.int32, (Q, 2 * HD), 1)
        lane_x = lax.broadcasted_iota(jnp.int32, (1, GW), 1)
        dcb = jnp.zeros((Q, Q), F32)
        dx_pairs = []
        for p in range(hpg // 2):
            sl = slice(2 * HD * p, 2 * HD * (p + 1))
            xp = xb[:, sl]
            gp = gb[:, sl]
            dxp = jnp.zeros((Q, 2 * HD), F32)
            for hh, lm in enumerate(_decay_pair(acs_x, acst_ref, p, tri)):
                r = 2 * p + hh
                mm = cbm * lm
                half = (lane >= HD) if hh == 1 else (lane < HD)
                xm = jnp.where(half, xp, jnp.zeros_like(xp))
                gm = jnp.where(half, gp, jnp.zeros_like(gp))
                mt16 = mm.T.astype(BF16)
                dxp = dxp + jnp.dot(mt16, gm, preferred_element_type=F32)
                dm = lax.dot_general(gm, xp, (((1,), (1,)), ((), ())), preferred_element_type=F32)
                dmt = lax.dot_general(xm, gp, (((1,), (1,)), ((), ())), preferred_element_type=F32)
                dcb = dcb + dm * lm
                wdiff = jnp.sum(dm * mm - dmt * mm.T, axis=1, keepdims=True)
                dacs = dacs + wdiff * (lane_x == HD * r).astype(F32)
            dx_pairs.append(dxp)
        dxd = dxd + jnp.concatenate(dx_pairs, axis=1)
        dcm = dcm + jnp.dot(dcb.astype(BF16), bb16, preferred_element_type=F32)
        dbm = dbm + jnp.dot(dcb.T.astype(BF16), cc16, preferred_element_type=F32)
        db_ref[...] = dbm
        dc_ref[...] = dcm

        last = lax.broadcasted_iota(jnp.int32, (Q, GW), 0) == Q - 1
        dacs_ref[...] = dacs + jnp.where(last, datot, 0.0)
        dxdt_ref[...] = dxd * xs
        dx_ref[...] = dxd * dtxv + gy * d_ref[...]

    rev = lambda g, c: (nc - 1 - c, g)
    return _pc(body, name=name, grid=(NG, nc),
               in_specs=[pl.BlockSpec((Q, GW), rev),
                         pl.BlockSpec((Q, NS), lambda g, c: (nc - 1 - c, DI // NS + g)),
                         pl.BlockSpec((Q, NS), lambda g, c: (nc - 1 - c, DI // NS + NG + g)),
                         pl.BlockSpec((Q, GW), rev),
                         pl.BlockSpec((Q, GW), rev),
                         pl.BlockSpec((hpg, Q), lambda g, c: (g, nc - 1 - c)),
                         pl.BlockSpec((1, GW), lambda g, c: (0, g)),
                         pl.BlockSpec((1, NS, GW), lambda g, c: (nc - 1 - c, 0, g)),
                         pl.BlockSpec((Q, GW), rev)],
               out_specs=[pl.BlockSpec((Q, GW), rev), pl.BlockSpec((Q, NS), rev), pl.BlockSpec((Q, NS), rev),
                          pl.BlockSpec((Q, GW), rev), pl.BlockSpec((Q, GW), rev),
                          pl.BlockSpec((1, GW), lambda g, c: (0, g))],
               out_shape=[_sds((t, DI), F32), _sds((t, NG * NS), F32), _sds((t, NG * NS), F32), _sds((t, DI), F32),
                          _sds((t, DI), F32), _sds((1, DI), F32)],
               scratch=[pltpu.VMEM((NS, GW), F32)])(xbc, xbc, xbc, dtx, acs_x, acs_t, d_x, sin, dy)


def _gated_norm_fwd(y, z, gn, *, name):
    t = y.shape[0]
    tm = min(t, 256)

    def body(y_ref, z_ref, g_ref, o_ref):
        for k in range(NG):
            sl = slice(GW * k, GW * (k + 1))
            zz = z_ref[:, sl]
            h = y_ref[:, sl] * (zz * _sigmoid(zz))
            r = lax.rsqrt(jnp.mean(h * h, axis=-1, keepdims=True) + EPS)
            o_ref[:, sl] = (h * r * g_ref[:, sl]).astype(BF16)

    return _pc(body, name=name, grid=(t // tm,), in_specs=[_row(tm, DI), _row(tm, DI), _const((1, DI))],
               out_specs=_row(tm, DI), out_shape=_sds((t, DI), BF16))(y, z, gn.reshape(1, DI))


def _gated_norm_bwd(y, z, gn, dout, *, name):
    t = y.shape[0]
    tm = min(t, 256)

    def body(y_ref, z_ref, g_ref, do_ref, dy_ref, dz_ref, dg_ref):
        @pl.when(pl.program_id(0) == 0)
        def _():
            dg_ref[...] = jnp.zeros_like(dg_ref)

        for k in range(NG):
            sl = slice(GW * k, GW * (k + 1))
            zz = z_ref[:, sl]
            yy = y_ref[:, sl]
            sz = zz * _sigmoid(zz)
            h = yy * sz
            r = lax.rsqrt(jnp.mean(h * h, axis=-1, keepdims=True) + EPS)
            d = do_ref[:, sl]
            dg_ref[:, sl] += jnp.sum(d * h * r, axis=0, keepdims=True)
            dgd = d * g_ref[:, sl]
            dot = jnp.mean(dgd * h, axis=-1, keepdims=True)
            dh = r * dgd - h * (r * r * r * dot)
            dy_ref[:, sl] = dh * sz
            dz_ref[:, sl] = (dh * yy * _dsilu(zz)).astype(BF16)

    dy, dz, dg = _pc(body, name=name, grid=(t // tm,),
                     in_specs=[_row(tm, DI), _row(tm, DI), _const((1, DI)), _row(tm, DI)],
                     out_specs=[_row(tm, DI), _row(tm, DI), _const((1, DI))],
                     out_shape=[_sds((t, DI), F32), _sds((t, DI), BF16), _sds((1, DI), F32)])(y, z, gn.reshape(1, DI), dout)
    return dy, dz, dg.reshape(DI)


def _pad_heads(v):
    return jnp.pad(v.reshape(1, NH), ((0, 0), (0, NHP - NH)))


def _residual_out(a, w, h, next_norm, name, bias=None):
    if next_norm is None:
        return _mm(a, w, "nn", bias=bias, res=h, name=name), None
    return _mm(a, w, "nn", bias=bias, res=h, norm_g=next_norm, name=name)


def _ssm_fwd(h, u, p, next_norm):
    z = _mm(u, p["wz"], "nn", name="ssm_in_z")
    xp = _mm(u, p["wxbc"], "nn", name="ssm_in_xbc")
    dtraw = _mm(u, p["wdt"], "nn", name="ssm_in_dt")
    xbc = _conv4_fwd(xp, p["conv_w"], p["conv_b"], name="ssm_conv_fwd")
    dtb, alog = _pad_heads(p["dt_bias"]), _pad_heads(p["a_log"])
    d_x = jnp.repeat(p["d"], HD).reshape(1, DI)
    dtx, acs_x, acs_t = _ssd_pre(dtraw, dtb, alog, name="ssd_pre")
    y, sin = _ssd_fwd(xbc, dtx, acs_x, acs_t, d_x, name="ssd_fwd")
    yn = _gated_norm_fwd(y, z, p["gate_norm"], name="ssm_gate_fwd")
    h2, u2 = _residual_out(yn, p["wout"], h, next_norm, "ssm_out")
    return h2, u2, (h, u, z, xp, dtraw, xbc, dtx, acs_x, acs_t, d_x, dtb, alog, y, sin, yn)


def _ssm_bwd(dh2, p, saved, after=None):
    h, u, z, xp, dtraw, xbc, dtx, acs_x, acs_t, d_x, dtb, alog, y, sin, yn = saved
    dyn = _mm(dh2, p["wout"], "nt", name="ssm_out_dx", after=after)
    g = {"w_out": _mm(yn, dh2, "tn", out_dtype=BF16, name="ssm_out_dw")}
    dy, dz, g["gate_norm"] = _gated_norm_bwd(y, z, p["gate_norm"], dyn, name="ssm_gate_bwd")
    dxs, dbm, dcm, dacs, dxdt, ddx = _ssd_bwd(xbc, dtx, acs_x, acs_t, d_x, sin, dy, name="ssd_bwd")
    ddtraw, ddtb, dav = _ssd_post(dacs, dxdt, dtraw, dtb, alog, name="ssd_post")
    dpre, g["conv_w"], g["conv_b"] = _conv4_bwd_act(xp, p["conv_w"], p["conv_b"], dxs, dbm, dcm, name="ssm_conv_bwd_act")
    dxp = _conv_bwd_x(dpre, p["conv_w"], KC, name="ssm_conv_bwd_x", out_dtype=BF16)
    dh, g["norm"] = _dx_rms_bwd([(dz, p["wz"], "nt"), (dxp, p["wxbc"], "nt"), (ddtraw, p["wdt"], "nt")], h, p["norm"], dh2,
                                name="ssm_in_dx_rms_bwd")
    g["w_in"] = jnp.concatenate([_mm(u, dz, "tn", out_dtype=BF16, name="ssm_in_dw_z"),
                                 _mm(u, dxp, "tn", out_dtype=BF16, name="ssm_in_dw_xbc"),
                                 _mm(u, ddtraw, "tn", out_dtype=BF16, name="ssm_in_dw_dt")[:, :NH]], axis=1)
    g["dt_bias"] = ddtb[0, :NH]
    g["a_log"] = dav[0, :NH] * (-jnp.exp(p["a_log"]))
    g["d"] = ddx.reshape(NH, HD).sum(-1)
    return dh, g


def _cv_fwd(h, u, p, next_norm):
    pre = _mm(u, p["wpw1"], "nn", bias=p["b_pw1"].reshape(1, 2 * D), name="cv_pw1")
    gl = _glu_fwd(pre, name="cv_glu_fwd")
    c, s = _dwconv_ln_fwd(gl, p["dw_w"], p["dw_b"], p["ln_g"], p["ln_b"], name="cv_dwconv_ln_fwd")
    h2, u2 = _residual_out(s, p["wpw2"], h, next_norm, "cv_pw2", bias=p["b_pw2"].reshape(1, D))
    return h2, u2, (h, u, pre, gl, c, s)


def _cv_bwd(dh2, p, saved, after=None):
    h, u, pre, gl, c, s = saved
    ds = _mm(dh2, p["wpw2"], "nt", name="cv_pw2_dx", after=after)
    g = {"w_pw2": _mm(s, dh2, "tn", out_dtype=BF16, name="cv_pw2_dw"), "b_pw2": _colsum(dh2, name="cv_pw2_db")}
    dc, g["ln_g"], g["ln_b"] = _ln_silu_bwd(c, p["ln_g"], p["ln_b"], ds, name="cv_ln_bwd")
    dgl = _conv_bwd_x(dc, p["dw_w"], CK, name="cv_dwconv_bwd_x", out_dtype=F32)
    g["dw_w"], g["dw_b"] = _dwconv_bwd_w(gl, dc, name="cv_dwconv_bwd_w")
    dpre, g["b_pw1"] = _glu_bwd(pre, dgl, name="cv_glu_bwd")
    g["w_pw1"] = _mm(u, dpre, "tn", out_dtype=BF16, name="cv_pw1_dw")
    dh, g["norm"] = _dx_rms_bwd([(dpre, p["wpw1"], "nt")], h, p["norm"], dh2, name="cv_pw1_dx_rms_bwd")
    return dh, g


def _ffn_fwd(h, u, p, next_norm):
    gate, up, a = _ffn_up_act(u, p["wg_t"], p["wu_t"], name="ffn_gate_up_act")
    h2, u2 = _residual_out(a, p["wd"], h, next_norm, "ffn_down")
    return h2, u2, (h, u, gate, up, a)


def _ffn_bwd(dh2, p, saved, after=None):
    h, u, gate, up, a = saved
    dgate, dup = _ffn_down_dx_act(dh2, p["wd"], gate, up, name="ffn_down_dx_act", after=after)
    g = {"w_down": _mm(a, dh2, "tn", out_dtype=BF16, name="ffn_down_dw"),
         "w_gate": _mm(dgate, u, "tn", out_dtype=BF16, name="ffn_gate_dw"),
         "w_up": _mm(dup, u, "tn", out_dtype=BF16, name="ffn_up_dw")}
    dh, g["norm"] = _dx_rms_bwd([(dgate, p["wg_t"], "nn"), (dup, p["wu_t"], "nn")], h, p["norm"], dh2, name="ffn_dx_rms_bwd")
    return dh, g


def _fwd_bwd(x, target, depth, mixer_weights, ffn_weights, final_norm, layer_done=None, ffn_done=None):
    h, u, tape, weights = x, None, [], []
    for i in range(depth):
        mix_w = mixer_weights(i, h)
        if u is None:
            u = _rms_fwd(h, mix_w["norm"], name="first_rms_fwd")
        h, u, s_mix = (_ssm_fwd if i % 2 == 0 else _cv_fwd)(h, u, mix_w, mix_w["ffn_norm"])
        ffn_w = ffn_weights(i, h)
        h, u, s_ffn = _ffn_fwd(h, u, ffn_w, ffn_w["next_norm"])
        tape.append((s_mix, s_ffn))
        weights.append((mix_w, ffn_w))
    lsum, dh, g_final = _final_loss(h, final_norm, target, name="loss_head")
    g_mix, g_ffn, token = [None] * depth, [None] * depth, None
    for i in reversed(range(depth)):
        s_mix, s_ffn = tape[i]
        dh, g_ffn[i] = _ffn_bwd(dh, weights[i][1], s_ffn, after=token)
        token = ffn_done(i, g_ffn[i]) if ffn_done is not None else None
        dh, g_mix[i] = (_ssm_bwd if i % 2 == 0 else _cv_bwd)(dh, weights[i][0], s_mix, after=token)
        if layer_done is not None:
            token = layer_done(i, g_mix[i], g_ffn[i], g_final)
    return lsum, dh, g_mix, g_ffn, g_final


ANY = pl.BlockSpec(memory_space=pl.ANY)


def _place():
    x, y, c = lax.axis_index("x"), lax.axis_index("y"), lax.axis_index("c")
    return x, y, c, [(1 - x, y), (x, 1 - y), (1 - x, 1 - y)]


def _remote(src, dst, ssem, rsem, dev):
    return pltpu.make_async_remote_copy(src_ref=src, dst_ref=dst, send_sem=ssem, recv_sem=rsem, device_id=dev,
                                        device_id_type=MESH)


def _comm_call(body, name, ins, out_shape, n_sems):
    return pl.pallas_call(
        body, name=name, in_specs=[ANY] * len(ins), out_specs=[ANY] * len(out_shape), out_shape=out_shape,
        scratch_shapes=[pltpu.SemaphoreType.DMA((k,)) for k in n_sems])(*ins)


HBM = pl.BlockSpec(memory_space=pltpu.HBM)
SEM = pl.BlockSpec(memory_space=pltpu.SEMAPHORE)
EFFECT = pltpu.SideEffectType.DATAFLOW_SIDE_EFFECTING


def _gather_copies(src, land, ssem, rsem):
    x, y, c, chips = _place()
    me, pairs = 2 * x + y, []
    for a in range(len(src)):
        h = src[a].shape[0] // 2
        rows = pl.ds(c * h, h)
        for j, (px, py) in enumerate(chips):
            sems = (ssem.at[3 * a + j], rsem.at[3 * a + j], (px, py, c))
            pairs.append((_remote(src[a].at[rows], land[a].at[me, rows], *sems),
                          _remote(src[a].at[rows], land[a].at[2 * px + py, rows], *sems)))
    return pairs


def _chip_copies(src, land, ssem, rsem):
    x, y, c, chips = _place()
    me, pairs = 2 * x + y, []
    for a in range(len(src)):
        for j, (px, py) in enumerate(chips):
            sems = (ssem.at[3 * a + j], rsem.at[3 * a + j], (px, py, c))
            pairs.append((_remote(src[a].at[2 * px + py], land[a].at[me], *sems),
                          _remote(src[a].at[2 * px + py], land[a].at[2 * px + py], *sems)))
    return pairs


def _split_start(copies, srcs, land_shapes, dep, name):
    n = len(srcs)

    def body(*refs):
        for send, _ in copies(refs[:n], refs[n:2 * n], refs[2 * n + 1], refs[2 * n + 2]):
            send.start()
        refs[-1][...] = jnp.zeros_like(refs[-1])

    lands = [pltpu.with_memory_space_constraint(lax.empty(s, w.dtype), pltpu.HBM) for s, w in zip(land_shapes, srcs)]
    out = pl.pallas_call(
        body, name=name,
        out_shape=(pltpu.SemaphoreType.DMA((3 * n,)), pltpu.SemaphoreType.DMA((3 * n,)),
                   *[pltpu.HBM(w.shape, w.dtype) for w in srcs], *[pltpu.HBM(w.shape, w.dtype) for w in lands],
                   _sds((8, 128), F32)),
        in_specs=[HBM] * (2 * n) + [ANY], out_specs=(SEM, SEM, *[HBM] * (2 * n), pl.BlockSpec(memory_space=pltpu.VMEM)),
        input_output_aliases={a: 2 + a for a in range(2 * n)},
        compiler_params=pltpu.CompilerParams(has_side_effects=EFFECT))(
            *[pltpu.with_memory_space_constraint(w, pltpu.HBM) for w in srcs], *lands, dep)
    return out[0], out[1], list(out[2:2 + n]), list(out[2 + n:2 + 2 * n]), out[-1]


def _split_wait(copies, ssem, rsem, srcs, lands, after, name):
    n = len(srcs)

    def body(*refs):
        for send, recv in copies(refs[:n], refs[n:2 * n], refs[2 * n], refs[2 * n + 1]):
            send.wait_send()
            recv.wait_recv()

    out = pl.pallas_call(
        body, name=name,
        out_shape=(*[pltpu.HBM(w.shape, w.dtype) for w in srcs], *[pltpu.HBM(w.shape, w.dtype) for w in lands]),
        in_specs=[HBM] * (2 * n) + [SEM, SEM, ANY], out_specs=tuple([HBM] * (2 * n)),
        input_output_aliases={a: a for a in range(2 * n)},
        compiler_params=pltpu.CompilerParams(has_side_effects=EFFECT))(*srcs, *lands, ssem, rsem, after)
    return list(out[:n]), list(out[n:])


def _gather_start(ws, dep, tag):
    return _split_start(_gather_copies, ws, [(4,) + w.shape for w in ws], dep, "gather_start_" + tag)


def _gather_wait(ssem, rsem, srcs, lands, after, tag):
    return _split_wait(_gather_copies, ssem, rsem, srcs, lands, after, "gather_wait_" + tag)


def _gather_forward(lands, tag):
    n = len(lands)

    def body(*refs):
        land = refs[n:2 * n]
        ssem, rsem = refs[2 * n:]
        x, y, c, chips = _place()
        sib = (x, y, 1 - c)
        cps = []
        for a in range(n):
            h = lands[a].shape[1] // 2
            for j, (px, py) in enumerate(chips):
                blk = land[a].at[2 * px + py, pl.ds(c * h, h)]
                cps.append(_remote(blk, blk, ssem.at[3 * a + j], rsem.at[3 * a + j], sib))
        for cp in cps:
            cp.start()
        for a in range(n):
            h = lands[a].shape[1] // 2
            for j, (px, py) in enumerate(chips):
                blk = land[a].at[2 * px + py, pl.ds((1 - c) * h, h)]
                _remote(blk, blk, ssem.at[3 * a + j], rsem.at[3 * a + j], sib).wait_recv()
        for cp in cps:
            cp.wait_send()

    return pl.pallas_call(
        body, name="gather_forward_" + tag, in_specs=[ANY] * n, out_specs=[ANY] * n,
        out_shape=[_sds(w.shape, w.dtype) for w in lands], input_output_aliases={a: a for a in range(n)},
        scratch_shapes=[pltpu.SemaphoreType.DMA((3 * n,)), pltpu.SemaphoreType.DMA((3 * n,))])(*lands)


def _exchange_pairs(gs, tag, dep=None):
    n = len(gs)
    extra = [] if dep is None else [dep]

    def body(*refs):
        ins, got = refs[:n], refs[n + len(extra):2 * n + len(extra)]
        ssem, rsem = refs[2 * n + len(extra):]
        x, y, c, _ = _place()
        cps = []
        for a in range(n):
            lh = gs[a].shape[1] // 2
            cps.append(_remote(ins[a].at[:, pl.ds((1 - c) * lh, lh)], got[a], ssem.at[a], rsem.at[a], (x, y, 1 - c)))
        for cp in cps:
            cp.start()
        for cp in cps:
            cp.wait()

    halves = [_sds((4, g.shape[1] // 2) + g.shape[2:], g.dtype) for g in gs]
    return _comm_call(body, "reduce_pair_exchange_" + tag, list(gs) + extra, halves, [n, n])


def _exchange_final(fs, tag):
    n = len(fs)

    def body(*refs):
        ins, outs = refs[:n], refs[n:2 * n]
        ssem, rsem = refs[2 * n:]
        x, y, c, _ = _place()
        cps = [_remote(ins[a], outs[a], ssem.at[a], rsem.at[a], (x, y, 1 - c)) for a in range(n)]
        for cp in cps:
            cp.start()
        for cp in cps:
            cp.wait()

    return _comm_call(body, "reduce_final_exchange_" + tag, fs, [_sds(f.shape, f.dtype) for f in fs], [n, n])


def _rows_tile(r, c, n_arrays):
    cap = max(8, (16 * 1024 * 1024) // (2 * n_arrays * c * 4))
    fits = [t for t in range(8, min(r, cap) + 1, 8) if r % t == 0]
    return max(fits) if fits else r


REDUCE_ROW_BLOCKS = 2


def _add_pairs(gs, gots, core, *, name):
    n = len(gs)

    def body(core_ref, *refs):
        for k in range(n):
            refs[2 * n + k][...] = (refs[k][...].astype(F32) + refs[n + k][...].astype(F32)).astype(BF16)

    halves, wholes, shapes = [], [], []
    for g in gs:
        _, r, c = g.shape
        tm = r // (2 * REDUCE_ROW_BLOCKS)
        wholes.append(pl.BlockSpec((None, None, tm, c), lambda s, i, core_ref: (s, core_ref[0], i, 0)))
        halves.append(pl.BlockSpec((None, tm, c), lambda s, i, core_ref: (s, i, 0)))
        shapes.append(_sds((4, r // 2, c), BF16))
    grid_spec = pltpu.PrefetchScalarGridSpec(num_scalar_prefetch=1, grid=(4, REDUCE_ROW_BLOCKS),
                                             in_specs=wholes + halves, out_specs=halves)
    return pl.pallas_call(body, name=name, grid_spec=grid_spec, out_shape=shapes,
                          compiler_params=pltpu.CompilerParams(dimension_semantics=("arbitrary", "arbitrary"),
                                                               vmem_limit_bytes=VMEM_LIMIT))(
                                                                   core, *[g.reshape(4, 2, g.shape[1] // 2, g.shape[2]) for g in gs], *gots)


def _sum_slots(rbs, *, name):
    n = len(rbs)

    def body(*refs):
        for k in range(n):
            acc = refs[k][0].astype(F32)
            for s in range(1, 4):
                acc = acc + refs[k][s].astype(F32)
            refs[n + k][...] = acc

    ins, outs, shapes = [], [], []
    for rb in rbs:
        _, rows, c = rb.shape
        tm = rows // REDUCE_ROW_BLOCKS
        ins.append(pl.BlockSpec((4, tm, c), lambda i: (0, i, 0)))
        outs.append(_row(tm, c))
        shapes.append(_sds((rows, c), F32))
    return _pc(body, name=name, grid=(REDUCE_ROW_BLOCKS,), in_specs=ins, out_specs=outs, out_shape=shapes)(*rbs)


def _reduce_start(gs, core, tag, dep=None):
    got = _exchange_pairs(gs, tag, dep)
    core1 = core.reshape(1).astype(jnp.int32)
    ps = _add_pairs(gs, got, core1, name="reduce_pair_add_" + tag)
    ssem, rsem, ps, rbs, token = _split_start(_chip_copies, ps, [p.shape for p in ps], got[0], "reduce_start_" + tag)
    return (ssem, rsem, ps, rbs), token


def _sibling_copies(src, land, ssem, rsem):
    x, y, c, _ = _place()
    cps = [_remote(src[a], land[a], ssem.at[a], rsem.at[a], (x, y, 1 - c)) for a in range(len(src))]
    return [(cp, cp) for cp in cps]


def _reduce_finish(state, after, chip, tag):
    ps, rbs = _split_wait(_chip_copies, *state, after, "reduce_wait_" + tag)
    rbs = [lax.dynamic_update_index_in_dim(b, lax.dynamic_index_in_dim(p, chip, 0, keepdims=False), chip, 0)
           for b, p in zip(rbs, ps)]
    fs = _sum_slots(rbs, name="reduce_chip_sum_" + tag)
    return _split_start(_sibling_copies, fs, [f.shape for f in fs], rbs[0], "reduce_final_start_" + tag)[:4]


def _reduce_collect(state, after, core, tag):
    fs, theirs = _split_wait(_sibling_copies, *state, after, "reduce_final_wait_" + tag)
    return [jnp.where(core == 0, jnp.concatenate([f, t], axis=0), jnp.concatenate([t, f], axis=0))
            for f, t in zip(fs, theirs)]


def _gather8(v, reduce):
    m = v.shape[0]

    def body(x_ref, *rest):
        if reduce:
            sum_ref, out_ref, send_sems, recv_sems, local_sem = rest
        else:
            out_ref, send_sems, recv_sems, local_sem = rest
        x, y, c, chips = _place()
        me, sibling = (x, y, c), (x, y, 1 - c)

        def rows(px, py, pc):
            return out_ref.at[pl.ds((4 * px + 2 * py + pc) * m, m), :]

        def copy(k, block, to, src=None):
            return _remote(rows(*block) if src is None else src, rows(*block), send_sems.at[k], recv_sems.at[k], to)

        mine = pltpu.make_async_copy(x_ref, rows(*me), local_sem)
        mine.start()
        first = [copy(0, me, sibling, src=x_ref)]
        first += [copy(1 + j, me, (*chip, c), src=x_ref) for j, chip in enumerate(chips)]
        for cp in first:
            cp.start()
        passed = [copy(4 + j, (*chip, c), sibling) for j, chip in enumerate(chips)]
        for j, chip in enumerate(chips):
            copy(1 + j, (*chip, c), me).wait_recv()
            passed[j].start()
        copy(0, sibling, me).wait_recv()
        for j, chip in enumerate(chips):
            copy(4 + j, (*chip, 1 - c), me).wait_recv()
        for cp in first + passed:
            cp.wait_send()
        mine.wait()
        if reduce:
            acc = out_ref[pl.ds(0, m), :]
            for k in range(1, 8):
                acc = acc + out_ref[pl.ds(k * m, m), :]
            sum_ref[...] = acc

    vm = pl.BlockSpec(memory_space=pltpu.VMEM)
    out_shape = [_sds((8 * m, 128), F32)]
    if reduce:
        out_shape = [_sds((m, 128), F32)] + out_shape
    out = pl.pallas_call(
        body, name="allreduce_small" if reduce else "gather_small", in_specs=[vm], out_specs=[vm] * len(out_shape),
        out_shape=out_shape,
        scratch_shapes=[pltpu.SemaphoreType.DMA((7,)), pltpu.SemaphoreType.DMA((7,)), pltpu.SemaphoreType.DMA(())],
        compiler_params=pltpu.CompilerParams(vmem_limit_bytes=VMEM_LIMIT))(v)
    return out[0]


def _adamw(w, g, m, v, *, name):
    if w.ndim == 2:
        return [o[0] for o in _adamw(w[None], g[None], m[None], v[None], name=name)]
    nl, r, c = w.shape
    tm = _rows_tile(r, c, 7)
    blk = pl.BlockSpec((None, tm, c), lambda l, i: (l, i, 0))

    def body(w_ref, g_ref, m_ref, v_ref, d_ref, mo_ref, vo_ref):
        gg = g_ref[...]
        mm = B1 * m_ref[...] + (1.0 - B1) * gg
        vv = B2 * v_ref[...] + (1.0 - B2) * (gg * gg)
        mo_ref[...] = mm
        vo_ref[...] = vv
        m_hat = mm / (1.0 - B1 ** STEP)
        v_hat = vv / (1.0 - B2 ** STEP)
        d_ref[...] = -LR * (m_hat / (jnp.sqrt(v_hat) + AEPS) + WD * w_ref[...])

    return _pc(body, name=name, grid=(nl, r // tm), in_specs=[blk] * 4, out_specs=[blk] * 3,
               out_shape=[_sds((nl, r, c), F32)] * 3)(w, g, m, v)


WEIGHTS = ["ssm_norm", "ssm_w_in", "ssm_conv_w", "ssm_conv_b", "ssm_dt_bias", "ssm_a_log", "ssm_d", "ssm_gate_norm",
           "ssm_w_out", "cv_norm", "cv_w_pw1", "cv_b_pw1", "cv_dw_w", "cv_dw_b", "cv_ln_g", "cv_ln_b", "cv_w_pw2",
           "cv_b_pw2", "ffn_norm", "ffn_w_gate", "ffn_w_up", "ffn_w_down", "final_norm"]
TRANSPOSED = ("ffn_w_gate", "ffn_w_up")
ADAMW_TRANSPOSED = ("ssm_w_in",) + TRANSPOSED
BIG = {"ssm_w_in": "col", "ssm_w_out": "row", "cv_w_pw1": "col", "cv_w_pw2": "row",
       "ffn_w_gate": "row", "ffn_w_up": "row", "ffn_w_down": "row"}
SMALL_SHARDED = {"ssm_conv_w": 2, "cv_norm": 1, "cv_b_pw1": 1, "cv_dw_w": 2, "cv_dw_b": 1, "cv_ln_g": 1, "cv_ln_b": 1,
                 "cv_b_pw2": 1}
SMALL = [n for n in WEIGHTS if n not in BIG]
LANES = 128


def _pack(arrays):
    flat = jnp.concatenate([a.reshape(-1) for a in arrays])
    rows = -(-flat.size // (8 * LANES)) * 8
    return jnp.pad(flat, (0, rows * LANES - flat.size)).reshape(rows, LANES)


def _unpack(packed, shapes):
    flat, out, off = packed.reshape(-1), [], 0
    for s in shapes:
        n = 1
        for d_ in s:
            n *= d_
        out.append(flat[off:off + n].reshape(s))
        off += n
    return out


def _unshard(parts, axis):
    return jnp.concatenate(parts, axis=axis)


def kernel(x, ssm_norm, ssm_w_in, ssm_conv_w, ssm_conv_b, ssm_dt_bias, ssm_a_log, ssm_d, ssm_gate_norm, ssm_w_out, cv_norm, cv_w_pw1, cv_b_pw1, cv_dw_w, cv_dw_b, cv_ln_g, cv_ln_b, cv_w_pw2, cv_b_pw2, ffn_norm, ffn_w_gate, ffn_w_up, ffn_w_down, final_norm, loss_target, m_ssm_norm, m_ssm_w_in, m_ssm_conv_w, m_ssm_conv_b, m_ssm_dt_bias, m_ssm_a_log, m_ssm_d, m_ssm_gate_norm, m_ssm_w_out, m_cv_norm, m_cv_w_pw1, m_cv_b_pw1, m_cv_dw_w, m_cv_dw_b, m_cv_ln_g, m_cv_ln_b, m_cv_w_pw2, m_cv_b_pw2, m_ffn_norm, m_ffn_w_gate, m_ffn_w_up, m_ffn_w_down, m_final_norm, v_ssm_norm, v_ssm_w_in, v_ssm_conv_w, v_ssm_conv_b, v_ssm_dt_bias, v_ssm_a_log, v_ssm_d, v_ssm_gate_norm, v_ssm_w_out, v_cv_norm, v_cv_w_pw1, v_cv_b_pw1, v_cv_dw_w, v_cv_dw_b, v_cv_ln_g, v_cv_ln_b, v_cv_w_pw2, v_cv_b_pw2, v_ffn_norm, v_ffn_w_gate, v_ffn_w_up, v_ffn_w_down, v_final_norm):
    a = dict(locals())
    chip = 2 * lax.axis_index("x") + lax.axis_index("y")
    n_ssm, n_cv, depth = ssm_norm.shape[0], cv_norm.shape[0], ffn_norm.shape[0]

    def own(n, layer):
        w = a[n][layer].astype(BF16)
        return w.T if n in TRANSPOSED else w

    sh_names = list(SMALL_SHARDED)
    got = _gather8(_pack([a[n] for n in sh_names]), reduce=False)
    got = got.reshape(8, -1)[0::2]
    per_chip = [_unpack(got[s], [a[n].shape for n in sh_names]) for s in range(4)]
    full = {n: a[n] for n in SMALL}
    for k, n in enumerate(sh_names):
        full[n] = _unshard([per_chip[s][k] for s in range(4)], SMALL_SHARDED[n])

    def matrices_of(i, part="layer"):
        mixer = [(n, i // 2) for n in (("ssm_w_in", "ssm_w_out") if i % 2 == 0 else ("cv_w_pw1", "cv_w_pw2"))]
        ffn_part = [(n, i) for n in ("ffn_w_gate", "ffn_w_up", "ffn_w_down")]
        return {"mixer": mixer, "ffn": ffn_part, "layer": mixer + ffn_part}[part]

    stages = [matrices_of(0, "mixer"), matrices_of(0, "ffn")] + [matrices_of(i) for i in range(1, depth)]
    in_flight, landed = {}, {}

    def matrix(n, layer, h):
        if not in_flight and not landed:
            dep = h
            for k, stage in enumerate(stages):
                in_flight[k] = _gather_start([own(m, l) for m, l in stage], dep, str(k))
                dep = in_flight[k][-1]
            in_flight["token"] = dep[0, 0]
        if (n, layer) not in landed:
            k = next(k for k, st in enumerate(stages) if (n, layer) in st)
            ssem, rsem, srcs, lands, _ = in_flight.pop(k)
            srcs, lands = _gather_wait(ssem, rsem, srcs, lands, h, str(k))
            lands = _gather_forward(lands, str(k))
            for key, src, land in zip(stages[k], srcs, lands):
                landed[key] = (src, land, in_flight["token"])
        return landed[(n, layer)]

    def whole(h, *keys):
        got = [matrix(n, l, h) for n, l in keys]
        if BIG[keys[0][0]] == "row" and len(keys) == 1:
            src, land, token = got[0]
            return lax.dynamic_update_index_in_dim(land, src, chip, 0).reshape(-1, land.shape[-1]), token
        parts = [jnp.where(chip == s, src, land[s]) for src, land, _ in got for s in range(4)]
        return jnp.concatenate(parts, axis=1 if BIG[keys[0][0]] == "col" else 0), got[0][2]

    def mixer_weights(i, h):
        j = i // 2
        if i % 2 == 0:
            (w_in, token), (w_out, _) = whole(h, ("ssm_w_in", j)), whole(h, ("ssm_w_out", j))
            return dict(norm=full["ssm_norm"][j] + token, ffn_norm=full["ffn_norm"][i], wz=w_in[:, :DI], wxbc=w_in[:, DI:DI + CD],
                        wdt=jnp.pad(w_in[:, DI + CD:], ((0, 0), (0, NHP - NH))),
                        conv_w=full["ssm_conv_w"][j], conv_b=full["ssm_conv_b"][j], dt_bias=full["ssm_dt_bias"][j],
                        a_log=full["ssm_a_log"][j], d=full["ssm_d"][j], gate_norm=full["ssm_gate_norm"][j], wout=w_out)
        (w_pw1, token), (w_pw2, _) = whole(h, ("cv_w_pw1", j)), whole(h, ("cv_w_pw2", j))
        return dict(norm=full["cv_norm"][j] + token, ffn_norm=full["ffn_norm"][i], wpw1=w_pw1, b_pw1=full["cv_b_pw1"][j], dw_w=full["cv_dw_w"][j],
                    dw_b=full["cv_dw_b"][j], ln_g=full["cv_ln_g"][j], ln_b=full["cv_ln_b"][j], wpw2=w_pw2,
                    b_pw2=full["cv_b_pw2"][j])

    def ffn_weights(i, h):
        (wg_t, token), (wu_t, _), (wd, _) = (whole(h, (n, i)) for n in ("ffn_w_gate", "ffn_w_up", "ffn_w_down"))
        after = None if i + 1 == depth else (full["ssm_norm"] if i % 2 else full["cv_norm"])[(i + 1) // 2]
        return dict(norm=full["ffn_norm"][i] + token, next_norm=after, wg_t=wg_t, wu_t=wu_t, wd=wd)

    core = lax.axis_index("c")
    reducing, seen = {}, {}

    def reduce_small(g_final):
        g_mixers, g_ffns = [seen[i][0] for i in range(depth)], [seen[i][1] for i in range(depth)]
        local_small = {"final_norm": g_final, "ffn_norm": jnp.stack([g["norm"] for g in g_ffns])}
        for n in SMALL:
            if n.startswith("ssm_"):
                local_small[n] = jnp.stack([g[n[4:]] for g in g_mixers[0::2]])
            elif n.startswith("cv_"):
                local_small[n] = jnp.stack([g[n[3:]] for g in g_mixers[1::2]])
        return _gather8(_pack([local_small[n] for n in SMALL]), reduce=True)

    started = []

    def start_reduce(i, part, g_by_name, dep=None):
        slots = []
        for n, _ in matrices_of(i, part):
            g = g_by_name(n)
            if BIG[n] == "col":
                slots.append(g.reshape(g.shape[0], 4, g.shape[1] // 4).transpose(1, 0, 2))
            else:
                slots.append(g.reshape(4, g.shape[0] // 4, g.shape[1]))
        reducing[(i, part)], reducing["token"] = _reduce_start(slots, core, f"{i}{part[0]}", dep)
        started.append((i, part))
        return reducing["token"]

    def ffn_done(i, g_ffn_i):
        return start_reduce(0, "ffn", lambda n: g_ffn_i[n.split("_", 1)[1]]) if i == 0 else None

    def layer_done(i, g_mixer, g_ffn_i, g_final):
        seen[i] = (g_mixer, g_ffn_i)
        pick = lambda n: (g_ffn_i if n.startswith("ffn_") else g_mixer)[n.split("_", 1)[1]]
        if i > 0:
            return start_reduce(i, "layer", pick)
        reducing["small"] = reduce_small(g_final)
        return start_reduce(0, "mixer", pick, reducing["small"])

    lsum, grad_x, g_mix, g_ffn, g_final = _fwd_bwd(x[0], loss_target[0], depth, mixer_weights, ffn_weights,
                                                   full["final_norm"], layer_done, ffn_done)
    loss = (0.5 / D) * lax.psum(jnp.sum(lsum), ("x", "y", "c"))

    grads = {}
    for n, g in zip(SMALL, _unpack(reducing.pop("small"), [full[n].shape for n in SMALL])):
        if n in SMALL_SHARDED:
            ax = SMALL_SHARDED[n]
            g = lax.dynamic_slice_in_dim(g, chip * a[n].shape[ax], a[n].shape[ax], axis=ax)
        grads[n] = g
    last_start = reducing.pop("token")

    per_layer, after = {n: [None] * a[n].shape[0] for n in BIG}, last_start
    for i, part in started:
        reducing[(i, part)] = _reduce_finish(reducing[(i, part)], after, chip, f"{i}{part[0]}")
        after = reducing[(i, part)][2][0]
    for i, part in started:
        done = _reduce_collect(reducing.pop((i, part)), after, core, f"{i}{part[0]}")
        for (n, l), g in zip(matrices_of(i, part), done):
            per_layer[n][l] = g
    grads_t = {}
    for n in BIG:
        if n in TRANSPOSED:
            grads_t[n] = jnp.stack(per_layer[n])
        else:
            grads[n] = jnp.stack(per_layer[n])

    delta, new_m, new_v = {}, {}, {}
    for n in BIG:
        if n in ADAMW_TRANSPOSED:
            g_t = grads_t[n] if n in grads_t else grads[n].transpose(0, 2, 1)
            outs = _adamw(a[n].transpose(0, 2, 1), g_t, a["m_" + n].transpose(0, 2, 1), a["v_" + n].transpose(0, 2, 1),
                          name="adamw_" + n)
            grads[n] = g_t.transpose(0, 2, 1)
            delta[n], new_m[n], new_v[n] = [o.transpose(0, 2, 1) for o in outs]
        else:
            delta[n], new_m[n], new_v[n] = _adamw(a[n], grads[n], a["m_" + n], a["v_" + n], name="adamw_" + n)
    shapes = [a[n].shape for n in SMALL]
    upd = _adamw(_pack([a[n] for n in SMALL]), _pack([grads[n] for n in SMALL]), _pack([a["m_" + n] for n in SMALL]),
                 _pack([a["v_" + n] for n in SMALL]), name="adamw_small")
    for dst, packed in zip((delta, new_m, new_v), upd):
        dst.update(zip(SMALL, _unpack(packed, shapes)))

    return (loss, grad_x[None], *[grads[n] for n in WEIGHTS], *[delta[n] for n in WEIGHTS],
            *[new_m[n] for n in WEIGHTS], *[new_v[n] for n in WEIGHTS])
```

```python
import jax
import jax.numpy as jnp
from jax import lax
from jax.experimental import pallas as pl
from jax.experimental.pallas import tpu as pltpu

F32 = jnp.float32
BF16 = jnp.bfloat16
MESH = pl.DeviceIdType.MESH

D = 1024
DI = 2048
HD = 64
NH = 32
NG = 4
GW = DI // NG
NS = 128
KC = 4
CD = DI + 2 * NG * NS
Q = 128
DFF = 2816
CK = 31
EPS = 1e-5
NHP = 128

LR, B1, B2, AEPS, WD, STEP = 0.001, 0.9, 0.999, 1e-08, 0.01, 10

VMEM_LIMIT = 56 * 1024 * 1024
MM_VMEM_BUDGET = 36 * 1024 * 1024


def _pc(body, *, name, grid, in_specs, out_specs, out_shape, scratch=()):
    return pl.pallas_call(
        body, name=name, grid=grid, in_specs=in_specs, out_specs=out_specs, out_shape=out_shape,
        scratch_shapes=list(scratch),
        compiler_params=pltpu.CompilerParams(dimension_semantics=("arbitrary",) * len(grid),
                                             vmem_limit_bytes=VMEM_LIMIT))


def _sds(shape, dtype):
    return jax.ShapeDtypeStruct(tuple(shape), dtype)


def _row(tm, c):
    return pl.BlockSpec((tm, c), lambda i: (i, 0))


def _const(shape):
    return pl.BlockSpec(tuple(shape), lambda i: (0,) * len(shape))


def _tile(n, cap):
    if n <= cap:
        return n
    best = 128
    for t in range(128, cap + 1, 128):
        if n % t == 0:
            best = t
    return best


def _sigmoid(x):
    return 1.0 / (1.0 + jnp.exp(-x))


def _dsilu(x):
    s = _sigmoid(x)
    return s * (1.0 + x * (1.0 - s))


def _mm(a, b, mode, *, name, out_dtype=F32, bias=None, res=None, tm=1024, after=None, norm_g=None):
    if mode == "nn":
        (m, k), (k2, n) = a.shape, b.shape
    elif mode == "nt":
        (m, k), (n, k2) = a.shape, b.shape
    else:
        (k, m), (k2, n) = a.shape, b.shape
    assert k == k2, (a.shape, b.shape, mode)
    tm = _tile(m, tm if k <= 2048 else 512)
    so = jnp.dtype(out_dtype).itemsize + (4 if res is not None else 0)
    fits = [c for c in range(128, min(n, 2048) + 1, 128) if n % c == 0 and
            2 * (tm * k * a.dtype.itemsize + c * k * b.dtype.itemsize + tm * c * so) <= MM_VMEM_BUDGET]
    tn = n if n <= 128 else max(fits)
    dn = {"nn": (((1,), (0,)), ((), ())), "nt": (((1,), (1,)), ((), ())), "tn": (((0,), (0,)), ((), ()))}[mode]
    nb, nr, nn = bias is not None, res is not None, norm_g is not None
    assert not nn or tn == n == D, (n, tn)

    def body(*refs):
        acc = lax.dot_general(refs[0][...].astype(BF16), refs[1][...].astype(BF16), dn, preferred_element_type=F32)
        if nb:
            acc = acc + refs[2][...]
        if nr:
            acc = acc + refs[2 + nb][...]
        if nn:
            r = lax.rsqrt(jnp.mean(acc * acc, axis=-1, keepdims=True) + EPS)
            refs[-1][...] = (acc * r * refs[2 + nb + nr][...]).astype(BF16)
            refs[-2][...] = acc.astype(out_dtype)
        else:
            refs[-1][...] = acc.astype(out_dtype)

    a_spec = pl.BlockSpec((k, tm), lambda i, j: (0, i)) if mode == "tn" else pl.BlockSpec((tm, k), lambda i, j: (i, 0))
    b_spec = pl.BlockSpec((tn, k), lambda i, j: (j, 0)) if mode == "nt" else pl.BlockSpec((k, tn), lambda i, j: (0, j))
    ins, specs = [a, b], [a_spec, b_spec]
    if nb:
        ins.append(bias)
        specs.append(pl.BlockSpec((1, tn), lambda i, j: (0, j)))
    if nr:
        ins.append(res)
        specs.append(pl.BlockSpec((tm, tn), lambda i, j: (i, j)))
    if nn:
        ins.append(norm_g.reshape(1, D))
        specs.append(pl.BlockSpec((1, D), lambda i, j: (0, 0)))
    if after is not None:
        ins.append(after)
        specs.append(pl.BlockSpec((8, 128), lambda i, j: (0, 0)))
    tile = pl.BlockSpec((tm, tn), lambda i, j: (i, j))
    if nn:
        return _pc(body, name=name, grid=(m // tm, n // tn), in_specs=specs, out_specs=[tile, tile],
                   out_shape=[_sds((m, n), out_dtype), _sds((m, n), BF16)])(*ins)
    return _pc(body, name=name, grid=(m // tm, n // tn), in_specs=specs, out_specs=tile, out_shape=_sds((m, n), out_dtype))(*ins)


def _rms_fwd(h, g, *, name):
    t = h.shape[0]
    tm = min(t, 512)

    def body(h_ref, g_ref, u_ref):
        x = h_ref[...]
        r = lax.rsqrt(jnp.mean(x * x, axis=-1, keepdims=True) + EPS)
        u_ref[...] = (x * r * g_ref[...]).astype(BF16)

    return _pc(body, name=name, grid=(t // tm,), in_specs=[_row(tm, D), _const((1, D))], out_specs=_row(tm, D),
               out_shape=_sds((t, D), BF16))(h, g.reshape(1, D))


def _final_loss(h, g, target, *, name):
    t = h.shape[0]
    tm = min(t, 512)

    def body(h_ref, g_ref, t_ref, dh_ref, dg_ref, l_ref):
        x = h_ref[...]
        gg = g_ref[...]
        r = lax.rsqrt(jnp.mean(x * x, axis=-1, keepdims=True) + EPS)
        err = x * r * gg - t_ref[...]
        dy = err * (1.0 / D)
        dyg = dy * gg
        dot = jnp.mean(dyg * x, axis=-1, keepdims=True)
        dh_ref[...] = r * dyg - x * (r * r * r * dot)

        @pl.when(pl.program_id(0) == 0)
        def _():
            dg_ref[...] = jnp.zeros_like(dg_ref)
            l_ref[...] = jnp.zeros_like(l_ref)

        dg_ref[...] += jnp.sum(dy * x * r, axis=0, keepdims=True)
        l_ref[...] += jnp.sum(err * err, axis=0, keepdims=True)

    dh, dg, l = _pc(body, name=name, grid=(t // tm,),
                    in_specs=[_row(tm, D), _const((1, D)), _row(tm, D)],
                    out_specs=[_row(tm, D), _const((1, D)), _const((1, D))],
                    out_shape=[_sds((t, D), F32), _sds((1, D), F32), _sds((1, D), F32)])(h, g.reshape(1, D), target)
    return l, dh, dg.reshape(D)


NT = (((1,), (1,)), ((), ()))
FFN_COLS = 1408


def _ffn_up_act(u, wg_t, wu_t, *, name):
    t = u.shape[0]
    tm, tn = min(t, 512), FFN_COLS

    def body(u_ref, wg_ref, wu_ref, g_ref, up_ref, a_ref):
        uu = u_ref[...]
        gate = lax.dot_general(uu, wg_ref[...], NT, preferred_element_type=F32)
        up = lax.dot_general(uu, wu_ref[...], NT, preferred_element_type=F32)
        g_ref[...] = gate
        up_ref[...] = up
        a_ref[...] = (gate * _sigmoid(gate) * up).astype(BF16)

    wspec = pl.BlockSpec((tn, D), lambda i, j: (j, 0))
    tile = pl.BlockSpec((tm, tn), lambda i, j: (i, j))
    return _pc(body, name=name, grid=(t // tm, DFF // tn),
               in_specs=[pl.BlockSpec((tm, D), lambda i, j: (i, 0)), wspec, wspec], out_specs=[tile, tile, tile],
               out_shape=[_sds((t, DFF), F32), _sds((t, DFF), F32), _sds((t, DFF), BF16)])(u, wg_t, wu_t)


def _ffn_down_dx_act(dh2, wd, gate, up, *, name, after=None):
    t = dh2.shape[0]
    tm, tn = min(t, 512), FFN_COLS

    def body(d_ref, w_ref, g_ref, up_ref, *rest):
        dg_ref, du_ref = rest[-2:]
        da = lax.dot_general(d_ref[...].astype(BF16), w_ref[...], NT, preferred_element_type=F32)
        gg = g_ref[...]
        sg = _sigmoid(gg)
        dg_ref[...] = (da * up_ref[...] * (sg * (1.0 + gg * (1.0 - sg)))).astype(BF16)
        du_ref[...] = (da * gg * sg).astype(BF16)

    tile = pl.BlockSpec((tm, tn), lambda i, j: (i, j))
    ins = [dh2, wd, gate, up]
    specs = [pl.BlockSpec((tm, D), lambda i, j: (i, 0)), pl.BlockSpec((tn, D), lambda i, j: (j, 0)), tile, tile]
    if after is not None:
        ins.append(after)
        specs.append(pl.BlockSpec((8, 128), lambda i, j: (0, 0)))
    return _pc(body, name=name, grid=(t // tm, DFF // tn), in_specs=specs, out_specs=[tile, tile],
               out_shape=[_sds((t, DFF), BF16), _sds((t, DFF), BF16)])(*ins)


def _dx_rms_bwd(pairs, h, g, dres, *, name):
    t = h.shape[0]
    tm = min(t, 256)
    n = len(pairs)
    dns = [NT if mode == "nt" else (((1,), (0,)), ((), ())) for _, _, mode in pairs]

    def body(*refs):
        h_ref, g_ref, dres_ref, dh_ref, dg_ref = refs[2 * n:]

        @pl.when(pl.program_id(0) == 0)
        def _():
            dg_ref[...] = jnp.zeros_like(dg_ref)

        dy = None
        for k in range(n):
            part = lax.dot_general(refs[k][...].astype(BF16), refs[n + k][...], dns[k], preferred_element_type=F32)
            dy = part if dy is None else dy + part
        x = h_ref[...]
        r = lax.rsqrt(jnp.mean(x * x, axis=-1, keepdims=True) + EPS)
        dyg = dy * g_ref[...]
        dot = jnp.mean(dyg * x, axis=-1, keepdims=True)
        dh_ref[...] = dres_ref[...] + r * dyg - x * (r * r * r * dot)
        dg_ref[...] += jnp.sum(dy * x * r, axis=0, keepdims=True)

    a_specs = [_row(tm, a.shape[1]) for a, _, _ in pairs]
    b_specs = [_const(b.shape) for _, b, _ in pairs]
    dh, dg = _pc(body, name=name, grid=(t // tm,),
                 in_specs=a_specs + b_specs + [_row(tm, D), _const((1, D)), _row(tm, D)],
                 out_specs=[_row(tm, D), _const((1, D))], out_shape=[_sds((t, D), F32), _sds((1, D), F32)])(
                     *[a for a, _, _ in pairs], *[b for _, b, _ in pairs], h, g.reshape(1, D), dres)
    return dh, dg.reshape(D)


ROWS = 8


def _halo_before(tm, halo, tc):
    return pl.BlockSpec((halo, tc), lambda j, i: (jnp.maximum(i * (tm // halo) - 1, 0), j))


def _halo_after(tm, halo, tc, t):
    return pl.BlockSpec((halo, tc), lambda j, i: (jnp.minimum((i + 1) * (tm // halo), t // halo - 1), j))


def _conv4_fwd(xp, w, b, *, name):
    t = xp.shape[0]
    tm, tc, halo = min(t, 512), 1024, 8

    def body(x_ref, h_ref, w_ref, b_ref, o_ref, pad):
        i = pl.program_id(1)
        pad[pl.ds(0, halo), :] = jnp.where(i == 0, 0.0, h_ref[...])
        pad[pl.ds(halo, tm), :] = x_ref[...]
        acc = jnp.zeros((tm, tc), F32) + b_ref[...]
        for k in range(KC):
            acc = acc + pad[pl.ds(halo - (KC - 1) + k, tm), :] * w_ref[pl.ds(k, 1), :]
        o_ref[...] = acc * _sigmoid(acc)

    tile = pl.BlockSpec((tm, tc), lambda j, i: (i, j))
    return _pc(body, name=name, grid=(CD // tc, t // tm),
               in_specs=[tile, _halo_before(tm, halo, tc), pl.BlockSpec((KC, tc), lambda j, i: (0, j)),
                         pl.BlockSpec((1, tc), lambda j, i: (0, j))],
               out_specs=tile, out_shape=_sds((t, CD), F32), scratch=[pltpu.VMEM((halo + tm, tc), F32)])(xp, xp, w, b.reshape(1, CD))


def _conv4_bwd_act(xp, w, b, dxs, dbm, dcm, *, name):
    t = xp.shape[0]
    tm, tc, halo = min(t, 512), 1024, 8
    nx = DI // tc
    assert CD // tc == nx + 1 and dbm.shape[1] + dcm.shape[1] == tc

    def body(x_ref, h_ref, w_ref, b_ref, dx_ref, dbm_ref, dcm_ref, dp_ref, dw_ref, db_ref, pad):
        i = pl.program_id(1)
        pad[pl.ds(0, halo), :] = jnp.where(i == 0, 0.0, h_ref[...])
        pad[pl.ds(halo, tm), :] = x_ref[...]
        acc = jnp.zeros((tm, tc), F32) + b_ref[...]
        for k in range(KC):
            acc = acc + pad[pl.ds(halo - (KC - 1) + k, tm), :] * w_ref[pl.ds(k, 1), :]
        dout = jnp.where(pl.program_id(0) < nx, dx_ref[...], jnp.concatenate([dbm_ref[...], dcm_ref[...]], axis=1))
        dpre = dout * _dsilu(acc)
        dp_ref[...] = dpre

        @pl.when(i == 0)
        def _():
            dw_ref[...] = jnp.zeros_like(dw_ref)
            db_ref[...] = jnp.zeros_like(db_ref)

        db_ref[...] += jnp.sum(dpre, axis=0, keepdims=True)
        for k in range(KC):
            dw_ref[pl.ds(k, 1), :] += jnp.sum(dpre * pad[pl.ds(halo - (KC - 1) + k, tm), :], axis=0, keepdims=True)

    tile = pl.BlockSpec((tm, tc), lambda j, i: (i, j))
    wspec = pl.BlockSpec((KC, tc), lambda j, i: (0, j))
    bspec = pl.BlockSpec((1, tc), lambda j, i: (0, j))
    dp, dw, db = _pc(body, name=name, grid=(CD // tc, t // tm),
                     in_specs=[tile, _halo_before(tm, halo, tc), wspec, bspec,
                               pl.BlockSpec((tm, tc), lambda j, i: (i, jnp.minimum(j, nx - 1))),
                               pl.BlockSpec((tm, tc // 2), lambda j, i: (i, 0)), pl.BlockSpec((tm, tc // 2), lambda j, i: (i, 0))],
                     out_specs=[tile, wspec, bspec],
                     out_shape=[_sds((t, CD), F32), _sds((KC, CD), F32), _sds((1, CD), F32)],
                     scratch=[pltpu.VMEM((halo + tm, tc), F32)])(xp, xp, w, b.reshape(1, CD), dxs, dbm, dcm)
    return dp, dw, db.reshape(CD)


def _conv_bwd_x(dy, w, kk, *, name, out_dtype):
    t, c = dy.shape
    halo = 8 if kk <= 8 else 32
    tm, tc = (min(t, 512), min(c, 1024)) if kk <= 8 else (min(t, 256), min(c, 512))
    rows = 2 * ROWS if kk > 8 else tm

    def body(y_ref, h_ref, w_ref, o_ref, pad):
        i = pl.program_id(1)
        pad[pl.ds(0, tm), :] = y_ref[...]
        pad[pl.ds(tm, halo), :] = jnp.where(i == t // tm - 1, 0.0, h_ref[...])
        for r in range(0, tm, rows):
            acc = pad[pl.ds(kk - 1 + r, rows), :] * w_ref[pl.ds(0, 1), :]
            for k in range(1, kk):
                acc = acc + pad[pl.ds(kk - 1 - k + r, rows), :] * w_ref[pl.ds(k, 1), :]
            o_ref[pl.ds(r, rows), :] = acc.astype(out_dtype)

    tile = pl.BlockSpec((tm, tc), lambda j, i: (i, j))
    return _pc(body, name=name, grid=(c // tc, t // tm),
               in_specs=[tile, _halo_after(tm, halo, tc, t), pl.BlockSpec((kk, tc), lambda j, i: (0, j))],
               out_specs=tile, out_shape=_sds((t, c), out_dtype), scratch=[pltpu.VMEM((tm + halo, tc), F32)])(dy, dy, w)


def _glu_fwd(p, *, name):
    t = p.shape[0]
    tm = min(t, 512)

    def body(p_ref, o_ref):
        o_ref[...] = p_ref[:, :D] * _sigmoid(p_ref[:, D:])

    return _pc(body, name=name, grid=(t // tm,), in_specs=[_row(tm, 2 * D)], out_specs=_row(tm, D),
               out_shape=_sds((t, D), F32))(p)


def _glu_bwd(p, dg, *, name):
    t = p.shape[0]
    tm = min(t, 512)

    def body(p_ref, dg_ref, dp_ref, db_ref):
        a = p_ref[:, :D]
        s = _sigmoid(p_ref[:, D:])
        d = dg_ref[...]
        da = d * s
        dgate = d * a * s * (1.0 - s)
        dp_ref[:, :D] = da.astype(BF16)
        dp_ref[:, D:] = dgate.astype(BF16)

        @pl.when(pl.program_id(0) == 0)
        def _():
            db_ref[...] = jnp.zeros_like(db_ref)

        db_ref[:, :D] += jnp.sum(da, axis=0, keepdims=True)
        db_ref[:, D:] += jnp.sum(dgate, axis=0, keepdims=True)

    dp, db = _pc(body, name=name, grid=(t // tm,), in_specs=[_row(tm, 2 * D), _row(tm, D)],
                 out_specs=[_row(tm, 2 * D), _const((1, 2 * D))],
                 out_shape=[_sds((t, 2 * D), BF16), _sds((1, 2 * D), F32)])(p, dg)
    return dp, db.reshape(2 * D)


def _dwconv_ln_fwd(g, w, b, lg, lb, *, name):
    t = g.shape[0]
    tm, halo = min(t, 256), 32

    def body(x_ref, h_ref, w_ref, b_ref, lg_ref, lb_ref, c_ref, s_ref, pad):
        i = pl.program_id(0)
        pad[pl.ds(0, halo), :] = jnp.where(i == 0, 0.0, h_ref[...])
        pad[pl.ds(halo, tm), :] = x_ref[...]
        acc = jnp.zeros((tm, D), F32) + b_ref[...]
        for k in range(CK):
            acc = acc + pad[pl.ds(halo - (CK - 1) + k, tm), :] * w_ref[pl.ds(k, 1), :]
        c_ref[...] = acc
        mu = jnp.mean(acc, axis=-1, keepdims=True)
        xc = acc - mu
        r = lax.rsqrt(jnp.mean(xc * xc, axis=-1, keepdims=True) + EPS)
        n = xc * r * lg_ref[...] + lb_ref[...]
        s_ref[...] = (n * _sigmoid(n)).astype(BF16)

    c, s = _pc(body, name=name, grid=(t // tm,),
               in_specs=[_row(tm, D), pl.BlockSpec((halo, D), lambda i: (jnp.maximum(i * (tm // halo) - 1, 0), 0)),
                         _const((CK, D)), _const((1, D)), _const((1, D)), _const((1, D))],
               out_specs=[_row(tm, D), _row(tm, D)], out_shape=[_sds((t, D), F32), _sds((t, D), BF16)],
               scratch=[pltpu.VMEM((halo + tm, D), F32)])(g, g, w, b.reshape(1, D), lg.reshape(1, D), lb.reshape(1, D))
    return c, s


def _ln_silu_bwd(c, lg, lb, ds, *, name):
    t = c.shape[0]
    tm = min(t, 512)

    def body(c_ref, lg_ref, lb_ref, ds_ref, dc_ref, dlg_ref, dlb_ref):
        x = c_ref[...]
        mu = jnp.mean(x, axis=-1, keepdims=True)
        xc = x - mu
        r = lax.rsqrt(jnp.mean(xc * xc, axis=-1, keepdims=True) + EPS)
        xh = xc * r
        n = xh * lg_ref[...] + lb_ref[...]
        dn = ds_ref[...] * _dsilu(n)
        dxh = dn * lg_ref[...]
        m1 = jnp.mean(dxh, axis=-1, keepdims=True)
        m2 = jnp.mean(dxh * xh, axis=-1, keepdims=True)
        dc_ref[...] = r * (dxh - m1 - xh * m2)

        @pl.when(pl.program_id(0) == 0)
        def _():
            dlg_ref[...] = jnp.zeros_like(dlg_ref)
            dlb_ref[...] = jnp.zeros_like(dlb_ref)

        dlg_ref[...] += jnp.sum(dn * xh, axis=0, keepdims=True)
        dlb_ref[...] += jnp.sum(dn, axis=0, keepdims=True)

    dc, dlg, dlb = _pc(body, name=name, grid=(t // tm,),
                       in_specs=[_row(tm, D), _const((1, D)), _const((1, D)), _row(tm, D)],
                       out_specs=[_row(tm, D), _const((1, D)), _const((1, D))],
                       out_shape=[_sds((t, D), F32), _sds((1, D), F32), _sds((1, D), F32)])(
                           c, lg.reshape(1, D), lb.reshape(1, D), ds)
    return dc, dlg.reshape(D), dlb.reshape(D)


def _dwconv_bwd_w(g, dc, *, name):
    t = g.shape[0]
    tm, tc, halo = min(t, 256), 512, 32

    def body(x_ref, h_ref, dc_ref, dw_ref, db_ref, pad):
        i = pl.program_id(1)
        pad[pl.ds(0, halo), :] = jnp.where(i == 0, 0.0, h_ref[...])
        pad[pl.ds(halo, tm), :] = x_ref[...]
        @pl.when(i == 0)
        def _():
            dw_ref[...] = jnp.zeros_like(dw_ref)
            db_ref[...] = jnp.zeros_like(db_ref)

        d = dc_ref[...]
        db_ref[...] += jnp.sum(d, axis=0, keepdims=True)
        for k in range(CK):
            dw_ref[pl.ds(k, 1), :] += jnp.sum(d * pad[pl.ds(halo - (CK - 1) + k, tm), :], axis=0, keepdims=True)

    tile = pl.BlockSpec((tm, tc), lambda j, i: (i, j))
    dw, db = _pc(body, name=name, grid=(D // tc, t // tm),
                 in_specs=[tile, _halo_before(tm, halo, tc), tile],
                 out_specs=[pl.BlockSpec((CK, tc), lambda j, i: (0, j)), pl.BlockSpec((1, tc), lambda j, i: (0, j))],
                 out_shape=[_sds((CK, D), F32), _sds((1, D), F32)],
                 scratch=[pltpu.VMEM((halo + tm, tc), F32)])(g, g, dc)
    return dw, db.reshape(D)


def _colsum(x, *, name):
    t, c = x.shape
    tm = min(t, 512)

    def body(x_ref, o_ref):
        @pl.when(pl.program_id(0) == 0)
        def _():
            o_ref[...] = jnp.zeros_like(o_ref)

        o_ref[...] += jnp.sum(x_ref[...], axis=0, keepdims=True)

    return _pc(body, name=name, grid=(t // tm,), in_specs=[_row(tm, c)], out_specs=_const((1, c)),
               out_shape=_sds((1, c), F32))(x).reshape(c)


def _head_expand(n_lanes, first_head=0):
    h = lax.broadcasted_iota(jnp.int32, (NHP, n_lanes), 0)
    j = lax.broadcasted_iota(jnp.int32, (NHP, n_lanes), 1)
    return (h == first_head + j // HD).astype(F32)


def _softplus(v):
    return jnp.maximum(v, 0.0) + jnp.log(1.0 + jnp.exp(-jnp.abs(v)))


def _chunk_tri(n, upper):
    ri = lax.broadcasted_iota(jnp.int32, (n, n), 0)
    ci = lax.broadcasted_iota(jnp.int32, (n, n), 1)
    keep = (ri <= ci) if upper else (ri >= ci)
    return (keep & (ri // Q == ci // Q)).astype(F32)


def _split3(x):
    hi = x.astype(BF16)
    r1 = x - hi.astype(F32)
    mid = r1.astype(BF16)
    return hi, mid, (r1 - mid.astype(F32)).astype(BF16)


def _dot_sel(x, sel, sel_left=False):
    s = sel.astype(BF16)
    parts = [jnp.dot(s, t, preferred_element_type=F32) if sel_left else jnp.dot(t, s, preferred_element_type=F32)
             for t in _split3(x)]
    return (parts[2] + parts[1]) + parts[0]


def _ssd_pre(dtraw, dtb, alog, *, name):
    t = dtraw.shape[0]
    tm = min(t, 256)

    def body(r_ref, b_ref, al_ref, dtx_ref, acsx_ref, acst_ref):
        dt = _softplus(r_ref[...] + b_ref[...])
        da = dt * (-jnp.exp(al_ref[...]))
        acs = _dot_sel(da, _chunk_tri(tm, False), sel_left=True)
        expand = _head_expand(DI)
        dtx_ref[...] = _dot_sel(dt, expand)
        acsx_ref[...] = _dot_sel(acs, expand)
        acst_ref[...] = acs.T

    return _pc(body, name=name, grid=(t // tm,), in_specs=[_row(tm, NHP), _const((1, NHP)), _const((1, NHP))],
               out_specs=[_row(tm, DI), _row(tm, DI), pl.BlockSpec((NHP, tm), lambda i: (0, i))],
               out_shape=[_sds((t, DI), F32), _sds((t, DI), F32), _sds((NHP, t), F32)])(dtraw, dtb, alog)


def _ssd_post(dacs, dxdt, dtraw, dtb, alog, *, name):
    t = dtraw.shape[0]
    tm = min(t, 256)

    def body(g1_ref, g2_ref, r_ref, b_ref, al_ref, o_ref, db_ref, da_ref):
        h = lax.broadcasted_iota(jnp.int32, (DI, NHP), 1)
        j = lax.broadcasted_iota(jnp.int32, (DI, NHP), 0)
        red = (h == j // HD).astype(F32)
        dda = _dot_sel(_dot_sel(g1_ref[...], red), _chunk_tri(tm, True), sel_left=True)
        v = r_ref[...] + b_ref[...]
        ddt = dda * (-jnp.exp(al_ref[...])) + _dot_sel(g2_ref[...], red)
        draw = ddt * _sigmoid(v)
        o_ref[...] = draw.astype(BF16)

        @pl.when(pl.program_id(0) == 0)
        def _():
            db_ref[...] = jnp.zeros_like(db_ref)
            da_ref[...] = jnp.zeros_like(da_ref)

        db_ref[...] += jnp.sum(draw, axis=0, keepdims=True)
        da_ref[...] += jnp.sum(dda * _softplus(v), axis=0, keepdims=True)

    return _pc(body, name=name, grid=(t // tm,),
               in_specs=[_row(tm, DI), _row(tm, DI), _row(tm, NHP), _const((1, NHP)), _const((1, NHP))],
               out_specs=[_row(tm, NHP), _const((1, NHP)), _const((1, NHP))],
               out_shape=[_sds((t, NHP), BF16), _sds((1, NHP), F32), _sds((1, NHP), F32)])(dacs, dxdt, dtraw, dtb, alog)


def _decay_pair(acs_x, acst_ref, p, tri):
    lane = lax.broadcasted_iota(jnp.int32, (Q, 2 * HD), 1)
    v = acs_x[:, 2 * HD * p:2 * HD * (p + 1)]
    swapped = pltpu.roll(v, HD, axis=1)
    out = []
    for hh in range(2):
        colb = jnp.where((lane < HD) if hh == 0 else (lane >= HD), v, swapped)
        rowb = acst_ref[pl.ds(2 * p + hh, 1), :]
        out.append(jnp.exp(jnp.where(tri, colb - rowb, -1e30)))
    return out


def _ssd_fwd(xbc, dtx, acs_x, acs_t, d_x, *, name):
    t = xbc.shape[0]
    nc = t // Q
    hpg = NH // NG

    def body(x_ref, b_ref, c_ref, dtx_ref, acsx_ref, acst_ref, d_ref, y_ref, sin_ref, state):
        @pl.when(pl.program_id(1) == 0)
        def _():
            state[...] = jnp.zeros_like(state)

        xs = x_ref[...]
        tri = lax.broadcasted_iota(jnp.int32, (Q, Q), 0) >= lax.broadcasted_iota(jnp.int32, (Q, Q), 1)
        acs_x = acsx_ref[...]
        atot_x = acsx_ref[pl.ds(Q - 1, 1), :]
        xd = xs * dtx_ref[...]
        xb = xd.astype(BF16)
        bb = b_ref[...]
        cb16 = c_ref[...].astype(BF16)
        cbm = lax.dot_general(cb16, bb.astype(BF16), (((1,), (1,)), ((), ())), preferred_element_type=F32)
        s_in = state[...]
        sin_ref[0] = s_in
        y = jnp.dot(cb16, s_in.astype(BF16), preferred_element_type=F32) * jnp.exp(acs_x) + xs * d_ref[...]
        lane = lax.broadcasted_iota(jnp.int32, (Q, 2 * HD), 1)
        for p in range(hpg // 2):
            xp = xb[:, 2 * HD * p:2 * HD * (p + 1)]
            yp = jnp.zeros((Q, 2 * HD), F32)
            for hh, lm in enumerate(_decay_pair(acs_x, acst_ref, p, tri)):
                m = (cbm * lm).astype(BF16)
                xm = jnp.where((lane >= HD) if hh == 1 else (lane < HD), xp, jnp.zeros_like(xp))
                yp = yp + jnp.dot(m, xm, preferred_element_type=F32)
            y_ref[:, 2 * HD * p:2 * HD * (p + 1)] = y[:, 2 * HD * p:2 * HD * (p + 1)] + yp
        dec = jnp.exp(atot_x - acs_x)
        s_new = jnp.dot(bb.T.astype(BF16), (xd * dec).astype(BF16), preferred_element_type=F32)
        state[...] = jnp.exp(atot_x) * s_in + s_new

    grp = lambda g, c: (c, g)
    return _pc(body, name=name, grid=(NG, nc),
               in_specs=[pl.BlockSpec((Q, GW), grp),
                         pl.BlockSpec((Q, NS), lambda g, c: (c, DI // NS + g)),
                         pl.BlockSpec((Q, NS), lambda g, c: (c, DI // NS + NG + g)),
                         pl.BlockSpec((Q, GW), grp),
                         pl.BlockSpec((Q, GW), grp),
                         pl.BlockSpec((hpg, Q), lambda g, c: (g, c)),
                         pl.BlockSpec((1, GW), lambda g, c: (0, g))],
               out_specs=[pl.BlockSpec((Q, GW), grp), pl.BlockSpec((1, NS, GW), lambda g, c: (c, 0, g))],
               out_shape=[_sds((t, DI), F32), _sds((nc, NS, DI), F32)],
               scratch=[pltpu.VMEM((NS, GW), F32)])(xbc, xbc, xbc, dtx, acs_x, acs_t, d_x)


def _ssd_bwd(xbc, dtx, acs_x, acs_t, d_x, sin, dy, *, name):
    t = xbc.shape[0]
    nc = t // Q
    hpg = NH // NG

    def body(x_ref, b_ref, c_ref, dtx_ref, acsx_ref, acst_ref, d_ref, sin_ref, dy_ref,
             dx_ref, db_ref, dc_ref, dacs_ref, dxdt_ref, ddx_ref, dstate):
        @pl.when(pl.program_id(1) == 0)
        def _():
            dstate[...] = jnp.zeros_like(dstate)
            ddx_ref[...] = jnp.zeros_like(ddx_ref)

        xs = x_ref[...]
        dtxv = dtx_ref[...]
        gy = dy_ref[...]
        tri = lax.broadcasted_iota(jnp.int32, (Q, Q), 0) >= lax.broadcasted_iota(jnp.int32, (Q, Q), 1)
        acs_x = acsx_ref[...]
        atot_x = acsx_ref[pl.ds(Q - 1, 1), :]
        xd = xs * dtxv
        xb = xd.astype(BF16)
        gb = gy.astype(BF16)
        bb = b_ref[...]
        cc = c_ref[...]
        bb16, cc16 = bb.astype(BF16), cc.astype(BF16)
        cbm = lax.dot_general(cc16, bb16, (((1,), (1,)), ((), ())), preferred_element_type=F32)
        s_in = sin_ref[0]
        s_in16 = s_in.astype(BF16)
        ds_next = dstate[...]
        ds16 = ds_next.astype(BF16)
        ecs = jnp.exp(acs_x)
        dec = jnp.exp(atot_x - acs_x)
        etot = jnp.exp(atot_x)

        ddx_ref[...] += jnp.sum(gy * xs, axis=0, keepdims=True)
        z = jnp.dot(cc16, s_in16, preferred_element_type=F32)
        dz16 = (gy * ecs).astype(BF16)
        dacs = gy * z * ecs
        dcm = lax.dot_general(dz16, s_in16, (((1,), (1,)), ((), ())), preferred_element_type=F32)
        ds_here = jnp.dot(cc.T.astype(BF16), dz16, preferred_element_type=F32)
        u = jnp.dot(bb16, ds16, preferred_element_type=F32)
        dxd = u * dec
        w_dec = u * xd * dec
        dacs = dacs - w_dec
        datot = jnp.sum(w_dec, axis=0, keepdims=True) + jnp.sum(etot * ds_next * s_in, axis=0, keepdims=True)
        dbm = lax.dot_general((xd * dec).astype(BF16), ds16, (((1,), (1,)), ((), ())), preferred_element_type=F32)
        dstate[...] = etot * ds_next + ds_here

        lane = lax.broadcasted_iota(jnp.int32, (Q, 2 * HD), 1)
        lane_x = lax.broadcasted_iota(jnp.int32, (1, GW), 1)
        dcb = jnp.zeros((Q, Q), F32)
        dx_pairs = []
        for p in range(hpg // 2):
            sl = slice(2 * HD * p, 2 * HD * (p + 1))
            xp = xb[:, sl]
            gp = gb[:, sl]
            dxp = jnp.zeros((Q, 2 * HD), F32)
            for hh, lm in enumerate(_decay_pair(acs_x, acst_ref, p, tri)):
                r = 2 * p + hh
                mm = cbm * lm
                half = (lane >= HD) if hh == 1 else (lane < HD)
                xm = jnp.where(half, xp, jnp.zeros_like(xp))
                gm = jnp.where(half, gp, jnp.zeros_like(gp))
                mt16 = mm.T.astype(BF16)
                dxp = dxp + jnp.dot(mt16, gm, preferred_element_type=F32)
                dm = lax.dot_general(gm, xp, (((1,), (1,)), ((), ())), preferred_element_type=F32)
                dmt = lax.dot_general(xm, gp, (((1,), (1,)), ((), ())), preferred_element_type=F32)
                dcb = dcb + dm * lm
                wdiff = jnp.sum(dm * mm - dmt * mm.T, axis=1, keepdims=True)
                dacs = dacs + wdiff * (lane_x == HD * r).astype(F32)
            dx_pairs.append(dxp)
        dxd = dxd + jnp.concatenate(dx_pairs, axis=1)
        dcm = dcm + jnp.dot(dcb.astype(BF16), bb16, preferred_element_type=F32)
        dbm = dbm + jnp.dot(dcb.T.astype(BF16), cc16, preferred_element_type=F32)
        db_ref[...] = dbm
        dc_ref[...] = dcm

        last = lax.broadcasted_iota(jnp.int32, (Q, GW), 0) == Q - 1
        dacs_ref[...] = dacs + jnp.where(last, datot, 0.0)
        dxdt_ref[...] = dxd * xs
        dx_ref[...] = dxd * dtxv + gy * d_ref[...]

    rev = lambda g, c: (nc - 1 - c, g)
    return _pc(body, name=name, grid=(NG, nc),
               in_specs=[pl.BlockSpec((Q, GW), rev),
                         pl.BlockSpec((Q, NS), lambda g, c: (nc - 1 - c, DI // NS + g)),
                         pl.BlockSpec((Q, NS), lambda g, c: (nc - 1 - c, DI // NS + NG + g)),
                         pl.BlockSpec((Q, GW), rev),
                         pl.BlockSpec((Q, GW), rev),
                         pl.BlockSpec((hpg, Q), lambda g, c: (g, nc - 1 - c)),
                         pl.BlockSpec((1, GW), lambda g, c: (0, g)),
                         pl.BlockSpec((1, NS, GW), lambda g, c: (nc - 1 - c, 0, g)),
                         pl.BlockSpec((Q, GW), rev)],
               out_specs=[pl.BlockSpec((Q, GW), rev), pl.BlockSpec((Q, NS), rev), pl.BlockSpec((Q, NS), rev),
                          pl.BlockSpec((Q, GW), rev), pl.BlockSpec((Q, GW), rev),
                          pl.BlockSpec((1, GW), lambda g, c: (0, g))],
               out_shape=[_sds((t, DI), F32), _sds((t, NG * NS), F32), _sds((t, NG * NS), F32), _sds((t, DI), F32),
                          _sds((t, DI), F32), _sds((1, DI), F32)],
               scratch=[pltpu.VMEM((NS, GW), F32)])(xbc, xbc, xbc, dtx, acs_x, acs_t, d_x, sin, dy)


def _gated_norm_fwd(y, z, gn, *, name):
    t = y.shape[0]
    tm = min(t, 256)

    def body(y_ref, z_ref, g_ref, o_ref):
        for k in range(NG):
            sl = slice(GW * k, GW * (k + 1))
            zz = z_ref[:, sl]
            h = y_ref[:, sl] * (zz * _sigmoid(zz))
            r = lax.rsqrt(jnp.mean(h * h, axis=-1, keepdims=True) + EPS)
            o_ref[:, sl] = (h * r * g_ref[:, sl]).astype(BF16)

    return _pc(body, name=name, grid=(t // tm,), in_specs=[_row(tm, DI), _row(tm, DI), _const((1, DI))],
               out_specs=_row(tm, DI), out_shape=_sds((t, DI), BF16))(y, z, gn.reshape(1, DI))


def _gated_norm_bwd(y, z, gn, dout, *, name):
    t = y.shape[0]
    tm = min(t, 256)

    def body(y_ref, z_ref, g_ref, do_ref, dy_ref, dz_ref, dg_ref):
        @pl.when(pl.program_id(0) == 0)
        def _():
            dg_ref[...] = jnp.zeros_like(dg_ref)

        for k in range(NG):
            sl = slice(GW * k, GW * (k + 1))
            zz = z_ref[:, sl]
            yy = y_ref[:, sl]
            sz = zz * _sigmoid(zz)
            h = yy * sz
            r = lax.rsqrt(jnp.mean(h * h, axis=-1, keepdims=True) + EPS)
            d = do_ref[:, sl]
            dg_ref[:, sl] += jnp.sum(d * h * r, axis=0, keepdims=True)
            dgd = d * g_ref[:, sl]
            dot = jnp.mean(dgd * h, axis=-1, keepdims=True)
            dh = r * dgd - h * (r * r * r * dot)
            dy_ref[:, sl] = dh * sz
            dz_ref[:, sl] = (dh * yy * _dsilu(zz)).astype(BF16)

    dy, dz, dg = _pc(body, name=name, grid=(t // tm,),
                     in_specs=[_row(tm, DI), _row(tm, DI), _const((1, DI)), _row(tm, DI)],
                     out_specs=[_row(tm, DI), _row(tm, DI), _const((1, DI))],
                     out_shape=[_sds((t, DI), F32), _sds((t, DI), BF16), _sds((1, DI), F32)])(y, z, gn.reshape(1, DI), dout)
    return dy, dz, dg.reshape(DI)


def _pad_heads(v):
    return jnp.pad(v.reshape(1, NH), ((0, 0), (0, NHP - NH)))


def _residual_out(a, w, h, next_norm, name, bias=None):
    if next_norm is None:
        return _mm(a, w, "nn", bias=bias, res=h, name=name), None
    return _mm(a, w, "nn", bias=bias, res=h, norm_g=next_norm, name=name)


def _ssm_fwd(h, u, p, next_norm):
    z = _mm(u, p["wz"], "nn", name="ssm_in_z")
    xp = _mm(u, p["wxbc"], "nn", name="ssm_in_xbc")
    dtraw = _mm(u, p["wdt"], "nn", name="ssm_in_dt")
    xbc = _conv4_fwd(xp, p["conv_w"], p["conv_b"], name="ssm_conv_fwd")
    dtb, alog = _pad_heads(p["dt_bias"]), _pad_heads(p["a_log"])
    d_x = jnp.repeat(p["d"], HD).reshape(1, DI)
    dtx, acs_x, acs_t = _ssd_pre(dtraw, dtb, alog, name="ssd_pre")
    y, sin = _ssd_fwd(xbc, dtx, acs_x, acs_t, d_x, name="ssd_fwd")
    yn = _gated_norm_fwd(y, z, p["gate_norm"], name="ssm_gate_fwd")
    if callable(p["wout"]):
        p["wout"] = p["wout"](yn)
    h2, u2 = _residual_out(yn, p["wout"], h, next_norm, "ssm_out")
    return h2, u2, (h, u, z, xp, dtraw, xbc, dtx, acs_x, acs_t, d_x, dtb, alog, y, sin, yn)


def _ssm_bwd(dh2, p, saved, after=None):
    h, u, z, xp, dtraw, xbc, dtx, acs_x, acs_t, d_x, dtb, alog, y, sin, yn = saved
    dyn = _mm(dh2, p["wout"], "nt", name="ssm_out_dx", after=after)
    g = {"w_out": _mm(yn, dh2, "tn", out_dtype=BF16, name="ssm_out_dw")}
    dy, dz, g["gate_norm"] = _gated_norm_bwd(y, z, p["gate_norm"], dyn, name="ssm_gate_bwd")
    dxs, dbm, dcm, dacs, dxdt, ddx = _ssd_bwd(xbc, dtx, acs_x, acs_t, d_x, sin, dy, name="ssd_bwd")
    ddtraw, ddtb, dav = _ssd_post(dacs, dxdt, dtraw, dtb, alog, name="ssd_post")
    dpre, g["conv_w"], g["conv_b"] = _conv4_bwd_act(xp, p["conv_w"], p["conv_b"], dxs, dbm, dcm, name="ssm_conv_bwd_act")
    dxp = _conv_bwd_x(dpre, p["conv_w"], KC, name="ssm_conv_bwd_x", out_dtype=BF16)
    dh, g["norm"] = _dx_rms_bwd([(dz, p["wz"], "nt"), (dxp, p["wxbc"], "nt"), (ddtraw, p["wdt"], "nt")], h, p["norm"], dh2,
                                name="ssm_in_dx_rms_bwd")
    g["w_in"] = jnp.concatenate([_mm(u, dz, "tn", out_dtype=BF16, name="ssm_in_dw_z"),
                                 _mm(u, dxp, "tn", out_dtype=BF16, name="ssm_in_dw_xbc"),
                                 _mm(u, ddtraw, "tn", out_dtype=BF16, name="ssm_in_dw_dt")[:, :NH]], axis=1)
    g["dt_bias"] = ddtb[0, :NH]
    g["a_log"] = dav[0, :NH] * (-jnp.exp(p["a_log"]))
    g["d"] = ddx.reshape(NH, HD).sum(-1)
    return dh, g


def _cv_fwd(h, u, p, next_norm):
    pre = _mm(u, p["wpw1"], "nn", bias=p["b_pw1"].reshape(1, 2 * D), name="cv_pw1")
    gl = _glu_fwd(pre, name="cv_glu_fwd")
    c, s = _dwconv_ln_fwd(gl, p["dw_w"], p["dw_b"], p["ln_g"], p["ln_b"], name="cv_dwconv_ln_fwd")
    h2, u2 = _residual_out(s, p["wpw2"], h, next_norm, "cv_pw2", bias=p["b_pw2"].reshape(1, D))
    return h2, u2, (h, u, pre, gl, c, s)


def _cv_bwd(dh2, p, saved, after=None):
    h, u, pre, gl, c, s = saved
    ds = _mm(dh2, p["wpw2"], "nt", name="cv_pw2_dx", after=after)
    g = {"w_pw2": _mm(s, dh2, "tn", out_dtype=BF16, name="cv_pw2_dw"), "b_pw2": _colsum(dh2, name="cv_pw2_db")}
    dc, g["ln_g"], g["ln_b"] = _ln_silu_bwd(c, p["ln_g"], p["ln_b"], ds, name="cv_ln_bwd")
    dgl = _conv_bwd_x(dc, p["dw_w"], CK, name="cv_dwconv_bwd_x", out_dtype=F32)
    g["dw_w"], g["dw_b"] = _dwconv_bwd_w(gl, dc, name="cv_dwconv_bwd_w")
    dpre, g["b_pw1"] = _glu_bwd(pre, dgl, name="cv_glu_bwd")
    g["w_pw1"] = _mm(u, dpre, "tn", out_dtype=BF16, name="cv_pw1_dw")
    dh, g["norm"] = _dx_rms_bwd([(dpre, p["wpw1"], "nt")], h, p["norm"], dh2, name="cv_pw1_dx_rms_bwd")
    return dh, g


def _ffn_fwd(h, u, p, next_norm):
    gate, up, a = _ffn_up_act(u, p["wg_t"], p["wu_t"], name="ffn_gate_up_act")
    h2, u2 = _residual_out(a, p["wd"], h, next_norm, "ffn_down")
    return h2, u2, (h, u, gate, up, a)


def _ffn_bwd(dh2, p, saved, after=None):
    h, u, gate, up, a = saved
    dgate, dup = _ffn_down_dx_act(dh2, p["wd"], gate, up, name="ffn_down_dx_act", after=after)
    g = {"w_down": _mm(a, dh2, "tn", out_dtype=BF16, name="ffn_down_dw"),
         "w_gate": _mm(dgate, u, "tn", out_dtype=BF16, name="ffn_gate_dw"),
         "w_up": _mm(dup, u, "tn", out_dtype=BF16, name="ffn_up_dw")}
    dh, g["norm"] = _dx_rms_bwd([(dgate, p["wg_t"], "nn"), (dup, p["wu_t"], "nn")], h, p["norm"], dh2, name="ffn_dx_rms_bwd")
    return dh, g


def _fwd_bwd(x, target, depth, mixer_weights, ffn_weights, final_norm, layer_done=None, ffn_done=None):
    h, u, tape, weights = x, None, [], []
    for i in range(depth):
        mix_w = mixer_weights(i, h)
        if u is None:
            u = _rms_fwd(h, mix_w["norm"], name="first_rms_fwd")
        h, u, s_mix = (_ssm_fwd if i % 2 == 0 else _cv_fwd)(h, u, mix_w, mix_w["ffn_norm"])
        ffn_w = ffn_weights(i, h)
        h, u, s_ffn = _ffn_fwd(h, u, ffn_w, ffn_w["next_norm"])
        tape.append((s_mix, s_ffn))
        weights.append((mix_w, ffn_w))
    lsum, dh, g_final = _final_loss(h, final_norm, target, name="loss_head")
    g_mix, g_ffn, token = [None] * depth, [None] * depth, None
    for i in reversed(range(depth)):
        s_mix, s_ffn = tape[i]
        dh, g_ffn[i] = _ffn_bwd(dh, weights[i][1], s_ffn, after=token)
        token = ffn_done(i, g_ffn[i]) if ffn_done is not None else None
        dh, g_mix[i] = (_ssm_bwd if i % 2 == 0 else _cv_bwd)(dh, weights[i][0], s_mix, after=token)
        if layer_done is not None:
            token = layer_done(i, g_mix[i], g_ffn[i], g_final)
    return lsum, dh, g_mix, g_ffn, g_final


ANY = pl.BlockSpec(memory_space=pl.ANY)


def _place():
    x, y, c = lax.axis_index("x"), lax.axis_index("y"), lax.axis_index("c")
    return x, y, c, [(1 - x, y), (x, 1 - y), (1 - x, 1 - y)]


def _remote(src, dst, ssem, rsem, dev):
    return pltpu.make_async_remote_copy(src_ref=src, dst_ref=dst, send_sem=ssem, recv_sem=rsem, device_id=dev,
                                        device_id_type=MESH)


HBM = pl.BlockSpec(memory_space=pltpu.HBM)
SEM = pl.BlockSpec(memory_space=pltpu.SEMAPHORE)
EFFECT = pltpu.SideEffectType.DATAFLOW_SIDE_EFFECTING


def _gather_copies(src, land, ssem, rsem):
    x, y, c, chips = _place()
    me, pairs = 2 * x + y, []
    for a in range(len(src)):
        h = src[a].shape[0] // 2
        rows = pl.ds(c * h, h)
        for j, (px, py) in enumerate(chips):
            sems = (ssem.at[3 * a + j], rsem.at[3 * a + j], (px, py, c))
            pairs.append((_remote(src[a].at[rows], land[a].at[me, rows], *sems),
                          _remote(src[a].at[rows], land[a].at[2 * px + py, rows], *sems)))
    return pairs


def _chip_copies(src, land, ssem, rsem):
    x, y, c, chips = _place()
    me, pairs = 2 * x + y, []
    for a in range(len(src)):
        for j, (px, py) in enumerate(chips):
            sems = (ssem.at[3 * a + j], rsem.at[3 * a + j], (px, py, c))
            pairs.append((_remote(src[a].at[2 * px + py], land[a].at[me], *sems),
                          _remote(src[a].at[2 * px + py], land[a].at[2 * px + py], *sems)))
    return pairs


def _split_start(copies, srcs, land_shapes, dep, name):
    n = len(srcs)

    def body(*refs):
        for send, _ in copies(refs[:n], refs[n:2 * n], refs[2 * n + 1], refs[2 * n + 2]):
            send.start()
        refs[-1][...] = jnp.zeros_like(refs[-1])

    lands = [pltpu.with_memory_space_constraint(lax.empty(s, w.dtype), pltpu.HBM) for s, w in zip(land_shapes, srcs)]
    out = pl.pallas_call(
        body, name=name,
        out_shape=(pltpu.SemaphoreType.DMA((3 * n,)), pltpu.SemaphoreType.DMA((3 * n,)),
                   *[pltpu.HBM(w.shape, w.dtype) for w in srcs], *[pltpu.HBM(w.shape, w.dtype) for w in lands],
                   _sds((8, 128), F32)),
        in_specs=[HBM] * (2 * n) + [ANY], out_specs=(SEM, SEM, *[HBM] * (2 * n), pl.BlockSpec(memory_space=pltpu.VMEM)),
        input_output_aliases={a: 2 + a for a in range(2 * n)},
        compiler_params=pltpu.CompilerParams(has_side_effects=EFFECT))(
            *[pltpu.with_memory_space_constraint(w, pltpu.HBM) for w in srcs], *lands, dep)
    return out[0], out[1], list(out[2:2 + n]), list(out[2 + n:2 + 2 * n]), out[-1]


def _split_wait(copies, ssem, rsem, srcs, lands, after, name):
    n = len(srcs)

    def body(*refs):
        for send, recv in copies(refs[:n], refs[n:2 * n], refs[2 * n], refs[2 * n + 1]):
            send.wait_send()
            recv.wait_recv()

    out = pl.pallas_call(
        body, name=name,
        out_shape=(*[pltpu.HBM(w.shape, w.dtype) for w in srcs], *[pltpu.HBM(w.shape, w.dtype) for w in lands]),
        in_specs=[HBM] * (2 * n) + [SEM, SEM, ANY], out_specs=tuple([HBM] * (2 * n)),
        input_output_aliases={a: a for a in range(2 * n)},
        compiler_params=pltpu.CompilerParams(has_side_effects=EFFECT))(*srcs, *lands, ssem, rsem, after)
    return list(out[:n]), list(out[n:])


def _gather_start(ws, dep, tag):
    return _split_start(_gather_copies, ws, [(4,) + w.shape for w in ws], dep, "gather_start_" + tag)


def _gather_wait(ssem, rsem, srcs, lands, after, tag):
    return _split_wait(_gather_copies, ssem, rsem, srcs, lands, after, "gather_wait_" + tag)


def _gather_forward(lands, tag):
    n = len(lands)

    def body(*refs):
        land = refs[n:2 * n]
        ssem, rsem = refs[2 * n:]
        x, y, c, chips = _place()
        sib = (x, y, 1 - c)
        cps = []
        for a in range(n):
            h = lands[a].shape[1] // 2
            for j, (px, py) in enumerate(chips):
                blk = land[a].at[2 * px + py, pl.ds(c * h, h)]
                cps.append(_remote(blk, blk, ssem.at[3 * a + j], rsem.at[3 * a + j], sib))
        for cp in cps:
            cp.start()
        for a in range(n):
            h = lands[a].shape[1] // 2
            for j, (px, py) in enumerate(chips):
                blk = land[a].at[2 * px + py, pl.ds((1 - c) * h, h)]
                _remote(blk, blk, ssem.at[3 * a + j], rsem.at[3 * a + j], sib).wait_recv()
        for cp in cps:
            cp.wait_send()

    return pl.pallas_call(
        body, name="gather_forward_" + tag, in_specs=[ANY] * n, out_specs=[ANY] * n,
        out_shape=[_sds(w.shape, w.dtype) for w in lands], input_output_aliases={a: a for a in range(n)},
        scratch_shapes=[pltpu.SemaphoreType.DMA((3 * n,)), pltpu.SemaphoreType.DMA((3 * n,))])(*lands)


def _rows_tile(r, c, n_arrays):
    cap = max(8, (16 * 1024 * 1024) // (2 * n_arrays * c * 4))
    fits = [t for t in range(8, min(r, cap) + 1, 8) if r % t == 0]
    return max(fits) if fits else r


REDUCE_ROW_BLOCKS = 2


def _add_pairs(gs, gots, core, *, name):
    n = len(gs)

    def body(core_ref, *refs):
        for k in range(n):
            refs[2 * n + k][...] = (refs[k][...].astype(F32) + refs[n + k][...].astype(F32)).astype(BF16)

    halves, wholes, shapes = [], [], []
    for g in gs:
        _, r, c = g.shape
        tm = r // (2 * REDUCE_ROW_BLOCKS)
        wholes.append(pl.BlockSpec((None, None, tm, c), lambda s, i, core_ref: (s, core_ref[0], i, 0)))
        halves.append(pl.BlockSpec((None, tm, c), lambda s, i, core_ref: (s, i, 0)))
        shapes.append(_sds((4, r // 2, c), BF16))
    grid_spec = pltpu.PrefetchScalarGridSpec(num_scalar_prefetch=1, grid=(4, REDUCE_ROW_BLOCKS),
                                             in_specs=wholes + halves, out_specs=halves)
    return pl.pallas_call(body, name=name, grid_spec=grid_spec, out_shape=shapes,
                          compiler_params=pltpu.CompilerParams(dimension_semantics=("arbitrary", "arbitrary"),
                                                               vmem_limit_bytes=VMEM_LIMIT))(
                                                                   core, *[g.reshape(4, 2, g.shape[1] // 2, g.shape[2]) for g in gs], *gots)


def _sum_slots(rbs, *, name):
    n = len(rbs)

    def body(*refs):
        for k in range(n):
            acc = refs[k][0].astype(F32)
            for s in range(1, 4):
                acc = acc + refs[k][s].astype(F32)
            refs[n + k][...] = acc

    ins, outs, shapes = [], [], []
    for rb in rbs:
        _, rows, c = rb.shape
        tm = rows // REDUCE_ROW_BLOCKS
        ins.append(pl.BlockSpec((4, tm, c), lambda i: (0, i, 0)))
        outs.append(_row(tm, c))
        shapes.append(_sds((rows, c), F32))
    return _pc(body, name=name, grid=(REDUCE_ROW_BLOCKS,), in_specs=ins, out_specs=outs, out_shape=shapes)(*rbs)


def _exchange_pairs(gs, tag, dep=None):
    n = len(gs)
    extra = [] if dep is None else [dep]

    def body(*refs):
        ins, got = refs[:n], refs[n + len(extra):2 * n + len(extra)]
        ssem, rsem = refs[2 * n + len(extra):]
        x, y, c, _ = _place()
        cps = []
        for a in range(n):
            lh = gs[a].shape[1] // 2
            cps.append(_remote(ins[a].at[:, pl.ds((1 - c) * lh, lh)], got[a], ssem.at[a], rsem.at[a], (x, y, 1 - c)))
        for cp in cps:
            cp.start()
        for cp in cps:
            cp.wait()

    halves = [_sds((4, g.shape[1] // 2) + g.shape[2:], g.dtype) for g in gs]
    return pl.pallas_call(
        body, name="reduce_pair_exchange_" + tag, in_specs=[ANY] * (n + len(extra)), out_specs=[ANY] * n, out_shape=halves,
        scratch_shapes=[pltpu.SemaphoreType.DMA((n,)), pltpu.SemaphoreType.DMA((n,))])(*gs, *extra)


def _reduce_start(gs, core, tag, dep=None):
    got = _exchange_pairs(gs, tag, dep)
    core1 = core.reshape(1).astype(jnp.int32)
    ps = _add_pairs(gs, got, core1, name="reduce_pair_add_" + tag)
    ssem, rsem, ps, rbs, token = _split_start(_chip_copies, ps, [p.shape for p in ps], got[0], "reduce_start_" + tag)
    return (ssem, rsem, ps, rbs), token


def _sibling_copies(src, land, ssem, rsem):
    x, y, c, _ = _place()
    cps = [_remote(src[a], land[a], ssem.at[a], rsem.at[a], (x, y, 1 - c)) for a in range(len(src))]
    return [(cp, cp) for cp in cps]


def _reduce_finish(state, after, chip, tag):
    ps, rbs = _split_wait(_chip_copies, *state, after, "reduce_wait_" + tag)
    rbs = [lax.dynamic_update_index_in_dim(b, lax.dynamic_index_in_dim(p, chip, 0, keepdims=False), chip, 0)
           for b, p in zip(rbs, ps)]
    fs = _sum_slots(rbs, name="reduce_chip_sum_" + tag)
    return _split_start(_sibling_copies, fs, [f.shape for f in fs], rbs[0], "reduce_final_start_" + tag)[:4]


def _reduce_collect(state, after, core, tag):
    fs, theirs = _split_wait(_sibling_copies, *state, after, "reduce_final_wait_" + tag)
    return [jnp.where(core == 0, jnp.concatenate([f, t], axis=0), jnp.concatenate([t, f], axis=0))
            for f, t in zip(fs, theirs)]


def _gather8(v, reduce):
    m = v.shape[0]

    def body(x_ref, *rest):
        if reduce:
            sum_ref, out_ref, send_sems, recv_sems, local_sem = rest
        else:
            out_ref, send_sems, recv_sems, local_sem = rest
        x, y, c, chips = _place()
        me, sibling = (x, y, c), (x, y, 1 - c)

        def rows(px, py, pc):
            return out_ref.at[pl.ds((4 * px + 2 * py + pc) * m, m), :]

        def copy(k, block, to, src=None):
            return _remote(rows(*block) if src is None else src, rows(*block), send_sems.at[k], recv_sems.at[k], to)

        mine = pltpu.make_async_copy(x_ref, rows(*me), local_sem)
        mine.start()
        first = [copy(0, me, sibling, src=x_ref)]
        first += [copy(1 + j, me, (*chip, c), src=x_ref) for j, chip in enumerate(chips)]
        for cp in first:
            cp.start()
        passed = [copy(4 + j, (*chip, c), sibling) for j, chip in enumerate(chips)]
        for j, chip in enumerate(chips):
            copy(1 + j, (*chip, c), me).wait_recv()
            passed[j].start()
        copy(0, sibling, me).wait_recv()
        for j, chip in enumerate(chips):
            copy(4 + j, (*chip, 1 - c), me).wait_recv()
        for cp in first + passed:
            cp.wait_send()
        mine.wait()
        if reduce:
            acc = out_ref[pl.ds(0, m), :]
            for k in range(1, 8):
                acc = acc + out_ref[pl.ds(k * m, m), :]
            sum_ref[...] = acc

    vm = pl.BlockSpec(memory_space=pltpu.VMEM)
    out_shape = [_sds((8 * m, 128), F32)]
    if reduce:
        out_shape = [_sds((m, 128), F32)] + out_shape
    out = pl.pallas_call(
        body, name="allreduce_small" if reduce else "gather_small", in_specs=[vm], out_specs=[vm] * len(out_shape),
        out_shape=out_shape,
        scratch_shapes=[pltpu.SemaphoreType.DMA((7,)), pltpu.SemaphoreType.DMA((7,)), pltpu.SemaphoreType.DMA(())],
        compiler_params=pltpu.CompilerParams(vmem_limit_bytes=VMEM_LIMIT))(v)
    return out[0]


def _adamw(w, g, m, v, *, name):
    if w.ndim == 2:
        return [o[0] for o in _adamw(w[None], g[None], m[None], v[None], name=name)]
    nl, r, c = w.shape
    tm = _rows_tile(r, c, 7)
    blk = pl.BlockSpec((None, tm, c), lambda l, i: (l, i, 0))

    def body(w_ref, g_ref, m_ref, v_ref, d_ref, mo_ref, vo_ref):
        gg = g_ref[...]
        mm = B1 * m_ref[...] + (1.0 - B1) * gg
        vv = B2 * v_ref[...] + (1.0 - B2) * (gg * gg)
        mo_ref[...] = mm
        vo_ref[...] = vv
        m_hat = mm / (1.0 - B1 ** STEP)
        v_hat = vv / (1.0 - B2 ** STEP)
        d_ref[...] = -LR * (m_hat / (jnp.sqrt(v_hat) + AEPS) + WD * w_ref[...])

    return _pc(body, name=name, grid=(nl, r // tm), in_specs=[blk] * 4, out_specs=[blk] * 3,
               out_shape=[_sds((nl, r, c), F32)] * 3)(w, g, m, v)


WEIGHTS = ["ssm_norm", "ssm_w_in", "ssm_conv_w", "ssm_conv_b", "ssm_dt_bias", "ssm_a_log", "ssm_d", "ssm_gate_norm",
           "ssm_w_out", "cv_norm", "cv_w_pw1", "cv_b_pw1", "cv_dw_w", "cv_dw_b", "cv_ln_g", "cv_ln_b", "cv_w_pw2",
           "cv_b_pw2", "ffn_norm", "ffn_w_gate", "ffn_w_up", "ffn_w_down", "final_norm"]
TRANSPOSED = ("ffn_w_gate", "ffn_w_up")
ADAMW_TRANSPOSED = ("ssm_w_in",) + TRANSPOSED
BIG = {"ssm_w_in": "col", "ssm_w_out": "row", "cv_w_pw1": "col", "cv_w_pw2": "row",
       "ffn_w_gate": "row", "ffn_w_up": "row", "ffn_w_down": "row"}
SMALL_SHARDED = {"ssm_conv_w": 2, "cv_norm": 1, "cv_b_pw1": 1, "cv_dw_w": 2, "cv_dw_b": 1, "cv_ln_g": 1, "cv_ln_b": 1,
                 "cv_b_pw2": 1}
SMALL = [n for n in WEIGHTS if n not in BIG]
LANES = 128


def _pack(arrays):
    flat = jnp.concatenate([a.reshape(-1) for a in arrays])
    rows = -(-flat.size // (8 * LANES)) * 8
    return jnp.pad(flat, (0, rows * LANES - flat.size)).reshape(rows, LANES)


def _unpack(packed, shapes):
    flat, out, off = packed.reshape(-1), [], 0
    for s in shapes:
        n = 1
        for d_ in s:
            n *= d_
        out.append(flat[off:off + n].reshape(s))
        off += n
    return out


def _unshard(parts, axis):
    return jnp.concatenate(parts, axis=axis)


def kernel(x, ssm_norm, ssm_w_in, ssm_conv_w, ssm_conv_b, ssm_dt_bias, ssm_a_log, ssm_d, ssm_gate_norm, ssm_w_out, cv_norm, cv_w_pw1, cv_b_pw1, cv_dw_w, cv_dw_b, cv_ln_g, cv_ln_b, cv_w_pw2, cv_b_pw2, ffn_norm, ffn_w_gate, ffn_w_up, ffn_w_down, final_norm, loss_target, m_ssm_norm, m_ssm_w_in, m_ssm_conv_w, m_ssm_conv_b, m_ssm_dt_bias, m_ssm_a_log, m_ssm_d, m_ssm_gate_norm, m_ssm_w_out, m_cv_norm, m_cv_w_pw1, m_cv_b_pw1, m_cv_dw_w, m_cv_dw_b, m_cv_ln_g, m_cv_ln_b, m_cv_w_pw2, m_cv_b_pw2, m_ffn_norm, m_ffn_w_gate, m_ffn_w_up, m_ffn_w_down, m_final_norm, v_ssm_norm, v_ssm_w_in, v_ssm_conv_w, v_ssm_conv_b, v_ssm_dt_bias, v_ssm_a_log, v_ssm_d, v_ssm_gate_norm, v_ssm_w_out, v_cv_norm, v_cv_w_pw1, v_cv_b_pw1, v_cv_dw_w, v_cv_dw_b, v_cv_ln_g, v_cv_ln_b, v_cv_w_pw2, v_cv_b_pw2, v_ffn_norm, v_ffn_w_gate, v_ffn_w_up, v_ffn_w_down, v_final_norm):
    a = dict(locals())
    chip = 2 * lax.axis_index("x") + lax.axis_index("y")
    n_ssm, n_cv, depth = ssm_norm.shape[0], cv_norm.shape[0], ffn_norm.shape[0]

    def own(n, layer):
        w = a[n][layer].astype(BF16)
        return w.T if n in TRANSPOSED else w

    sh_names = list(SMALL_SHARDED)
    got = _gather8(_pack([a[n] for n in sh_names]), reduce=False)
    got = got.reshape(8, -1)[0::2]
    per_chip = [_unpack(got[s], [a[n].shape for n in sh_names]) for s in range(4)]
    full = {n: a[n] for n in SMALL}
    for k, n in enumerate(sh_names):
        full[n] = _unshard([per_chip[s][k] for s in range(4)], SMALL_SHARDED[n])

    def matrices_of(i, part="layer"):
        mixer = [(n, i // 2) for n in (("ssm_w_in", "ssm_w_out") if i % 2 == 0 else ("cv_w_pw1", "cv_w_pw2"))]
        ffn_part = [(n, i) for n in ("ffn_w_gate", "ffn_w_up", "ffn_w_down")]
        return {"mixer": mixer, "ffn": ffn_part, "layer": mixer + ffn_part}[part]

    first = matrices_of(0, "mixer")
    stages = [first[:1], first[1:] + matrices_of(0, "ffn")] + [matrices_of(i) for i in range(1, depth)]
    in_flight, landed = {}, {}

    def matrix(n, layer, h):
        if not in_flight and not landed:
            dep = h
            for k, stage in enumerate(stages):
                in_flight[k] = _gather_start([own(m, l) for m, l in stage], dep, str(k))
                dep = in_flight[k][-1]
            in_flight["token"] = dep[0, 0]
        if (n, layer) not in landed:
            k = next(k for k, st in enumerate(stages) if (n, layer) in st)
            ssem, rsem, srcs, lands, _ = in_flight.pop(k)
            srcs, lands = _gather_wait(ssem, rsem, srcs, lands, h, str(k))
            lands = _gather_forward(lands, str(k))
            for key, src, land in zip(stages[k], srcs, lands):
                landed[key] = (src, land, in_flight["token"])
        return landed[(n, layer)]

    def whole(h, *keys):
        got = [matrix(n, l, h) for n, l in keys]
        if BIG[keys[0][0]] == "row" and len(keys) == 1:
            src, land, token = got[0]
            return lax.dynamic_update_index_in_dim(land, src, chip, 0).reshape(-1, land.shape[-1]), token
        parts = [jnp.where(chip == s, src, land[s]) for src, land, _ in got for s in range(4)]
        return jnp.concatenate(parts, axis=1 if BIG[keys[0][0]] == "col" else 0), got[0][2]

    def mixer_weights(i, h):
        j = i // 2
        if i % 2 == 0:
            w_in, token = whole(h, ("ssm_w_in", j))
            return dict(norm=full["ssm_norm"][j] + token, ffn_norm=full["ffn_norm"][i], wz=w_in[:, :DI], wxbc=w_in[:, DI:DI + CD],
                        wdt=jnp.pad(w_in[:, DI + CD:], ((0, 0), (0, NHP - NH))),
                        conv_w=full["ssm_conv_w"][j], conv_b=full["ssm_conv_b"][j], dt_bias=full["ssm_dt_bias"][j],
                        a_log=full["ssm_a_log"][j], d=full["ssm_d"][j], gate_norm=full["ssm_gate_norm"][j],
                        wout=lambda act: whole(act, ("ssm_w_out", j))[0])
        (w_pw1, token), (w_pw2, _) = whole(h, ("cv_w_pw1", j)), whole(h, ("cv_w_pw2", j))
        return dict(norm=full["cv_norm"][j] + token, ffn_norm=full["ffn_norm"][i], wpw1=w_pw1, b_pw1=full["cv_b_pw1"][j], dw_w=full["cv_dw_w"][j],
                    dw_b=full["cv_dw_b"][j], ln_g=full["cv_ln_g"][j], ln_b=full["cv_ln_b"][j], wpw2=w_pw2,
                    b_pw2=full["cv_b_pw2"][j])

    def ffn_weights(i, h):
        (wg_t, token), (wu_t, _), (wd, _) = (whole(h, (n, i)) for n in ("ffn_w_gate", "ffn_w_up", "ffn_w_down"))
        after = None if i + 1 == depth else (full["ssm_norm"] if i % 2 else full["cv_norm"])[(i + 1) // 2]
        return dict(norm=full["ffn_norm"][i] + token, next_norm=after, wg_t=wg_t, wu_t=wu_t, wd=wd)

    core = lax.axis_index("c")
    reducing, seen = {}, {}

    def reduce_small(g_final):
        g_mixers, g_ffns = [seen[i][0] for i in range(depth)], [seen[i][1] for i in range(depth)]
        local_small = {"final_norm": g_final, "ffn_norm": jnp.stack([g["norm"] for g in g_ffns])}
        for n in SMALL:
            if n.startswith("ssm_"):
                local_small[n] = jnp.stack([g[n[4:]] for g in g_mixers[0::2]])
            elif n.startswith("cv_"):
                local_small[n] = jnp.stack([g[n[3:]] for g in g_mixers[1::2]])
        return _gather8(_pack([local_small[n] for n in SMALL]), reduce=True)

    started = []

    def start_reduce(i, part, g_by_name, dep=None):
        slots = []
        for n, _ in matrices_of(i, part):
            g = g_by_name(n)
            if BIG[n] == "col":
                slots.append(g.reshape(g.shape[0], 4, g.shape[1] // 4).transpose(1, 0, 2))
            else:
                slots.append(g.reshape(4, g.shape[0] // 4, g.shape[1]))
        reducing[(i, part)], reducing["token"] = _reduce_start(slots, core, f"{i}{part[0]}", dep)
        started.append((i, part))
        return reducing["token"]

    def ffn_done(i, g_ffn_i):
        return start_reduce(0, "ffn", lambda n: g_ffn_i[n.split("_", 1)[1]]) if i == 0 else None

    def layer_done(i, g_mixer, g_ffn_i, g_final):
        seen[i] = (g_mixer, g_ffn_i)
        pick = lambda n: (g_ffn_i if n.startswith("ffn_") else g_mixer)[n.split("_", 1)[1]]
        if i > 0:
            return start_reduce(i, "layer", pick)
        reducing["small"] = reduce_small(g_final)
        return start_reduce(0, "mixer", pick, reducing["small"])

    lsum, grad_x, g_mix, g_ffn, g_final = _fwd_bwd(x[0], loss_target[0], depth, mixer_weights, ffn_weights,
                                                   full["final_norm"], layer_done, ffn_done)
    loss = (0.5 / D) * lax.psum(jnp.sum(lsum), ("x", "y", "c"))

    grads = {}
    for n, g in zip(SMALL, _unpack(reducing.pop("small"), [full[n].shape for n in SMALL])):
        if n in SMALL_SHARDED:
            ax = SMALL_SHARDED[n]
            g = lax.dynamic_slice_in_dim(g, chip * a[n].shape[ax], a[n].shape[ax], axis=ax)
        grads[n] = g
    last_start = reducing.pop("token")

    per_layer, after = {n: [None] * a[n].shape[0] for n in BIG}, last_start
    for i, part in started:
        reducing[(i, part)] = _reduce_finish(reducing[(i, part)], after, chip, f"{i}{part[0]}")
        after = reducing[(i, part)][2][0]
    for i, part in started:
        done = _reduce_collect(reducing.pop((i, part)), after, core, f"{i}{part[0]}")
        for (n, l), g in zip(matrices_of(i, part), done):
            per_layer[n][l] = g
    grads_t = {}
    for n in BIG:
        if n in TRANSPOSED:
            grads_t[n] = jnp.stack(per_layer[n])
        else:
            grads[n] = jnp.stack(per_layer[n])

    delta, new_m, new_v = {}, {}, {}
    for n in BIG:
        if n in ADAMW_TRANSPOSED:
            g_t = grads_t[n] if n in grads_t else grads[n].transpose(0, 2, 1)
            outs = _adamw(a[n].transpose(0, 2, 1), g_t, a["m_" + n].transpose(0, 2, 1), a["v_" + n].transpose(0, 2, 1),
                          name="adamw_" + n)
            grads[n] = g_t.transpose(0, 2, 1)
            delta[n], new_m[n], new_v[n] = [o.transpose(0, 2, 1) for o in outs]
        else:
            delta[n], new_m[n], new_v[n] = _adamw(a[n], grads[n], a["m_" + n], a["v_" + n], name="adamw_" + n)
    shapes = [a[n].shape for n in SMALL]
    upd = _adamw(_pack([a[n] for n in SMALL]), _pack([grads[n] for n in SMALL]), _pack([a["m_" + n] for n in SMALL]),
                 _pack([a["v_" + n] for n in SMALL]), name="adamw_small")
    for dst, packed in zip((delta, new_m, new_v), upd):
        dst.update(zip(SMALL, _unpack(packed, shapes)))

    return (loss, grad_x[None], *[grads[n] for n in WEIGHTS], *[delta[n] for n in WEIGHTS],
            *[new_m[n] for n in WEIGHTS], *[new_v[n] for n in WEIGHTS])
```

```python
import jax
import jax.numpy as jnp
from jax import lax
from jax.experimental import pallas as pl
from jax.experimental.pallas import tpu as pltpu

F32 = jnp.float32
BF16 = jnp.bfloat16
MESH = pl.DeviceIdType.MESH

D = 1024
DI = 2048
HD = 64
NH = 32
NG = 4
GW = DI // NG
NS = 128
KC = 4
CD = DI + 2 * NG * NS
Q = 128
DFF = 2816
CK = 31
EPS = 1e-5
NHP = 128

LR, B1, B2, AEPS, WD, STEP = 0.001, 0.9, 0.999, 1e-08, 0.01, 10

VMEM_LIMIT = 56 * 1024 * 1024
MM_VMEM_BUDGET = 36 * 1024 * 1024


def _pc(body, *, name, grid, in_specs, out_specs, out_shape, scratch=()):
    return pl.pallas_call(
        body, name=name, grid=grid, in_specs=in_specs, out_specs=out_specs, out_shape=out_shape,
        scratch_shapes=list(scratch),
        compiler_params=pltpu.CompilerParams(dimension_semantics=("arbitrary",) * len(grid),
                                             vmem_limit_bytes=VMEM_LIMIT))


def _sds(shape, dtype):
    return jax.ShapeDtypeStruct(tuple(shape), dtype)


def _row(tm, c):
    return pl.BlockSpec((tm, c), lambda i: (i, 0))


def _const(shape):
    return pl.BlockSpec(tuple(shape), lambda i: (0,) * len(shape))


def _tile(n, cap):
    if n <= cap:
        return n
    best = 128
    for t in range(128, cap + 1, 128):
        if n % t == 0:
            best = t
    return best


def _sigmoid(x):
    return 1.0 / (1.0 + jnp.exp(-x))


def _dsilu(x):
    s = _sigmoid(x)
    return s * (1.0 + x * (1.0 - s))


def _mm(a, b, mode, *, name, out_dtype=F32, bias=None, res=None, tm=1024, after=None, norm_g=None):
    if mode == "nn":
        (m, k), (k2, n) = a.shape, b.shape
    elif mode == "nt":
        (m, k), (n, k2) = a.shape, b.shape
    else:
        (k, m), (k2, n) = a.shape, b.shape
    assert k == k2, (a.shape, b.shape, mode)
    tm = _tile(m, tm if k <= 2048 else 512)
    so = jnp.dtype(out_dtype).itemsize + (4 if res is not None else 0)
    fits = [c for c in range(128, min(n, 2048) + 1, 128) if n % c == 0 and
            2 * (tm * k * a.dtype.itemsize + c * k * b.dtype.itemsize + tm * c * so) <= MM_VMEM_BUDGET]
    tn = n if n <= 128 else max(fits)
    dn = {"nn": (((1,), (0,)), ((), ())), "nt": (((1,), (1,)), ((), ())), "tn": (((0,), (0,)), ((), ()))}[mode]
    nb, nr, nn = bias is not None, res is not None, norm_g is not None
    assert not nn or tn == n == D, (n, tn)

    def body(*refs):
        acc = lax.dot_general(refs[0][...].astype(BF16), refs[1][...].astype(BF16), dn, preferred_element_type=F32)
        if nb:
            acc = acc + refs[2][...]
        if nr:
            acc = acc + refs[2 + nb][...]
        if nn:
            r = lax.rsqrt(jnp.mean(acc * acc, axis=-1, keepdims=True) + EPS)
            refs[-1][...] = (acc * r * refs[2 + nb + nr][...]).astype(BF16)
            refs[-2][...] = acc.astype(out_dtype)
        else:
            refs[-1][...] = acc.astype(out_dtype)

    a_spec = pl.BlockSpec((k, tm), lambda i, j: (0, i)) if mode == "tn" else pl.BlockSpec((tm, k), lambda i, j: (i, 0))
    b_spec = pl.BlockSpec((tn, k), lambda i, j: (j, 0)) if mode == "nt" else pl.BlockSpec((k, tn), lambda i, j: (0, j))
    ins, specs = [a, b], [a_spec, b_spec]
    if nb:
        ins.append(bias)
        specs.append(pl.BlockSpec((1, tn), lambda i, j: (0, j)))
    if nr:
        ins.append(res)
        specs.append(pl.BlockSpec((tm, tn), lambda i, j: (i, j)))
    if nn:
        ins.append(norm_g.reshape(1, D))
        specs.append(pl.BlockSpec((1, D), lambda i, j: (0, 0)))
    if after is not None:
        ins.append(after)
        specs.append(pl.BlockSpec((8, 128), lambda i, j: (0, 0)))
    tile = pl.BlockSpec((tm, tn), lambda i, j: (i, j))
    if nn:
        return _pc(body, name=name, grid=(m // tm, n // tn), in_specs=specs, out_specs=[tile, tile],
                   out_shape=[_sds((m, n), out_dtype), _sds((m, n), BF16)])(*ins)
    return _pc(body, name=name, grid=(m // tm, n // tn), in_specs=specs, out_specs=tile, out_shape=_sds((m, n), out_dtype))(*ins)


def _rms_fwd(h, g, *, name):
    t = h.shape[0]
    tm = min(t, 512)

    def body(h_ref, g_ref, u_ref):
        x = h_ref[...]
        r = lax.rsqrt(jnp.mean(x * x, axis=-1, keepdims=True) + EPS)
        u_ref[...] = (x * r * g_ref[...]).astype(BF16)

    return _pc(body, name=name, grid=(t // tm,), in_specs=[_row(tm, D), _const((1, D))], out_specs=_row(tm, D),
               out_shape=_sds((t, D), BF16))(h, g.reshape(1, D))


def _final_loss(h, g, target, *, name):
    t = h.shape[0]
    tm = min(t, 512)

    def body(h_ref, g_ref, t_ref, dh_ref, dh16_ref, dg_ref, l_ref):
        x = h_ref[...]
        gg = g_ref[...]
        r = lax.rsqrt(jnp.mean(x * x, axis=-1, keepdims=True) + EPS)
        err = x * r * gg - t_ref[...]
        dy = err * (1.0 / D)
        dyg = dy * gg
        dot = jnp.mean(dyg * x, axis=-1, keepdims=True)
        dh = r * dyg - x * (r * r * r * dot)
        dh_ref[...] = dh
        dh16_ref[...] = dh.astype(BF16)

        @pl.when(pl.program_id(0) == 0)
        def _():
            dg_ref[...] = jnp.zeros_like(dg_ref)
            l_ref[...] = jnp.zeros_like(l_ref)

        dg_ref[...] += jnp.sum(dy * x * r, axis=0, keepdims=True)
        l_ref[...] += jnp.sum(err * err, axis=0, keepdims=True)

    dh, dh16, dg, l = _pc(body, name=name, grid=(t // tm,),
                          in_specs=[_row(tm, D), _const((1, D)), _row(tm, D)],
                          out_specs=[_row(tm, D), _row(tm, D), _const((1, D)), _const((1, D))],
                          out_shape=[_sds((t, D), F32), _sds((t, D), BF16), _sds((1, D), F32), _sds((1, D), F32)])(
                              h, g.reshape(1, D), target)
    return l, dh, dh16, dg.reshape(D)


NT = (((1,), (1,)), ((), ()))
FFN_COLS = 1408


def _ffn_up_act(u, wg_t, wu_t, *, name):
    t = u.shape[0]
    tm, tn = min(t, 512), FFN_COLS

    def body(u_ref, wg_ref, wu_ref, g_ref, up_ref, a_ref):
        uu = u_ref[...]
        gate = lax.dot_general(uu, wg_ref[...], NT, preferred_element_type=F32)
        up = lax.dot_general(uu, wu_ref[...], NT, preferred_element_type=F32)
        g_ref[...] = gate
        up_ref[...] = up
        a_ref[...] = (gate * _sigmoid(gate) * up).astype(BF16)

    wspec = pl.BlockSpec((tn, D), lambda i, j: (j, 0))
    tile = pl.BlockSpec((tm, tn), lambda i, j: (i, j))
    return _pc(body, name=name, grid=(t // tm, DFF // tn),
               in_specs=[pl.BlockSpec((tm, D), lambda i, j: (i, 0)), wspec, wspec], out_specs=[tile, tile, tile],
               out_shape=[_sds((t, DFF), F32), _sds((t, DFF), F32), _sds((t, DFF), BF16)])(u, wg_t, wu_t)


def _ffn_down_dx_act(dh2, wd, gate, up, *, name, after=None):
    t = dh2.shape[0]
    tm, tn = min(t, 512), FFN_COLS

    def body(d_ref, w_ref, g_ref, up_ref, *rest):
        dg_ref, du_ref = rest[-2:]
        da = lax.dot_general(d_ref[...].astype(BF16), w_ref[...], NT, preferred_element_type=F32)
        gg = g_ref[...]
        sg = _sigmoid(gg)
        dg_ref[...] = (da * up_ref[...] * (sg * (1.0 + gg * (1.0 - sg)))).astype(BF16)
        du_ref[...] = (da * gg * sg).astype(BF16)

    tile = pl.BlockSpec((tm, tn), lambda i, j: (i, j))
    ins = [dh2, wd, gate, up]
    specs = [pl.BlockSpec((tm, D), lambda i, j: (i, 0)), pl.BlockSpec((tn, D), lambda i, j: (j, 0)), tile, tile]
    if after is not None:
        ins.append(after)
        specs.append(pl.BlockSpec((8, 128), lambda i, j: (0, 0)))
    return _pc(body, name=name, grid=(t // tm, DFF // tn), in_specs=specs, out_specs=[tile, tile],
               out_shape=[_sds((t, DFF), BF16), _sds((t, DFF), BF16)])(*ins)


def _dx_rms_bwd(pairs, h, g, dres, *, name):
    t = h.shape[0]
    tm = min(t, 256)
    n = len(pairs)
    dns = [NT if mode == "nt" else (((1,), (0,)), ((), ())) for _, _, mode in pairs]

    def body(*refs):
        h_ref, g_ref, dres_ref, dh_ref, dh16_ref, dg_ref = refs[2 * n:]

        @pl.when(pl.program_id(0) == 0)
        def _():
            dg_ref[...] = jnp.zeros_like(dg_ref)

        dy = None
        for k in range(n):
            part = lax.dot_general(refs[k][...].astype(BF16), refs[n + k][...], dns[k], preferred_element_type=F32)
            dy = part if dy is None else dy + part
        x = h_ref[...]
        r = lax.rsqrt(jnp.mean(x * x, axis=-1, keepdims=True) + EPS)
        dyg = dy * g_ref[...]
        dot = jnp.mean(dyg * x, axis=-1, keepdims=True)
        dh = dres_ref[...] + r * dyg - x * (r * r * r * dot)
        dh_ref[...] = dh
        dh16_ref[...] = dh.astype(BF16)
        dg_ref[...] += jnp.sum(dy * x * r, axis=0, keepdims=True)

    a_specs = [_row(tm, a.shape[1]) for a, _, _ in pairs]
    b_specs = [_const(b.shape) for _, b, _ in pairs]
    dh, dh16, dg = _pc(body, name=name, grid=(t // tm,),
                       in_specs=a_specs + b_specs + [_row(tm, D), _const((1, D)), _row(tm, D)],
                       out_specs=[_row(tm, D), _row(tm, D), _const((1, D))],
                       out_shape=[_sds((t, D), F32), _sds((t, D), BF16), _sds((1, D), F32)])(
                           *[a for a, _, _ in pairs], *[b for _, b, _ in pairs], h, g.reshape(1, D), dres)
    return dh, dh16, dg.reshape(D)


ROWS = 8


def _halo_before(tm, halo, tc):
    return pl.BlockSpec((halo, tc), lambda j, i: (jnp.maximum(i * (tm // halo) - 1, 0), j))


def _halo_after(tm, halo, tc, t):
    return pl.BlockSpec((halo, tc), lambda j, i: (jnp.minimum((i + 1) * (tm // halo), t // halo - 1), j))


def _conv4_fwd(xp, w, b, *, name):
    t = xp.shape[0]
    tm, tc, halo = min(t, 512), 1024, 8

    def body(x_ref, h_ref, w_ref, b_ref, o_ref, pad):
        i = pl.program_id(1)
        pad[pl.ds(0, halo), :] = jnp.where(i == 0, 0.0, h_ref[...])
        pad[pl.ds(halo, tm), :] = x_ref[...]
        acc = jnp.zeros((tm, tc), F32) + b_ref[...]
        for k in range(KC):
            acc = acc + pad[pl.ds(halo - (KC - 1) + k, tm), :] * w_ref[pl.ds(k, 1), :]
        o_ref[...] = acc * _sigmoid(acc)

    tile = pl.BlockSpec((tm, tc), lambda j, i: (i, j))
    return _pc(body, name=name, grid=(CD // tc, t // tm),
               in_specs=[tile, _halo_before(tm, halo, tc), pl.BlockSpec((KC, tc), lambda j, i: (0, j)),
                         pl.BlockSpec((1, tc), lambda j, i: (0, j))],
               out_specs=tile, out_shape=_sds((t, CD), F32), scratch=[pltpu.VMEM((halo + tm, tc), F32)])(xp, xp, w, b.reshape(1, CD))


def _conv4_bwd_act(xp, w, b, dxs, dbm, dcm, *, name):
    t = xp.shape[0]
    tm, tc, halo = min(t, 512), 1024, 8
    nx = DI // tc
    assert CD // tc == nx + 1 and dbm.shape[1] + dcm.shape[1] == tc

    def body(x_ref, h_ref, w_ref, b_ref, dx_ref, dbm_ref, dcm_ref, dp_ref, dw_ref, db_ref, pad):
        i = pl.program_id(1)
        pad[pl.ds(0, halo), :] = jnp.where(i == 0, 0.0, h_ref[...])
        pad[pl.ds(halo, tm), :] = x_ref[...]
        acc = jnp.zeros((tm, tc), F32) + b_ref[...]
        for k in range(KC):
            acc = acc + pad[pl.ds(halo - (KC - 1) + k, tm), :] * w_ref[pl.ds(k, 1), :]
        dout = jnp.where(pl.program_id(0) < nx, dx_ref[...], jnp.concatenate([dbm_ref[...], dcm_ref[...]], axis=1))
        dpre = dout * _dsilu(acc)
        dp_ref[...] = dpre

        @pl.when(i == 0)
        def _():
            dw_ref[...] = jnp.zeros_like(dw_ref)
            db_ref[...] = jnp.zeros_like(db_ref)

        db_ref[...] += jnp.sum(dpre, axis=0, keepdims=True)
        for k in range(KC):
            dw_ref[pl.ds(k, 1), :] += jnp.sum(dpre * pad[pl.ds(halo - (KC - 1) + k, tm), :], axis=0, keepdims=True)

    tile = pl.BlockSpec((tm, tc), lambda j, i: (i, j))
    wspec = pl.BlockSpec((KC, tc), lambda j, i: (0, j))
    bspec = pl.BlockSpec((1, tc), lambda j, i: (0, j))
    dp, dw, db = _pc(body, name=name, grid=(CD // tc, t // tm),
                     in_specs=[tile, _halo_before(tm, halo, tc), wspec, bspec,
                               pl.BlockSpec((tm, tc), lambda j, i: (i, jnp.minimum(j, nx - 1))),
                               pl.BlockSpec((tm, tc // 2), lambda j, i: (i, 0)), pl.BlockSpec((tm, tc // 2), lambda j, i: (i, 0))],
                     out_specs=[tile, wspec, bspec],
                     out_shape=[_sds((t, CD), F32), _sds((KC, CD), F32), _sds((1, CD), F32)],
                     scratch=[pltpu.VMEM((halo + tm, tc), F32)])(xp, xp, w, b.reshape(1, CD), dxs, dbm, dcm)
    return dp, dw, db.reshape(CD)


def _conv_bwd_x(dy, w, kk, *, name, out_dtype):
    t, c = dy.shape
    halo = 8 if kk <= 8 else 32
    tm, tc = (min(t, 512), min(c, 1024)) if kk <= 8 else (min(t, 256), min(c, 512))
    rows = 2 * ROWS if kk > 8 else tm

    def body(y_ref, h_ref, w_ref, o_ref, pad):
        i = pl.program_id(1)
        pad[pl.ds(0, tm), :] = y_ref[...]
        pad[pl.ds(tm, halo), :] = jnp.where(i == t // tm - 1, 0.0, h_ref[...])
        for r in range(0, tm, rows):
            acc = pad[pl.ds(kk - 1 + r, rows), :] * w_ref[pl.ds(0, 1), :]
            for k in range(1, kk):
                acc = acc + pad[pl.ds(kk - 1 - k + r, rows), :] * w_ref[pl.ds(k, 1), :]
            o_ref[pl.ds(r, rows), :] = acc.astype(out_dtype)

    tile = pl.BlockSpec((tm, tc), lambda j, i: (i, j))
    return _pc(body, name=name, grid=(c // tc, t // tm),
               in_specs=[tile, _halo_after(tm, halo, tc, t), pl.BlockSpec((kk, tc), lambda j, i: (0, j))],
               out_specs=tile, out_shape=_sds((t, c), out_dtype), scratch=[pltpu.VMEM((tm + halo, tc), F32)])(dy, dy, w)


def _glu_fwd(p, *, name):
    t = p.shape[0]
    tm = min(t, 512)

    def body(p_ref, o_ref):
        o_ref[...] = p_ref[:, :D] * _sigmoid(p_ref[:, D:])

    return _pc(body, name=name, grid=(t // tm,), in_specs=[_row(tm, 2 * D)], out_specs=_row(tm, D),
               out_shape=_sds((t, D), F32))(p)


def _glu_bwd(p, dg, *, name):
    t = p.shape[0]
    tm = min(t, 512)

    def body(p_ref, dg_ref, dp_ref, db_ref):
        a = p_ref[:, :D]
        s = _sigmoid(p_ref[:, D:])
        d = dg_ref[...]
        da = d * s
        dgate = d * a * s * (1.0 - s)
        dp_ref[:, :D] = da.astype(BF16)
        dp_ref[:, D:] = dgate.astype(BF16)

        @pl.when(pl.program_id(0) == 0)
        def _():
            db_ref[...] = jnp.zeros_like(db_ref)

        db_ref[:, :D] += jnp.sum(da, axis=0, keepdims=True)
        db_ref[:, D:] += jnp.sum(dgate, axis=0, keepdims=True)

    dp, db = _pc(body, name=name, grid=(t // tm,), in_specs=[_row(tm, 2 * D), _row(tm, D)],
                 out_specs=[_row(tm, 2 * D), _const((1, 2 * D))],
                 out_shape=[_sds((t, 2 * D), BF16), _sds((1, 2 * D), F32)])(p, dg)
    return dp, db.reshape(2 * D)


def _dwconv_ln_fwd(g, w, b, lg, lb, *, name):
    t = g.shape[0]
    tm, halo = min(t, 256), 32

    def body(x_ref, h_ref, w_ref, b_ref, lg_ref, lb_ref, c_ref, s_ref, pad):
        i = pl.program_id(0)
        pad[pl.ds(0, halo), :] = jnp.where(i == 0, 0.0, h_ref[...])
        pad[pl.ds(halo, tm), :] = x_ref[...]
        acc = jnp.zeros((tm, D), F32) + b_ref[...]
        for k in range(CK):
            acc = acc + pad[pl.ds(halo - (CK - 1) + k, tm), :] * w_ref[pl.ds(k, 1), :]
        c_ref[...] = acc
        mu = jnp.mean(acc, axis=-1, keepdims=True)
        xc = acc - mu
        r = lax.rsqrt(jnp.mean(xc * xc, axis=-1, keepdims=True) + EPS)
        n = xc * r * lg_ref[...] + lb_ref[...]
        s_ref[...] = (n * _sigmoid(n)).astype(BF16)

    c, s = _pc(body, name=name, grid=(t // tm,),
               in_specs=[_row(tm, D), pl.BlockSpec((halo, D), lambda i: (jnp.maximum(i * (tm // halo) - 1, 0), 0)),
                         _const((CK, D)), _const((1, D)), _const((1, D)), _const((1, D))],
               out_specs=[_row(tm, D), _row(tm, D)], out_shape=[_sds((t, D), F32), _sds((t, D), BF16)],
               scratch=[pltpu.VMEM((halo + tm, D), F32)])(g, g, w, b.reshape(1, D), lg.reshape(1, D), lb.reshape(1, D))
    return c, s


def _ln_silu_bwd(c, lg, lb, ds, *, name):
    t = c.shape[0]
    tm = min(t, 512)

    def body(c_ref, lg_ref, lb_ref, ds_ref, dc_ref, dlg_ref, dlb_ref):
        x = c_ref[...]
        mu = jnp.mean(x, axis=-1, keepdims=True)
        xc = x - mu
        r = lax.rsqrt(jnp.mean(xc * xc, axis=-1, keepdims=True) + EPS)
        xh = xc * r
        n = xh * lg_ref[...] + lb_ref[...]
        dn = ds_ref[...] * _dsilu(n)
        dxh = dn * lg_ref[...]
        m1 = jnp.mean(dxh, axis=-1, keepdims=True)
        m2 = jnp.mean(dxh * xh, axis=-1, keepdims=True)
        dc_ref[...] = r * (dxh - m1 - xh * m2)

        @pl.when(pl.program_id(0) == 0)
        def _():
            dlg_ref[...] = jnp.zeros_like(dlg_ref)
            dlb_ref[...] = jnp.zeros_like(dlb_ref)

        dlg_ref[...] += jnp.sum(dn * xh, axis=0, keepdims=True)
        dlb_ref[...] += jnp.sum(dn, axis=0, keepdims=True)

    dc, dlg, dlb = _pc(body, name=name, grid=(t // tm,),
                       in_specs=[_row(tm, D), _const((1, D)), _const((1, D)), _row(tm, D)],
                       out_specs=[_row(tm, D), _const((1, D)), _const((1, D))],
                       out_shape=[_sds((t, D), F32), _sds((1, D), F32), _sds((1, D), F32)])(
                           c, lg.reshape(1, D), lb.reshape(1, D), ds)
    return dc, dlg.reshape(D), dlb.reshape(D)


def _dwconv_bwd_w(g, dc, *, name):
    t = g.shape[0]
    tm, tc, halo = min(t, 256), 512, 32

    def body(x_ref, h_ref, dc_ref, dw_ref, db_ref, pad):
        i = pl.program_id(1)
        pad[pl.ds(0, halo), :] = jnp.where(i == 0, 0.0, h_ref[...])
        pad[pl.ds(halo, tm), :] = x_ref[...]
        @pl.when(i == 0)
        def _():
            dw_ref[...] = jnp.zeros_like(dw_ref)
            db_ref[...] = jnp.zeros_like(db_ref)

        d = dc_ref[...]
        db_ref[...] += jnp.sum(d, axis=0, keepdims=True)
        for k in range(CK):
            dw_ref[pl.ds(k, 1), :] += jnp.sum(d * pad[pl.ds(halo - (CK - 1) + k, tm), :], axis=0, keepdims=True)

    tile = pl.BlockSpec((tm, tc), lambda j, i: (i, j))
    dw, db = _pc(body, name=name, grid=(D // tc, t // tm),
                 in_specs=[tile, _halo_before(tm, halo, tc), tile],
                 out_specs=[pl.BlockSpec((CK, tc), lambda j, i: (0, j)), pl.BlockSpec((1, tc), lambda j, i: (0, j))],
                 out_shape=[_sds((CK, D), F32), _sds((1, D), F32)],
                 scratch=[pltpu.VMEM((halo + tm, tc), F32)])(g, g, dc)
    return dw, db.reshape(D)


def _colsum(x, *, name):
    t, c = x.shape
    tm = min(t, 512)

    def body(x_ref, o_ref):
        @pl.when(pl.program_id(0) == 0)
        def _():
            o_ref[...] = jnp.zeros_like(o_ref)

        o_ref[...] += jnp.sum(x_ref[...], axis=0, keepdims=True)

    return _pc(body, name=name, grid=(t // tm,), in_specs=[_row(tm, c)], out_specs=_const((1, c)),
               out_shape=_sds((1, c), F32))(x).reshape(c)


def _head_expand(n_lanes, first_head=0):
    h = lax.broadcasted_iota(jnp.int32, (NHP, n_lanes), 0)
    j = lax.broadcasted_iota(jnp.int32, (NHP, n_lanes), 1)
    return (h == first_head + j // HD).astype(F32)


def _softplus(v):
    return jnp.maximum(v, 0.0) + jnp.log(1.0 + jnp.exp(-jnp.abs(v)))


def _chunk_tri(n, upper):
    ri = lax.broadcasted_iota(jnp.int32, (n, n), 0)
    ci = lax.broadcasted_iota(jnp.int32, (n, n), 1)
    keep = (ri <= ci) if upper else (ri >= ci)
    return (keep & (ri // Q == ci // Q)).astype(F32)


def _split3(x):
    hi = x.astype(BF16)
    r1 = x - hi.astype(F32)
    mid = r1.astype(BF16)
    return hi, mid, (r1 - mid.astype(F32)).astype(BF16)


def _dot_sel(x, sel, sel_left=False):
    s = sel.astype(BF16)
    parts = [jnp.dot(s, t, preferred_element_type=F32) if sel_left else jnp.dot(t, s, preferred_element_type=F32)
             for t in _split3(x)]
    return (parts[2] + parts[1]) + parts[0]


def _ssd_pre(dtraw, dtb, alog, *, name):
    t = dtraw.shape[0]
    tm = min(t, 256)

    def body(r_ref, b_ref, al_ref, dtx_ref, acsx_ref, acst_ref):
        dt = _softplus(r_ref[...] + b_ref[...])
        da = dt * (-jnp.exp(al_ref[...]))
        acs = _dot_sel(da, _chunk_tri(tm, False), sel_left=True)
        expand = _head_expand(DI)
        dtx_ref[...] = _dot_sel(dt, expand)
        acsx_ref[...] = _dot_sel(acs, expand)
        acst_ref[...] = acs.T

    return _pc(body, name=name, grid=(t // tm,), in_specs=[_row(tm, NHP), _const((1, NHP)), _const((1, NHP))],
               out_specs=[_row(tm, DI), _row(tm, DI), pl.BlockSpec((NHP, tm), lambda i: (0, i))],
               out_shape=[_sds((t, DI), F32), _sds((t, DI), F32), _sds((NHP, t), F32)])(dtraw, dtb, alog)


def _ssd_post(dacs, dxdt, dtraw, dtb, alog, *, name):
    t = dtraw.shape[0]
    tm = min(t, 256)

    def body(g1_ref, g2_ref, r_ref, b_ref, al_ref, o_ref, db_ref, da_ref):
        h = lax.broadcasted_iota(jnp.int32, (DI, NHP), 1)
        j = lax.broadcasted_iota(jnp.int32, (DI, NHP), 0)
        red = (h == j // HD).astype(F32)
        dda = _dot_sel(_dot_sel(g1_ref[...], red), _chunk_tri(tm, True), sel_left=True)
        v = r_ref[...] + b_ref[...]
        ddt = dda * (-jnp.exp(al_ref[...])) + _dot_sel(g2_ref[...], red)
        draw = ddt * _sigmoid(v)
        o_ref[...] = draw.astype(BF16)

        @pl.when(pl.program_id(0) == 0)
        def _():
            db_ref[...] = jnp.zeros_like(db_ref)
            da_ref[...] = jnp.zeros_like(da_ref)

        db_ref[...] += jnp.sum(draw, axis=0, keepdims=True)
        da_ref[...] += jnp.sum(dda * _softplus(v), axis=0, keepdims=True)

    return _pc(body, name=name, grid=(t // tm,),
               in_specs=[_row(tm, DI), _row(tm, DI), _row(tm, NHP), _const((1, NHP)), _const((1, NHP))],
               out_specs=[_row(tm, NHP), _const((1, NHP)), _const((1, NHP))],
               out_shape=[_sds((t, NHP), BF16), _sds((1, NHP), F32), _sds((1, NHP), F32)])(dacs, dxdt, dtraw, dtb, alog)


def _decay_pair(acs_x, acst_ref, p, tri):
    lane = lax.broadcasted_iota(jnp.int32, (Q, 2 * HD), 1)
    v = acs_x[:, 2 * HD * p:2 * HD * (p + 1)]
    swapped = pltpu.roll(v, HD, axis=1)
    out = []
    for hh in range(2):
        colb = jnp.where((lane < HD) if hh == 0 else (lane >= HD), v, swapped)
        rowb = acst_ref[pl.ds(2 * p + hh, 1), :]
        out.append(jnp.exp(jnp.where(tri, colb - rowb, -1e30)))
    return out


def _ssd_fwd(xbc, dtx, acs_x, acs_t, d_x, *, name):
    t = xbc.shape[0]
    nc = t // Q
    hpg = NH // NG

    def body(x_ref, b_ref, c_ref, dtx_ref, acsx_ref, acst_ref, d_ref, y_ref, sin_ref, state):
        @pl.when(pl.program_id(1) == 0)
        def _():
            state[...] = jnp.zeros_like(state)

        xs = x_ref[...]
        tri = lax.broadcasted_iota(jnp.int32, (Q, Q), 0) >= lax.broadcasted_iota(jnp.int32, (Q, Q), 1)
        acs_x = acsx_ref[...]
        atot_x = acsx_ref[pl.ds(Q - 1, 1), :]
        xd = xs * dtx_ref[...]
        xb = xd.astype(BF16)
        bb = b_ref[...]
        cb16 = c_ref[...].astype(BF16)
        cbm = lax.dot_general(cb16, bb.astype(BF16), (((1,), (1,)), ((), ())), preferred_element_type=F32)
        s_in = state[...]
        sin_ref[0] = s_in
        y = jnp.dot(cb16, s_in.astype(BF16), preferred_element_type=F32) * jnp.exp(acs_x) + xs * d_ref[...]
        lane = lax.broadcasted_iota(jnp.int32, (Q, 2 * HD), 1)
        for p in range(hpg // 2):
            xp = xb[:, 2 * HD * p:2 * HD * (p + 1)]
            yp = jnp.zeros((Q, 2 * HD), F32)
            for hh, lm in enumerate(_decay_pair(acs_x, acst_ref, p, tri)):
                m = (cbm * lm).astype(BF16)
                xm = jnp.where((lane >= HD) if hh == 1 else (lane < HD), xp, jnp.zeros_like(xp))
                yp = yp + jnp.dot(m, xm, preferred_element_type=F32)
            y_ref[:, 2 * HD * p:2 * HD * (p + 1)] = y[:, 2 * HD * p:2 * HD * (p + 1)] + yp
        dec = jnp.exp(atot_x - acs_x)
        s_new = jnp.dot(bb.T.astype(BF16), (xd * dec).astype(BF16), preferred_element_type=F32)
        state[...] = jnp.exp(atot_x) * s_in + s_new

    grp = lambda g, c: (c, g)
    return _pc(body, name=name, grid=(NG, nc),
               in_specs=[pl.BlockSpec((Q, GW), grp),
                         pl.BlockSpec((Q, NS), lambda g, c: (c, DI // NS + g)),
                         pl.BlockSpec((Q, NS), lambda g, c: (c, DI // NS + NG + g)),
                         pl.BlockSpec((Q, GW), grp),
                         pl.BlockSpec((Q, GW), grp),
                         pl.BlockSpec((hpg, Q), lambda g, c: (g, c)),
                         pl.BlockSpec((1, GW), lambda g, c: (0, g))],
               out_specs=[pl.BlockSpec((Q, GW), grp), pl.BlockSpec((1, NS, GW), lambda g, c: (c, 0, g))],
               out_shape=[_sds((t, DI), F32), _sds((nc, NS, DI), F32)],
               scratch=[pltpu.VMEM((NS, GW), F32)])(xbc, xbc, xbc, dtx, acs_x, acs_t, d_x)


def _ssd_bwd(xbc, dtx, acs_x, acs_t, d_x, sin, dy, *, name):
    t = xbc.shape[0]
    nc = t // Q
    hpg = NH // NG

    def body(x_ref, b_ref, c_ref, dtx_ref, acsx_ref, acst_ref, d_ref, sin_ref, dy_ref,
             dx_ref, db_ref, dc_ref, dacs_ref, dxdt_ref, ddx_ref, dstate):
        @pl.when(pl.program_id(1) == 0)
        def _():
            dstate[...] = jnp.zeros_like(dstate)
            ddx_ref[...] = jnp.zeros_like(ddx_ref)

        xs = x_ref[...]
        dtxv = dtx_ref[...]
        gy = dy_ref[...]
        tri = lax.broadcasted_iota(jnp.int32, (Q, Q), 0) >= lax.broadcasted_iota(jnp.int32, (Q, Q), 1)
        acs_x = acsx_ref[...]
        atot_x = acsx_ref[pl.ds(Q - 1, 1), :]
        xd = xs * dtxv
        xb = xd.astype(BF16)
        gb = gy.astype(BF16)
        bb = b_ref[...]
        cc = c_ref[...]
        bb16, cc16 = bb.astype(BF16), cc.astype(BF16)
        cbm = lax.dot_general(cc16, bb16, (((1,), (1,)), ((), ())), preferred_element_type=F32)
        s_in = sin_ref[0]
        s_in16 = s_in.astype(BF16)
        ds_next = dstate[...]
        ds16 = ds_next.astype(BF16)
        ecs = jnp.exp(acs_x)
        dec = jnp.exp(atot_x - acs_x)
        etot = jnp.exp(atot_x)

        ddx_ref[...] += jnp.sum(gy * xs, axis=0, keepdims=True)
        z = jnp.dot(cc16, s_in16, preferred_element_type=F32)
        dz16 = (gy * ecs).astype(BF16)
        dacs = gy * z * ecs
        dcm = lax.dot_general(dz16, s_in16, (((1,), (1,)), ((), ())), preferred_element_type=F32)
        ds_here = jnp.dot(cc.T.astype(BF16), dz16, preferred_element_type=F32)
        u = jnp.dot(bb16, ds16, preferred_element_type=F32)
        dxd = u * dec
        w_dec = u * xd * dec
        dacs = dacs - w_dec
        datot = jnp.sum(w_dec, axis=0, keepdims=True) + jnp.sum(etot * ds_next * s_in, axis=0, keepdims=True)
        dbm = lax.dot_general((xd * dec).astype(BF16), ds16, (((1,), (1,)), ((), ())), preferred_element_type=F32)
        dstate[...] = etot * ds_next + ds_here

        lane = lax.broadcasted_iota(jnp.int32, (Q, 2 * HD), 1)
        lane_x = lax.broadcasted_iota(jnp.int32, (1, GW), 1)
        dcb = jnp.zeros((Q, Q), F32)
        dx_pairs = []
        for p in range(hpg // 2):
            sl = slice(2 * HD * p, 2 * HD * (p + 1))
            xp = xb[:, sl]
            gp = gb[:, sl]
            dxp = jnp.zeros((Q, 2 * HD), F32)
            for hh, lm in enumerate(_decay_pair(acs_x, acst_ref, p, tri)):
                r = 2 * p + hh
                mm = cbm * lm
                half = (lane >= HD) if hh == 1 else (lane < HD)
                xm = jnp.where(half, xp, jnp.zeros_like(xp))
                gm = jnp.where(half, gp, jnp.zeros_like(gp))
                mt16 = mm.T.astype(BF16)
                dxp = dxp + jnp.dot(mt16, gm, preferred_element_type=F32)
                dm = lax.dot_general(gm, xp, (((1,), (1,)), ((), ())), preferred_element_type=F32)
                dmt = lax.dot_general(xm, gp, (((1,), (1,)), ((), ())), preferred_element_type=F32)
                dcb = dcb + dm * lm
                wdiff = jnp.sum(dm * mm - dmt * mm.T, axis=1, keepdims=True)
                dacs = dacs + wdiff * (lane_x == HD * r).astype(F32)
            dx_pairs.append(dxp)
        dxd = dxd + jnp.concatenate(dx_pairs, axis=1)
        dcm = dcm + jnp.dot(dcb.astype(BF16), bb16, preferred_element_type=F32)
        dbm = dbm + jnp.dot(dcb.T.astype(BF16), cc16, preferred_element_type=F32)
        db_ref[...] = dbm
        dc_ref[...] = dcm

        last = lax.broadcasted_iota(jnp.int32, (Q, GW), 0) == Q - 1
        dacs_ref[...] = dacs + jnp.where(last, datot, 0.0)
        dxdt_ref[...] = dxd * xs
        dx_ref[...] = dxd * dtxv + gy * d_ref[...]

    rev = lambda g, c: (nc - 1 - c, g)
    return _pc(body, name=name, grid=(NG, nc),
               in_specs=[pl.BlockSpec((Q, GW), rev),
                         pl.BlockSpec((Q, NS), lambda g, c: (nc - 1 - c, DI // NS + g)),
                         pl.BlockSpec((Q, NS), lambda g, c: (nc - 1 - c, DI // NS + NG + g)),
                         pl.BlockSpec((Q, GW), rev),
                         pl.BlockSpec((Q, GW), rev),
                         pl.BlockSpec((hpg, Q), lambda g, c: (g, nc - 1 - c)),
                         pl.BlockSpec((1, GW), lambda g, c: (0, g)),
                         pl.BlockSpec((1, NS, GW), lambda g, c: (nc - 1 - c, 0, g)),
                         pl.BlockSpec((Q, GW), rev)],
               out_specs=[pl.BlockSpec((Q, GW), rev), pl.BlockSpec((Q, NS), rev), pl.BlockSpec((Q, NS), rev),
                          pl.BlockSpec((Q, GW), rev), pl.BlockSpec((Q, GW), rev),
                          pl.BlockSpec((1, GW), lambda g, c: (0, g))],
               out_shape=[_sds((t, DI), F32), _sds((t, NG * NS), F32), _sds((t, NG * NS), F32), _sds((t, DI), F32),
                          _sds((t, DI), F32), _sds((1, DI), F32)],
               scratch=[pltpu.VMEM((NS, GW), F32)])(xbc, xbc, xbc, dtx, acs_x, acs_t, d_x, sin, dy)


def _gated_norm_fwd(y, z, gn, *, name):
    t = y.shape[0]
    tm = min(t, 256)

    def body(y_ref, z_ref, g_ref, o_ref):
        for k in range(NG):
            sl = slice(GW * k, GW * (k + 1))
            zz = z_ref[:, sl]
            h = y_ref[:, sl] * (zz * _sigmoid(zz))
            r = lax.rsqrt(jnp.mean(h * h, axis=-1, keepdims=True) + EPS)
            o_ref[:, sl] = (h * r * g_ref[:, sl]).astype(BF16)

    return _pc(body, name=name, grid=(t // tm,), in_specs=[_row(tm, DI), _row(tm, DI), _const((1, DI))],
               out_specs=_row(tm, DI), out_shape=_sds((t, DI), BF16))(y, z, gn.reshape(1, DI))


def _gated_norm_bwd(y, z, gn, dout, *, name):
    t = y.shape[0]
    tm = min(t, 256)

    def body(y_ref, z_ref, g_ref, do_ref, dy_ref, dz_ref, dg_ref):
        @pl.when(pl.program_id(0) == 0)
        def _():
            dg_ref[...] = jnp.zeros_like(dg_ref)

        for k in range(NG):
            sl = slice(GW * k, GW * (k + 1))
            zz = z_ref[:, sl]
            yy = y_ref[:, sl]
            sz = zz * _sigmoid(zz)
            h = yy * sz
            r = lax.rsqrt(jnp.mean(h * h, axis=-1, keepdims=True) + EPS)
            d = do_ref[:, sl]
            dg_ref[:, sl] += jnp.sum(d * h * r, axis=0, keepdims=True)
            dgd = d * g_ref[:, sl]
            dot = jnp.mean(dgd * h, axis=-1, keepdims=True)
            dh = r * dgd - h * (r * r * r * dot)
            dy_ref[:, sl] = dh * sz
            dz_ref[:, sl] = (dh * yy * _dsilu(zz)).astype(BF16)

    dy, dz, dg = _pc(body, name=name, grid=(t // tm,),
                     in_specs=[_row(tm, DI), _row(tm, DI), _const((1, DI)), _row(tm, DI)],
                     out_specs=[_row(tm, DI), _row(tm, DI), _const((1, DI))],
                     out_shape=[_sds((t, DI), F32), _sds((t, DI), BF16), _sds((1, DI), F32)])(y, z, gn.reshape(1, DI), dout)
    return dy, dz, dg.reshape(DI)


def _pad_heads(v):
    return jnp.pad(v.reshape(1, NH), ((0, 0), (0, NHP - NH)))


def _residual_out(a, w, h, next_norm, name, bias=None):
    if next_norm is None:
        return _mm(a, w, "nn", bias=bias, res=h, name=name), None
    return _mm(a, w, "nn", bias=bias, res=h, norm_g=next_norm, name=name)


def _ssm_fwd(h, u, p, next_norm):
    z = _mm(u, p["wz"], "nn", name="ssm_in_z")
    xp = _mm(u, p["wxbc"], "nn", name="ssm_in_xbc")
    dtraw = _mm(u, p["wdt"], "nn", name="ssm_in_dt")
    xbc = _conv4_fwd(xp, p["conv_w"], p["conv_b"], name="ssm_conv_fwd")
    dtb, alog = _pad_heads(p["dt_bias"]), _pad_heads(p["a_log"])
    d_x = jnp.repeat(p["d"], HD).reshape(1, DI)
    dtx, acs_x, acs_t = _ssd_pre(dtraw, dtb, alog, name="ssd_pre")
    y, sin = _ssd_fwd(xbc, dtx, acs_x, acs_t, d_x, name="ssd_fwd")
    yn = _gated_norm_fwd(y, z, p["gate_norm"], name="ssm_gate_fwd")
    if callable(p["wout"]):
        p["wout"] = p["wout"](yn)
    h2, u2 = _residual_out(yn, p["wout"], h, next_norm, "ssm_out")
    return h2, u2, (h, u, z, xp, dtraw, xbc, dtx, acs_x, acs_t, d_x, dtb, alog, y, sin, yn)


def _ssm_bwd(dh2_pair, p, saved, after=None):
    dh2, dh2_16 = dh2_pair
    h, u, z, xp, dtraw, xbc, dtx, acs_x, acs_t, d_x, dtb, alog, y, sin, yn = saved
    dyn = _mm(dh2_16, p["wout"], "nt", name="ssm_out_dx", after=after)
    g = {"w_out": _mm(yn, dh2_16, "tn", out_dtype=BF16, name="ssm_out_dw")}
    dy, dz, g["gate_norm"] = _gated_norm_bwd(y, z, p["gate_norm"], dyn, name="ssm_gate_bwd")
    dxs, dbm, dcm, dacs, dxdt, ddx = _ssd_bwd(xbc, dtx, acs_x, acs_t, d_x, sin, dy, name="ssd_bwd")
    ddtraw, ddtb, dav = _ssd_post(dacs, dxdt, dtraw, dtb, alog, name="ssd_post")
    dpre, g["conv_w"], g["conv_b"] = _conv4_bwd_act(xp, p["conv_w"], p["conv_b"], dxs, dbm, dcm, name="ssm_conv_bwd_act")
    dxp = _conv_bwd_x(dpre, p["conv_w"], KC, name="ssm_conv_bwd_x", out_dtype=BF16)
    dh, dh16, g["norm"] = _dx_rms_bwd([(dz, p["wz"], "nt"), (dxp, p["wxbc"], "nt"), (ddtraw, p["wdt"], "nt")], h, p["norm"],
                                      dh2, name="ssm_in_dx_rms_bwd")
    g["w_in"] = jnp.concatenate([_mm(u, dz, "tn", out_dtype=BF16, name="ssm_in_dw_z"),
                                 _mm(u, dxp, "tn", out_dtype=BF16, name="ssm_in_dw_xbc"),
                                 _mm(u, ddtraw, "tn", out_dtype=BF16, name="ssm_in_dw_dt")[:, :NH]], axis=1)
    g["dt_bias"] = ddtb[0, :NH]
    g["a_log"] = dav[0, :NH] * (-jnp.exp(p["a_log"]))
    g["d"] = ddx.reshape(NH, HD).sum(-1)
    return (dh, dh16), g


def _cv_fwd(h, u, p, next_norm):
    pre = _mm(u, p["wpw1"], "nn", bias=p["b_pw1"].reshape(1, 2 * D), name="cv_pw1")
    gl = _glu_fwd(pre, name="cv_glu_fwd")
    c, s = _dwconv_ln_fwd(gl, p["dw_w"], p["dw_b"], p["ln_g"], p["ln_b"], name="cv_dwconv_ln_fwd")
    h2, u2 = _residual_out(s, p["wpw2"], h, next_norm, "cv_pw2", bias=p["b_pw2"].reshape(1, D))
    return h2, u2, (h, u, pre, gl, c, s)


def _cv_bwd(dh2_pair, p, saved, after=None):
    dh2, dh2_16 = dh2_pair
    h, u, pre, gl, c, s = saved
    ds = _mm(dh2_16, p["wpw2"], "nt", name="cv_pw2_dx", after=after)
    g = {"w_pw2": _mm(s, dh2_16, "tn", out_dtype=BF16, name="cv_pw2_dw"), "b_pw2": _colsum(dh2, name="cv_pw2_db")}
    dc, g["ln_g"], g["ln_b"] = _ln_silu_bwd(c, p["ln_g"], p["ln_b"], ds, name="cv_ln_bwd")
    dgl = _conv_bwd_x(dc, p["dw_w"], CK, name="cv_dwconv_bwd_x", out_dtype=F32)
    g["dw_w"], g["dw_b"] = _dwconv_bwd_w(gl, dc, name="cv_dwconv_bwd_w")
    dpre, g["b_pw1"] = _glu_bwd(pre, dgl, name="cv_glu_bwd")
    g["w_pw1"] = _mm(u, dpre, "tn", out_dtype=BF16, name="cv_pw1_dw")
    dh, dh16, g["norm"] = _dx_rms_bwd([(dpre, p["wpw1"], "nt")], h, p["norm"], dh2, name="cv_pw1_dx_rms_bwd")
    return (dh, dh16), g


def _ffn_fwd(h, u, p, next_norm):
    gate, up, a = _ffn_up_act(u, p["wg_t"], p["wu_t"], name="ffn_gate_up_act")
    h2, u2 = _residual_out(a, p["wd"], h, next_norm, "ffn_down")
    return h2, u2, (h, u, gate, up, a)


def _ffn_bwd(dh2_pair, p, saved, after=None):
    dh2, dh2_16 = dh2_pair
    h, u, gate, up, a = saved
    dgate, dup = _ffn_down_dx_act(dh2_16, p["wd"], gate, up, name="ffn_down_dx_act", after=after)
    g = {"w_down": _mm(a, dh2_16, "tn", out_dtype=BF16, name="ffn_down_dw"),
         "w_gate": _mm(dgate, u, "tn", out_dtype=BF16, name="ffn_gate_dw"),
         "w_up": _mm(dup, u, "tn", out_dtype=BF16, name="ffn_up_dw")}
    dh, dh16, g["norm"] = _dx_rms_bwd([(dgate, p["wg_t"], "nn"), (dup, p["wu_t"], "nn")], h, p["norm"], dh2,
                                      name="ffn_dx_rms_bwd")
    return (dh, dh16), g


def _fwd_bwd(x, target, depth, mixer_weights, ffn_weights, final_norm, layer_done=None, ffn_done=None):
    h, u, tape, weights = x, None, [], []
    for i in range(depth):
        mix_w = mixer_weights(i, h)
        if u is None:
            u = _rms_fwd(h, mix_w["norm"], name="first_rms_fwd")
        h, u, s_mix = (_ssm_fwd if i % 2 == 0 else _cv_fwd)(h, u, mix_w, mix_w["ffn_norm"])
        ffn_w = ffn_weights(i, h)
        h, u, s_ffn = _ffn_fwd(h, u, ffn_w, ffn_w["next_norm"])
        tape.append((s_mix, s_ffn))
        weights.append((mix_w, ffn_w))
    lsum, dh, dh16, g_final = _final_loss(h, final_norm, target, name="loss_head")
    dh = (dh, dh16)
    g_mix, g_ffn, token = [None] * depth, [None] * depth, None
    for i in reversed(range(depth)):
        s_mix, s_ffn = tape[i]
        dh, g_ffn[i] = _ffn_bwd(dh, weights[i][1], s_ffn, after=token)
        token = ffn_done(i, g_ffn[i]) if ffn_done is not None else None
        dh, g_mix[i] = (_ssm_bwd if i % 2 == 0 else _cv_bwd)(dh, weights[i][0], s_mix, after=token)
        if layer_done is not None:
            token = layer_done(i, g_mix[i], g_ffn[i], g_final)
    return lsum, dh[0], g_mix, g_ffn, g_final


ANY = pl.BlockSpec(memory_space=pl.ANY)


def _place():
    x, y, c = lax.axis_index("x"), lax.axis_index("y"), lax.axis_index("c")
    return x, y, c, [(1 - x, y), (x, 1 - y), (1 - x, 1 - y)]


def _remote(src, dst, ssem, rsem, dev):
    return pltpu.make_async_remote_copy(src_ref=src, dst_ref=dst, send_sem=ssem, recv_sem=rsem, device_id=dev,
                                        device_id_type=MESH)


HBM = pl.BlockSpec(memory_space=pltpu.HBM)
SEM = pl.BlockSpec(memory_space=pltpu.SEMAPHORE)
EFFECT = pltpu.SideEffectType.DATAFLOW_SIDE_EFFECTING


def _gather_copies(src, land, ssem, rsem):
    x, y, c, chips = _place()
    me, pairs = 2 * x + y, []
    for a in range(len(src)):
        h = src[a].shape[0] // 2
        rows = pl.ds(c * h, h)
        for j, (px, py) in enumerate(chips):
            sems = (ssem.at[3 * a + j], rsem.at[3 * a + j], (px, py, c))
            pairs.append((_remote(src[a].at[rows], land[a].at[me, rows], *sems),
                          _remote(src[a].at[rows], land[a].at[2 * px + py, rows], *sems)))
    return pairs


def _chip_copies(src, land, ssem, rsem):
    x, y, c, chips = _place()
    me, pairs = 2 * x + y, []
    for a in range(len(src)):
        for j, (px, py) in enumerate(chips):
            sems = (ssem.at[3 * a + j], rsem.at[3 * a + j], (px, py, c))
            pairs.append((_remote(src[a].at[2 * px + py], land[a].at[me], *sems),
                          _remote(src[a].at[2 * px + py], land[a].at[2 * px + py], *sems)))
    return pairs


def _split_start(copies, srcs, land_shapes, dep, name):
    n = len(srcs)

    def body(*refs):
        for send, _ in copies(refs[:n], refs[n:2 * n], refs[2 * n + 1], refs[2 * n + 2]):
            send.start()
        refs[-1][...] = jnp.zeros_like(refs[-1])

    lands = [pltpu.with_memory_space_constraint(lax.empty(s, w.dtype), pltpu.HBM) for s, w in zip(land_shapes, srcs)]
    out = pl.pallas_call(
        body, name=name,
        out_shape=(pltpu.SemaphoreType.DMA((3 * n,)), pltpu.SemaphoreType.DMA((3 * n,)),
                   *[pltpu.HBM(w.shape, w.dtype) for w in srcs], *[pltpu.HBM(w.shape, w.dtype) for w in lands],
                   _sds((8, 128), F32)),
        in_specs=[HBM] * (2 * n) + [ANY], out_specs=(SEM, SEM, *[HBM] * (2 * n), pl.BlockSpec(memory_space=pltpu.VMEM)),
        input_output_aliases={a: 2 + a for a in range(2 * n)},
        compiler_params=pltpu.CompilerParams(has_side_effects=EFFECT))(
            *[pltpu.with_memory_space_constraint(w, pltpu.HBM) for w in srcs], *lands, dep)
    return out[0], out[1], list(out[2:2 + n]), list(out[2 + n:2 + 2 * n]), out[-1]


def _split_wait(copies, ssem, rsem, srcs, lands, after, name):
    n = len(srcs)

    def body(*refs):
        for send, recv in copies(refs[:n], refs[n:2 * n], refs[2 * n], refs[2 * n + 1]):
            send.wait_send()
            recv.wait_recv()

    out = pl.pallas_call(
        body, name=name,
        out_shape=(*[pltpu.HBM(w.shape, w.dtype) for w in srcs], *[pltpu.HBM(w.shape, w.dtype) for w in lands]),
        in_specs=[HBM] * (2 * n) + [SEM, SEM, ANY], out_specs=tuple([HBM] * (2 * n)),
        input_output_aliases={a: a for a in range(2 * n)},
        compiler_params=pltpu.CompilerParams(has_side_effects=EFFECT))(*srcs, *lands, ssem, rsem, after)
    return list(out[:n]), list(out[n:])


def _gather_start(ws, dep, tag):
    return _split_start(_gather_copies, ws, [(4,) + w.shape for w in ws], dep, "gather_start_" + tag)


def _gather_wait(ssem, rsem, srcs, lands, after, tag):
    return _split_wait(_gather_copies, ssem, rsem, srcs, lands, after, "gather_wait_" + tag)


def _gather_forward(lands, tag):
    n = len(lands)

    def body(*refs):
        land = refs[n:2 * n]
        ssem, rsem = refs[2 * n:]
        x, y, c, chips = _place()
        sib = (x, y, 1 - c)
        cps = []
        for a in range(n):
            h = lands[a].shape[1] // 2
            for j, (px, py) in enumerate(chips):
                blk = land[a].at[2 * px + py, pl.ds(c * h, h)]
                cps.append(_remote(blk, blk, ssem.at[3 * a + j], rsem.at[3 * a + j], sib))
        for cp in cps:
            cp.start()
        for a in range(n):
            h = lands[a].shape[1] // 2
            for j, (px, py) in enumerate(chips):
                blk = land[a].at[2 * px + py, pl.ds((1 - c) * h, h)]
                _remote(blk, blk, ssem.at[3 * a + j], rsem.at[3 * a + j], sib).wait_recv()
        for cp in cps:
            cp.wait_send()

    return pl.pallas_call(
        body, name="gather_forward_" + tag, in_specs=[ANY] * n, out_specs=[ANY] * n,
        out_shape=[_sds(w.shape, w.dtype) for w in lands], input_output_aliases={a: a for a in range(n)},
        scratch_shapes=[pltpu.SemaphoreType.DMA((3 * n,)), pltpu.SemaphoreType.DMA((3 * n,))])(*lands)


def _rows_tile(r, c, n_arrays):
    cap = max(8, (16 * 1024 * 1024) // (2 * n_arrays * c * 4))
    fits = [t for t in range(8, min(r, cap) + 1, 8) if r % t == 0]
    return max(fits) if fits else r


REDUCE_ROW_BLOCKS = 2


def _add_pairs(gs, gots, core, *, name):
    n = len(gs)

    def body(core_ref, *refs):
        for k in range(n):
            refs[2 * n + k][...] = (refs[k][...].astype(F32) + refs[n + k][...].astype(F32)).astype(BF16)

    halves, wholes, shapes = [], [], []
    for g in gs:
        _, r, c = g.shape
        tm = r // (2 * REDUCE_ROW_BLOCKS)
        wholes.append(pl.BlockSpec((None, None, tm, c), lambda s, i, core_ref: (s, core_ref[0], i, 0)))
        halves.append(pl.BlockSpec((None, tm, c), lambda s, i, core_ref: (s, i, 0)))
        shapes.append(_sds((4, r // 2, c), BF16))
    grid_spec = pltpu.PrefetchScalarGridSpec(num_scalar_prefetch=1, grid=(4, REDUCE_ROW_BLOCKS),
                                             in_specs=wholes + halves, out_specs=halves)
    return pl.pallas_call(body, name=name, grid_spec=grid_spec, out_shape=shapes,
                          compiler_params=pltpu.CompilerParams(dimension_semantics=("arbitrary", "arbitrary"),
                                                               vmem_limit_bytes=VMEM_LIMIT))(
                                                                   core, *[g.reshape(4, 2, g.shape[1] // 2, g.shape[2]) for g in gs], *gots)


def _sum_slots(rbs, *, name):
    n = len(rbs)

    def body(*refs):
        for k in range(n):
            acc = refs[k][0].astype(F32)
            for s in range(1, 4):
                acc = acc + refs[k][s].astype(F32)
            refs[n + k][...] = acc

    ins, outs, shapes = [], [], []
    for rb in rbs:
        _, rows, c = rb.shape
        tm = rows // REDUCE_ROW_BLOCKS
        ins.append(pl.BlockSpec((4, tm, c), lambda i: (0, i, 0)))
        outs.append(_row(tm, c))
        shapes.append(_sds((rows, c), F32))
    return _pc(body, name=name, grid=(REDUCE_ROW_BLOCKS,), in_specs=ins, out_specs=outs, out_shape=shapes)(*rbs)


def _exchange_pairs(gs, tag, dep=None):
    n = len(gs)
    extra = [] if dep is None else [dep]

    def body(*refs):
        ins, got = refs[:n], refs[n + len(extra):2 * n + len(extra)]
        ssem, rsem = refs[2 * n + len(extra):]
        x, y, c, _ = _place()
        cps = []
        for a in range(n):
            lh = gs[a].shape[1] // 2
            cps.append(_remote(ins[a].at[:, pl.ds((1 - c) * lh, lh)], got[a], ssem.at[a], rsem.at[a], (x, y, 1 - c)))
        for cp in cps:
            cp.start()
        for cp in cps:
            cp.wait()

    halves = [_sds((4, g.shape[1] // 2) + g.shape[2:], g.dtype) for g in gs]
    return pl.pallas_call(
        body, name="reduce_pair_exchange_" + tag, in_specs=[ANY] * (n + len(extra)), out_specs=[ANY] * n, out_shape=halves,
        scratch_shapes=[pltpu.SemaphoreType.DMA((n,)), pltpu.SemaphoreType.DMA((n,))])(*gs, *extra)


def _reduce_start(gs, core, tag, dep=None):
    got = _exchange_pairs(gs, tag, dep)
    core1 = core.reshape(1).astype(jnp.int32)
    ps = _add_pairs(gs, got, core1, name="reduce_pair_add_" + tag)
    ssem, rsem, ps, rbs, token = _split_start(_chip_copies, ps, [p.shape for p in ps], got[0], "reduce_start_" + tag)
    return (ssem, rsem, ps, rbs), token


def _sibling_copies(src, land, ssem, rsem):
    x, y, c, _ = _place()
    cps = [_remote(src[a], land[a], ssem.at[a], rsem.at[a], (x, y, 1 - c)) for a in range(len(src))]
    return [(cp, cp) for cp in cps]


def _reduce_finish(state, after, chip, tag):
    ps, rbs = _split_wait(_chip_copies, *state, after, "reduce_wait_" + tag)
    rbs = [lax.dynamic_update_index_in_dim(b, lax.dynamic_index_in_dim(p, chip, 0, keepdims=False), chip, 0)
           for b, p in zip(rbs, ps)]
    fs = _sum_slots(rbs, name="reduce_chip_sum_" + tag)
    return _split_start(_sibling_copies, fs, [f.shape for f in fs], rbs[0], "reduce_final_start_" + tag)[:4]


def _reduce_collect(state, after, core, tag):
    fs, theirs = _split_wait(_sibling_copies, *state, after, "reduce_final_wait_" + tag)
    return [jnp.where(core == 0, jnp.concatenate([f, t], axis=0), jnp.concatenate([t, f], axis=0))
            for f, t in zip(fs, theirs)]


def _gather8(v, reduce):
    m = v.shape[0]

    def body(x_ref, *rest):
        if reduce:
            sum_ref, out_ref, send_sems, recv_sems, local_sem = rest
        else:
            out_ref, send_sems, recv_sems, local_sem = rest
        x, y, c, chips = _place()
        me, sibling = (x, y, c), (x, y, 1 - c)

        def rows(px, py, pc):
            return out_ref.at[pl.ds((4 * px + 2 * py + pc) * m, m), :]

        def copy(k, block, to, src=None):
            return _remote(rows(*block) if src is None else src, rows(*block), send_sems.at[k], recv_sems.at[k], to)

        mine = pltpu.make_async_copy(x_ref, rows(*me), local_sem)
        mine.start()
        first = [copy(0, me, sibling, src=x_ref)]
        first += [copy(1 + j, me, (*chip, c), src=x_ref) for j, chip in enumerate(chips)]
        for cp in first:
            cp.start()
        passed = [copy(4 + j, (*chip, c), sibling) for j, chip in enumerate(chips)]
        for j, chip in enumerate(chips):
            copy(1 + j, (*chip, c), me).wait_recv()
            passed[j].start()
        copy(0, sibling, me).wait_recv()
        for j, chip in enumerate(chips):
            copy(4 + j, (*chip, 1 - c), me).wait_recv()
        for cp in first + passed:
            cp.wait_send()
        mine.wait()
        if reduce:
            acc = out_ref[pl.ds(0, m), :]
            for k in range(1, 8):
                acc = acc + out_ref[pl.ds(k * m, m), :]
            sum_ref[...] = acc

    vm = pl.BlockSpec(memory_space=pltpu.VMEM)
    out_shape = [_sds((8 * m, 128), F32)]
    if reduce:
        out_shape = [_sds((m, 128), F32)] + out_shape
    out = pl.pallas_call(
        body, name="allreduce_small" if reduce else "gather_small", in_specs=[vm], out_specs=[vm] * len(out_shape),
        out_shape=out_shape,
        scratch_shapes=[pltpu.SemaphoreType.DMA((7,)), pltpu.SemaphoreType.DMA((7,)), pltpu.SemaphoreType.DMA(())],
        compiler_params=pltpu.CompilerParams(vmem_limit_bytes=VMEM_LIMIT))(v)
    return out[0]


def _adamw(w, g, m, v, *, name):
    if w.ndim == 2:
        return [o[0] for o in _adamw(w[None], g[None], m[None], v[None], name=name)]
    nl, r, c = w.shape
    tm = _rows_tile(r, c, 7)
    blk = pl.BlockSpec((None, tm, c), lambda l, i: (l, i, 0))

    def body(w_ref, g_ref, m_ref, v_ref, d_ref, mo_ref, vo_ref):
        gg = g_ref[...]
        mm = B1 * m_ref[...] + (1.0 - B1) * gg
        vv = B2 * v_ref[...] + (1.0 - B2) * (gg * gg)
        mo_ref[...] = mm
        vo_ref[...] = vv
        m_hat = mm / (1.0 - B1 ** STEP)
        v_hat = vv / (1.0 - B2 ** STEP)
        d_ref[...] = -LR * (m_hat / (jnp.sqrt(v_hat) + AEPS) + WD * w_ref[...])

    return _pc(body, name=name, grid=(nl, r // tm), in_specs=[blk] * 4, out_specs=[blk] * 3,
               out_shape=[_sds((nl, r, c), F32)] * 3)(w, g, m, v)


WEIGHTS = ["ssm_norm", "ssm_w_in", "ssm_conv_w", "ssm_conv_b", "ssm_dt_bias", "ssm_a_log", "ssm_d", "ssm_gate_norm",
           "ssm_w_out", "cv_norm", "cv_w_pw1", "cv_b_pw1", "cv_dw_w", "cv_dw_b", "cv_ln_g", "cv_ln_b", "cv_w_pw2",
           "cv_b_pw2", "ffn_norm", "ffn_w_gate", "ffn_w_up", "ffn_w_down", "final_norm"]
TRANSPOSED = ("ffn_w_gate", "ffn_w_up")
ADAMW_TRANSPOSED = ("ssm_w_in",) + TRANSPOSED
BIG = {"ssm_w_in": "col", "ssm_w_out": "row", "cv_w_pw1": "col", "cv_w_pw2": "row",
       "ffn_w_gate": "row", "ffn_w_up": "row", "ffn_w_down": "row"}
SMALL_SHARDED = {"ssm_conv_w": 2, "cv_norm": 1, "cv_b_pw1": 1, "cv_dw_w": 2, "cv_dw_b": 1, "cv_ln_g": 1, "cv_ln_b": 1,
                 "cv_b_pw2": 1}
SMALL = [n for n in WEIGHTS if n not in BIG]
LANES = 128


def _pack(arrays):
    flat = jnp.concatenate([a.reshape(-1) for a in arrays])
    rows = -(-flat.size // (8 * LANES)) * 8
    return jnp.pad(flat, (0, rows * LANES - flat.size)).reshape(rows, LANES)


def _unpack(packed, shapes):
    flat, out, off = packed.reshape(-1), [], 0
    for s in shapes:
        n = 1
        for d_ in s:
            n *= d_
        out.append(flat[off:off + n].reshape(s))
        off += n
    return out


def _unshard(parts, axis):
    return jnp.concatenate(parts, axis=axis)


def kernel(x, ssm_norm, ssm_w_in, ssm_conv_w, ssm_conv_b, ssm_dt_bias, ssm_a_log, ssm_d, ssm_gate_norm, ssm_w_out, cv_norm, cv_w_pw1, cv_b_pw1, cv_dw_w, cv_dw_b, cv_ln_g, cv_ln_b, cv_w_pw2, cv_b_pw2, ffn_norm, ffn_w_gate, ffn_w_up, ffn_w_down, final_norm, loss_target, m_ssm_norm, m_ssm_w_in, m_ssm_conv_w, m_ssm_conv_b, m_ssm_dt_bias, m_ssm_a_log, m_ssm_d, m_ssm_gate_norm, m_ssm_w_out, m_cv_norm, m_cv_w_pw1, m_cv_b_pw1, m_cv_dw_w, m_cv_dw_b, m_cv_ln_g, m_cv_ln_b, m_cv_w_pw2, m_cv_b_pw2, m_ffn_norm, m_ffn_w_gate, m_ffn_w_up, m_ffn_w_down, m_final_norm, v_ssm_norm, v_ssm_w_in, v_ssm_conv_w, v_ssm_conv_b, v_ssm_dt_bias, v_ssm_a_log, v_ssm_d, v_ssm_gate_norm, v_ssm_w_out, v_cv_norm, v_cv_w_pw1, v_cv_b_pw1, v_cv_dw_w, v_cv_dw_b, v_cv_ln_g, v_cv_ln_b, v_cv_w_pw2, v_cv_b_pw2, v_ffn_norm, v_ffn_w_gate, v_ffn_w_up, v_ffn_w_down, v_final_norm):
    a = dict(locals())
    chip = 2 * lax.axis_index("x") + lax.axis_index("y")
    n_ssm, n_cv, depth = ssm_norm.shape[0], cv_norm.shape[0], ffn_norm.shape[0]

    def own(n, layer):
        w = a[n][layer].astype(BF16)
        return w.T if n in TRANSPOSED else w

    sh_names = list(SMALL_SHARDED)
    got = _gather8(_pack([a[n] for n in sh_names]), reduce=False)
    got = got.reshape(8, -1)[0::2]
    per_chip = [_unpack(got[s], [a[n].shape for n in sh_names]) for s in range(4)]
    full = {n: a[n] for n in SMALL}
    for k, n in enumerate(sh_names):
        full[n] = _unshard([per_chip[s][k] for s in range(4)], SMALL_SHARDED[n])

    def matrices_of(i, part="layer"):
        mixer = [(n, i // 2) for n in (("ssm_w_in", "ssm_w_out") if i % 2 == 0 else ("cv_w_pw1", "cv_w_pw2"))]
        ffn_part = [(n, i) for n in ("ffn_w_gate", "ffn_w_up", "ffn_w_down")]
        return {"mixer": mixer, "ffn": ffn_part, "layer": mixer + ffn_part}[part]

    first = matrices_of(0, "mixer")
    stages = [first[:1], first[1:] + matrices_of(0, "ffn")] + [matrices_of(i) for i in range(1, depth)]
    in_flight, landed = {}, {}

    def matrix(n, layer, h):
        if not in_flight and not landed:
            dep = h
            for k, stage in enumerate(stages):
                in_flight[k] = _gather_start([own(m, l) for m, l in stage], dep, str(k))
                dep = in_flight[k][-1]
            in_flight["token"] = dep[0, 0]
        if (n, layer) not in landed:
            k = next(k for k, st in enumerate(stages) if (n, layer) in st)
            ssem, rsem, srcs, lands, _ = in_flight.pop(k)
            srcs, lands = _gather_wait(ssem, rsem, srcs, lands, h, str(k))
            lands = _gather_forward(lands, str(k))
            for key, src, land in zip(stages[k], srcs, lands):
                landed[key] = (src, land, in_flight["token"])
        return landed[(n, layer)]

    def whole(h, *keys):
        got = [matrix(n, l, h) for n, l in keys]
        if BIG[keys[0][0]] == "row" and len(keys) == 1:
            src, land, token = got[0]
            return lax.dynamic_update_index_in_dim(land, src, chip, 0).reshape(-1, land.shape[-1]), token
        parts = [jnp.where(chip == s, src, land[s]) for src, land, _ in got for s in range(4)]
        return jnp.concatenate(parts, axis=1 if BIG[keys[0][0]] == "col" else 0), got[0][2]

    def mixer_weights(i, h):
        j = i // 2
        if i % 2 == 0:
            w_in, token = whole(h, ("ssm_w_in", j))
            return dict(norm=full["ssm_norm"][j] + token, ffn_norm=full["ffn_norm"][i], wz=w_in[:, :DI], wxbc=w_in[:, DI:DI + CD],
                        wdt=jnp.pad(w_in[:, DI + CD:], ((0, 0), (0, NHP - NH))),
                        conv_w=full["ssm_conv_w"][j], conv_b=full["ssm_conv_b"][j], dt_bias=full["ssm_dt_bias"][j],
                        a_log=full["ssm_a_log"][j], d=full["ssm_d"][j], gate_norm=full["ssm_gate_norm"][j],
                        wout=lambda act: whole(act, ("ssm_w_out", j))[0])
        (w_pw1, token), (w_pw2, _) = whole(h, ("cv_w_pw1", j)), whole(h, ("cv_w_pw2", j))
        return dict(norm=full["cv_norm"][j] + token, ffn_norm=full["ffn_norm"][i], wpw1=w_pw1, b_pw1=full["cv_b_pw1"][j], dw_w=full["cv_dw_w"][j],
                    dw_b=full["cv_dw_b"][j], ln_g=full["cv_ln_g"][j], ln_b=full["cv_ln_b"][j], wpw2=w_pw2,
                    b_pw2=full["cv_b_pw2"][j])

    def ffn_weights(i, h):
        (wg_t, token), (wu_t, _), (wd, _) = (whole(h, (n, i)) for n in ("ffn_w_gate", "ffn_w_up", "ffn_w_down"))
        after = None if i + 1 == depth else (full["ssm_norm"] if i % 2 else full["cv_norm"])[(i + 1) // 2]
        return dict(norm=full["ffn_norm"][i] + token, next_norm=after, wg_t=wg_t, wu_t=wu_t, wd=wd)

    core = lax.axis_index("c")
    reducing, seen = {}, {}

    def reduce_small(g_final):
        g_mixers, g_ffns = [seen[i][0] for i in range(depth)], [seen[i][1] for i in range(depth)]
        local_small = {"final_norm": g_final, "ffn_norm": jnp.stack([g["norm"] for g in g_ffns])}
        for n in SMALL:
            if n.startswith("ssm_"):
                local_small[n] = jnp.stack([g[n[4:]] for g in g_mixers[0::2]])
            elif n.startswith("cv_"):
                local_small[n] = jnp.stack([g[n[3:]] for g in g_mixers[1::2]])
        return _gather8(_pack([local_small[n] for n in SMALL]), reduce=True)

    started = []

    def start_reduce(i, part, g_by_name, dep=None):
        slots = []
        for n, _ in matrices_of(i, part):
            g = g_by_name(n)
            if BIG[n] == "col":
                slots.append(g.reshape(g.shape[0], 4, g.shape[1] // 4).transpose(1, 0, 2))
            else:
                slots.append(g.reshape(4, g.shape[0] // 4, g.shape[1]))
        reducing[(i, part)], reducing["token"] = _reduce_start(slots, core, f"{i}{part[0]}", dep)
        started.append((i, part))
        return reducing["token"]

    def ffn_done(i, g_ffn_i):
        return start_reduce(0, "ffn", lambda n: g_ffn_i[n.split("_", 1)[1]]) if i == 0 else None

    def layer_done(i, g_mixer, g_ffn_i, g_final):
        seen[i] = (g_mixer, g_ffn_i)
        pick = lambda n: (g_ffn_i if n.startswith("ffn_") else g_mixer)[n.split("_", 1)[1]]
        if i > 0:
            return start_reduce(i, "layer", pick)
        reducing["small"] = reduce_small(g_final)
        return start_reduce(0, "mixer", pick, reducing["small"])

    lsum, grad_x, g_mix, g_ffn, g_final = _fwd_bwd(x[0], loss_target[0], depth, mixer_weights, ffn_weights,
                                                   full["final_norm"], layer_done, ffn_done)
    loss = (0.5 / D) * lax.psum(jnp.sum(lsum), ("x", "y", "c"))

    grads = {}
    for n, g in zip(SMALL, _unpack(reducing.pop("small"), [full[n].shape for n in SMALL])):
        if n in SMALL_SHARDED:
            ax = SMALL_SHARDED[n]
            g = lax.dynamic_slice_in_dim(g, chip * a[n].shape[ax], a[n].shape[ax], axis=ax)
        grads[n] = g
    last_start = reducing.pop("token")

    per_layer, after = {n: [None] * a[n].shape[0] for n in BIG}, last_start
    for i, part in started:
        reducing[(i, part)] = _reduce_finish(reducing[(i, part)], after, chip, f"{i}{part[0]}")
        after = reducing[(i, part)][2][0]
    for i, part in started:
        done = _reduce_collect(reducing.pop((i, part)), after, core, f"{i}{part[0]}")
        for (n, l), g in zip(matrices_of(i, part), done):
            per_layer[n][l] = g
    grads_t = {}
    for n in BIG:
        if n in TRANSPOSED:
            grads_t[n] = jnp.stack(per_layer[n])
        else:
            grads[n] = jnp.stack(per_layer[n])

    delta, new_m, new_v = {}, {}, {}
    for n in BIG:
        if n in ADAMW_TRANSPOSED:
            g_t = grads_t[n] if n in grads_t else grads[n].transpose(0, 2, 1)
            outs = _adamw(a[n].transpose(0, 2, 1), g_t, a["m_" + n].transpose(0, 2, 1), a["v_" + n].transpose(0, 2, 1),
                          name="adamw_" + n)
            grads[n] = g_t.transpose(0, 2, 1)
            delta[n], new_m[n], new_v[n] = [o.transpose(0, 2, 1) for o in outs]
        else:
            delta[n], new_m[n], new_v[n] = _adamw(a[n], grads[n], a["m_" + n], a["v_" + n], name="adamw_" + n)
    shapes = [a[n].shape for n in SMALL]
    upd = _adamw(_pack([a[n] for n in SMALL]), _pack([grads[n] for n in SMALL]), _pack([a["m_" + n] for n in SMALL]),
                 _pack([a["v_" + n] for n in SMALL]), name="adamw_small")
    for dst, packed in zip((delta, new_m, new_v), upd):
        dst.update(zip(SMALL, _unpack(packed, shapes)))

    return (loss, grad_x[None], *[grads[n] for n in WEIGHTS], *[delta[n] for n in WEIGHTS],
            *[new_m[n] for n in WEIGHTS], *[new_v[n] for n in WEIGHTS])
```

```python
import jax
import jax.numpy as jnp
from jax import lax
from jax.experimental import pallas as pl
from jax.experimental.pallas import tpu as pltpu

F32 = jnp.float32
BF16 = jnp.bfloat16
MESH = pl.DeviceIdType.MESH

D = 1024
DI = 2048
HD = 64
NH = 32
NG = 4
GW = DI // NG
NS = 128
KC = 4
CD = DI + 2 * NG * NS
Q = 128
DFF = 2816
CK = 31
EPS = 1e-5
NHP = 128

LR, B1, B2, AEPS, WD, STEP = 0.001, 0.9, 0.999, 1e-08, 0.01, 10

VMEM_LIMIT = 56 * 1024 * 1024
MM_VMEM_BUDGET = 36 * 1024 * 1024


def _pc(body, *, name, grid, in_specs, out_specs, out_shape, scratch=()):
    return pl.pallas_call(
        body, name=name, grid=grid, in_specs=in_specs, out_specs=out_specs, out_shape=out_shape,
        scratch_shapes=list(scratch),
        compiler_params=pltpu.CompilerParams(dimension_semantics=("arbitrary",) * len(grid),
                                             vmem_limit_bytes=VMEM_LIMIT))


def _sds(shape, dtype):
    return jax.ShapeDtypeStruct(tuple(shape), dtype)


def _row(tm, c):
    return pl.BlockSpec((tm, c), lambda i: (i, 0))


def _const(shape):
    return pl.BlockSpec(tuple(shape), lambda i: (0,) * len(shape))


def _tile(n, cap):
    if n <= cap:
        return n
    best = 128
    for t in range(128, cap + 1, 128):
        if n % t == 0:
            best = t
    return best


def _sigmoid(x):
    return 1.0 / (1.0 + jnp.exp(-x))


def _dsilu(x):
    s = _sigmoid(x)
    return s * (1.0 + x * (1.0 - s))


def _mm(a, b, mode, *, name, out_dtype=F32, bias=None, res=None, tm=1024, after=None, norm_g=None):
    if mode == "nn":
        (m, k), (k2, n) = a.shape, b.shape
    elif mode == "nt":
        (m, k), (n, k2) = a.shape, b.shape
    else:
        (k, m), (k2, n) = a.shape, b.shape
    assert k == k2, (a.shape, b.shape, mode)
    tm = _tile(m, tm if k <= 2048 else 512)
    so = jnp.dtype(out_dtype).itemsize + (4 if res is not None else 0)
    fits = [c for c in range(128, min(n, 2048) + 1, 128) if n % c == 0 and
            2 * (tm * k * a.dtype.itemsize + c * k * b.dtype.itemsize + tm * c * so) <= MM_VMEM_BUDGET]
    tn = n if n <= 128 else max(fits)
    dn = {"nn": (((1,), (0,)), ((), ())), "nt": (((1,), (1,)), ((), ())), "tn": (((0,), (0,)), ((), ()))}[mode]
    nb, nr, nn = bias is not None, res is not None, norm_g is not None
    assert not nn or tn == n == D, (n, tn)

    def body(*refs):
        acc = lax.dot_general(refs[0][...].astype(BF16), refs[1][...].astype(BF16), dn, preferred_element_type=F32)
        if nb:
            acc = acc + refs[2][...]
        if nr:
            acc = acc + refs[2 + nb][...]
        if nn:
            r = lax.rsqrt(jnp.mean(acc * acc, axis=-1, keepdims=True) + EPS)
            refs[-1][...] = (acc * r * refs[2 + nb + nr][...]).astype(BF16)
            refs[-2][...] = acc.astype(out_dtype)
        else:
            refs[-1][...] = acc.astype(out_dtype)

    a_spec = pl.BlockSpec((k, tm), lambda i, j: (0, i)) if mode == "tn" else pl.BlockSpec((tm, k), lambda i, j: (i, 0))
    b_spec = pl.BlockSpec((tn, k), lambda i, j: (j, 0)) if mode == "nt" else pl.BlockSpec((k, tn), lambda i, j: (0, j))
    ins, specs = [a, b], [a_spec, b_spec]
    if nb:
        ins.append(bias)
        specs.append(pl.BlockSpec((1, tn), lambda i, j: (0, j)))
    if nr:
        ins.append(res)
        specs.append(pl.BlockSpec((tm, tn), lambda i, j: (i, j)))
    if nn:
        ins.append(norm_g.reshape(1, D))
        specs.append(pl.BlockSpec((1, D), lambda i, j: (0, 0)))
    if after is not None:
        ins.append(after)
        specs.append(pl.BlockSpec((8, 128), lambda i, j: (0, 0)))
    tile = pl.BlockSpec((tm, tn), lambda i, j: (i, j))
    if nn:
        return _pc(body, name=name, grid=(m // tm, n // tn), in_specs=specs, out_specs=[tile, tile],
                   out_shape=[_sds((m, n), out_dtype), _sds((m, n), BF16)])(*ins)
    return _pc(body, name=name, grid=(m // tm, n // tn), in_specs=specs, out_specs=tile, out_shape=_sds((m, n), out_dtype))(*ins)


def _rms_fwd(h, g, *, name):
    t = h.shape[0]
    tm = min(t, 512)

    def body(h_ref, g_ref, u_ref):
        x = h_ref[...]
        r = lax.rsqrt(jnp.mean(x * x, axis=-1, keepdims=True) + EPS)
        u_ref[...] = (x * r * g_ref[...]).astype(BF16)

    return _pc(body, name=name, grid=(t // tm,), in_specs=[_row(tm, D), _const((1, D))], out_specs=_row(tm, D),
               out_shape=_sds((t, D), BF16))(h, g.reshape(1, D))


def _final_loss(h, g, target, *, name):
    t = h.shape[0]
    tm = min(t, 512)

    def body(h_ref, g_ref, t_ref, dh_ref, dh16_ref, dg_ref, l_ref):
        x = h_ref[...]
        gg = g_ref[...]
        r = lax.rsqrt(jnp.mean(x * x, axis=-1, keepdims=True) + EPS)
        err = x * r * gg - t_ref[...]
        dy = err * (1.0 / D)
        dyg = dy * gg
        dot = jnp.mean(dyg * x, axis=-1, keepdims=True)
        dh = r * dyg - x * (r * r * r * dot)
        dh_ref[...] = dh
        dh16_ref[...] = dh.astype(BF16)

        @pl.when(pl.program_id(0) == 0)
        def _():
            dg_ref[...] = jnp.zeros_like(dg_ref)
            l_ref[...] = jnp.zeros_like(l_ref)

        dg_ref[...] += jnp.sum(dy * x * r, axis=0, keepdims=True)
        l_ref[...] += jnp.sum(err * err, axis=0, keepdims=True)

    dh, dh16, dg, l = _pc(body, name=name, grid=(t // tm,),
                          in_specs=[_row(tm, D), _const((1, D)), _row(tm, D)],
                          out_specs=[_row(tm, D), _row(tm, D), _const((1, D)), _const((1, D))],
                          out_shape=[_sds((t, D), F32), _sds((t, D), BF16), _sds((1, D), F32), _sds((1, D), F32)])(
                              h, g.reshape(1, D), target)
    return l, dh, dh16, dg.reshape(D)


NT = (((1,), (1,)), ((), ()))
FFN_COLS = 1408


def _ffn_up_act(u, wg_t, wu_t, *, name):
    t = u.shape[0]
    tm, tn = min(t, 512), FFN_COLS

    def body(u_ref, wg_ref, wu_ref, g_ref, up_ref, a_ref):
        uu = u_ref[...]
        gate = lax.dot_general(uu, wg_ref[...], NT, preferred_element_type=F32)
        up = lax.dot_general(uu, wu_ref[...], NT, preferred_element_type=F32)
        g_ref[...] = gate
        up_ref[...] = up
        a_ref[...] = (gate * _sigmoid(gate) * up).astype(BF16)

    wspec = pl.BlockSpec((tn, D), lambda i, j: (j, 0))
    tile = pl.BlockSpec((tm, tn), lambda i, j: (i, j))
    return _pc(body, name=name, grid=(t // tm, DFF // tn),
               in_specs=[pl.BlockSpec((tm, D), lambda i, j: (i, 0)), wspec, wspec], out_specs=[tile, tile, tile],
               out_shape=[_sds((t, DFF), F32), _sds((t, DFF), F32), _sds((t, DFF), BF16)])(u, wg_t, wu_t)


def _ffn_down_dx_act(dh2, wd, gate, up, *, name, after=None):
    t = dh2.shape[0]
    tm, tn = min(t, 512), FFN_COLS

    def body(d_ref, w_ref, g_ref, up_ref, *rest):
        dg_ref, du_ref = rest[-2:]
        da = lax.dot_general(d_ref[...].astype(BF16), w_ref[...], NT, preferred_element_type=F32)
        gg = g_ref[...]
        sg = _sigmoid(gg)
        dg_ref[...] = (da * up_ref[...] * (sg * (1.0 + gg * (1.0 - sg)))).astype(BF16)
        du_ref[...] = (da * gg * sg).astype(BF16)

    tile = pl.BlockSpec((tm, tn), lambda i, j: (i, j))
    ins = [dh2, wd, gate, up]
    specs = [pl.BlockSpec((tm, D), lambda i, j: (i, 0)), pl.BlockSpec((tn, D), lambda i, j: (j, 0)), tile, tile]
    if after is not None:
        ins.append(after)
        specs.append(pl.BlockSpec((8, 128), lambda i, j: (0, 0)))
    return _pc(body, name=name, grid=(t // tm, DFF // tn), in_specs=specs, out_specs=[tile, tile],
               out_shape=[_sds((t, DFF), BF16), _sds((t, DFF), BF16)])(*ins)


def _dx_rms_bwd(pairs, h, g, dres, *, name):
    t = h.shape[0]
    tm = min(t, 256)
    n = len(pairs)
    dns = [NT if mode == "nt" else (((1,), (0,)), ((), ())) for _, _, mode in pairs]

    def body(*refs):
        h_ref, g_ref, dres_ref, dh_ref, dh16_ref, dg_ref = refs[2 * n:]

        @pl.when(pl.program_id(0) == 0)
        def _():
            dg_ref[...] = jnp.zeros_like(dg_ref)

        dy = None
        for k in range(n):
            part = lax.dot_general(refs[k][...].astype(BF16), refs[n + k][...], dns[k], preferred_element_type=F32)
            dy = part if dy is None else dy + part
        x = h_ref[...]
        r = lax.rsqrt(jnp.mean(x * x, axis=-1, keepdims=True) + EPS)
        dyg = dy * g_ref[...]
        dot = jnp.mean(dyg * x, axis=-1, keepdims=True)
        dh = dres_ref[...] + r * dyg - x * (r * r * r * dot)
        dh_ref[...] = dh
        dh16_ref[...] = dh.astype(BF16)
        dg_ref[...] += jnp.sum(dy * x * r, axis=0, keepdims=True)

    a_specs = [_row(tm, a.shape[1]) for a, _, _ in pairs]
    b_specs = [_const(b.shape) for _, b, _ in pairs]
    dh, dh16, dg = _pc(body, name=name, grid=(t // tm,),
                       in_specs=a_specs + b_specs + [_row(tm, D), _const((1, D)), _row(tm, D)],
                       out_specs=[_row(tm, D), _row(tm, D), _const((1, D))],
                       out_shape=[_sds((t, D), F32), _sds((t, D), BF16), _sds((1, D), F32)])(
                           *[a for a, _, _ in pairs], *[b for _, b, _ in pairs], h, g.reshape(1, D), dres)
    return dh, dh16, dg.reshape(D)


ROWS = 8


def _halo_before(tm, halo, tc):
    return pl.BlockSpec((halo, tc), lambda j, i: (jnp.maximum(i * (tm // halo) - 1, 0), j))


def _halo_after(tm, halo, tc, t):
    return pl.BlockSpec((halo, tc), lambda j, i: (jnp.minimum((i + 1) * (tm // halo), t // halo - 1), j))


def _conv4_fwd(xp, w, b, *, name):
    t = xp.shape[0]
    tm, tc, halo = min(t, 512), 1024, 8

    def body(x_ref, h_ref, w_ref, b_ref, o_ref, pad):
        i = pl.program_id(1)
        pad[pl.ds(0, halo), :] = jnp.where(i == 0, 0.0, h_ref[...])
        pad[pl.ds(halo, tm), :] = x_ref[...]
        acc = jnp.zeros((tm, tc), F32) + b_ref[...]
        for k in range(KC):
            acc = acc + pad[pl.ds(halo - (KC - 1) + k, tm), :] * w_ref[pl.ds(k, 1), :]
        o_ref[...] = acc * _sigmoid(acc)

    tile = pl.BlockSpec((tm, tc), lambda j, i: (i, j))
    return _pc(body, name=name, grid=(CD // tc, t // tm),
               in_specs=[tile, _halo_before(tm, halo, tc), pl.BlockSpec((KC, tc), lambda j, i: (0, j)),
                         pl.BlockSpec((1, tc), lambda j, i: (0, j))],
               out_specs=tile, out_shape=_sds((t, CD), F32), scratch=[pltpu.VMEM((halo + tm, tc), F32)])(xp, xp, w, b.reshape(1, CD))


def _conv4_bwd_act(xp, w, b, dxs, dbm, dcm, *, name):
    t = xp.shape[0]
    tm, tc, halo = min(t, 512), 1024, 8
    nx = DI // tc
    assert CD // tc == nx + 1 and dbm.shape[1] + dcm.shape[1] == tc

    def body(x_ref, h_ref, w_ref, b_ref, dx_ref, dbm_ref, dcm_ref, dp_ref, dw_ref, db_ref, pad):
        i = pl.program_id(1)
        pad[pl.ds(0, halo), :] = jnp.where(i == 0, 0.0, h_ref[...])
        pad[pl.ds(halo, tm), :] = x_ref[...]
        acc = jnp.zeros((tm, tc), F32) + b_ref[...]
        for k in range(KC):
            acc = acc + pad[pl.ds(halo - (KC - 1) + k, tm), :] * w_ref[pl.ds(k, 1), :]
        dout = jnp.where(pl.program_id(0) < nx, dx_ref[...], jnp.concatenate([dbm_ref[...], dcm_ref[...]], axis=1))
        dpre = dout * _dsilu(acc)
        dp_ref[...] = dpre

        @pl.when(i == 0)
        def _():
            dw_ref[...] = jnp.zeros_like(dw_ref)
            db_ref[...] = jnp.zeros_like(db_ref)

        db_ref[...] += jnp.sum(dpre, axis=0, keepdims=True)
        for k in range(KC):
            dw_ref[pl.ds(k, 1), :] += jnp.sum(dpre * pad[pl.ds(halo - (KC - 1) + k, tm), :], axis=0, keepdims=True)

    tile = pl.BlockSpec((tm, tc), lambda j, i: (i, j))
    wspec = pl.BlockSpec((KC, tc), lambda j, i: (0, j))
    bspec = pl.BlockSpec((1, tc), lambda j, i: (0, j))
    dp, dw, db = _pc(body, name=name, grid=(CD // tc, t // tm),
                     in_specs=[tile, _halo_before(tm, halo, tc), wspec, bspec,
                               pl.BlockSpec((tm, tc), lambda j, i: (i, jnp.minimum(j, nx - 1))),
                               pl.BlockSpec((tm, tc // 2), lambda j, i: (i, 0)), pl.BlockSpec((tm, tc // 2), lambda j, i: (i, 0))],
                     out_specs=[tile, wspec, bspec],
                     out_shape=[_sds((t, CD), F32), _sds((KC, CD), F32), _sds((1, CD), F32)],
                     scratch=[pltpu.VMEM((halo + tm, tc), F32)])(xp, xp, w, b.reshape(1, CD), dxs, dbm, dcm)
    return dp, dw, db.reshape(CD)


def _conv_bwd_x(dy, w, kk, *, name, out_dtype):
    t, c = dy.shape
    halo = 8 if kk <= 8 else 32
    tm, tc = (min(t, 512), min(c, 1024)) if kk <= 8 else (min(t, 256), min(c, 512))
    rows = 2 * ROWS if kk > 8 else tm

    def body(y_ref, h_ref, w_ref, o_ref, pad):
        i = pl.program_id(1)
        pad[pl.ds(0, tm), :] = y_ref[...]
        pad[pl.ds(tm, halo), :] = jnp.where(i == t // tm - 1, 0.0, h_ref[...])
        for r in range(0, tm, rows):
            acc = pad[pl.ds(kk - 1 + r, rows), :] * w_ref[pl.ds(0, 1), :]
            for k in range(1, kk):
                acc = acc + pad[pl.ds(kk - 1 - k + r, rows), :] * w_ref[pl.ds(k, 1), :]
            o_ref[pl.ds(r, rows), :] = acc.astype(out_dtype)

    tile = pl.BlockSpec((tm, tc), lambda j, i: (i, j))
    return _pc(body, name=name, grid=(c // tc, t // tm),
               in_specs=[tile, _halo_after(tm, halo, tc, t), pl.BlockSpec((kk, tc), lambda j, i: (0, j))],
               out_specs=tile, out_shape=_sds((t, c), out_dtype), scratch=[pltpu.VMEM((tm + halo, tc), F32)])(dy, dy, w)


def _glu_fwd(p, *, name):
    t = p.shape[0]
    tm = min(t, 512)

    def body(p_ref, o_ref):
        o_ref[...] = p_ref[:, :D] * _sigmoid(p_ref[:, D:])

    return _pc(body, name=name, grid=(t // tm,), in_specs=[_row(tm, 2 * D)], out_specs=_row(tm, D),
               out_shape=_sds((t, D), F32))(p)


def _glu_bwd(p, dg, *, name):
    t = p.shape[0]
    tm = min(t, 512)

    def body(p_ref, dg_ref, dp_ref, db_ref):
        a = p_ref[:, :D]
        s = _sigmoid(p_ref[:, D:])
        d = dg_ref[...]
        da = d * s
        dgate = d * a * s * (1.0 - s)
        dp_ref[:, :D] = da.astype(BF16)
        dp_ref[:, D:] = dgate.astype(BF16)

        @pl.when(pl.program_id(0) == 0)
        def _():
            db_ref[...] = jnp.zeros_like(db_ref)

        db_ref[:, :D] += jnp.sum(da, axis=0, keepdims=True)
        db_ref[:, D:] += jnp.sum(dgate, axis=0, keepdims=True)

    dp, db = _pc(body, name=name, grid=(t // tm,), in_specs=[_row(tm, 2 * D), _row(tm, D)],
                 out_specs=[_row(tm, 2 * D), _const((1, 2 * D))],
                 out_shape=[_sds((t, 2 * D), BF16), _sds((1, 2 * D), F32)])(p, dg)
    return dp, db.reshape(2 * D)


def _dwconv_ln_fwd(g, w, b, lg, lb, *, name):
    t = g.shape[0]
    tm, halo = min(t, 256), 32

    def body(x_ref, h_ref, w_ref, b_ref, lg_ref, lb_ref, c_ref, s_ref, pad):
        i = pl.program_id(0)
        pad[pl.ds(0, halo), :] = jnp.where(i == 0, 0.0, h_ref[...])
        pad[pl.ds(halo, tm), :] = x_ref[...]
        acc = jnp.zeros((tm, D), F32) + b_ref[...]
        for k in range(CK):
            acc = acc + pad[pl.ds(halo - (CK - 1) + k, tm), :] * w_ref[pl.ds(k, 1), :]
        c_ref[...] = acc
        mu = jnp.mean(acc, axis=-1, keepdims=True)
        xc = acc - mu
        r = lax.rsqrt(jnp.mean(xc * xc, axis=-1, keepdims=True) + EPS)
        n = xc * r * lg_ref[...] + lb_ref[...]
        s_ref[...] = (n * _sigmoid(n)).astype(BF16)

    c, s = _pc(body, name=name, grid=(t // tm,),
               in_specs=[_row(tm, D), pl.BlockSpec((halo, D), lambda i: (jnp.maximum(i * (tm // halo) - 1, 0), 0)),
                         _const((CK, D)), _const((1, D)), _const((1, D)), _const((1, D))],
               out_specs=[_row(tm, D), _row(tm, D)], out_shape=[_sds((t, D), F32), _sds((t, D), BF16)],
               scratch=[pltpu.VMEM((halo + tm, D), F32)])(g, g, w, b.reshape(1, D), lg.reshape(1, D), lb.reshape(1, D))
    return c, s


def _ln_silu_bwd(c, lg, lb, ds, *, name):
    t = c.shape[0]
    tm = min(t, 512)

    def body(c_ref, lg_ref, lb_ref, ds_ref, dc_ref, dlg_ref, dlb_ref):
        x = c_ref[...]
        mu = jnp.mean(x, axis=-1, keepdims=True)
        xc = x - mu
        r = lax.rsqrt(jnp.mean(xc * xc, axis=-1, keepdims=True) + EPS)
        xh = xc * r
        n = xh * lg_ref[...] + lb_ref[...]
        dn = ds_ref[...] * _dsilu(n)
        dxh = dn * lg_ref[...]
        m1 = jnp.mean(dxh, axis=-1, keepdims=True)
        m2 = jnp.mean(dxh * xh, axis=-1, keepdims=True)
        dc_ref[...] = r * (dxh - m1 - xh * m2)

        @pl.when(pl.program_id(0) == 0)
        def _():
            dlg_ref[...] = jnp.zeros_like(dlg_ref)
            dlb_ref[...] = jnp.zeros_like(dlb_ref)

        dlg_ref[...] += jnp.sum(dn * xh, axis=0, keepdims=True)
        dlb_ref[...] += jnp.sum(dn, axis=0, keepdims=True)

    dc, dlg, dlb = _pc(body, name=name, grid=(t // tm,),
                       in_specs=[_row(tm, D), _const((1, D)), _const((1, D)), _row(tm, D)],
                       out_specs=[_row(tm, D), _const((1, D)), _const((1, D))],
                       out_shape=[_sds((t, D), F32), _sds((1, D), F32), _sds((1, D), F32)])(
                           c, lg.reshape(1, D), lb.reshape(1, D), ds)
    return dc, dlg.reshape(D), dlb.reshape(D)


def _dwconv_bwd_w(g, dc, *, name):
    t = g.shape[0]
    tm, tc, halo = min(t, 256), 512, 32

    def body(x_ref, h_ref, dc_ref, dw_ref, db_ref, pad):
        i = pl.program_id(1)
        pad[pl.ds(0, halo), :] = jnp.where(i == 0, 0.0, h_ref[...])
        pad[pl.ds(halo, tm), :] = x_ref[...]
        @pl.when(i == 0)
        def _():
            dw_ref[...] = jnp.zeros_like(dw_ref)
            db_ref[...] = jnp.zeros_like(db_ref)

        d = dc_ref[...]
        db_ref[...] += jnp.sum(d, axis=0, keepdims=True)
        for k in range(CK):
            dw_ref[pl.ds(k, 1), :] += jnp.sum(d * pad[pl.ds(halo - (CK - 1) + k, tm), :], axis=0, keepdims=True)

    tile = pl.BlockSpec((tm, tc), lambda j, i: (i, j))
    dw, db = _pc(body, name=name, grid=(D // tc, t // tm),
                 in_specs=[tile, _halo_before(tm, halo, tc), tile],
                 out_specs=[pl.BlockSpec((CK, tc), lambda j, i: (0, j)), pl.BlockSpec((1, tc), lambda j, i: (0, j))],
                 out_shape=[_sds((CK, D), F32), _sds((1, D), F32)],
                 scratch=[pltpu.VMEM((halo + tm, tc), F32)])(g, g, dc)
    return dw, db.reshape(D)


def _colsum(x, *, name):
    t, c = x.shape
    tm = min(t, 512)

    def body(x_ref, o_ref):
        @pl.when(pl.program_id(0) == 0)
        def _():
            o_ref[...] = jnp.zeros_like(o_ref)

        o_ref[...] += jnp.sum(x_ref[...], axis=0, keepdims=True)

    return _pc(body, name=name, grid=(t // tm,), in_specs=[_row(tm, c)], out_specs=_const((1, c)),
               out_shape=_sds((1, c), F32))(x).reshape(c)


def _head_expand(n_lanes, first_head=0):
    h = lax.broadcasted_iota(jnp.int32, (NHP, n_lanes), 0)
    j = lax.broadcasted_iota(jnp.int32, (NHP, n_lanes), 1)
    return (h == first_head + j // HD).astype(F32)


def _softplus(v):
    return jnp.maximum(v, 0.0) + jnp.log(1.0 + jnp.exp(-jnp.abs(v)))


def _chunk_tri(n, upper):
    ri = lax.broadcasted_iota(jnp.int32, (n, n), 0)
    ci = lax.broadcasted_iota(jnp.int32, (n, n), 1)
    keep = (ri <= ci) if upper else (ri >= ci)
    return (keep & (ri // Q == ci // Q)).astype(F32)


def _split3(x):
    hi = x.astype(BF16)
    r1 = x - hi.astype(F32)
    mid = r1.astype(BF16)
    return hi, mid, (r1 - mid.astype(F32)).astype(BF16)


def _dot_sel(x, sel, sel_left=False):
    s = sel.astype(BF16)
    parts = [jnp.dot(s, t, preferred_element_type=F32) if sel_left else jnp.dot(t, s, preferred_element_type=F32)
             for t in _split3(x)]
    return (parts[2] + parts[1]) + parts[0]


def _ssd_pre(dtraw, dtb, alog, *, name):
    t = dtraw.shape[0]
    tm = min(t, 256)

    def body(r_ref, b_ref, al_ref, dtx_ref, acsx_ref, acst_ref):
        dt = _softplus(r_ref[...] + b_ref[...])
        da = dt * (-jnp.exp(al_ref[...]))
        acs = _dot_sel(da, _chunk_tri(tm, False), sel_left=True)
        expand = _head_expand(DI)
        dtx_ref[...] = _dot_sel(dt, expand)
        acsx_ref[...] = _dot_sel(acs, expand)
        acst_ref[...] = acs.T

    return _pc(body, name=name, grid=(t // tm,), in_specs=[_row(tm, NHP), _const((1, NHP)), _const((1, NHP))],
               out_specs=[_row(tm, DI), _row(tm, DI), pl.BlockSpec((NHP, tm), lambda i: (0, i))],
               out_shape=[_sds((t, DI), F32), _sds((t, DI), F32), _sds((NHP, t), F32)])(dtraw, dtb, alog)


def _ssd_post(dacs, dxdt, dtraw, dtb, alog, *, name):
    t = dtraw.shape[0]
    tm = min(t, 256)

    def body(g1_ref, g2_ref, r_ref, b_ref, al_ref, o_ref, db_ref, da_ref):
        h = lax.broadcasted_iota(jnp.int32, (DI, NHP), 1)
        j = lax.broadcasted_iota(jnp.int32, (DI, NHP), 0)
        red = (h == j // HD).astype(F32)
        dda = _dot_sel(_dot_sel(g1_ref[...], red), _chunk_tri(tm, True), sel_left=True)
        v = r_ref[...] + b_ref[...]
        ddt = dda * (-jnp.exp(al_ref[...])) + _dot_sel(g2_ref[...], red)
        draw = ddt * _sigmoid(v)
        o_ref[...] = draw.astype(BF16)

        @pl.when(pl.program_id(0) == 0)
        def _():
            db_ref[...] = jnp.zeros_like(db_ref)
            da_ref[...] = jnp.zeros_like(da_ref)

        db_ref[...] += jnp.sum(draw, axis=0, keepdims=True)
        da_ref[...] += jnp.sum(dda * _softplus(v), axis=0, keepdims=True)

    return _pc(body, name=name, grid=(t // tm,),
               in_specs=[_row(tm, DI), _row(tm, DI), _row(tm, NHP), _const((1, NHP)), _const((1, NHP))],
               out_specs=[_row(tm, NHP), _const((1, NHP)), _const((1, NHP))],
               out_shape=[_sds((t, NHP), BF16), _sds((1, NHP), F32), _sds((1, NHP), F32)])(dacs, dxdt, dtraw, dtb, alog)


def _decay_pair(acs_x, acst_ref, p, tri):
    lane = lax.broadcasted_iota(jnp.int32, (Q, 2 * HD), 1)
    v = acs_x[:, 2 * HD * p:2 * HD * (p + 1)]
    swapped = pltpu.roll(v, HD, axis=1)
    out = []
    for hh in range(2):
        colb = jnp.where((lane < HD) if hh == 0 else (lane >= HD), v, swapped)
        rowb = acst_ref[pl.ds(2 * p + hh, 1), :]
        out.append(jnp.exp(jnp.where(tri, colb - rowb, -1e30)))
    return out


def _ssd_fwd(xbc, dtx, acs_x, acs_t, d_x, *, name):
    t = xbc.shape[0]
    nc = t // Q
    hpg = NH // NG

    def body(x_ref, b_ref, c_ref, dtx_ref, acsx_ref, acst_ref, d_ref, y_ref, sin_ref, state):
        @pl.when(pl.program_id(1) == 0)
        def _():
            state[...] = jnp.zeros_like(state)

        xs = x_ref[...]
        tri = lax.broadcasted_iota(jnp.int32, (Q, Q), 0) >= lax.broadcasted_iota(jnp.int32, (Q, Q), 1)
        acs_x = acsx_ref[...]
        atot_x = acsx_ref[pl.ds(Q - 1, 1), :]
        xd = xs * dtx_ref[...]
        xb = xd.astype(BF16)
        bb = b_ref[...]
        cb16 = c_ref[...].astype(BF16)
        cbm = lax.dot_general(cb16, bb.astype(BF16), (((1,), (1,)), ((), ())), preferred_element_type=F32)
        s_in = state[...]
        sin_ref[0] = s_in
        y = jnp.dot(cb16, s_in.astype(BF16), preferred_element_type=F32) * jnp.exp(acs_x) + xs * d_ref[...]
        lane = lax.broadcasted_iota(jnp.int32, (Q, 2 * HD), 1)
        for p in range(hpg // 2):
            xp = xb[:, 2 * HD * p:2 * HD * (p + 1)]
            yp = jnp.zeros((Q, 2 * HD), F32)
            for hh, lm in enumerate(_decay_pair(acs_x, acst_ref, p, tri)):
                m = (cbm * lm).astype(BF16)
                xm = jnp.where((lane >= HD) if hh == 1 else (lane < HD), xp, jnp.zeros_like(xp))
                yp = yp + jnp.dot(m, xm, preferred_element_type=F32)
            y_ref[:, 2 * HD * p:2 * HD * (p + 1)] = y[:, 2 * HD * p:2 * HD * (p + 1)] + yp
        dec = jnp.exp(atot_x - acs_x)
        s_new = jnp.dot(bb.T.astype(BF16), (xd * dec).astype(BF16), preferred_element_type=F32)
        state[...] = jnp.exp(atot_x) * s_in + s_new

    grp = lambda g, c: (c, g)
    return _pc(body, name=name, grid=(NG, nc),
               in_specs=[pl.BlockSpec((Q, GW), grp),
                         pl.BlockSpec((Q, NS), lambda g, c: (c, DI // NS + g)),
                         pl.BlockSpec((Q, NS), lambda g, c: (c, DI // NS + NG + g)),
                         pl.BlockSpec((Q, GW), grp),
                         pl.BlockSpec((Q, GW), grp),
                         pl.BlockSpec((hpg, Q), lambda g, c: (g, c)),
                         pl.BlockSpec((1, GW), lambda g, c: (0, g))],
               out_specs=[pl.BlockSpec((Q, GW), grp), pl.BlockSpec((1, NS, GW), lambda g, c: (c, 0, g))],
               out_shape=[_sds((t, DI), F32), _sds((nc, NS, DI), F32)],
               scratch=[pltpu.VMEM((NS, GW), F32)])(xbc, xbc, xbc, dtx, acs_x, acs_t, d_x)


def _ssd_bwd(xbc, dtx, acs_x, acs_t, d_x, sin, dy, *, name):
    t = xbc.shape[0]
    nc = t // Q
    hpg = NH // NG

    def body(x_ref, b_ref, c_ref, dtx_ref, acsx_ref, acst_ref, d_ref, sin_ref, dy_ref,
             dx_ref, db_ref, dc_ref, dacs_ref, dxdt_ref, ddx_ref, dstate):
        @pl.when(pl.program_id(1) == 0)
        def _():
            dstate[...] = jnp.zeros_like(dstate)
            ddx_ref[...] = jnp.zeros_like(ddx_ref)

        xs = x_ref[...]
        dtxv = dtx_ref[...]
        gy = dy_ref[...]
        tri = lax.broadcasted_iota(jnp.int32, (Q, Q), 0) >= lax.broadcasted_iota(jnp.int32, (Q, Q), 1)
        acs_x = acsx_ref[...]
        atot_x = acsx_ref[pl.ds(Q - 1, 1), :]
        xd = xs * dtxv
        xb = xd.astype(BF16)
        gb = gy.astype(BF16)
        bb = b_ref[...]
        cc = c_ref[...]
        bb16, cc16 = bb.astype(BF16), cc.astype(BF16)
        cbm = lax.dot_general(cc16, bb16, (((1,), (1,)), ((), ())), preferred_element_type=F32)
        s_in = sin_ref[0]
        s_in16 = s_in.astype(BF16)
        ds_next = dstate[...]
        ds16 = ds_next.astype(BF16)
        ecs = jnp.exp(acs_x)
        dec = jnp.exp(atot_x - acs_x)
        etot = jnp.exp(atot_x)

        ddx_ref[...] += jnp.sum(gy * xs, axis=0, keepdims=True)
        z = jnp.dot(cc16, s_in16, preferred_element_type=F32)
        dz16 = (gy * ecs).astype(BF16)
        dacs = gy * z * ecs
        dcm = lax.dot_general(dz16, s_in16, (((1,), (1,)), ((), ())), preferred_element_type=F32)
        ds_here = jnp.dot(cc.T.astype(BF16), dz16, preferred_element_type=F32)
        u = jnp.dot(bb16, ds16, preferred_element_type=F32)
        dxd = u * dec
        w_dec = u * xd * dec
        dacs = dacs - w_dec
        datot = jnp.sum(w_dec, axis=0, keepdims=True) + jnp.sum(etot * ds_next * s_in, axis=0, keepdims=True)
        dbm = lax.dot_general((xd * dec).astype(BF16), ds16, (((1,), (1,)), ((), ())), preferred_element_type=F32)
        dstate[...] = etot * ds_next + ds_here

        lane = lax.broadcasted_iota(jnp.int32, (Q, 2 * HD), 1)
        lane_x = lax.broadcasted_iota(jnp.int32, (1, GW), 1)
        dcb = jnp.zeros((Q, Q), F32)
        dx_pairs = []
        for p in range(hpg // 2):
            sl = slice(2 * HD * p, 2 * HD * (p + 1))
            xp = xb[:, sl]
            gp = gb[:, sl]
            dxp = jnp.zeros((Q, 2 * HD), F32)
            for hh, lm in enumerate(_decay_pair(acs_x, acst_ref, p, tri)):
                r = 2 * p + hh
                mm = cbm * lm
                half = (lane >= HD) if hh == 1 else (lane < HD)
                xm = jnp.where(half, xp, jnp.zeros_like(xp))
                gm = jnp.where(half, gp, jnp.zeros_like(gp))
                mt16 = mm.T.astype(BF16)
                dxp = dxp + jnp.dot(mt16, gm, preferred_element_type=F32)
                dm = lax.dot_general(gm, xp, (((1,), (1,)), ((), ())), preferred_element_type=F32)
                dmt = lax.dot_general(xm, gp, (((1,), (1,)), ((), ())), preferred_element_type=F32)
                dcb = dcb + dm * lm
                wdiff = jnp.sum(dm * mm - dmt * mm.T, axis=1, keepdims=True)
                dacs = dacs + wdiff * (lane_x == HD * r).astype(F32)
            dx_pairs.append(dxp)
        dxd = dxd + jnp.concatenate(dx_pairs, axis=1)
        dcm = dcm + jnp.dot(dcb.astype(BF16), bb16, preferred_element_type=F32)
        dbm = dbm + jnp.dot(dcb.T.astype(BF16), cc16, preferred_element_type=F32)
        db_ref[...] = dbm
        dc_ref[...] = dcm

        last = lax.broadcasted_iota(jnp.int32, (Q, GW), 0) == Q - 1
        dacs_ref[...] = dacs + jnp.where(last, datot, 0.0)
        dxdt_ref[...] = dxd * xs
        dx_ref[...] = dxd * dtxv + gy * d_ref[...]

    rev = lambda g, c: (nc - 1 - c, g)
    return _pc(body, name=name, grid=(NG, nc),
               in_specs=[pl.BlockSpec((Q, GW), rev),
                         pl.BlockSpec((Q, NS), lambda g, c: (nc - 1 - c, DI // NS + g)),
                         pl.BlockSpec((Q, NS), lambda g, c: (nc - 1 - c, DI // NS + NG + g)),
                         pl.BlockSpec((Q, GW), rev),
                         pl.BlockSpec((Q, GW), rev),
                         pl.BlockSpec((hpg, Q), lambda g, c: (g, nc - 1 - c)),
                         pl.BlockSpec((1, GW), lambda g, c: (0, g)),
                         pl.BlockSpec((1, NS, GW), lambda g, c: (nc - 1 - c, 0, g)),
                         pl.BlockSpec((Q, GW), rev)],
               out_specs=[pl.BlockSpec((Q, GW), rev), pl.BlockSpec((Q, NS), rev), pl.BlockSpec((Q, NS), rev),
                          pl.BlockSpec((Q, GW), rev), pl.BlockSpec((Q, GW), rev),
                          pl.BlockSpec((1, GW), lambda g, c: (0, g))],
               out_shape=[_sds((t, DI), F32), _sds((t, NG * NS), F32), _sds((t, NG * NS), F32), _sds((t, DI), F32),
                          _sds((t, DI), F32), _sds((1, DI), F32)],
               scratch=[pltpu.VMEM((NS, GW), F32)])(xbc, xbc, xbc, dtx, acs_x, acs_t, d_x, sin, dy)


def _gated_norm_fwd(y, z, gn, *, name):
    t = y.shape[0]
    tm = min(t, 256)

    def body(y_ref, z_ref, g_ref, o_ref):
        for k in range(NG):
            sl = slice(GW * k, GW * (k + 1))
            zz = z_ref[:, sl]
            h = y_ref[:, sl] * (zz * _sigmoid(zz))
            r = lax.rsqrt(jnp.mean(h * h, axis=-1, keepdims=True) + EPS)
            o_ref[:, sl] = (h * r * g_ref[:, sl]).astype(BF16)

    return _pc(body, name=name, grid=(t // tm,), in_specs=[_row(tm, DI), _row(tm, DI), _const((1, DI))],
               out_specs=_row(tm, DI), out_shape=_sds((t, DI), BF16))(y, z, gn.reshape(1, DI))


def _gated_norm_bwd(y, z, gn, dout, *, name):
    t = y.shape[0]
    tm = min(t, 256)

    def body(y_ref, z_ref, g_ref, do_ref, dy_ref, dz_ref, dg_ref):
        @pl.when(pl.program_id(0) == 0)
        def _():
            dg_ref[...] = jnp.zeros_like(dg_ref)

        for k in range(NG):
            sl = slice(GW * k, GW * (k + 1))
            zz = z_ref[:, sl]
            yy = y_ref[:, sl]
            sz = zz * _sigmoid(zz)
            h = yy * sz
            r = lax.rsqrt(jnp.mean(h * h, axis=-1, keepdims=True) + EPS)
            d = do_ref[:, sl]
            dg_ref[:, sl] += jnp.sum(d * h * r, axis=0, keepdims=True)
            dgd = d * g_ref[:, sl]
            dot = jnp.mean(dgd * h, axis=-1, keepdims=True)
            dh = r * dgd - h * (r * r * r * dot)
            dy_ref[:, sl] = dh * sz
            dz_ref[:, sl] = (dh * yy * _dsilu(zz)).astype(BF16)

    dy, dz, dg = _pc(body, name=name, grid=(t // tm,),
                     in_specs=[_row(tm, DI), _row(tm, DI), _const((1, DI)), _row(tm, DI)],
                     out_specs=[_row(tm, DI), _row(tm, DI), _const((1, DI))],
                     out_shape=[_sds((t, DI), F32), _sds((t, DI), BF16), _sds((1, DI), F32)])(y, z, gn.reshape(1, DI), dout)
    return dy, dz, dg.reshape(DI)


def _pad_heads(v):
    return jnp.pad(v.reshape(1, NH), ((0, 0), (0, NHP - NH)))


def _residual_out(a, w, h, next_norm, name, bias=None):
    if next_norm is None:
        return _mm(a, w, "nn", bias=bias, res=h, name=name), None
    return _mm(a, w, "nn", bias=bias, res=h, norm_g=next_norm, name=name)


def _ssm_fwd(h, u, p, next_norm):
    z = _mm(u, p["wz"], "nn", name="ssm_in_z")
    xp = _mm(u, p["wxbc"], "nn", name="ssm_in_xbc")
    dtraw = _mm(u, p["wdt"], "nn", name="ssm_in_dt")
    xbc = _conv4_fwd(xp, p["conv_w"], p["conv_b"], name="ssm_conv_fwd")
    dtb, alog = _pad_heads(p["dt_bias"]), _pad_heads(p["a_log"])
    d_x = jnp.repeat(p["d"], HD).reshape(1, DI)
    dtx, acs_x, acs_t = _ssd_pre(dtraw, dtb, alog, name="ssd_pre")
    y, sin = _ssd_fwd(xbc, dtx, acs_x, acs_t, d_x, name="ssd_fwd")
    yn = _gated_norm_fwd(y, z, p["gate_norm"], name="ssm_gate_fwd")
    if callable(p["wout"]):
        p["wout"] = p["wout"](yn)
    h2, u2 = _residual_out(yn, p["wout"], h, next_norm, "ssm_out")
    return h2, u2, (h, u, z, xp, dtraw, xbc, dtx, acs_x, acs_t, d_x, dtb, alog, y, sin, yn)


def _ssm_bwd(dh2_pair, p, saved, after=None):
    dh2, dh2_16 = dh2_pair
    h, u, z, xp, dtraw, xbc, dtx, acs_x, acs_t, d_x, dtb, alog, y, sin, yn = saved
    dyn = _mm(dh2_16, p["wout"], "nt", name="ssm_out_dx", after=after)
    g = {"w_out": _mm(yn, dh2_16, "tn", out_dtype=BF16, name="ssm_out_dw")}
    dy, dz, g["gate_norm"] = _gated_norm_bwd(y, z, p["gate_norm"], dyn, name="ssm_gate_bwd")
    dxs, dbm, dcm, dacs, dxdt, ddx = _ssd_bwd(xbc, dtx, acs_x, acs_t, d_x, sin, dy, name="ssd_bwd")
    ddtraw, ddtb, dav = _ssd_post(dacs, dxdt, dtraw, dtb, alog, name="ssd_post")
    dpre, g["conv_w"], g["conv_b"] = _conv4_bwd_act(xp, p["conv_w"], p["conv_b"], dxs, dbm, dcm, name="ssm_conv_bwd_act")
    dxp = _conv_bwd_x(dpre, p["conv_w"], KC, name="ssm_conv_bwd_x", out_dtype=BF16)
    dh, dh16, g["norm"] = _dx_rms_bwd([(dz, p["wz"], "nt"), (dxp, p["wxbc"], "nt"), (ddtraw, p["wdt"], "nt")], h, p["norm"],
                                      dh2, name="ssm_in_dx_rms_bwd")
    g["w_in"] = jnp.concatenate([_mm(u, dz, "tn", out_dtype=BF16, name="ssm_in_dw_z"),
                                 _mm(u, dxp, "tn", out_dtype=BF16, name="ssm_in_dw_xbc"),
                                 _mm(u, ddtraw, "tn", out_dtype=BF16, name="ssm_in_dw_dt")[:, :NH]], axis=1)
    g["dt_bias"] = ddtb[0, :NH]
    g["a_log"] = dav[0, :NH] * (-jnp.exp(p["a_log"]))
    g["d"] = ddx.reshape(NH, HD).sum(-1)
    return (dh, dh16), g


def _cv_fwd(h, u, p, next_norm):
    pre = _mm(u, p["wpw1"], "nn", bias=p["b_pw1"].reshape(1, 2 * D), name="cv_pw1")
    gl = _glu_fwd(pre, name="cv_glu_fwd")
    c, s = _dwconv_ln_fwd(gl, p["dw_w"], p["dw_b"], p["ln_g"], p["ln_b"], name="cv_dwconv_ln_fwd")
    h2, u2 = _residual_out(s, p["wpw2"], h, next_norm, "cv_pw2", bias=p["b_pw2"].reshape(1, D))
    return h2, u2, (h, u, pre, gl, c, s)


def _cv_bwd(dh2_pair, p, saved, after=None):
    dh2, dh2_16 = dh2_pair
    h, u, pre, gl, c, s = saved
    ds = _mm(dh2_16, p["wpw2"], "nt", name="cv_pw2_dx", after=after)
    g = {"w_pw2": _mm(s, dh2_16, "tn", out_dtype=BF16, name="cv_pw2_dw"), "b_pw2": _colsum(dh2, name="cv_pw2_db")}
    dc, g["ln_g"], g["ln_b"] = _ln_silu_bwd(c, p["ln_g"], p["ln_b"], ds, name="cv_ln_bwd")
    dgl = _conv_bwd_x(dc, p["dw_w"], CK, name="cv_dwconv_bwd_x", out_dtype=F32)
    g["dw_w"], g["dw_b"] = _dwconv_bwd_w(gl, dc, name="cv_dwconv_bwd_w")
    dpre, g["b_pw1"] = _glu_bwd(pre, dgl, name="cv_glu_bwd")
    g["w_pw1"] = _mm(u, dpre, "tn", out_dtype=BF16, name="cv_pw1_dw")
    dh, dh16, g["norm"] = _dx_rms_bwd([(dpre, p["wpw1"], "nt")], h, p["norm"], dh2, name="cv_pw1_dx_rms_bwd")
    return (dh, dh16), g


def _ffn_fwd(h, u, p, next_norm):
    gate, up, a = _ffn_up_act(u, p["wg_t"], p["wu_t"], name="ffn_gate_up_act")
    h2, u2 = _residual_out(a, p["wd"], h, next_norm, "ffn_down")
    return h2, u2, (h, u, gate, up, a)


def _ffn_bwd(dh2_pair, p, saved, after=None):
    dh2, dh2_16 = dh2_pair
    h, u, gate, up, a = saved
    dgate, dup = _ffn_down_dx_act(dh2_16, p["wd"], gate, up, name="ffn_down_dx_act", after=after)
    g = {"w_down": _mm(a, dh2_16, "tn", out_dtype=BF16, name="ffn_down_dw"),
         "w_gate": _mm(dgate, u, "tn", out_dtype=BF16, name="ffn_gate_dw"),
         "w_up": _mm(dup, u, "tn", out_dtype=BF16, name="ffn_up_dw")}
    dh, dh16, g["norm"] = _dx_rms_bwd([(dgate, p["wg_t"], "nn"), (dup, p["wu_t"], "nn")], h, p["norm"], dh2,
                                      name="ffn_dx_rms_bwd")
    return (dh, dh16), g


def _fwd_bwd(x, target, depth, mixer_weights, ffn_weights, final_norm, layer_done=None, ffn_done=None):
    h, u, tape, weights = x, None, [], []
    for i in range(depth):
        mix_w = mixer_weights(i, h)
        if u is None:
            u = _rms_fwd(h, mix_w["norm"], name="first_rms_fwd")
        h, u, s_mix = (_ssm_fwd if i % 2 == 0 else _cv_fwd)(h, u, mix_w, mix_w["ffn_norm"])
        ffn_w = ffn_weights(i, h)
        h, u, s_ffn = _ffn_fwd(h, u, ffn_w, ffn_w["next_norm"])
        tape.append((s_mix, s_ffn))
        weights.append((mix_w, ffn_w))
    lsum, dh, dh16, g_final = _final_loss(h, final_norm, target, name="loss_head")
    dh = (dh, dh16)
    g_mix, g_ffn, token = [None] * depth, [None] * depth, None
    for i in reversed(range(depth)):
        s_mix, s_ffn = tape[i]
        dh, g_ffn[i] = _ffn_bwd(dh, weights[i][1], s_ffn, after=token)
        token = ffn_done(i, g_ffn[i]) if ffn_done is not None else None
        dh, g_mix[i] = (_ssm_bwd if i % 2 == 0 else _cv_bwd)(dh, weights[i][0], s_mix, after=token)
        if layer_done is not None:
            token = layer_done(i, g_mix[i], g_ffn[i], g_final)
    return lsum, dh[0], g_mix, g_ffn, g_final


ANY = pl.BlockSpec(memory_space=pl.ANY)


def _place():
    x, y, c = lax.axis_index("x"), lax.axis_index("y"), lax.axis_index("c")
    return x, y, c, [(1 - x, y), (x, 1 - y), (1 - x, 1 - y)]


def _remote(src, dst, ssem, rsem, dev):
    return pltpu.make_async_remote_copy(src_ref=src, dst_ref=dst, send_sem=ssem, recv_sem=rsem, device_id=dev,
                                        device_id_type=MESH)


HBM = pl.BlockSpec(memory_space=pltpu.HBM)
SEM = pl.BlockSpec(memory_space=pltpu.SEMAPHORE)
EFFECT = pltpu.SideEffectType.DATAFLOW_SIDE_EFFECTING


def _gather_copies(src, land, ssem, rsem):
    x, y, c, chips = _place()
    me, pairs = 2 * x + y, []
    for a in range(len(src)):
        h = src[a].shape[0] // 2
        rows = pl.ds(c * h, h)
        for j, (px, py) in enumerate(chips):
            sems = (ssem.at[3 * a + j], rsem.at[3 * a + j], (px, py, c))
            pairs.append((_remote(src[a].at[rows], land[a].at[me, rows], *sems),
                          _remote(src[a].at[rows], land[a].at[2 * px + py, rows], *sems)))
    return pairs


def _chip_copies(src, land, ssem, rsem):
    x, y, c, chips = _place()
    me, pairs = 2 * x + y, []
    for a in range(len(src)):
        for j, (px, py) in enumerate(chips):
            sems = (ssem.at[3 * a + j], rsem.at[3 * a + j], (px, py, c))
            pairs.append((_remote(src[a].at[2 * px + py], land[a].at[me], *sems),
                          _remote(src[a].at[2 * px + py], land[a].at[2 * px + py], *sems)))
    return pairs


def _split_start(copies, srcs, land_shapes, dep, name):
    n = len(srcs)

    def body(*refs):
        for send, _ in copies(refs[:n], refs[n:2 * n], refs[2 * n + 1], refs[2 * n + 2]):
            send.start()
        refs[-1][...] = jnp.zeros_like(refs[-1])

    lands = [pltpu.with_memory_space_constraint(lax.empty(s, w.dtype), pltpu.HBM) for s, w in zip(land_shapes, srcs)]
    out = pl.pallas_call(
        body, name=name,
        out_shape=(pltpu.SemaphoreType.DMA((3 * n,)), pltpu.SemaphoreType.DMA((3 * n,)),
                   *[pltpu.HBM(w.shape, w.dtype) for w in srcs], *[pltpu.HBM(w.shape, w.dtype) for w in lands],
                   _sds((8, 128), F32)),
        in_specs=[HBM] * (2 * n) + [ANY], out_specs=(SEM, SEM, *[HBM] * (2 * n), pl.BlockSpec(memory_space=pltpu.VMEM)),
        input_output_aliases={a: 2 + a for a in range(2 * n)},
        compiler_params=pltpu.CompilerParams(has_side_effects=EFFECT))(
            *[pltpu.with_memory_space_constraint(w, pltpu.HBM) for w in srcs], *lands, dep)
    return out[0], out[1], list(out[2:2 + n]), list(out[2 + n:2 + 2 * n]), out[-1]


def _split_wait(copies, ssem, rsem, srcs, lands, after, name):
    n = len(srcs)

    def body(*refs):
        for send, recv in copies(refs[:n], refs[n:2 * n], refs[2 * n], refs[2 * n + 1]):
            send.wait_send()
            recv.wait_recv()

    out = pl.pallas_call(
        body, name=name,
        out_shape=(*[pltpu.HBM(w.shape, w.dtype) for w in srcs], *[pltpu.HBM(w.shape, w.dtype) for w in lands]),
        in_specs=[HBM] * (2 * n) + [SEM, SEM, ANY], out_specs=tuple([HBM] * (2 * n)),
        input_output_aliases={a: a for a in range(2 * n)},
        compiler_params=pltpu.CompilerParams(has_side_effects=EFFECT))(*srcs, *lands, ssem, rsem, after)
    return list(out[:n]), list(out[n:])


def _gather_start(ws, dep, tag):
    return _split_start(_gather_copies, ws, [(4,) + w.shape for w in ws], dep, "gather_start_" + tag)


def _gather_wait(ssem, rsem, srcs, lands, after, tag):
    return _split_wait(_gather_copies, ssem, rsem, srcs, lands, after, "gather_wait_" + tag)


def _gather_forward(lands, tag):
    n = len(lands)

    def body(*refs):
        land = refs[n:2 * n]
        ssem, rsem = refs[2 * n:]
        x, y, c, chips = _place()
        sib = (x, y, 1 - c)
        cps = []
        for a in range(n):
            h = lands[a].shape[1] // 2
            for j, (px, py) in enumerate(chips):
                blk = land[a].at[2 * px + py, pl.ds(c * h, h)]
                cps.append(_remote(blk, blk, ssem.at[3 * a + j], rsem.at[3 * a + j], sib))
        for cp in cps:
            cp.start()
        for a in range(n):
            h = lands[a].shape[1] // 2
            for j, (px, py) in enumerate(chips):
                blk = land[a].at[2 * px + py, pl.ds((1 - c) * h, h)]
                _remote(blk, blk, ssem.at[3 * a + j], rsem.at[3 * a + j], sib).wait_recv()
        for cp in cps:
            cp.wait_send()

    return pl.pallas_call(
        body, name="gather_forward_" + tag, in_specs=[ANY] * n, out_specs=[ANY] * n,
        out_shape=[_sds(w.shape, w.dtype) for w in lands], input_output_aliases={a: a for a in range(n)},
        scratch_shapes=[pltpu.SemaphoreType.DMA((3 * n,)), pltpu.SemaphoreType.DMA((3 * n,))])(*lands)


def _rows_tile(r, c, n_arrays):
    cap = max(8, (32 * 1024 * 1024) // (2 * n_arrays * c * 4))
    fits = [t for t in range(8, min(r, cap) + 1, 8) if r % t == 0]
    return max(fits) if fits else r


REDUCE_ROW_BLOCKS = 2


def _add_pairs(gs, gots, core, *, name):
    n = len(gs)

    def body(core_ref, *refs):
        for k in range(n):
            refs[2 * n + k][...] = (refs[k][...].astype(F32) + refs[n + k][...].astype(F32)).astype(BF16)

    halves, wholes, shapes = [], [], []
    for g in gs:
        _, r, c = g.shape
        tm = r // (2 * REDUCE_ROW_BLOCKS)
        wholes.append(pl.BlockSpec((None, None, tm, c), lambda s, i, core_ref: (s, core_ref[0], i, 0)))
        halves.append(pl.BlockSpec((None, tm, c), lambda s, i, core_ref: (s, i, 0)))
        shapes.append(_sds((4, r // 2, c), BF16))
    grid_spec = pltpu.PrefetchScalarGridSpec(num_scalar_prefetch=1, grid=(4, REDUCE_ROW_BLOCKS),
                                             in_specs=wholes + halves, out_specs=halves)
    return pl.pallas_call(body, name=name, grid_spec=grid_spec, out_shape=shapes,
                          compiler_params=pltpu.CompilerParams(dimension_semantics=("arbitrary", "arbitrary"),
                                                               vmem_limit_bytes=VMEM_LIMIT))(
                                                                   core, *[g.reshape(4, 2, g.shape[1] // 2, g.shape[2]) for g in gs], *gots)


def _sum_slots(rbs, *, name):
    n = len(rbs)

    def body(*refs):
        for k in range(n):
            acc = refs[k][0].astype(F32)
            for s in range(1, 4):
                acc = acc + refs[k][s].astype(F32)
            refs[n + k][...] = acc

    ins, outs, shapes = [], [], []
    for rb in rbs:
        _, rows, c = rb.shape
        tm = rows // REDUCE_ROW_BLOCKS
        ins.append(pl.BlockSpec((4, tm, c), lambda i: (0, i, 0)))
        outs.append(_row(tm, c))
        shapes.append(_sds((rows, c), F32))
    return _pc(body, name=name, grid=(REDUCE_ROW_BLOCKS,), in_specs=ins, out_specs=outs, out_shape=shapes)(*rbs)


def _exchange_pairs(gs, tag, dep=None):
    n = len(gs)
    extra = [] if dep is None else [dep]

    def body(*refs):
        ins, got = refs[:n], refs[n + len(extra):2 * n + len(extra)]
        ssem, rsem = refs[2 * n + len(extra):]
        x, y, c, _ = _place()
        cps = []
        for a in range(n):
            lh = gs[a].shape[1] // 2
            cps.append(_remote(ins[a].at[:, pl.ds((1 - c) * lh, lh)], got[a], ssem.at[a], rsem.at[a], (x, y, 1 - c)))
        for cp in cps:
            cp.start()
        for cp in cps:
            cp.wait()

    halves = [_sds((4, g.shape[1] // 2) + g.shape[2:], g.dtype) for g in gs]
    return pl.pallas_call(
        body, name="reduce_pair_exchange_" + tag, in_specs=[ANY] * (n + len(extra)), out_specs=[ANY] * n, out_shape=halves,
        scratch_shapes=[pltpu.SemaphoreType.DMA((n,)), pltpu.SemaphoreType.DMA((n,))])(*gs, *extra)


def _reduce_start(gs, core, tag, dep=None):
    got = _exchange_pairs(gs, tag, dep)
    core1 = core.reshape(1).astype(jnp.int32)
    ps = _add_pairs(gs, got, core1, name="reduce_pair_add_" + tag)
    ssem, rsem, ps, rbs, token = _split_start(_chip_copies, ps, [p.shape for p in ps], got[0], "reduce_start_" + tag)
    return (ssem, rsem, ps, rbs), token


def _sibling_copies(src, land, ssem, rsem):
    x, y, c, _ = _place()
    cps = [_remote(src[a], land[a], ssem.at[a], rsem.at[a], (x, y, 1 - c)) for a in range(len(src))]
    return [(cp, cp) for cp in cps]


def _reduce_finish(state, after, chip, tag):
    ps, rbs = _split_wait(_chip_copies, *state, after, "reduce_wait_" + tag)
    rbs = [lax.dynamic_update_index_in_dim(b, lax.dynamic_index_in_dim(p, chip, 0, keepdims=False), chip, 0)
           for b, p in zip(rbs, ps)]
    fs = _sum_slots(rbs, name="reduce_chip_sum_" + tag)
    return _split_start(_sibling_copies, fs, [f.shape for f in fs], rbs[0], "reduce_final_start_" + tag)[:4]


def _reduce_collect(state, after, core, tag):
    fs, theirs = _split_wait(_sibling_copies, *state, after, "reduce_final_wait_" + tag)
    return [jnp.where(core == 0, jnp.concatenate([f, t], axis=0), jnp.concatenate([t, f], axis=0))
            for f, t in zip(fs, theirs)]


def _gather8(v, reduce):
    m = v.shape[0]

    def body(x_ref, *rest):
        if reduce:
            sum_ref, out_ref, send_sems, recv_sems, local_sem = rest
        else:
            out_ref, send_sems, recv_sems, local_sem = rest
        x, y, c, chips = _place()
        me, sibling = (x, y, c), (x, y, 1 - c)

        def rows(px, py, pc):
            return out_ref.at[pl.ds((4 * px + 2 * py + pc) * m, m), :]

        def copy(k, block, to, src=None):
            return _remote(rows(*block) if src is None else src, rows(*block), send_sems.at[k], recv_sems.at[k], to)

        mine = pltpu.make_async_copy(x_ref, rows(*me), local_sem)
        mine.start()
        first = [copy(0, me, sibling, src=x_ref)]
        first += [copy(1 + j, me, (*chip, c), src=x_ref) for j, chip in enumerate(chips)]
        for cp in first:
            cp.start()
        passed = [copy(4 + j, (*chip, c), sibling) for j, chip in enumerate(chips)]
        for j, chip in enumerate(chips):
            copy(1 + j, (*chip, c), me).wait_recv()
            passed[j].start()
        copy(0, sibling, me).wait_recv()
        for j, chip in enumerate(chips):
            copy(4 + j, (*chip, 1 - c), me).wait_recv()
        for cp in first + passed:
            cp.wait_send()
        mine.wait()
        if reduce:
            acc = out_ref[pl.ds(0, m), :]
            for k in range(1, 8):
                acc = acc + out_ref[pl.ds(k * m, m), :]
            sum_ref[...] = acc

    vm = pl.BlockSpec(memory_space=pltpu.VMEM)
    out_shape = [_sds((8 * m, 128), F32)]
    if reduce:
        out_shape = [_sds((m, 128), F32)] + out_shape
    out = pl.pallas_call(
        body, name="allreduce_small" if reduce else "gather_small", in_specs=[vm], out_specs=[vm] * len(out_shape),
        out_shape=out_shape,
        scratch_shapes=[pltpu.SemaphoreType.DMA((7,)), pltpu.SemaphoreType.DMA((7,)), pltpu.SemaphoreType.DMA(())],
        compiler_params=pltpu.CompilerParams(vmem_limit_bytes=VMEM_LIMIT))(v)
    return out[0]


def _adamw(w, g, m, v, *, name):
    if w.ndim == 2:
        return [o[0] for o in _adamw(w[None], g[None], m[None], v[None], name=name)]
    nl, r, c = w.shape
    tm = _rows_tile(r, c, 7)
    blk = pl.BlockSpec((None, tm, c), lambda l, i: (l, i, 0))

    def body(w_ref, g_ref, m_ref, v_ref, d_ref, mo_ref, vo_ref):
        gg = g_ref[...]
        mm = B1 * m_ref[...] + (1.0 - B1) * gg
        vv = B2 * v_ref[...] + (1.0 - B2) * (gg * gg)
        mo_ref[...] = mm
        vo_ref[...] = vv
        m_hat = mm / (1.0 - B1 ** STEP)
        v_hat = vv / (1.0 - B2 ** STEP)
        d_ref[...] = -LR * (m_hat / (jnp.sqrt(v_hat) + AEPS) + WD * w_ref[...])

    return _pc(body, name=name, grid=(nl, r // tm), in_specs=[blk] * 4, out_specs=[blk] * 3,
               out_shape=[_sds((nl, r, c), F32)] * 3)(w, g, m, v)


WEIGHTS = ["ssm_norm", "ssm_w_in", "ssm_conv_w", "ssm_conv_b", "ssm_dt_bias", "ssm_a_log", "ssm_d", "ssm_gate_norm",
           "ssm_w_out", "cv_norm", "cv_w_pw1", "cv_b_pw1", "cv_dw_w", "cv_dw_b", "cv_ln_g", "cv_ln_b", "cv_w_pw2",
           "cv_b_pw2", "ffn_norm", "ffn_w_gate", "ffn_w_up", "ffn_w_down", "final_norm"]
TRANSPOSED = ("ffn_w_gate", "ffn_w_up")
ADAMW_TRANSPOSED = ("ssm_w_in",) + TRANSPOSED
BIG = {"ssm_w_in": "col", "ssm_w_out": "row", "cv_w_pw1": "col", "cv_w_pw2": "row",
       "ffn_w_gate": "row", "ffn_w_up": "row", "ffn_w_down": "row"}
SMALL_SHARDED = {"ssm_conv_w": 2, "cv_norm": 1, "cv_b_pw1": 1, "cv_dw_w": 2, "cv_dw_b": 1, "cv_ln_g": 1, "cv_ln_b": 1,
                 "cv_b_pw2": 1}
SMALL = [n for n in WEIGHTS if n not in BIG]
LANES = 128


def _pack(arrays):
    flat = jnp.concatenate([a.reshape(-1) for a in arrays])
    rows = -(-flat.size // (8 * LANES)) * 8
    return jnp.pad(flat, (0, rows * LANES - flat.size)).reshape(rows, LANES)


def _unpack(packed, shapes):
    flat, out, off = packed.reshape(-1), [], 0
    for s in shapes:
        n = 1
        for d_ in s:
            n *= d_
        out.append(flat[off:off + n].reshape(s))
        off += n
    return out


def _unshard(parts, axis):
    return jnp.concatenate(parts, axis=axis)


def kernel(x, ssm_norm, ssm_w_in, ssm_conv_w, ssm_conv_b, ssm_dt_bias, ssm_a_log, ssm_d, ssm_gate_norm, ssm_w_out, cv_norm, cv_w_pw1, cv_b_pw1, cv_dw_w, cv_dw_b, cv_ln_g, cv_ln_b, cv_w_pw2, cv_b_pw2, ffn_norm, ffn_w_gate, ffn_w_up, ffn_w_down, final_norm, loss_target, m_ssm_norm, m_ssm_w_in, m_ssm_conv_w, m_ssm_conv_b, m_ssm_dt_bias, m_ssm_a_log, m_ssm_d, m_ssm_gate_norm, m_ssm_w_out, m_cv_norm, m_cv_w_pw1, m_cv_b_pw1, m_cv_dw_w, m_cv_dw_b, m_cv_ln_g, m_cv_ln_b, m_cv_w_pw2, m_cv_b_pw2, m_ffn_norm, m_ffn_w_gate, m_ffn_w_up, m_ffn_w_down, m_final_norm, v_ssm_norm, v_ssm_w_in, v_ssm_conv_w, v_ssm_conv_b, v_ssm_dt_bias, v_ssm_a_log, v_ssm_d, v_ssm_gate_norm, v_ssm_w_out, v_cv_norm, v_cv_w_pw1, v_cv_b_pw1, v_cv_dw_w, v_cv_dw_b, v_cv_ln_g, v_cv_ln_b, v_cv_w_pw2, v_cv_b_pw2, v_ffn_norm, v_ffn_w_gate, v_ffn_w_up, v_ffn_w_down, v_final_norm):
    a = dict(locals())
    chip = 2 * lax.axis_index("x") + lax.axis_index("y")
    n_ssm, n_cv, depth = ssm_norm.shape[0], cv_norm.shape[0], ffn_norm.shape[0]

    def own(n, layer):
        w = a[n][layer].astype(BF16)
        return w.T if n in TRANSPOSED else w

    sh_names = list(SMALL_SHARDED)
    got = _gather8(_pack([a[n] for n in sh_names]), reduce=False)
    got = got.reshape(8, -1)[0::2]
    per_chip = [_unpack(got[s], [a[n].shape for n in sh_names]) for s in range(4)]
    full = {n: a[n] for n in SMALL}
    for k, n in enumerate(sh_names):
        full[n] = _unshard([per_chip[s][k] for s in range(4)], SMALL_SHARDED[n])

    def matrices_of(i, part="layer"):
        mixer = [(n, i // 2) for n in (("ssm_w_in", "ssm_w_out") if i % 2 == 0 else ("cv_w_pw1", "cv_w_pw2"))]
        ffn_part = [(n, i) for n in ("ffn_w_gate", "ffn_w_up", "ffn_w_down")]
        return {"mixer": mixer, "ffn": ffn_part, "layer": mixer + ffn_part}[part]

    first = matrices_of(0, "mixer")
    stages = [first[:1], first[1:] + matrices_of(0, "ffn")] + [matrices_of(i) for i in range(1, depth)]
    in_flight, landed = {}, {}

    def matrix(n, layer, h):
        if not in_flight and not landed:
            dep = h
            for k, stage in enumerate(stages):
                in_flight[k] = _gather_start([own(m, l) for m, l in stage], dep, str(k))
                dep = in_flight[k][-1]
            in_flight["token"] = dep[0, 0]
        if (n, layer) not in landed:
            k = next(k for k, st in enumerate(stages) if (n, layer) in st)
            ssem, rsem, srcs, lands, _ = in_flight.pop(k)
            srcs, lands = _gather_wait(ssem, rsem, srcs, lands, h, str(k))
            lands = _gather_forward(lands, str(k))
            for key, src, land in zip(stages[k], srcs, lands):
                landed[key] = (src, land, in_flight["token"])
        return landed[(n, layer)]

    def whole(h, *keys):
        got = [matrix(n, l, h) for n, l in keys]
        if BIG[keys[0][0]] == "row" and len(keys) == 1:
            src, land, token = got[0]
            return lax.dynamic_update_index_in_dim(land, src, chip, 0).reshape(-1, land.shape[-1]), token
        parts = [jnp.where(chip == s, src, land[s]) for src, land, _ in got for s in range(4)]
        return jnp.concatenate(parts, axis=1 if BIG[keys[0][0]] == "col" else 0), got[0][2]

    def mixer_weights(i, h):
        j = i // 2
        if i % 2 == 0:
            w_in, token = whole(h, ("ssm_w_in", j))
            return dict(norm=full["ssm_norm"][j] + token, ffn_norm=full["ffn_norm"][i], wz=w_in[:, :DI], wxbc=w_in[:, DI:DI + CD],
                        wdt=jnp.pad(w_in[:, DI + CD:], ((0, 0), (0, NHP - NH))),
                        conv_w=full["ssm_conv_w"][j], conv_b=full["ssm_conv_b"][j], dt_bias=full["ssm_dt_bias"][j],
                        a_log=full["ssm_a_log"][j], d=full["ssm_d"][j], gate_norm=full["ssm_gate_norm"][j],
                        wout=lambda act: whole(act, ("ssm_w_out", j))[0])
        (w_pw1, token), (w_pw2, _) = whole(h, ("cv_w_pw1", j)), whole(h, ("cv_w_pw2", j))
        return dict(norm=full["cv_norm"][j] + token, ffn_norm=full["ffn_norm"][i], wpw1=w_pw1, b_pw1=full["cv_b_pw1"][j], dw_w=full["cv_dw_w"][j],
                    dw_b=full["cv_dw_b"][j], ln_g=full["cv_ln_g"][j], ln_b=full["cv_ln_b"][j], wpw2=w_pw2,
                    b_pw2=full["cv_b_pw2"][j])

    def ffn_weights(i, h):
        (wg_t, token), (wu_t, _), (wd, _) = (whole(h, (n, i)) for n in ("ffn_w_gate", "ffn_w_up", "ffn_w_down"))
        after = None if i + 1 == depth else (full["ssm_norm"] if i % 2 else full["cv_norm"])[(i + 1) // 2]
        return dict(norm=full["ffn_norm"][i] + token, next_norm=after, wg_t=wg_t, wu_t=wu_t, wd=wd)

    core = lax.axis_index("c")
    reducing, seen = {}, {}

    def reduce_small(g_final):
        g_mixers, g_ffns = [seen[i][0] for i in range(depth)], [seen[i][1] for i in range(depth)]
        local_small = {"final_norm": g_final, "ffn_norm": jnp.stack([g["norm"] for g in g_ffns])}
        for n in SMALL:
            if n.startswith("ssm_"):
                local_small[n] = jnp.stack([g[n[4:]] for g in g_mixers[0::2]])
            elif n.startswith("cv_"):
                local_small[n] = jnp.stack([g[n[3:]] for g in g_mixers[1::2]])
        return _gather8(_pack([local_small[n] for n in SMALL]), reduce=True)

    started = []

    def start_reduce(i, part, g_by_name, dep=None):
        slots = []
        for n, _ in matrices_of(i, part):
            g = g_by_name(n)
            if BIG[n] == "col":
                slots.append(g.reshape(g.shape[0], 4, g.shape[1] // 4).transpose(1, 0, 2))
            else:
                slots.append(g.reshape(4, g.shape[0] // 4, g.shape[1]))
        reducing[(i, part)], reducing["token"] = _reduce_start(slots, core, f"{i}{part[0]}", dep)
        started.append((i, part))
        return reducing["token"]

    def ffn_done(i, g_ffn_i):
        return start_reduce(0, "ffn", lambda n: g_ffn_i[n.split("_", 1)[1]]) if i == 0 else None

    def layer_done(i, g_mixer, g_ffn_i, g_final):
        seen[i] = (g_mixer, g_ffn_i)
        pick = lambda n: (g_ffn_i if n.startswith("ffn_") else g_mixer)[n.split("_", 1)[1]]
        if i > 0:
            return start_reduce(i, "layer", pick)
        reducing["small"] = reduce_small(g_final)
        return start_reduce(0, "mixer", pick, reducing["small"])

    lsum, grad_x, g_mix, g_ffn, g_final = _fwd_bwd(x[0], loss_target[0], depth, mixer_weights, ffn_weights,
                                                   full["final_norm"], layer_done, ffn_done)
    loss = (0.5 / D) * lax.psum(jnp.sum(lsum), ("x", "y", "c"))

    grads = {}
    for n, g in zip(SMALL, _unpack(reducing.pop("small"), [full[n].shape for n in SMALL])):
        if n in SMALL_SHARDED:
            ax = SMALL_SHARDED[n]
            g = lax.dynamic_slice_in_dim(g, chip * a[n].shape[ax], a[n].shape[ax], axis=ax)
        grads[n] = g
    last_start = reducing.pop("token")

    per_layer, after = {n: [None] * a[n].shape[0] for n in BIG}, last_start
    for i, part in started:
        reducing[(i, part)] = _reduce_finish(reducing[(i, part)], after, chip, f"{i}{part[0]}")
        after = reducing[(i, part)][2][0]
    for i, part in started:
        done = _reduce_collect(reducing.pop((i, part)), after, core, f"{i}{part[0]}")
        for (n, l), g in zip(matrices_of(i, part), done):
            per_layer[n][l] = g
    grads_t = {}
    for n in BIG:
        if n in TRANSPOSED:
            grads_t[n] = jnp.stack(per_layer[n])
        else:
            grads[n] = jnp.stack(per_layer[n])

    delta, new_m, new_v = {}, {}, {}
    for n in BIG:
        if n in ADAMW_TRANSPOSED:
            g_t = grads_t[n] if n in grads_t else grads[n].transpose(0, 2, 1)
            outs = _adamw(a[n].transpose(0, 2, 1), g_t, a["m_" + n].transpose(0, 2, 1), a["v_" + n].transpose(0, 2, 1),
                          name="adamw_" + n)
            grads[n] = g_t.transpose(0, 2, 1)
            delta[n], new_m[n], new_v[n] = [o.transpose(0, 2, 1) for o in outs]
        else:
            delta[n], new_m[n], new_v[n] = _adamw(a[n], grads[n], a["m_" + n], a["v_" + n], name="adamw_" + n)
    shapes = [a[n].shape for n in SMALL]
    upd = _adamw(_pack([a[n] for n in SMALL]), _pack([grads[n] for n in SMALL]), _pack([a["m_" + n] for n in SMALL]),
                 _pack([a["v_" + n] for n in SMALL]), name="adamw_small")
    for dst, packed in zip((delta, new_m, new_v), upd):
        dst.update(zip(SMALL, _unpack(packed, shapes)))

    return (loss, grad_x[None], *[grads[n] for n in WEIGHTS], *[delta[n] for n in WEIGHTS],
            *[new_m[n] for n in WEIGHTS], *[new_v[n] for n in WEIGHTS])
```
